```python
import math
import jax, jax.numpy as jnp
from jax import lax
import numpy as np

D_MODEL = 2048
BATCH = 8
SEQ = 4096
DEPTH = 4

GRID_W = 64
CTX_LEN = 256
RMS_EPS = 1e-6
NEG_INF = -1e30

BRANCH_WIDTH = D_MODEL // 4
N_BRANCHES = 4
NA_HEAD_DIM = 64
NA_HEADS = BRANCH_WIDTH // NA_HEAD_DIM
NA_WIDTH = NA_HEADS * NA_HEAD_DIM
NA_WIN_ROWS = 8
NA_WIN_COLS = 16
POOL_WIDTH = BRANCH_WIDTH
POOL_WINDOWS = (2, 4, 8, 16)
POOL_GROUPS = 4
POOL_GROUP_DIM = POOL_WIDTH // POOL_GROUPS
CONV_WIDTH = BRANCH_WIDTH
CONV_K = 3
SSM_WIDTH = BRANCH_WIDTH
SSM_GROUP_DIM = 16
SSM_GROUPS = SSM_WIDTH // SSM_GROUP_DIM
SSM_STATE = 64
SSM_DT_MIN = 1e-3
SSM_DT_MAX = 1e-1

BRANCH_TOTAL = NA_WIDTH + POOL_WIDTH + CONV_WIDTH + SSM_WIDTH
IN_LAYOUT = (
    ("na_q", NA_WIDTH), ("na_k", NA_WIDTH), ("na_v", NA_WIDTH), ("na_z", NA_WIDTH),
    ("pool_u", POOL_WIDTH), ("pool_z", POOL_WIDTH),
    ("conv_x", CONV_WIDTH), ("conv_b", CONV_WIDTH), ("conv_c", CONV_WIDTH), ("conv_z", CONV_WIDTH),
    ("ssm_u", SSM_WIDTH), ("ssm_z", SSM_WIDTH),
    ("merge", N_BRANCHES * D_MODEL),
)
IN_TOTAL = sum(size for _, size in IN_LAYOUT)

kernel_name = "hybrid_gated_mixer_dit_block"


def _in_slices():
    out, start = {}, 0
    for name, size in IN_LAYOUT:
        out[name] = (start, start + size)
        start += size
    return out


def rms_norm(x, g):
    xf = x.astype(jnp.float32)
    y = xf * lax.rsqrt(jnp.mean(xf * xf, axis=-1, keepdims=True) + RMS_EPS)
    return (y * g.astype(jnp.float32)).astype(x.dtype)


def neighbourhood_attention(q, k, v, qc, kc, vc, rpb):
    B, L, H, Dh = q.shape
    rows = L // GRID_W
    kr = min(NA_WIN_ROWS, rows)
    kcw = NA_WIN_COLS
    scale = Dh ** -0.5
    r = jnp.arange(rows)
    row_idx = jnp.clip(r - kr // 2, 0, rows - kr)[:, None] + jnp.arange(kr)[None, :]
    col = jnp.arange(GRID_W)
    col_start = jnp.clip(col - kcw // 2, 0, GRID_W - kcw)
    in_win = (col[None, :] >= col_start[:, None]) & (col[None, :] < col_start[:, None] + kcw)
    drow = row_idx - r[:, None] + (NA_WIN_ROWS - 1)
    dcol = jnp.clip(col[None, :] - col[:, None] + (NA_WIN_COLS - 1), 0, 2 * NA_WIN_COLS - 2)
    bias = rpb[:, drow[:, None, :, None], dcol[None, :, None, :]].astype(jnp.float32)
    bias = jnp.where(in_win[None, None, :, None, :], bias, NEG_INF)

    qg = q.reshape(B, rows, GRID_W, H, Dh)
    kg = k.reshape(B, rows, GRID_W, H, Dh)[:, row_idx]
    vg = v.reshape(B, rows, GRID_W, H, Dh)[:, row_idx]
    s_band = jnp.einsum('brqhd,brkwhd->bhrqkw', qg, kg,
                        preferred_element_type=jnp.float32) * scale + bias[None]
    s_ctx = jnp.einsum('brqhd,bnhd->bhrqn', qg, kc, preferred_element_type=jnp.float32) * scale
    n_band = kr * GRID_W
    s = jnp.concatenate([s_band.reshape(B, H, rows, GRID_W, n_band), s_ctx], axis=-1)
    p = jax.nn.softmax(s, axis=-1)
    p_band = p[..., :n_band].reshape(B, H, rows, GRID_W, kr, GRID_W).astype(v.dtype)
    p_ctx = p[..., n_band:].astype(v.dtype)
    o = (jnp.einsum('bhrqkw,brkwhd->brqhd', p_band, vg)
         + jnp.einsum('bhrqn,bnhd->brqhd', p_ctx, vc))
    o = o.reshape(B, L, H * Dh)
    oc = None
    if qc is not None:
        sc = jnp.einsum('bnhd,bmhd->bhnm', qc, kc, preferred_element_type=jnp.float32) * scale
        pc = jax.nn.softmax(sc, axis=-1).astype(vc.dtype)
        oc = jnp.einsum('bhnm,bmhd->bnhd', pc, vc).reshape(qc.shape[0], qc.shape[1], H * Dh)
    return o, oc


def centred_pool_minus_identity(x, window):
    B, L, C = x.shape
    xf = x.astype(jnp.float32)
    cs = jnp.concatenate([jnp.zeros((B, 1, C), jnp.float32), jnp.cumsum(xf, axis=1)], axis=1)
    t = jnp.arange(L)
    lo = jnp.clip(t - window // 2, 0, L)
    hi = jnp.clip(t + window - window // 2, 0, L)
    cnt = (hi - lo).astype(jnp.float32)[None, :, None]
    return ((cs[:, hi] - cs[:, lo]) / cnt - xf).astype(x.dtype)


def pool_branch(u, pool_w, pool_scale):
    B, L, _ = u.shape
    groups = jnp.split(u, POOL_GROUPS, axis=-1)
    pooled = jnp.stack([centred_pool_minus_identity(g, w) for g, w in zip(groups, POOL_WINDOWS)], axis=2)
    mixed = jnp.einsum('blgc,gcd->blgd', pooled, pool_w).reshape(B, L, POOL_WIDTH)
    return mixed * pool_scale


def dwconv3(x, w):
    L = x.shape[1]
    xp = jnp.pad(x, ((0, 0), (1, 1), (0, 0)))
    return xp[:, :L] * w[0] + xp[:, 1:L + 1] * w[1] + xp[:, 2:] * w[2]


def conv_branch(xv, gb, gc, conv_w):
    return gb * dwconv3(gc * xv, conv_w)


def _cmul(ar, ai, br, bi):
    return ar * br - ai * bi, ar * bi + ai * br


def s5_discretise(a_re, a_im, log_dt, b_re, b_im):
    f32 = jnp.float32
    a_re, a_im = a_re.astype(f32), a_im.astype(f32)
    dt = jnp.exp(log_dt.astype(f32))[:, None]
    mag = jnp.exp(a_re * dt)
    abar_re, abar_im = mag * jnp.cos(a_im * dt), mag * jnp.sin(a_im * dt)
    den = a_re * a_re + a_im * a_im
    num_re, num_im = abar_re - 1.0, abar_im
    f_re = (num_re * a_re + num_im * a_im) / den
    f_im = (num_im * a_re - num_re * a_im) / den
    bbar_re, bbar_im = _cmul(f_re[..., None], f_im[..., None], b_re.astype(f32), b_im.astype(f32))
    return abar_re, abar_im, bbar_re, bbar_im


def diag_scan(abar_re, abar_im, bu_re, bu_im, s0_re, s0_im, reverse):
    L = bu_re.shape[1]
    a_re = jnp.broadcast_to(abar_re, (1, L) + abar_re.shape)
    a_im = jnp.broadcast_to(abar_im, (1, L) + abar_im.shape)

    def combine(e1, e2):
        a1r, a1i, b1r, b1i = e1
        a2r, a2i, b2r, b2i = e2
        ar, ai = _cmul(a2r, a2i, a1r, a1i)
        br, bi = _cmul(a2r, a2i, b1r, b1i)
        return ar, ai, br + b2r, bi + b2i

    pr, pi, sr, si = lax.associative_scan(combine, (a_re, a_im, bu_re, bu_im), axis=1, reverse=reverse)
    if s0_re is not None:
        ir, ii = _cmul(pr, pi, s0_re[:, None], s0_im[:, None])
        sr, si = sr + ir, si + ii
    return sr, si


def s5_readout(s_re, s_im, c_re, c_im):
    B, L = s_re.shape[:2]
    y = (jnp.einsum('blgp,ghp->blgh', s_re, c_re.astype(jnp.float32))
         - jnp.einsum('blgp,ghp->blgh', s_im, c_im.astype(jnp.float32)))
    return y.reshape(B, L, SSM_WIDTH)


def s5_glu(y, glu_w, dtype):
    g = jax.nn.gelu(y).astype(dtype)
    ga, gb = jnp.split(g @ glu_w, 2, axis=-1)
    return ga * jax.nn.sigmoid(gb)


def s5_branch(u, uc, a_re, a_im, log_dt, b_re, b_im, c_re, c_im, d_skip, glu_w, with_ctx_out):
    dtype = u.dtype
    B, L, _ = u.shape
    N = uc.shape[1]
    uf, ucf = u.astype(jnp.float32), uc.astype(jnp.float32)
    ug = uf.reshape(B, L, SSM_GROUPS, SSM_GROUP_DIM)
    ucg = ucf.reshape(B, N, SSM_GROUPS, SSM_GROUP_DIM)
    dsk = d_skip.astype(jnp.float32)
    y = dsk * uf
    yc = dsk * ucf if with_ctx_out else None
    for direction, reverse in ((0, False), (1, True)):
        abr, abi, bbr, bbi = s5_discretise(a_re[direction], a_im[direction], log_dt[direction],
                                           b_re[direction], b_im[direction])
        buc_r = jnp.einsum('blgh,gph->blgp', ucg, bbr)
        buc_i = jnp.einsum('blgh,gph->blgp', ucg, bbi)
        sc_r, sc_i = diag_scan(abr, abi, buc_r, buc_i, None, None, reverse)
        last = 0 if reverse else N - 1
        bu_r = jnp.einsum('blgh,gph->blgp', ug, bbr)
        bu_i = jnp.einsum('blgh,gph->blgp', ug, bbi)
        s_r, s_i = diag_scan(abr, abi, bu_r, bu_i, sc_r[:, last], sc_i[:, last], reverse)
        y = y + s5_readout(s_r, s_i, c_re[direction], c_im[direction])
        if with_ctx_out:
            yc = yc + s5_readout(sc_r, sc_i, c_re[direction], c_im[direction])
    out = s5_glu(y, glu_w, dtype)
    outc = s5_glu(yc, glu_w, dtype) if with_ctx_out else None
    return out, outc


def gated_merge(outs, zs, gate_logits, b_gate, w_br, w_o):
    gates = jax.nn.sigmoid((gate_logits + b_gate).astype(jnp.float32)).astype(gate_logits.dtype)
    merged = None
    start = 0
    for i, (o, z) in enumerate(zip(outs, zs)):
        width = o.shape[-1]
        br = (o * jax.nn.silu(z)) @ w_br[start:start + width]
        term = gates[..., i * D_MODEL:(i + 1) * D_MODEL] * br
        merged = term if merged is None else merged + term
        start += width
    return merged @ w_o


def hybrid_mixer(h, hc, w_in, b_gate, na_rpb, pool_w, pool_scale, conv_w,
                 ssm_a_re, ssm_a_im, ssm_log_dt, ssm_b_re, ssm_b_im, ssm_c_re, ssm_c_im, ssm_d,
                 glu_w, w_br, w_o, with_ctx_out):
    sl = _in_slices()
    B, L, _ = h.shape
    N = hc.shape[1]
    proj = h @ w_in

    def part(name):
        a, b = sl[name]
        return proj[..., a:b]

    if with_ctx_out:
        projc = hc @ w_in

        def partc(name):
            a, b = sl[name]
            return projc[..., a:b]
    else:
        def partc(name):
            a, b = sl[name]
            return hc @ w_in[:, a:b]

    heads = lambda t, n: t.reshape(B if t.shape[1] == L else t.shape[0], n, NA_HEADS, NA_HEAD_DIM)
    q, k, v = heads(part("na_q"), L), heads(part("na_k"), L), heads(part("na_v"), L)
    kc, vc = heads(partc("na_k"), N), heads(partc("na_v"), N)
    qc = heads(partc("na_q"), N) if with_ctx_out else None
    o_na, oc_na = neighbourhood_attention(q, k, v, qc, kc, vc, na_rpb)

    o_pool = pool_branch(part("pool_u"), pool_w, pool_scale)
    o_conv = conv_branch(part("conv_x"), part("conv_b"), part("conv_c"), conv_w)
    o_ssm, oc_ssm = s5_branch(part("ssm_u"), partc("ssm_u"), ssm_a_re, ssm_a_im, ssm_log_dt,
                              ssm_b_re, ssm_b_im, ssm_c_re, ssm_c_im, ssm_d, glu_w, with_ctx_out)
    y = gated_merge([o_na, o_pool, o_conv, o_ssm],
                    [part("na_z"), part("pool_z"), part("conv_z"), part("ssm_z")],
                    part("merge"), b_gate, w_br, w_o)
    yc = None
    if with_ctx_out:
        oc_pool = pool_branch(partc("pool_u"), pool_w, pool_scale)
        oc_conv = conv_branch(partc("conv_x"), partc("conv_b"), partc("conv_c"), conv_w)
        yc = gated_merge([oc_na, oc_pool, oc_conv, oc_ssm],
                         [partc("na_z"), partc("pool_z"), partc("conv_z"), partc("ssm_z")],
                         partc("merge"), b_gate, w_br, w_o)
    return y, yc


def _fwd_setup_inputs(seed: int = 0) -> dict:
    key = jax.random.key(seed)
    ks = jax.random.split(key, 32)
    f32 = jnp.float32

    def nrm(k, shape, std):
        return jax.random.normal(k, shape, f32) * std

    D = D_MODEL
    n_idx = jnp.arange(SSM_STATE, dtype=f32)
    sp = (DEPTH, 2, SSM_GROUPS, SSM_STATE)
    return {
        "x": nrm(ks[0], (BATCH, SEQ, D), 1.0),
        "c": nrm(ks[1], (BATCH, D), 1.0),
        "ctx": nrm(ks[2], (BATCH, CTX_LEN, D), 1.0),
        "c_ctx": nrm(ks[3], (D,), 1.0),
        "w_mod": nrm(ks[4], (DEPTH, D, 3 * D), 0.5 * D ** -0.5),
        "b_mod": nrm(ks[5], (DEPTH, 3 * D), 0.02),
        "g_pre": 1.0 + nrm(ks[6], (DEPTH, D), 0.02),
        "g_post": 1.0 + nrm(ks[7], (DEPTH, D), 0.02),
        "w_in": nrm(ks[8], (DEPTH, D, IN_TOTAL), D ** -0.5),
        "b_gate": nrm(ks[9], (DEPTH, N_BRANCHES * D), 0.02),
        "na_rpb": nrm(ks[10], (DEPTH, NA_HEADS, 2 * NA_WIN_ROWS - 1, 2 * NA_WIN_COLS - 1), 0.02),
        "pool_w": nrm(ks[11], (DEPTH, POOL_GROUPS, POOL_GROUP_DIM, POOL_GROUP_DIM), POOL_GROUP_DIM ** -0.5),
        "pool_scale": 1.0 + nrm(ks[12], (DEPTH, POOL_WIDTH), 0.02),
        "conv_w": nrm(ks[13], (DEPTH, CONV_K, CONV_WIDTH), CONV_K ** -0.5),
        "ssm_a_re": -0.5 + nrm(ks[14], sp, 0.01),
        "ssm_a_im": math.pi * n_idx + nrm(ks[15], sp, 0.01),
        "ssm_log_dt": jax.random.uniform(ks[16], (DEPTH, 2, SSM_GROUPS), f32,
                                         math.log(SSM_DT_MIN), math.log(SSM_DT_MAX)),
        "ssm_b_re": nrm(ks[17], sp + (SSM_GROUP_DIM,), (2 * SSM_GROUP_DIM) ** -0.5),
        "ssm_b_im": nrm(ks[18], sp + (SSM_GROUP_DIM,), (2 * SSM_GROUP_DIM) ** -0.5),
        "ssm_c_re": nrm(ks[19], (DEPTH, 2, SSM_GROUPS, SSM_GROUP_DIM, SSM_STATE), SSM_STATE ** -0.5),
        "ssm_c_im": nrm(ks[20], (DEPTH, 2, SSM_GROUPS, SSM_GROUP_DIM, SSM_STATE), SSM_STATE ** -0.5),
        "ssm_d": nrm(ks[21], (DEPTH, SSM_WIDTH), 1.0),
        "glu_w": nrm(ks[22], (DEPTH, SSM_WIDTH, 2 * SSM_WIDTH), SSM_WIDTH ** -0.5),
        "w_br": nrm(ks[23], (DEPTH, BRANCH_TOTAL, D), BRANCH_WIDTH ** -0.5),
        "w_o": nrm(ks[24], (DEPTH, D, D), D ** -0.5),
    }


def _fwd_reference(x, c, ctx, c_ctx, w_mod, b_mod, g_pre, g_post, w_in, b_gate, na_rpb, pool_w,
              pool_scale, conv_w, ssm_a_re, ssm_a_im, ssm_log_dt, ssm_b_re, ssm_b_im,
              ssm_c_re, ssm_c_im, ssm_d, glu_w, w_br, w_o):
    c_act = jax.nn.silu(c)
    cc_act = jax.nn.silu(c_ctx)
    xc = ctx
    for i in range(DEPTH):
        with_ctx_out = i < DEPTH - 1
        shift, scale, gate = jnp.split(c_act @ w_mod[i] + b_mod[i], 3, axis=-1)
        shift_c, scale_c, gate_c = jnp.split(cc_act @ w_mod[i] + b_mod[i], 3, axis=-1)
        h = rms_norm(x, g_pre[i]) * (1.0 + scale[:, None]) + shift[:, None]
        hc = rms_norm(xc, g_pre[i]) * (1.0 + scale_c) + shift_c
        y, yc = hybrid_mixer(h, hc, w_in[i], b_gate[i], na_rpb[i], pool_w[i], pool_scale[i], conv_w[i],
                             ssm_a_re[i], ssm_a_im[i], ssm_log_dt[i], ssm_b_re[i], ssm_b_im[i],
                             ssm_c_re[i], ssm_c_im[i], ssm_d[i], glu_w[i], w_br[i], w_o[i], with_ctx_out)
        x = x + gate[:, None] * rms_norm(y, g_post[i])
        if with_ctx_out:
            xc = xc + gate_c * rms_norm(yc, g_post[i])
    return x


import jax as _jax
import jax.numpy as _jnp

TWIN_FORMAT = 'train_step'
FWD_PARAMS = ['x', 'c', 'ctx', 'c_ctx', 'w_mod', 'b_mod', 'g_pre', 'g_post', 'w_in', 'b_gate', 'na_rpb', 'pool_w', 'pool_scale', 'conv_w', 'ssm_a_re', 'ssm_a_im', 'ssm_log_dt', 'ssm_b_re', 'ssm_b_im', 'ssm_c_re', 'ssm_c_im', 'ssm_d', 'glu_w', 'w_br', 'w_o']
TWIN_WEIGHTS = ['c_ctx', 'w_mod', 'b_mod', 'g_pre', 'g_post', 'w_in', 'b_gate', 'na_rpb', 'pool_w', 'pool_scale', 'conv_w', 'ssm_a_re', 'ssm_a_im', 'ssm_log_dt', 'ssm_b_re', 'ssm_b_im', 'ssm_c_re', 'ssm_c_im', 'ssm_d', 'glu_w', 'w_br', 'w_o']
TWIN_DIFF_INPUT = 'x'
TWIN_INPUTS = ['x', 'c', 'ctx', 'c_ctx', 'w_mod', 'b_mod', 'g_pre', 'g_post', 'w_in', 'b_gate', 'na_rpb', 'pool_w', 'pool_scale', 'conv_w', 'ssm_a_re', 'ssm_a_im', 'ssm_log_dt', 'ssm_b_re', 'ssm_b_im', 'ssm_c_re', 'ssm_c_im', 'ssm_d', 'glu_w', 'w_br', 'w_o', 'loss_target', 'm_c_ctx', 'm_w_mod', 'm_b_mod', 'm_g_pre', 'm_g_post', 'm_w_in', 'm_b_gate', 'm_na_rpb', 'm_pool_w', 'm_pool_scale', 'm_conv_w', 'm_ssm_a_re', 'm_ssm_a_im', 'm_ssm_log_dt', 'm_ssm_b_re', 'm_ssm_b_im', 'm_ssm_c_re', 'm_ssm_c_im', 'm_ssm_d', 'm_glu_w', 'm_w_br', 'm_w_o', 'v_c_ctx', 'v_w_mod', 'v_b_mod', 'v_g_pre', 'v_g_post', 'v_w_in', 'v_b_gate', 'v_na_rpb', 'v_pool_w', 'v_pool_scale', 'v_conv_w', 'v_ssm_a_re', 'v_ssm_a_im', 'v_ssm_log_dt', 'v_ssm_b_re', 'v_ssm_b_im', 'v_ssm_c_re', 'v_ssm_c_im', 'v_ssm_d', 'v_glu_w', 'v_w_br', 'v_w_o']
TWIN_OUTPUTS = ['loss', 'grad_x', 'grad_c_ctx', 'grad_w_mod', 'grad_b_mod', 'grad_g_pre', 'grad_g_post', 'grad_w_in', 'grad_b_gate', 'grad_na_rpb', 'grad_pool_w', 'grad_pool_scale', 'grad_conv_w', 'grad_ssm_a_re', 'grad_ssm_a_im', 'grad_ssm_log_dt', 'grad_ssm_b_re', 'grad_ssm_b_im', 'grad_ssm_c_re', 'grad_ssm_c_im', 'grad_ssm_d', 'grad_glu_w', 'grad_w_br', 'grad_w_o', 'delta_c_ctx', 'delta_w_mod', 'delta_b_mod', 'delta_g_pre', 'delta_g_post', 'delta_w_in', 'delta_b_gate', 'delta_na_rpb', 'delta_pool_w', 'delta_pool_scale', 'delta_conv_w', 'delta_ssm_a_re', 'delta_ssm_a_im', 'delta_ssm_log_dt', 'delta_ssm_b_re', 'delta_ssm_b_im', 'delta_ssm_c_re', 'delta_ssm_c_im', 'delta_ssm_d', 'delta_glu_w', 'delta_w_br', 'delta_w_o', 'new_m_c_ctx', 'new_m_w_mod', 'new_m_b_mod', 'new_m_g_pre', 'new_m_g_post', 'new_m_w_in', 'new_m_b_gate', 'new_m_na_rpb', 'new_m_pool_w', 'new_m_pool_scale', 'new_m_conv_w', 'new_m_ssm_a_re', 'new_m_ssm_a_im', 'new_m_ssm_log_dt', 'new_m_ssm_b_re', 'new_m_ssm_b_im', 'new_m_ssm_c_re', 'new_m_ssm_c_im', 'new_m_ssm_d', 'new_m_glu_w', 'new_m_w_br', 'new_m_w_o', 'new_v_c_ctx', 'new_v_w_mod', 'new_v_b_mod', 'new_v_g_pre', 'new_v_g_post', 'new_v_w_in', 'new_v_b_gate', 'new_v_na_rpb', 'new_v_pool_w', 'new_v_pool_scale', 'new_v_conv_w', 'new_v_ssm_a_re', 'new_v_ssm_a_im', 'new_v_ssm_log_dt', 'new_v_ssm_b_re', 'new_v_ssm_b_im', 'new_v_ssm_c_re', 'new_v_ssm_c_im', 'new_v_ssm_d', 'new_v_glu_w', 'new_v_w_br', 'new_v_w_o']
TWIN_LEAF_KINDS = {'loss': 'loss', 'grad_x': 'grad_x', 'grad_c_ctx': 'grad_w', 'grad_w_mod': 'grad_w', 'grad_b_mod': 'grad_w', 'grad_g_pre': 'grad_w', 'grad_g_post': 'grad_w', 'grad_w_in': 'grad_w', 'grad_b_gate': 'grad_w', 'grad_na_rpb': 'grad_w', 'grad_pool_w': 'grad_w', 'grad_pool_scale': 'grad_w', 'grad_conv_w': 'grad_w', 'grad_ssm_a_re': 'grad_w', 'grad_ssm_a_im': 'grad_w', 'grad_ssm_log_dt': 'grad_w', 'grad_ssm_b_re': 'grad_w', 'grad_ssm_b_im': 'grad_w', 'grad_ssm_c_re': 'grad_w', 'grad_ssm_c_im': 'grad_w', 'grad_ssm_d': 'grad_w', 'grad_glu_w': 'grad_w', 'grad_w_br': 'grad_w', 'grad_w_o': 'grad_w', 'delta_c_ctx': 'delta_w', 'delta_w_mod': 'delta_w', 'delta_b_mod': 'delta_w', 'delta_g_pre': 'delta_w', 'delta_g_post': 'delta_w', 'delta_w_in': 'delta_w', 'delta_b_gate': 'delta_w', 'delta_na_rpb': 'delta_w', 'delta_pool_w': 'delta_w', 'delta_pool_scale': 'delta_w', 'delta_conv_w': 'delta_w', 'delta_ssm_a_re': 'delta_w', 'delta_ssm_a_im': 'delta_w', 'delta_ssm_log_dt': 'delta_w', 'delta_ssm_b_re': 'delta_w', 'delta_ssm_b_im': 'delta_w', 'delta_ssm_c_re': 'delta_w', 'delta_ssm_c_im': 'delta_w', 'delta_ssm_d': 'delta_w', 'delta_glu_w': 'delta_w', 'delta_w_br': 'delta_w', 'delta_w_o': 'delta_w', 'new_m_c_ctx': 'new_m', 'new_m_w_mod': 'new_m', 'new_m_b_mod': 'new_m', 'new_m_g_pre': 'new_m', 'new_m_g_post': 'new_m', 'new_m_w_in': 'new_m', 'new_m_b_gate': 'new_m', 'new_m_na_rpb': 'new_m', 'new_m_pool_w': 'new_m', 'new_m_pool_scale': 'new_m', 'new_m_conv_w': 'new_m', 'new_m_ssm_a_re': 'new_m', 'new_m_ssm_a_im': 'new_m', 'new_m_ssm_log_dt': 'new_m', 'new_m_ssm_b_re': 'new_m', 'new_m_ssm_b_im': 'new_m', 'new_m_ssm_c_re': 'new_m', 'new_m_ssm_c_im': 'new_m', 'new_m_ssm_d': 'new_m', 'new_m_glu_w': 'new_m', 'new_m_w_br': 'new_m', 'new_m_w_o': 'new_m', 'new_v_c_ctx': 'new_v', 'new_v_w_mod': 'new_v', 'new_v_b_mod': 'new_v', 'new_v_g_pre': 'new_v', 'new_v_g_post': 'new_v', 'new_v_w_in': 'new_v', 'new_v_b_gate': 'new_v', 'new_v_na_rpb': 'new_v', 'new_v_pool_w': 'new_v', 'new_v_pool_scale': 'new_v', 'new_v_conv_w': 'new_v', 'new_v_ssm_a_re': 'new_v', 'new_v_ssm_a_im': 'new_v', 'new_v_ssm_log_dt': 'new_v', 'new_v_ssm_b_re': 'new_v', 'new_v_ssm_b_im': 'new_v', 'new_v_ssm_c_re': 'new_v', 'new_v_ssm_c_im': 'new_v', 'new_v_ssm_d': 'new_v', 'new_v_glu_w': 'new_v', 'new_v_w_br': 'new_v', 'new_v_w_o': 'new_v'}


def _forward(args):
    return _fwd_reference(*[args[k] for k in FWD_PARAMS])


def _output_shape():
    def fwd():
        inp = _fwd_setup_inputs(0)
        return _fwd_reference(*[inp[k] for k in FWD_PARAMS])
    out = _jax.eval_shape(fwd)
    return out.shape, out.dtype

N_MICROBATCH = 1
ADAM_LR = 0.001
ADAM_B1 = 0.9
ADAM_B2 = 0.999
ADAM_EPS = 1e-08
ADAM_WD = 0.01
ADAM_STEP = 10
PER_EXAMPLE_BATCH_AXIS = {'x': 0, 'c': 0, 'ctx': 0, 'loss_target': 0}
SHARED_INPUTS = []
_WEIGHT_DTYPES = {'c_ctx': _jnp.float32, 'w_mod': _jnp.float32, 'b_mod': _jnp.float32, 'g_pre': _jnp.float32, 'g_post': _jnp.float32, 'w_in': _jnp.float32, 'b_gate': _jnp.float32, 'na_rpb': _jnp.float32, 'pool_w': _jnp.float32, 'pool_scale': _jnp.float32, 'conv_w': _jnp.float32, 'ssm_a_re': _jnp.float32, 'ssm_a_im': _jnp.float32, 'ssm_log_dt': _jnp.float32, 'ssm_b_re': _jnp.float32, 'ssm_b_im': _jnp.float32, 'ssm_c_re': _jnp.float32, 'ssm_c_im': _jnp.float32, 'ssm_d': _jnp.float32, 'glu_w': _jnp.float32, 'w_br': _jnp.float32, 'w_o': _jnp.float32}
MOMENT_SCALE = {'c_ctx': 1.628663e-02, 'w_mod': 5.881212e-01, 'b_mod': 1.264771e+00, 'g_pre': 9.123186e-02, 'g_post': 1.636250e+00, 'w_in': 3.570779e-02, 'b_gate': 1.057513e-02, 'na_rpb': 1.655851e-03, 'pool_w': 5.943623e-02, 'pool_scale': 6.299633e-02, 'conv_w': 7.828684e-02, 'ssm_a_re': 2.086536e-03, 'ssm_a_im': 2.280257e-03, 'ssm_log_dt': 9.787350e-01, 'ssm_b_re': 1.376980e-03, 'ssm_b_im': 1.349327e-03, 'ssm_c_re': 2.000597e-03, 'ssm_c_im': 1.950482e-03, 'ssm_d': 2.753345e-02, 'glu_w': 2.008461e-02, 'w_br': 2.686732e-02, 'w_o': 5.508062e-02}


def _to_microbatches(a, axis):
    t = _jnp.moveaxis(a, axis, 0)
    t = t.reshape((N_MICROBATCH, t.shape[0] // N_MICROBATCH) + t.shape[1:])
    return _jnp.moveaxis(t, 1, axis + 1)


def setup_inputs(seed: int = 0) -> dict:
    inp = _fwd_setup_inputs(seed)
    key = _jax.random.fold_in(_jax.random.key(seed), 7919)
    shape, _ = _output_shape()
    out = dict(inp)
    out["loss_target"] = _jax.random.normal(_jax.random.fold_in(key, 0), shape, _jnp.float32)
    for i, name in enumerate(TWIN_WEIGHTS):
        w = inp[name].astype(_jnp.float32)
        if MOMENT_SCALE is None:
            s = _jnp.sqrt(_jnp.mean(_jnp.square(w)) + 1e-30)
        else:
            s = MOMENT_SCALE[name]
        km, kv = _jax.random.split(_jax.random.fold_in(key, i + 1))
        out[name] = w
        out["m_" + name] = s * _jax.random.normal(km, w.shape, _jnp.float32)
        out["v_" + name] = (s * s) * _jax.random.uniform(kv, w.shape, _jnp.float32, 0.5, 1.5)
    if N_MICROBATCH > 1:
        for name, axis in PER_EXAMPLE_BATCH_AXIS.items():
            out[name] = _to_microbatches(out[name], axis)
    return {'x': out['x'], 'c': out['c'], 'ctx': out['ctx'], 'c_ctx': out['c_ctx'], 'w_mod': out['w_mod'], 'b_mod': out['b_mod'], 'g_pre': out['g_pre'], 'g_post': out['g_post'], 'w_in': out['w_in'], 'b_gate': out['b_gate'], 'na_rpb': out['na_rpb'], 'pool_w': out['pool_w'], 'pool_scale': out['pool_scale'], 'conv_w': out['conv_w'], 'ssm_a_re': out['ssm_a_re'], 'ssm_a_im': out['ssm_a_im'], 'ssm_log_dt': out['ssm_log_dt'], 'ssm_b_re': out['ssm_b_re'], 'ssm_b_im': out['ssm_b_im'], 'ssm_c_re': out['ssm_c_re'], 'ssm_c_im': out['ssm_c_im'], 'ssm_d': out['ssm_d'], 'glu_w': out['glu_w'], 'w_br': out['w_br'], 'w_o': out['w_o'], 'loss_target': out['loss_target'], 'm_c_ctx': out['m_c_ctx'], 'm_w_mod': out['m_w_mod'], 'm_b_mod': out['m_b_mod'], 'm_g_pre': out['m_g_pre'], 'm_g_post': out['m_g_post'], 'm_w_in': out['m_w_in'], 'm_b_gate': out['m_b_gate'], 'm_na_rpb': out['m_na_rpb'], 'm_pool_w': out['m_pool_w'], 'm_pool_scale': out['m_pool_scale'], 'm_conv_w': out['m_conv_w'], 'm_ssm_a_re': out['m_ssm_a_re'], 'm_ssm_a_im': out['m_ssm_a_im'], 'm_ssm_log_dt': out['m_ssm_log_dt'], 'm_ssm_b_re': out['m_ssm_b_re'], 'm_ssm_b_im': out['m_ssm_b_im'], 'm_ssm_c_re': out['m_ssm_c_re'], 'm_ssm_c_im': out['m_ssm_c_im'], 'm_ssm_d': out['m_ssm_d'], 'm_glu_w': out['m_glu_w'], 'm_w_br': out['m_w_br'], 'm_w_o': out['m_w_o'], 'v_c_ctx': out['v_c_ctx'], 'v_w_mod': out['v_w_mod'], 'v_b_mod': out['v_b_mod'], 'v_g_pre': out['v_g_pre'], 'v_g_post': out['v_g_post'], 'v_w_in': out['v_w_in'], 'v_b_gate': out['v_b_gate'], 'v_na_rpb': out['v_na_rpb'], 'v_pool_w': out['v_pool_w'], 'v_pool_scale': out['v_pool_scale'], 'v_conv_w': out['v_conv_w'], 'v_ssm_a_re': out['v_ssm_a_re'], 'v_ssm_a_im': out['v_ssm_a_im'], 'v_ssm_log_dt': out['v_ssm_log_dt'], 'v_ssm_b_re': out['v_ssm_b_re'], 'v_ssm_b_im': out['v_ssm_b_im'], 'v_ssm_c_re': out['v_ssm_c_re'], 'v_ssm_c_im': out['v_ssm_c_im'], 'v_ssm_d': out['v_ssm_d'], 'v_glu_w': out['v_glu_w'], 'v_w_br': out['v_w_br'], 'v_w_o': out['v_w_o']}


def _loss(weights, diff, rest, loss_target):
    with _jax.named_scope("forward"):
        args = {**rest, TWIN_DIFF_INPUT: diff, **{k: w.astype(_WEIGHT_DTYPES[k]) for k, w in weights.items()}}
        y = _forward(args)
    with _jax.named_scope("loss_head"):
        err = _jnp.square(y.astype(_jnp.float32) - loss_target)
        return 0.5 * _jnp.sum(_jnp.mean(err, axis=-1)) if err.ndim else 0.5 * err


def _adamw(w, g, m, v):
    m = ADAM_B1 * m + (1.0 - ADAM_B1) * g
    v = ADAM_B2 * v + (1.0 - ADAM_B2) * _jnp.square(g)
    m_hat = m / (1.0 - ADAM_B1 ** ADAM_STEP)
    v_hat = v / (1.0 - ADAM_B2 ** ADAM_STEP)
    delta = -ADAM_LR * (m_hat / (_jnp.sqrt(v_hat) + ADAM_EPS) + ADAM_WD * w)
    return delta, m, v


def reference(x, c, ctx, c_ctx, w_mod, b_mod, g_pre, g_post, w_in, b_gate, na_rpb, pool_w, pool_scale, conv_w, ssm_a_re, ssm_a_im, ssm_log_dt, ssm_b_re, ssm_b_im, ssm_c_re, ssm_c_im, ssm_d, glu_w, w_br, w_o, loss_target, m_c_ctx, m_w_mod, m_b_mod, m_g_pre, m_g_post, m_w_in, m_b_gate, m_na_rpb, m_pool_w, m_pool_scale, m_conv_w, m_ssm_a_re, m_ssm_a_im, m_ssm_log_dt, m_ssm_b_re, m_ssm_b_im, m_ssm_c_re, m_ssm_c_im, m_ssm_d, m_glu_w, m_w_br, m_w_o, v_c_ctx, v_w_mod, v_b_mod, v_g_pre, v_g_post, v_w_in, v_b_gate, v_na_rpb, v_pool_w, v_pool_scale, v_conv_w, v_ssm_a_re, v_ssm_a_im, v_ssm_log_dt, v_ssm_b_re, v_ssm_b_im, v_ssm_c_re, v_ssm_c_im, v_ssm_d, v_glu_w, v_w_br, v_w_o):
    given = dict(x=x, c=c, ctx=ctx, c_ctx=c_ctx, w_mod=w_mod, b_mod=b_mod, g_pre=g_pre, g_post=g_post, w_in=w_in, b_gate=b_gate, na_rpb=na_rpb, pool_w=pool_w, pool_scale=pool_scale, conv_w=conv_w, ssm_a_re=ssm_a_re, ssm_a_im=ssm_a_im, ssm_log_dt=ssm_log_dt, ssm_b_re=ssm_b_re, ssm_b_im=ssm_b_im, ssm_c_re=ssm_c_re, ssm_c_im=ssm_c_im, ssm_d=ssm_d, glu_w=glu_w, w_br=w_br, w_o=w_o, loss_target=loss_target, m_c_ctx=m_c_ctx, m_w_mod=m_w_mod, m_b_mod=m_b_mod, m_g_pre=m_g_pre, m_g_post=m_g_post, m_w_in=m_w_in, m_b_gate=m_b_gate, m_na_rpb=m_na_rpb, m_pool_w=m_pool_w, m_pool_scale=m_pool_scale, m_conv_w=m_conv_w, m_ssm_a_re=m_ssm_a_re, m_ssm_a_im=m_ssm_a_im, m_ssm_log_dt=m_ssm_log_dt, m_ssm_b_re=m_ssm_b_re, m_ssm_b_im=m_ssm_b_im, m_ssm_c_re=m_ssm_c_re, m_ssm_c_im=m_ssm_c_im, m_ssm_d=m_ssm_d, m_glu_w=m_glu_w, m_w_br=m_w_br, m_w_o=m_w_o, v_c_ctx=v_c_ctx, v_w_mod=v_w_mod, v_b_mod=v_b_mod, v_g_pre=v_g_pre, v_g_post=v_g_post, v_w_in=v_w_in, v_b_gate=v_b_gate, v_na_rpb=v_na_rpb, v_pool_w=v_pool_w, v_pool_scale=v_pool_scale, v_conv_w=v_conv_w, v_ssm_a_re=v_ssm_a_re, v_ssm_a_im=v_ssm_a_im, v_ssm_log_dt=v_ssm_log_dt, v_ssm_b_re=v_ssm_b_re, v_ssm_b_im=v_ssm_b_im, v_ssm_c_re=v_ssm_c_re, v_ssm_c_im=v_ssm_c_im, v_ssm_d=v_ssm_d, v_glu_w=v_glu_w, v_w_br=v_w_br, v_w_o=v_w_o)
    weights = {n: given[n] for n in TWIN_WEIGHTS}
    shared = {n: given[n] for n in SHARED_INPUTS}
    per_example = {n: given[n] for n in ['x', 'c', 'ctx']}
    grad_fn = _jax.value_and_grad(_loss, argnums=(0, 1))

    def one_microbatch(ex, loss_target):
        ex = dict(ex)
        diff = ex.pop(TWIN_DIFF_INPUT)
        return grad_fn(weights, diff, {**shared, **ex}, loss_target)

    if N_MICROBATCH == 1:
        loss, (grad_w, grad_x) = one_microbatch(per_example, given["loss_target"])
    else:
        def body(carry, xs):
            loss_sum, grad_sum = carry
            l_k, (gw_k, gx_k) = one_microbatch(xs[0], xs[1])
            with _jax.named_scope("update"):
                return (loss_sum + l_k, _jax.tree.map(_jnp.add, grad_sum, gw_k)), gx_k

        init = (_jnp.zeros((), _jnp.float32), _jax.tree.map(_jnp.zeros_like, weights))
        (loss, grad_w), grad_x = _jax.lax.scan(body, init, (per_example, given["loss_target"]))
    with _jax.named_scope("update"):
        delta_w, new_m, new_v = {}, {}, {}
        for n in TWIN_WEIGHTS:
            delta_w[n], new_m[n], new_v[n] = _adamw(weights[n], grad_w[n], given["m_" + n], given["v_" + n])
    return (loss, grad_x, *[grad_w[n] for n in TWIN_WEIGHTS], *[delta_w[n] for n in TWIN_WEIGHTS],
            *[new_m[n] for n in TWIN_WEIGHTS], *[new_v[n] for n in TWIN_WEIGHTS])
```

```python
import functools
import math

import numpy as np
import jax
import jax.numpy as jnp
from jax import lax
from jax.experimental import pallas as pl
from jax.experimental.pallas import tpu as pltpu

F32 = jnp.float32
BF16 = jnp.bfloat16
MM_DTYPE = jnp.bfloat16

D_MODEL = 2048
BRANCH = 512
N_HEADS = 8
HEAD_DIM = 64
GRID_W = 64
WIN_ROWS = 8
WIN_COLS = 16
POOL_GROUPS = 4
POOL_DIM = 128
SSM_GROUPS = 32
SSM_GDIM = 16
SSM_STATE = 64
N_STATE = SSM_GROUPS * SSM_STATE
IN_TOTAL = 14336
RMS_EPS = 1e-6
NEG_INF = -1e30
COL = dict(q=0, k=512, v=1024, na_z=1536, pool_u=2048, pool_z=2560, conv_x=3072, conv_b=3584,
           conv_c=4096, conv_z=4608, ssm_u=5120, ssm_z=5632, merge=6144)
N_SHARDS = 4
W_IN_SHARD = IN_TOTAL // N_SHARDS
VMEM_LIMIT_BYTES = 48 * 1024 * 1024
ROW_TILE = 256

ADAM_LR = 0.001
ADAM_B1 = 0.9
ADAM_B2 = 0.999
ADAM_EPS = 1e-08
ADAM_WD = 0.01
ADAM_STEP = 10


def _params(*sem):
    return pltpu.CompilerParams(dimension_semantics=sem, vmem_limit_bytes=VMEM_LIMIT_BYTES)


def _sigmoid(x):
    return 1.0 / (1.0 + jnp.exp(-x))


def _matmul(a, b, a_spec, b_spec, o_spec, out_shape, grid, *, ta=False, tb=False, name):
    nk = grid[-1]
    kaxis = len(grid) - 1
    dims = (((0,) if ta else (1,), (1,) if tb else (0,)), ((), ()))

    def body(a_ref, b_ref, o_ref, *scratch):
        p = lax.dot_general(a_ref[...].astype(MM_DTYPE), b_ref[...].astype(MM_DTYPE), dims,
                            preferred_element_type=F32)
        if nk == 1:
            o_ref[...] = p.astype(o_ref.dtype)
            return
        acc_ref, = scratch
        k = pl.program_id(kaxis)

        @pl.when(k == 0)
        def _():
            acc_ref[...] = p

        @pl.when(k > 0)
        def _():
            acc_ref[...] += p

        @pl.when(k == nk - 1)
        def _():
            o_ref[...] = acc_ref[...].astype(o_ref.dtype)

    oblock = tuple(s for s in o_spec.block_shape if s is not None)
    scratch = [] if nk == 1 else [pltpu.VMEM(oblock, F32)]
    sem = ("parallel",) * (len(grid) - 1) + ("arbitrary",)
    return pl.pallas_call(body, out_shape=out_shape, grid=grid, in_specs=[a_spec, b_spec],
                          out_specs=o_spec, scratch_shapes=scratch, compiler_params=_params(*sem),
                          name=name)(a, b)


def _pick(n, cands):
    for c in cands:
        if n % c == 0:
            return c
    raise ValueError(f"no tile for {n}")


def _row_tile(T):
    return _pick(T, (544, 512, 256, 128))


def mm_nn(a, b, *, out_dtype, name, tn=512, a_rows=None, o_rows=None, a_cols=None):
    M = a.shape[0]
    c0, K = a_cols or (0, a.shape[1])
    N = b.shape[1]
    tm = ROW_TILE if (a_rows or o_rows) else _row_tile(M)
    tn = min(tn, N)
    tk = K if K <= 2048 else _pick(K, (2048, 1024, 512))
    kb0 = c0 // tk
    ar = a_rows or (lambda i: i)
    orr = o_rows or (lambda i: i)
    return _matmul(a, b, pl.BlockSpec((tm, tk), lambda i, j, k: (ar(i), kb0 + k)),
                   pl.BlockSpec((tk, tn), lambda i, j, k: (k, j)),
                   pl.BlockSpec((tm, tn), lambda i, j, k: (orr(i), j)),
                   jax.ShapeDtypeStruct((M, N), out_dtype), (M // tm, N // tn, K // tk), name=name)


def mm_nt(a, b, *, out_dtype, name, a_rows=None, o_rows=None):
    M, K = a.shape
    N = b.shape[0]
    tm = ROW_TILE if (a_rows or o_rows) else _row_tile(M)
    tn = min(N, 2048)
    tk = K if K <= 1024 else _pick(K, (1024, 512))
    ar = a_rows or (lambda i: i)
    orr = o_rows or (lambda i: i)
    return _matmul(a, b, pl.BlockSpec((tm, tk), lambda i, j, k: (ar(i), k)),
                   pl.BlockSpec((tn, tk), lambda i, j, k: (j, k)),
                   pl.BlockSpec((tm, tn), lambda i, j, k: (orr(i), j)),
                   jax.ShapeDtypeStruct((M, N), out_dtype), (M // tm, N // tn, K // tk), tb=True, name=name)


def mm_tn(a, b, *, out_dtype, name, a_rows=None, b_rows=None, tm=512, tn=1024, a_cols=None):
    K = a.shape[0]
    c0, M = a_cols or (0, a.shape[1])
    N = b.shape[1]
    tk = ROW_TILE if (a_rows or b_rows) else _row_tile(K)
    tm = min(tm, M)
    tn = min(tn, N)
    mb0 = c0 // tm
    ar = a_rows or (lambda k: k)
    br = b_rows or (lambda k: k)
    return _matmul(a, b, pl.BlockSpec((tk, tm), lambda i, j, k: (ar(k), mb0 + i)),
                   pl.BlockSpec((tk, tn), lambda i, j, k: (br(k), j)),
                   pl.BlockSpec((tm, tn), lambda i, j, k: (i, j)),
                   jax.ShapeDtypeStruct((M, N), out_dtype), (M // tm, N // tn, K // tk), ta=True, name=name)


def _ew(fn, ins, outs, colsums, *, T, L, name):
    tb = ROW_TILE
    nlat = L // tb
    seg = lambda i: jnp.where(i >= nlat, 1, 0)
    in_specs, arrays = [], []
    for arr, kind, cb, width in ins:
        arrays.append(arr)
        if kind == "row":
            in_specs.append(pl.BlockSpec((tb, width), lambda i, cb=cb: (i, cb)))
        elif kind == "bcast":
            in_specs.append(pl.BlockSpec((1, width), lambda i, cb=cb: (0, cb)))
        else:
            in_specs.append(pl.BlockSpec((None, 1, width), lambda i, cb=cb: (seg(i), 0, cb)))
    out_specs = [pl.BlockSpec((tb, w), lambda i: (i, 0)) for w, _ in outs]
    out_shapes = [jax.ShapeDtypeStruct((T, w), dt) for w, dt in outs]
    out_specs += [pl.BlockSpec((None, 1, w), lambda i: (seg(i), 0, 0)) for w in colsums]
    out_shapes += [jax.ShapeDtypeStruct((2, 1, w), F32) for w in colsums]
    n_in, n_out = len(ins), len(outs)

    def body(*refs):
        i = pl.program_id(0)
        res = fn(*[r[...] for r in refs[:n_in]])
        for r, v in zip(refs[n_in:n_in + n_out], res[:n_out]):
            r[...] = v.astype(r.dtype)
        first = (i == 0) | (i == nlat)
        for r, v in zip(refs[n_in + n_out:], res[n_out:]):
            s = jnp.sum(v, axis=0, keepdims=True)

            @pl.when(first)
            def _(r=r, s=s):
                r[...] = s

            @pl.when(jnp.logical_not(first))
            def _(r=r, s=s):
                r[...] += s

    res = pl.pallas_call(body, out_shape=out_shapes, grid=(T // tb,), in_specs=in_specs,
                         out_specs=out_specs, compiler_params=_params("arbitrary"), name=name)(*arrays)
    return res


def _rms(x):
    return lax.rsqrt(jnp.mean(x * x, axis=-1, keepdims=True) + RMS_EPS)


def prenorm_fwd(X, g, scale, shift, *, L):
    T = X.shape[0]

    def fn(x, g, sc, sh):
        return ((x * _rms(x)) * (g * (1.0 + sc)) + sh,)

    h, = _ew(fn, [(X, "row", 0, D_MODEL), (g, "bcast", 0, D_MODEL), (scale, "seg", 0, D_MODEL),
                  (shift, "seg", 0, D_MODEL)], [(D_MODEL, MM_DTYPE)], [], T=T, L=L, name="prenorm_fwd")
    return h


def prenorm_bwd(dh, X, g, scale, dres, *, L):
    T = X.shape[0]

    def fn(dh, x, g, sc, dres):
        r = _rms(x)
        xn = x * r
        dxn = dh * (g * (1.0 + sc))
        dx = r * (dxn - xn * jnp.mean(dxn * xn, axis=-1, keepdims=True))
        return dres + dx, dh, dh * xn

    return _ew(fn, [(dh, "row", 0, D_MODEL), (X, "row", 0, D_MODEL), (g, "bcast", 0, D_MODEL),
                    (scale, "seg", 0, D_MODEL), (dres, "row", 0, D_MODEL)],
               [(D_MODEL, F32)], [D_MODEL, D_MODEL], T=T, L=L, name="prenorm_bwd")


def postnorm_fwd(X, y, g, gate, *, L):
    T = X.shape[0]

    def fn(x, y, g, gate):
        return (x + gate * ((y * _rms(y)) * g),)

    out, = _ew(fn, [(X, "row", 0, D_MODEL), (y, "row", 0, D_MODEL), (g, "bcast", 0, D_MODEL),
                    (gate, "seg", 0, D_MODEL)], [(D_MODEL, F32)], [], T=T, L=L, name="postnorm_fwd")
    return out


def postnorm_bwd(dX, y, g, gate, *, L):
    T = dX.shape[0]

    def fn(dx, y, g, gate):
        r = _rms(y)
        yn = y * r
        dyn = dx * (gate * g)
        dy = r * (dyn - yn * jnp.mean(dyn * yn, axis=-1, keepdims=True))
        return dy, dx * yn

    return _ew(fn, [(dX, "row", 0, D_MODEL), (y, "row", 0, D_MODEL), (g, "bcast", 0, D_MODEL),
                    (gate, "seg", 0, D_MODEL)], [(D_MODEL, MM_DTYPE)], [D_MODEL], T=T, L=L, name="postnorm_bwd")


def loss_and_grad(X, target, *, L):
    T = X.shape[0]
    tb = ROW_TILE
    nlat = L // tb

    def body(x_ref, t_ref, dx_ref, part_ref):
        i = pl.program_id(0)

        @pl.when(i < nlat)
        def _():
            err = x_ref[...] - t_ref[...]
            dx_ref[...] = err * (1.0 / D_MODEL)
            part_ref[...] = jnp.full(part_ref.shape, 0.5 / D_MODEL * jnp.sum(err * err), F32)

        @pl.when(i >= nlat)
        def _():
            dx_ref[...] = jnp.zeros(dx_ref.shape, F32)
            part_ref[...] = jnp.zeros(part_ref.shape, F32)

    dx, part = pl.pallas_call(
        body, out_shape=[jax.ShapeDtypeStruct((T, D_MODEL), F32), jax.ShapeDtypeStruct((T // tb, 8, 128), F32)],
        grid=(T // tb,),
        in_specs=[pl.BlockSpec((tb, D_MODEL), lambda i: (i, 0)),
                  pl.BlockSpec((tb, D_MODEL), lambda i: (jnp.minimum(i, nlat - 1), 0))],
        out_specs=[pl.BlockSpec((tb, D_MODEL), lambda i: (i, 0)), pl.BlockSpec((None, 8, 128), lambda i: (i, 0, 0))],
        compiler_params=_params("parallel"), name="loss_and_grad")(X, target)
    return jnp.sum(part[:, 0, 0]), dx


Q_BLOCK = WIN_ROWS * GRID_W
BAND = 2 * WIN_ROWS * GRID_W


def _bias_constants(rows):
    col = np.arange(GRID_W)
    col_start = np.clip(col - WIN_COLS // 2, 0, GRID_W - WIN_COLS)
    in_win = (col[None, :] >= col_start[:, None]) & (col[None, :] < col_start[:, None] + WIN_COLS)
    dcol = np.clip(col[None, :] - col[:, None] + (WIN_COLS - 1), 0, 2 * WIN_COLS - 2)
    E = np.zeros((2 * WIN_COLS - 1, GRID_W, GRID_W), np.float32)
    for dc in range(2 * WIN_COLS - 1):
        E[dc] = (dcol == dc) & in_win
    sel = np.zeros((3, WIN_ROWS, 2 * WIN_ROWS, 2 * WIN_ROWS - 1), np.float32)
    valid = np.zeros((3, WIN_ROWS, GRID_W, 2 * WIN_ROWS, GRID_W), bool)
    for v, r0 in enumerate((0, WIN_ROWS, rows - WIN_ROWS)):
        kstart = int(np.clip(r0 - WIN_ROWS // 2, 0, rows - 2 * WIN_ROWS))
        for a in range(WIN_ROWS):
            qr = r0 + a
            wstart = int(np.clip(qr - WIN_ROWS // 2, 0, rows - WIN_ROWS))
            for b in range(2 * WIN_ROWS):
                kr = kstart + b
                if wstart <= kr < wstart + WIN_ROWS:
                    sel[v, a, b, kr - qr + WIN_ROWS - 1] = 1.0
                    valid[v, a, :, b, :] = in_win
    return E, sel, valid.reshape(3, Q_BLOCK, BAND)


def build_bias(rpb, rows):
    E, sel, valid = _bias_constants(rows)
    tiles = jnp.einsum("hrd,dqk->hrqk", rpb, E, precision=lax.Precision.HIGHEST)
    b = jnp.einsum("vabr,hrqk->vhaqbk", sel, tiles, precision=lax.Precision.HIGHEST)
    b = b.reshape(3, N_HEADS, Q_BLOCK, BAND)
    return jnp.where(valid[:, None], b, NEG_INF)


def _band_start(i, rows):
    return pl.multiple_of(jnp.clip(WIN_ROWS * i - WIN_ROWS // 2, 0, rows - 2 * WIN_ROWS) * GRID_W, 256)


def _variant(i, nq):
    return jnp.where(i == 0, 0, jnp.where(i == nq - 1, 2, 1))


def _dot_nt(a, b):
    return lax.dot_general(a, b, (((1,), (1,)), ((), ())), preferred_element_type=F32)


def _dot_tn(a, b):
    return lax.dot_general(a, b, (((0,), (0,)), ((), ())), preferred_element_type=F32)


def _dot(a, b):
    return jnp.dot(a, b, preferred_element_type=F32)


def attn_fwd(q, k, v, bias, *, L, N):
    H, T, _ = q.shape
    rows = L // GRID_W
    nq = L // Q_BLOCK
    scale = HEAD_DIM ** -0.5

    def body(q_ref, k_ref, v_ref, b_ref, o_ref, lse_ref):
        ks = _band_start(pl.program_id(1), rows)
        qv = q_ref[...]
        kb, vb = k_ref[pl.ds(ks, BAND), :], v_ref[pl.ds(ks, BAND), :]
        kc, vc = k_ref[pl.ds(L, N), :], v_ref[pl.ds(L, N), :]
        sb = _dot_nt(qv, kb) * scale + b_ref[...]
        sc = _dot_nt(qv, kc) * scale
        m = jnp.maximum(jnp.max(sb, axis=-1, keepdims=True), jnp.max(sc, axis=-1, keepdims=True))
        pb, pc = jnp.exp(sb - m), jnp.exp(sc - m)
        l = jnp.sum(pb, axis=-1, keepdims=True) + jnp.sum(pc, axis=-1, keepdims=True)
        o = _dot(pb.astype(MM_DTYPE), vb) + _dot(pc.astype(MM_DTYPE), vc)
        o_ref[...] = o / l
        lse_ref[...] = m + jnp.log(l)

    return pl.pallas_call(
        body, out_shape=[jax.ShapeDtypeStruct((H, L, HEAD_DIM), F32), jax.ShapeDtypeStruct((H, L, 1), F32)],
        grid=(H, nq),
        in_specs=[pl.BlockSpec((None, Q_BLOCK, HEAD_DIM), lambda h, i: (h, i, 0)),
                  pl.BlockSpec((None, T, HEAD_DIM), lambda h, i: (h, 0, 0)),
                  pl.BlockSpec((None, T, HEAD_DIM), lambda h, i: (h, 0, 0)),
                  pl.BlockSpec((None, None, Q_BLOCK, BAND), lambda h, i: (_variant(i, nq), h, 0, 0))],
        out_specs=[pl.BlockSpec((None, Q_BLOCK, HEAD_DIM), lambda h, i: (h, i, 0)),
                   pl.BlockSpec((None, Q_BLOCK, 1), lambda h, i: (h, i, 0))],
        compiler_params=_params("parallel", "arbitrary"), name="attn_fwd")(q, k, v, bias)


def attn_bwd(q, k, v, bias, o, do, lse, *, L, N):
    H, T, _ = q.shape
    rows = L // GRID_W
    nq = L // Q_BLOCK
    scale = HEAD_DIM ** -0.5

    def body(q_ref, k_ref, v_ref, b_ref, o_ref, do_ref, lse_ref, dq_ref, dk_ref, dv_ref, db_ref):
        i = pl.program_id(1)
        ks = _band_start(i, rows)

        @pl.when(i == 0)
        def _():
            dk_ref[...] = jnp.zeros(dk_ref.shape, F32)
            dv_ref[...] = jnp.zeros(dv_ref.shape, F32)

        @pl.when((i == 0) | (i == 1) | (i == nq - 1))
        def _():
            db_ref[...] = jnp.zeros(db_ref.shape, F32)

        qv = q_ref[...]
        kb, vb = k_ref[pl.ds(ks, BAND), :], v_ref[pl.ds(ks, BAND), :]
        kc, vc = k_ref[pl.ds(L, N), :], v_ref[pl.ds(L, N), :]
        lse = lse_ref[...]
        pb = jnp.exp(_dot_nt(qv, kb) * scale + b_ref[...] - lse)
        pc = jnp.exp(_dot_nt(qv, kc) * scale - lse)
        do_f = do_ref[...]
        delta = jnp.sum(do_f * o_ref[...], axis=-1, keepdims=True)
        dov = do_f.astype(MM_DTYPE)
        dsb = pb * (_dot_nt(dov, vb) - delta)
        dsc = pc * (_dot_nt(dov, vc) - delta)
        db_ref[...] += dsb
        dsb_s, dsc_s = (dsb * scale).astype(MM_DTYPE), (dsc * scale).astype(MM_DTYPE)
        dq_ref[...] = _dot(dsb_s, kb) + _dot(dsc_s, kc)
        dk_ref[pl.ds(ks, BAND), :] += _dot_tn(dsb_s, qv)
        dk_ref[pl.ds(L, N), :] += _dot_tn(dsc_s, qv)
        dv_ref[pl.ds(ks, BAND), :] += _dot_tn(pb.astype(MM_DTYPE), dov)
        dv_ref[pl.ds(L, N), :] += _dot_tn(pc.astype(MM_DTYPE), dov)

    qspec = pl.BlockSpec((None, Q_BLOCK, HEAD_DIM), lambda h, i: (h, i, 0))
    kspec = pl.BlockSpec((None, T, HEAD_DIM), lambda h, i: (h, 0, 0))
    bspec = pl.BlockSpec((None, None, Q_BLOCK, BAND), lambda h, i: (_variant(i, nq), h, 0, 0))
    return pl.pallas_call(
        body,
        out_shape=[jax.ShapeDtypeStruct((H, L, HEAD_DIM), F32), jax.ShapeDtypeStruct((H, T, HEAD_DIM), F32),
                   jax.ShapeDtypeStruct((H, T, HEAD_DIM), F32), jax.ShapeDtypeStruct((3, H, Q_BLOCK, BAND), F32)],
        grid=(H, nq),
        in_specs=[qspec, kspec, kspec, bspec, qspec, qspec, pl.BlockSpec((None, Q_BLOCK, 1), lambda h, i: (h, i, 0))],
        out_specs=[qspec, kspec, kspec, bspec],
        compiler_params=_params("parallel", "arbitrary"), name="attn_bwd")(q, k, v, bias, o, do, lse)


def cattn_fwd(q, k, v, *, L, N):
    H = q.shape[0]
    scale = HEAD_DIM ** -0.5
    cspec = pl.BlockSpec((None, N, HEAD_DIM), lambda h: (h, L // N, 0))

    def body(q_ref, k_ref, v_ref, o_ref, lse_ref):
        s = _dot_nt(q_ref[...], k_ref[...]) * scale
        m = jnp.max(s, axis=-1, keepdims=True)
        p = jnp.exp(s - m)
        l = jnp.sum(p, axis=-1, keepdims=True)
        o_ref[...] = _dot(p.astype(MM_DTYPE), v_ref[...]) / l
        lse_ref[...] = m + jnp.log(l)

    return pl.pallas_call(
        body, out_shape=[jax.ShapeDtypeStruct((H, N, HEAD_DIM), F32), jax.ShapeDtypeStruct((H, N, 1), F32)],
        grid=(H,), in_specs=[cspec, cspec, cspec],
        out_specs=[pl.BlockSpec((None, N, HEAD_DIM), lambda h: (h, 0, 0)), pl.BlockSpec((None, N, 1), lambda h: (h, 0, 0))],
        compiler_params=_params("parallel"), name="cattn_fwd")(q, k, v)


def cattn_bwd(q, k, v, o, do, lse, *, L, N):
    H = q.shape[0]
    scale = HEAD_DIM ** -0.5
    cspec = pl.BlockSpec((None, N, HEAD_DIM), lambda h: (h, L // N, 0))
    ospec = pl.BlockSpec((None, N, HEAD_DIM), lambda h: (h, 0, 0))

    def body(q_ref, k_ref, v_ref, o_ref, do_ref, lse_ref, dq_ref, dk_ref, dv_ref):
        qv, kv, vv = q_ref[...], k_ref[...], v_ref[...]
        p = jnp.exp(_dot_nt(qv, kv) * scale - lse_ref[...])
        do_f = do_ref[...]
        delta = jnp.sum(do_f * o_ref[...], axis=-1, keepdims=True)
        dov = do_f.astype(MM_DTYPE)
        ds = (p * (_dot_nt(dov, vv) - delta) * scale).astype(MM_DTYPE)
        dq_ref[...] = _dot(ds, kv)
        dk_ref[...] = _dot_tn(ds, qv)
        dv_ref[...] = _dot_tn(p.astype(MM_DTYPE), dov)

    return pl.pallas_call(
        body, out_shape=[jax.ShapeDtypeStruct((H, N, HEAD_DIM), F32)] * 3, grid=(H,),
        in_specs=[cspec, cspec, cspec, ospec, ospec, pl.BlockSpec((None, N, 1), lambda h: (h, 0, 0))],
        out_specs=[ospec, ospec, ospec], compiler_params=_params("parallel"), name="cattn_bwd")(q, k, v, o, do, lse)


def _heads(a):
    T = a.shape[0]
    return a.reshape(T, N_HEADS, HEAD_DIM).transpose(1, 0, 2)


def _unheads(a):
    return a.transpose(1, 0, 2).reshape(a.shape[1], N_HEADS * HEAD_DIM)


PAD = 16


def _row_ids(T):
    return lax.broadcasted_iota(jnp.int32, (T, POOL_DIM), 0)


def _same_segment(t, s, L, T):
    return (s >= 0) & (s < T) & ((t < L) == (s < L))


def _window_sum(buf_ref, x, half, *, L, T, transpose):
    buf_ref[pl.ds(PAD, T), :] = x
    t = _row_ids(T)
    acc = jnp.zeros((T, POOL_DIM), F32)
    for j in range(-8, 9):
        inside = ((j > -half) & (j <= half)) if transpose else ((j >= -half) & (j < half))
        ok = _same_segment(t, t + j, L, T) & inside
        acc = acc + jnp.where(ok, buf_ref[pl.ds(PAD + j, T), :], 0.0)
    return acc


def _window_count(half, *, L, T):
    t = _row_ids(T)
    pos = jnp.where(t < L, t, t - L)
    seg_len = jnp.where(t < L, L, T - L)
    return (jnp.minimum(pos + half, seg_len) - jnp.maximum(pos - half, 0)).astype(F32)


def _zero_pads(buf_ref, T):
    buf_ref[pl.ds(0, PAD), :] = jnp.zeros((PAD, POOL_DIM), F32)
    buf_ref[pl.ds(PAD + T, PAD), :] = jnp.zeros((PAD, POOL_DIM), F32)


def pool_fwd(proj, pool_w, pool_scale, *, L):
    T = proj.shape[0]
    cb0 = COL["pool_u"] // POOL_DIM

    def body(u_ref, w_ref, s_ref, o_ref, p_ref, buf_ref):
        half = jnp.left_shift(1, pl.program_id(0))
        _zero_pads(buf_ref, T)
        u = u_ref[...]
        pooled = _window_sum(buf_ref, u, half, L=L, T=T, transpose=False) / _window_count(half, L=L, T=T) - u
        pm = pooled.astype(MM_DTYPE)
        p_ref[...] = pm
        o_ref[...] = _dot(pm, w_ref[...].astype(MM_DTYPE)) * s_ref[...]

    cspec = pl.BlockSpec((T, POOL_DIM), lambda g: (0, g))
    return pl.pallas_call(
        body, out_shape=[jax.ShapeDtypeStruct((T, BRANCH), F32), jax.ShapeDtypeStruct((T, BRANCH), MM_DTYPE)],
        grid=(POOL_GROUPS,),
        in_specs=[pl.BlockSpec((T, POOL_DIM), lambda g: (0, cb0 + g)),
                  pl.BlockSpec((None, POOL_DIM, POOL_DIM), lambda g: (g, 0, 0)),
                  pl.BlockSpec((1, POOL_DIM), lambda g: (0, g))],
        out_specs=[cspec, cspec], scratch_shapes=[pltpu.VMEM((T + 2 * PAD, POOL_DIM), F32)],
        compiler_params=_params("parallel"), name="pool_fwd")(proj, pool_w, pool_scale)


def pool_bwd(do, pooled, pool_w, pool_scale, *, L):
    T = do.shape[0]

    def body(do_ref, p_ref, w_ref, s_ref, du_ref, dw_ref, ds_ref, buf_ref):
        half = jnp.left_shift(1, pl.program_id(0))
        _zero_pads(buf_ref, T)
        pm = p_ref[...]
        w = w_ref[...].astype(MM_DTYPE)
        mixed = _dot(pm, w)
        dov = do_ref[...]
        ds_ref[...] = jnp.broadcast_to(jnp.sum(dov * mixed, axis=0, keepdims=True), ds_ref.shape)
        dmixed = (dov * s_ref[...]).astype(MM_DTYPE)
        dw_ref[...] = _dot_tn(pm, dmixed)
        dpooled = _dot_nt(dmixed, w)
        scaled = dpooled / _window_count(half, L=L, T=T)
        du = _window_sum(buf_ref, scaled, half, L=L, T=T, transpose=True) - dpooled
        du_ref[...] = du.astype(du_ref.dtype)

    cspec = pl.BlockSpec((T, POOL_DIM), lambda g: (0, g))
    return pl.pallas_call(
        body, out_shape=[jax.ShapeDtypeStruct((T, BRANCH), MM_DTYPE),
                         jax.ShapeDtypeStruct((POOL_GROUPS, POOL_DIM, POOL_DIM), F32),
                         jax.ShapeDtypeStruct((8, BRANCH), F32)],
        grid=(POOL_GROUPS,),
        in_specs=[cspec, cspec, pl.BlockSpec((None, POOL_DIM, POOL_DIM), lambda g: (g, 0, 0)),
                  pl.BlockSpec((1, POOL_DIM), lambda g: (0, g))],
        out_specs=[cspec, pl.BlockSpec((None, POOL_DIM, POOL_DIM), lambda g: (g, 0, 0)),
                   pl.BlockSpec((8, POOL_DIM), lambda g: (0, g))],
        scratch_shapes=[pltpu.VMEM((T + 2 * PAD, POOL_DIM), F32)],
        compiler_params=_params("parallel"), name="pool_bwd")(do, pooled, pool_w, pool_scale)


def _shifted(buf_ref, x, j, *, L, T):
    buf_ref[pl.ds(PAD, T), :] = x
    t = _row_ids(T)
    return jnp.where(_same_segment(t, t + j, L, T), buf_ref[pl.ds(PAD + j, T), :], 0.0)


def conv_fwd(proj, conv_w, *, L):
    T = proj.shape[0]
    nb = BRANCH // POOL_DIM
    cx, cbb, cc = (COL[n] // POOL_DIM for n in ("conv_x", "conv_b", "conv_c"))

    def body(x_ref, b_ref, c_ref, w_ref, o_ref, buf_ref):
        _zero_pads(buf_ref, T)
        xc = c_ref[...] * x_ref[...]
        w = w_ref[...]
        conv = (w[0:1] * _shifted(buf_ref, xc, -1, L=L, T=T) + w[1:2] * xc
                + w[2:3] * _shifted(buf_ref, xc, 1, L=L, T=T))
        o_ref[...] = b_ref[...] * conv

    return pl.pallas_call(
        body, out_shape=jax.ShapeDtypeStruct((T, BRANCH), F32), grid=(nb,),
        in_specs=[pl.BlockSpec((T, POOL_DIM), lambda g: (0, cx + g)), pl.BlockSpec((T, POOL_DIM), lambda g: (0, cbb + g)),
                  pl.BlockSpec((T, POOL_DIM), lambda g: (0, cc + g)), pl.BlockSpec((8, POOL_DIM), lambda g: (0, g))],
        out_specs=pl.BlockSpec((T, POOL_DIM), lambda g: (0, g)),
        scratch_shapes=[pltpu.VMEM((T + 2 * PAD, POOL_DIM), F32)],
        compiler_params=_params("parallel"), name="conv_fwd")(proj, proj, proj, conv_w)


def conv_bwd(do, proj, conv_w, *, L):
    T = proj.shape[0]
    nb = BRANCH // POOL_DIM
    cx, cbb, cc = (COL[n] // POOL_DIM for n in ("conv_x", "conv_b", "conv_c"))

    def body(do_ref, x_ref, b_ref, c_ref, w_ref, dx_ref, db_ref, dc_ref, dw_ref, buf_ref):
        _zero_pads(buf_ref, T)
        xv, gb, gc = x_ref[...], b_ref[...], c_ref[...]
        xc = gc * xv
        w = w_ref[...]
        xm = _shifted(buf_ref, xc, -1, L=L, T=T)
        xp = _shifted(buf_ref, xc, 1, L=L, T=T)
        conv = w[0:1] * xm + w[1:2] * xc + w[2:3] * xp
        dov = do_ref[...]
        db_ref[...] = (dov * conv).astype(db_ref.dtype)
        dconv = dov * gb
        sums = [jnp.sum(dconv * a, axis=0, keepdims=True) for a in (xm, xc, xp)]
        dw_ref[...] = jnp.concatenate(sums + [jnp.zeros((5, POOL_DIM), F32)], axis=0)
        dxc = (w[0:1] * _shifted(buf_ref, dconv, 1, L=L, T=T) + w[1:2] * dconv
               + w[2:3] * _shifted(buf_ref, dconv, -1, L=L, T=T))
        dc_ref[...] = (dxc * xv).astype(dc_ref.dtype)
        dx_ref[...] = (dxc * gc).astype(dx_ref.dtype)

    ospec = lambda off: pl.BlockSpec((T, POOL_DIM), lambda g: (0, off + g))
    return pl.pallas_call(
        body, out_shape=[jax.ShapeDtypeStruct((T, BRANCH), MM_DTYPE)] * 3 + [jax.ShapeDtypeStruct((8, BRANCH), F32)],
        grid=(nb,),
        in_specs=[ospec(0), ospec(cx), ospec(cbb), ospec(cc), pl.BlockSpec((8, POOL_DIM), lambda g: (0, g))],
        out_specs=[ospec(0), ospec(0), ospec(0), pl.BlockSpec((8, POOL_DIM), lambda g: (0, g))],
        scratch_shapes=[pltpu.VMEM((T + 2 * PAD, POOL_DIM), F32)],
        compiler_params=_params("parallel"), name="conv_bwd")(do, proj, proj, proj, conv_w)


SCAN_COLS = 1024
SCAN_ROWS = 256


def ssm_operators(a_re, a_im, log_dt, b_re, b_im, c_re, c_im):
    dt = jnp.exp(log_dt)[..., None]
    mag = jnp.exp(a_re * dt)
    abar_re, abar_im = mag * jnp.cos(a_im * dt), mag * jnp.sin(a_im * dt)
    den = a_re * a_re + a_im * a_im
    num_re, num_im = abar_re - 1.0, abar_im
    f_re = (num_re * a_re + num_im * a_im) / den
    f_im = (num_im * a_re - num_re * a_im) / den
    bbar_re = f_re[..., None] * b_re - f_im[..., None] * b_im
    bbar_im = f_re[..., None] * b_im + f_im[..., None] * b_re
    eye = jnp.eye(SSM_GROUPS, dtype=bool)

    def in_map(bbar):
        t = jnp.where(eye[None, :, None, :, None], bbar.transpose(0, 1, 3, 2)[:, :, :, None, :], 0.0)
        return t.reshape(2, BRANCH, N_STATE)

    def out_map(c):
        t = jnp.where(eye[None, :, None, :, None], c.transpose(0, 1, 3, 2)[:, :, :, None, :], 0.0)
        return t.reshape(2, N_STATE, BRANCH)

    abar = jnp.concatenate([abar_re.reshape(2, 1, N_STATE), abar_im.reshape(2, 1, N_STATE)], axis=-1)
    bcat = jnp.concatenate([in_map(bbar_re), in_map(bbar_im)], axis=-1)
    ccat = jnp.concatenate([out_map(c_re), -out_map(c_im)], axis=1)
    return abar, bcat, ccat


def _time_block(T, reverse):
    nt = T // SCAN_ROWS
    tix = (lambda i: nt - 1 - i) if reverse else (lambda i: i)
    return nt, pl.BlockSpec((SCAN_ROWS, 2 * N_STATE), lambda i: (tix(i), 0))


def ssm_scan(bu, abar, *, reverse):
    T = bu.shape[0]
    nt, tspec = _time_block(T, reverse)

    def body(b_ref, a_ref, s_ref, c_ref):
        @pl.when(pl.program_id(0) == 0)
        def _():
            c_ref[...] = jnp.zeros(c_ref.shape, F32)

        for c0 in range(0, N_STATE, SCAN_COLS):
            re, im = pl.ds(c0, SCAN_COLS), pl.ds(N_STATE + c0, SCAN_COLS)
            ar, ai = a_ref[:, re], a_ref[:, im]

            def step(n, carry, re=re, im=im, ar=ar, ai=ai):
                sr, si = carry
                t = (SCAN_ROWS - 1 - n) if reverse else n
                nr = ar * sr - ai * si + b_ref[pl.ds(t, 1), re]
                ni = ar * si + ai * sr + b_ref[pl.ds(t, 1), im]
                s_ref[pl.ds(t, 1), re] = nr
                s_ref[pl.ds(t, 1), im] = ni
                return nr, ni

            sr, si = lax.fori_loop(0, SCAN_ROWS, step, (c_ref[:, re], c_ref[:, im]))
            c_ref[:, re] = sr
            c_ref[:, im] = si

    return pl.pallas_call(
        body, out_shape=jax.ShapeDtypeStruct((T, 2 * N_STATE), F32), grid=(nt,),
        in_specs=[tspec, pl.BlockSpec((1, 2 * N_STATE), lambda i: (0, 0))], out_specs=tspec,
        scratch_shapes=[pltpu.VMEM((1, 2 * N_STATE), F32)],
        compiler_params=_params("arbitrary"), name="ssm_scan_rev" if reverse else "ssm_scan_fwd")(bu, abar)


def ssm_scan_bwd(g, s, abar, *, reverse):
    T = g.shape[0]
    nt, tspec = _time_block(T, not reverse)
    back = not reverse

    def body(g_ref, s_ref, a_ref, l_ref, da_ref, c_ref):
        @pl.when(pl.program_id(0) == 0)
        def _():
            c_ref[...] = jnp.zeros(c_ref.shape, F32)
            da_ref[...] = jnp.zeros(da_ref.shape, F32)

        for c0 in range(0, N_STATE, SCAN_COLS):
            re, im = pl.ds(c0, SCAN_COLS), pl.ds(N_STATE + c0, SCAN_COLS)
            ar, ai = a_ref[:, re], a_ref[:, im]

            def step(n, carry, re=re, im=im, ar=ar, ai=ai):
                lr, li, dr, di = carry
                t = (SCAN_ROWS - 1 - n) if back else n
                sr, si = s_ref[pl.ds(t, 1), re], s_ref[pl.ds(t, 1), im]
                dr = dr + sr * lr + si * li
                di = di + sr * li - si * lr
                nr = g_ref[pl.ds(t, 1), re] + ar * lr + ai * li
                ni = g_ref[pl.ds(t, 1), im] + ar * li - ai * lr
                l_ref[pl.ds(t, 1), re] = nr
                l_ref[pl.ds(t, 1), im] = ni
                return nr, ni, dr, di

            zero = jnp.zeros((1, SCAN_COLS), F32)
            lr, li, dr, di = lax.fori_loop(0, SCAN_ROWS, step, (c_ref[:, re], c_ref[:, im], zero, zero))
            c_ref[:, re] = lr
            c_ref[:, im] = li
            da_ref[:, re] += jnp.broadcast_to(dr, (8, SCAN_COLS))
            da_ref[:, im] += jnp.broadcast_to(di, (8, SCAN_COLS))

    return pl.pallas_call(
        body, out_shape=[jax.ShapeDtypeStruct((T, 2 * N_STATE), F32), jax.ShapeDtypeStruct((8, 2 * N_STATE), F32)],
        grid=(nt,), in_specs=[tspec, tspec, pl.BlockSpec((1, 2 * N_STATE), lambda i: (0, 0))],
        out_specs=[tspec, pl.BlockSpec((8, 2 * N_STATE), lambda i: (0, 0))],
        scratch_shapes=[pltpu.VMEM((1, 2 * N_STATE), F32)],
        compiler_params=_params("arbitrary"),
        name="ssm_scan_bwd_rev" if reverse else "ssm_scan_bwd_fwd")(g, s, abar)


def _gelu(x):
    return 0.5 * x * (1.0 + jnp.tanh(0.7978845608028654 * (x + 0.044715 * x * x * x)))


def _gelu_grad(x):
    t = jnp.tanh(0.7978845608028654 * (x + 0.044715 * x * x * x))
    return 0.5 * (1.0 + t) + 0.5 * x * (1.0 - t * t) * 0.7978845608028654 * (1.0 + 3 * 0.044715 * x * x)


def _silu(z):
    return z * _sigmoid(z)


def _silu_grad(z):
    s = _sigmoid(z)
    return s * (1.0 + z * (1.0 - s))


def ssm_fwd(proj, ops, dsk, glu_w, *, L):
    T = proj.shape[0]
    abar, bcat, ccat = ops
    nb, nlat = T // ROW_TILE, L // ROW_TILE
    to_f = lambda i: (i + nlat) % nb
    ucols = (COL["ssm_u"], BRANCH)
    states, ys = [], []
    for d in (0, 1):
        perm = to_f if d == 0 else None
        bu = mm_nn(proj, bcat[d], out_dtype=F32, name=f"ssm_in{d}", a_rows=perm, a_cols=ucols)
        s = ssm_scan(bu, abar[d], reverse=(d == 1))
        states.append(s)
        ys.append(mm_nn(s, ccat[d], out_dtype=F32, name=f"ssm_out{d}", o_rows=perm))

    def pre(u, yf, yr, dsk):
        y = dsk * u + yf + yr
        return y, _gelu(y)

    ypre, gy = _ew(pre, [(proj, "row", COL["ssm_u"] // BRANCH, BRANCH), (ys[0], "row", 0, BRANCH),
                         (ys[1], "row", 0, BRANCH), (dsk, "bcast", 0, BRANCH)],
                   [(BRANCH, F32), (BRANCH, MM_DTYPE)], [], T=T, L=L, name="ssm_pre")
    gg = mm_nn(gy, glu_w, out_dtype=F32, name="ssm_glu")

    def post(ga, gb):
        return (ga * _sigmoid(gb),)

    o, = _ew(post, [(gg, "row", 0, BRANCH), (gg, "row", 1, BRANCH)], [(BRANCH, F32)], [], T=T, L=L, name="ssm_post")
    return o, dict(states=states, ypre=ypre, gy=gy, gg=gg)


def ssm_bwd(do, proj, ops, dsk, glu_w, saved, *, L):
    T = proj.shape[0]
    abar, bcat, ccat = ops
    nb, nlat = T // ROW_TILE, L // ROW_TILE
    to_f = lambda i: (i + nlat) % nb
    ucols = (COL["ssm_u"], BRANCH)
    gg, gy, ypre = saved["gg"], saved["gy"], saved["ypre"]

    def post_bwd(do, ga, gb):
        sg = _sigmoid(gb)
        return (jnp.concatenate([do * sg, do * ga * sg * (1.0 - sg)], axis=1),)

    dgg, = _ew(post_bwd, [(do, "row", 0, BRANCH), (gg, "row", 0, BRANCH), (gg, "row", 1, BRANCH)],
               [(2 * BRANCH, MM_DTYPE)], [], T=T, L=L, name="ssm_post_bwd")
    dgy = mm_nt(dgg, glu_w, out_dtype=F32, name="ssm_glu_dx")
    dglu = mm_tn(gy, dgg, out_dtype=F32, name="ssm_glu_dw")

    def pre_bwd(dgy, y, u, dsk):
        dy = dgy * _gelu_grad(y)
        return dy, dy * dsk, dy * u

    dy, du_skip, dd = _ew(pre_bwd, [(dgy, "row", 0, BRANCH), (ypre, "row", 0, BRANCH),
                                    (proj, "row", COL["ssm_u"] // BRANCH, BRANCH), (dsk, "bcast", 0, BRANCH)],
                          [(BRANCH, MM_DTYPE), (BRANCH, F32)], [BRANCH], T=T, L=L, name="ssm_pre_bwd")
    du = du_skip
    dabar, dbcat, dccat = [], [], []
    for d in (0, 1):
        perm = to_f if d == 0 else None
        s = saved["states"][d]
        g = mm_nt(dy, ccat[d], out_dtype=F32, name=f"ssm_out{d}_dx", a_rows=perm)
        lam, da = ssm_scan_bwd(g, s, abar[d], reverse=(d == 1))
        dabar.append(da[0:1])
        dccat.append(mm_tn(s, dy, out_dtype=F32, name=f"ssm_out{d}_dw", b_rows=perm))
        du = du + mm_nt(lam, bcat[d], out_dtype=F32, name=f"ssm_in{d}_dx", o_rows=perm)
        dbcat.append(mm_tn(proj, lam, out_dtype=F32, name=f"ssm_in{d}_dw", a_rows=perm, a_cols=ucols))
    d_ops = (jnp.stack(dabar), jnp.stack(dbcat), jnp.stack(dccat))
    return du, d_ops, dd[0, 0] + dd[1, 0], dglu


Z_COLS = tuple(COL[n] // BRANCH for n in ("na_z", "pool_z", "conv_z", "ssm_z"))


def gate_act(o, proj, *, L):
    T = o.shape[0]

    def fn(o, z0, z1, z2, z3):
        return (o * _silu(jnp.concatenate([z0, z1, z2, z3], axis=1)),)

    a, = _ew(fn, [(o, "row", 0, D_MODEL)] + [(proj, "row", c, BRANCH) for c in Z_COLS],
             [(D_MODEL, MM_DTYPE)], [], T=T, L=L, name="gate_act")
    return a


def gate_act_bwd(da, o, proj, *, L):
    T = o.shape[0]

    def fn(da, o, z0, z1, z2, z3):
        z = jnp.concatenate([z0, z1, z2, z3], axis=1)
        return da * _silu(z), da * o * _silu_grad(z)

    return _ew(fn, [(da, "row", 0, D_MODEL), (o, "row", 0, D_MODEL)] + [(proj, "row", c, BRANCH) for c in Z_COLS],
               [(D_MODEL, F32), (D_MODEL, MM_DTYPE)], [], T=T, L=L, name="gate_act_bwd")


MERGE_TN = 512


def merge_fwd(a, w_br, proj, b_gate):
    T = a.shape[0]
    tm, tn = _row_tile(T), MERGE_TN
    nn = D_MODEL // tn
    lb0 = COL["merge"] // tn

    def body(a_ref, w_ref, l_ref, b_ref, m_ref, br_ref, acc_ref):
        i = pl.program_id(2)
        br = _dot(a_ref[...].astype(MM_DTYPE), w_ref[...].astype(MM_DTYPE))
        br_ref[...] = br
        term = _sigmoid(l_ref[...] + b_ref[...]) * br

        @pl.when(i == 0)
        def _():
            acc_ref[...] = term

        @pl.when(i > 0)
        def _():
            acc_ref[...] += term

        @pl.when(i == 3)
        def _():
            m_ref[...] = acc_ref[...].astype(m_ref.dtype)

    return pl.pallas_call(
        body, out_shape=[jax.ShapeDtypeStruct((T, D_MODEL), MM_DTYPE), jax.ShapeDtypeStruct((T, 4 * D_MODEL), F32)],
        grid=(T // tm, nn, 4),
        in_specs=[pl.BlockSpec((tm, BRANCH), lambda m, n, i: (m, i)),
                  pl.BlockSpec((BRANCH, tn), lambda m, n, i: (i, n)),
                  pl.BlockSpec((tm, tn), lambda m, n, i: (m, lb0 + i * nn + n)),
                  pl.BlockSpec((1, tn), lambda m, n, i: (0, i * nn + n))],
        out_specs=[pl.BlockSpec((tm, tn), lambda m, n, i: (m, n)), pl.BlockSpec((tm, tn), lambda m, n, i: (m, i * nn + n))],
        scratch_shapes=[pltpu.VMEM((tm, tn), F32)],
        compiler_params=_params("parallel", "parallel", "arbitrary"), name="merge_fwd")(a, w_br, proj, b_gate)


def merge_bwd(dmerged, br, proj, b_gate):
    T = dmerged.shape[0]
    tb = ROW_TILE
    lb0 = COL["merge"] // D_MODEL

    def body(dm_ref, br_ref, l_ref, b_ref, dbr_ref, dl_ref, db_ref):
        dm = dm_ref[...]
        gates = _sigmoid(l_ref[...] + b_ref[...])
        dbr_ref[...] = (dm * gates).astype(dbr_ref.dtype)
        dl = dm * br_ref[...] * gates * (1.0 - gates)
        dl_ref[...] = dl.astype(dl_ref.dtype)
        s = jnp.broadcast_to(jnp.sum(dl, axis=0, keepdims=True), db_ref.shape)

        @pl.when(pl.program_id(1) == 0)
        def _():
            db_ref[...] = s

        @pl.when(pl.program_id(1) > 0)
        def _():
            db_ref[...] += s

    wide = pl.BlockSpec((tb, D_MODEL), lambda b, i: (i, b))
    return pl.pallas_call(
        body, out_shape=[jax.ShapeDtypeStruct((T, 4 * D_MODEL), MM_DTYPE)] * 2 + [jax.ShapeDtypeStruct((8, 4 * D_MODEL), F32)],
        grid=(4, T // tb),
        in_specs=[pl.BlockSpec((tb, D_MODEL), lambda b, i: (i, 0)), wide,
                  pl.BlockSpec((tb, D_MODEL), lambda b, i: (i, lb0 + b)), pl.BlockSpec((1, D_MODEL), lambda b, i: (0, b))],
        out_specs=[wide, wide, pl.BlockSpec((8, D_MODEL), lambda b, i: (0, b))],
        compiler_params=_params("parallel", "arbitrary"), name="merge_bwd")(dmerged, br, proj, b_gate)


def branch_dx(dbr, w_br):
    T = dbr.shape[0]
    tm, tk = _row_tile(T), 1024
    nk = D_MODEL // tk
    return _matmul(dbr, w_br, pl.BlockSpec((tm, tk), lambda m, i, k: (m, i * nk + k)),
                   pl.BlockSpec((BRANCH, tk), lambda m, i, k: (i, k)),
                   pl.BlockSpec((tm, BRANCH), lambda m, i, k: (m, i)),
                   jax.ShapeDtypeStruct((T, D_MODEL), F32), (T // tm, 4, nk), tb=True, name="branch_dx")


def branch_dw(a, dbr):
    T = a.shape[0]
    tk, tn = _row_tile(T), 1024
    nn = D_MODEL // tn
    return _matmul(a, dbr, pl.BlockSpec((tk, BRANCH), lambda i, n, k: (k, i)),
                   pl.BlockSpec((tk, tn), lambda i, n, k: (k, i * nn + n)),
                   pl.BlockSpec((BRANCH, tn), lambda i, n, k: (i, n)),
                   jax.ShapeDtypeStruct((D_MODEL, D_MODEL), F32), (4, nn, T // tk), ta=True, name="branch_dw")


def proj_fwd(h, w_in):
    T = h.shape[0]
    tm, tn = _row_tile(T), 512
    per = W_IN_SHARD // tn
    return _matmul(h, w_in, pl.BlockSpec((tm, D_MODEL), lambda i, j, k: (i, 0)),
                   pl.BlockSpec((None, D_MODEL, tn), lambda i, j, k: (j // per, 0, j % per)),
                   pl.BlockSpec((tm, tn), lambda i, j, k: (i, j)),
                   jax.ShapeDtypeStruct((T, IN_TOTAL), F32), (T // tm, IN_TOTAL // tn, 1), name="proj_fwd")


def proj_dx(dproj, w_in):
    T = dproj.shape[0]
    tm, tk = _row_tile(T), 512
    per = W_IN_SHARD // tk
    return _matmul(dproj, w_in, pl.BlockSpec((tm, tk), lambda i, j, k: (i, k)),
                   pl.BlockSpec((None, D_MODEL, tk), lambda i, j, k: (k // per, 0, k % per)),
                   pl.BlockSpec((tm, D_MODEL), lambda i, j, k: (i, 0)),
                   jax.ShapeDtypeStruct((T, D_MODEL), F32), (T // tm, 1, IN_TOTAL // tk), tb=True, name="proj_dx")


def proj_dw(h, dproj):
    T = h.shape[0]
    tk, tm, tn = _row_tile(T), 1024, 1792
    per = W_IN_SHARD // tn
    return _matmul(h, dproj, pl.BlockSpec((tk, tm), lambda i, j, k: (k, i)),
                   pl.BlockSpec((tk, tn), lambda i, j, k: (k, j)),
                   pl.BlockSpec((None, tm, tn), lambda i, j, k: (j // per, i, j % per)),
                   jax.ShapeDtypeStruct((N_SHARDS, D_MODEL, W_IN_SHARD), F32),
                   (D_MODEL // tm, IN_TOTAL // tn, T // tk), ta=True, name="proj_dw")


def layer_fwd(X, P, *, L):
    T = X.shape[0]
    N = T - L
    h = prenorm_fwd(X, P["g_pre"], P["scale"], P["shift"], L=L)
    proj = proj_fwd(h, P["w_in"])
    q, k, v = (_heads(proj[:, COL[n]:COL[n] + BRANCH]).astype(MM_DTYPE) for n in ("q", "k", "v"))
    o_lat, lse = attn_fwd(q, k, v, P["bias"], L=L, N=N)
    o_ctx, lse_c = cattn_fwd(q, k, v, L=L, N=N)
    o_na = jnp.concatenate([o_lat, o_ctx], axis=1)
    o_pool, pooled = pool_fwd(proj, P["pool_w"], P["pool_scale"], L=L)
    o_conv = conv_fwd(proj, P["conv_w"], L=L)
    o_ssm, ssm_saved = ssm_fwd(proj, P["ssm_ops"], P["ssm_d"], P["glu_w"], L=L)
    o = jnp.concatenate([_unheads(o_na), o_pool, o_conv, o_ssm], axis=1)
    a = gate_act(o, proj, L=L)
    merged, br = merge_fwd(a, P["w_br"], proj, P["b_gate"])
    y = mm_nn(merged, P["w_o"], out_dtype=F32, name="out_proj")
    Xn = postnorm_fwd(X, y, P["g_post"], P["gate"], L=L)
    saved = dict(X=X, h=h, proj=proj, q=q, k=k, v=v, o_na=o_na, lse=lse, lse_c=lse_c, pooled=pooled,
                 ssm=ssm_saved, o=o, a=a, merged=merged, br=br, y=y)
    return Xn, saved


def layer_bwd(dXn, P, S, bias_vjp, ops_vjp, *, L):
    T = dXn.shape[0]
    N = T - L
    proj = S["proj"]
    dy, cs_post = postnorm_bwd(dXn, S["y"], P["g_post"], P["gate"], L=L)
    dmerged = mm_nt(dy, P["w_o"], out_dtype=F32, name="out_proj_dx")
    d_w_o = mm_tn(S["merged"], dy, out_dtype=F32, name="out_proj_dw", tm=1024, tn=1024)
    dbr, dlogit, d_bgate = merge_bwd(dmerged, S["br"], proj, P["b_gate"])
    da = branch_dx(dbr, P["w_br"])
    d_w_br = branch_dw(S["a"], dbr)
    do, dz = gate_act_bwd(da, S["o"], proj, L=L)
    do_na = _heads(do[:, 0:BRANCH])
    dq_lat, dk, dv, dbias = attn_bwd(S["q"], S["k"], S["v"], P["bias"], S["o_na"][:, :L], do_na[:, :L], S["lse"], L=L, N=N)
    dq_c, dk_c, dv_c = cattn_bwd(S["q"], S["k"], S["v"], S["o_na"][:, L:], do_na[:, L:], S["lse_c"], L=L, N=N)
    dq = jnp.concatenate([dq_lat, dq_c], axis=1)
    dk = dk.at[:, L:].add(dk_c)
    dv = dv.at[:, L:].add(dv_c)
    d_rpb, = bias_vjp(dbias)
    dpool_u, d_pool_w, d_pool_scale = pool_bwd(do[:, BRANCH:2 * BRANCH], S["pooled"], P["pool_w"], P["pool_scale"], L=L)
    dcx, dcb, dcc, d_conv_w = conv_bwd(do[:, 2 * BRANCH:3 * BRANCH], proj, P["conv_w"], L=L)
    dssm_u, d_ops, d_ssm_d, d_glu = ssm_bwd(do[:, 3 * BRANCH:], proj, P["ssm_ops"], P["ssm_d"], P["glu_w"], S["ssm"], L=L)
    d_ssm = ops_vjp(d_ops)
    z = lambda i: dz[:, i * BRANCH:(i + 1) * BRANCH]
    cast = lambda t: t.astype(MM_DTYPE)
    dproj = jnp.concatenate([cast(_unheads(dq)), cast(_unheads(dk)), cast(_unheads(dv)), z(0), dpool_u, z(1),
                             dcx, dcb, dcc, z(2), cast(dssm_u), z(3), dlogit], axis=1)
    dh = proj_dx(dproj, P["w_in"])
    d_w_in = proj_dw(S["h"], dproj)
    dX, cs_h, cs_hx = prenorm_bwd(dh, S["X"], P["g_pre"], P["scale"], dXn, L=L)
    g_pre, g_post = P["g_pre"], P["g_post"]
    d_shift = cs_h
    d_scale = cs_hx * g_pre
    d_gate = cs_post * g_post
    d_g_pre = jnp.sum(cs_hx * (1.0 + P["scale"]), axis=0)[0]
    d_g_post = jnp.sum(cs_post * P["gate"], axis=0)[0]
    grads = dict(w_in=d_w_in, w_br=d_w_br, w_o=d_w_o, glu_w=d_glu, conv_w=d_conv_w[0:3], pool_w=d_pool_w,
                 pool_scale=d_pool_scale[0], b_gate=d_bgate[0], na_rpb=d_rpb, ssm=d_ssm, ssm_d=d_ssm_d,
                 g_pre=d_g_pre, g_post=d_g_post,
                 mod=jnp.concatenate([d_shift, d_scale, d_gate], axis=-1)[:, 0])
    return dX, grads


def local_step(x, ctx, target, mods, layers):
    L = x.shape[0]
    rows = L // GRID_W
    X = jnp.concatenate([x, ctx], axis=0)
    saved, Ps, vjps = [], [], []
    for i, lay in enumerate(layers):
        P = dict(lay)
        m = mods[i][:, None, :]
        P["shift"], P["scale"], P["gate"] = m[..., :D_MODEL], m[..., D_MODEL:2 * D_MODEL], m[..., 2 * D_MODEL:]
        P["bias"], bias_vjp = jax.vjp(lambda r: build_bias(r, rows), lay["na_rpb"])
        P["ssm_ops"], ops_vjp = jax.vjp(ssm_operators, *lay["ssm_params"])
        X, S = layer_fwd(X, P, L=L)
        saved.append(S)
        Ps.append(P)
        vjps.append((bias_vjp, ops_vjp))
    loss, dX = loss_and_grad(X, target, L=L)
    grads = [None] * len(layers)
    for i in reversed(range(len(layers))):
        dX, grads[i] = layer_bwd(dX, Ps[i], saved[i], *vjps[i], L=L)
    return loss, dX[:L], grads


MESH_ID = pl.DeviceIdType.MESH
HBM_SPEC = pl.BlockSpec(memory_space=pltpu.HBM)


def _place():
    return lax.axis_index("x"), lax.axis_index("y"), lax.axis_index("c")


def _other_chips(x, y):
    return [(1 - x, y), (x, 1 - y), (1 - x, 1 - y)]


def _remote(src, dst, send_sem, recv_sem, to):
    return pltpu.make_async_remote_copy(src_ref=src, dst_ref=dst, send_sem=send_sem, recv_sem=recv_sem,
                                        device_id=to, device_id_type=MESH_ID)


def all_gather8(blocks, name):
    K = len(blocks)

    def body(*refs):
        ins, outs = refs[:K], refs[K:2 * K]
        send_sems, recv_sems, local_sems = refs[2 * K:]
        x, y, c = _place()
        me, sibling = (x, y, c), (x, y, 1 - c)
        chips = _other_chips(x, y)

        def copy(k, j, block, to, src=None):
            px, py, pc = block
            slot = outs[k].at[4 * px + 2 * py + pc]
            return _remote(slot if src is None else src, slot, send_sems.at[k, j], recv_sems.at[k, j], to)

        mine = [pltpu.make_async_copy(ins[k], outs[k].at[4 * x + 2 * y + c], local_sems.at[k]) for k in range(K)]
        for cp in mine:
            cp.start()
        first = []
        for k in range(K):
            first.append(copy(k, 0, me, sibling, src=ins[k]))
            first += [copy(k, 1 + j, me, (*chip, c), src=ins[k]) for j, chip in enumerate(chips)]
        for cp in first:
            cp.start()
        passed = []
        for j, chip in enumerate(chips):
            for k in range(K):
                copy(k, 1 + j, (*chip, c), me).wait_recv()
                fwd = copy(k, 4 + j, (*chip, c), sibling)
                fwd.start()
                passed.append(fwd)
        for k in range(K):
            copy(k, 0, sibling, me).wait_recv()
            for j, chip in enumerate(chips):
                copy(k, 4 + j, (*chip, 1 - c), me).wait_recv()
        for cp in first + passed:
            cp.wait_send()
        for cp in mine:
            cp.wait()

    return pl.pallas_call(
        body, out_shape=[jax.ShapeDtypeStruct((8,) + b.shape, b.dtype) for b in blocks],
        in_specs=[HBM_SPEC] * K, out_specs=[HBM_SPEC] * K,
        scratch_shapes=[pltpu.SemaphoreType.DMA((K, 7)), pltpu.SemaphoreType.DMA((K, 7)), pltpu.SemaphoreType.DMA((K,))],
        name=name)(*blocks)


def sibling_split(grads, name):
    K = len(grads)

    def body(*refs):
        ins, mine, theirs = refs[:K], refs[K:2 * K], refs[2 * K:3 * K]
        send_sems, recv_sems, local_sems = refs[3 * K:]
        x, y, c = _place()
        remote, local = [], []
        for k in range(K):
            half = ins[k].shape[1] // 2
            keep = ins[k].at[:, pl.ds(pl.multiple_of(c * half, 8), half), :]
            give = ins[k].at[:, pl.ds(pl.multiple_of((1 - c) * half, 8), half), :]
            local.append(pltpu.make_async_copy(keep, mine[k], local_sems.at[k]))
            remote.append(_remote(give, theirs[k], send_sems.at[k], recv_sems.at[k], (x, y, 1 - c)))
        for cp in local + remote:
            cp.start()
        for cp in remote + local:
            cp.wait()

    shapes = [jax.ShapeDtypeStruct((g.shape[0], g.shape[1] // 2, g.shape[2]), g.dtype) for g in grads]
    res = pl.pallas_call(
        body, out_shape=shapes + shapes, in_specs=[HBM_SPEC] * K, out_specs=[HBM_SPEC] * (2 * K),
        scratch_shapes=[pltpu.SemaphoreType.DMA((K,)), pltpu.SemaphoreType.DMA((K,)), pltpu.SemaphoreType.DMA((K,))],
        name=name)(*grads)
    return res[:K], res[K:]


def chip_exchange(parts, name):
    K = len(parts)

    def body(*refs):
        ins, outs = refs[:K], refs[K:2 * K]
        send_sems, recv_sems, local_sems = refs[2 * K:]
        x, y, c = _place()
        p = 2 * x + y
        chips = _other_chips(x, y)
        local = [pltpu.make_async_copy(ins[k].at[p], outs[k].at[p], local_sems.at[k]) for k in range(K)]
        sends, recvs = [], []
        for k in range(K):
            for j, (cx, cy) in enumerate(chips):
                q = 2 * cx + cy
                sends.append(_remote(ins[k].at[q], outs[k].at[p], send_sems.at[k, j], recv_sems.at[k, j], (cx, cy, c)))
                recvs.append(_remote(ins[k].at[q], outs[k].at[q], send_sems.at[k, j], recv_sems.at[k, j], (cx, cy, c)))
        for cp in local + sends:
            cp.start()
        for cp in recvs:
            cp.wait_recv()
        for cp in sends:
            cp.wait_send()
        for cp in local:
            cp.wait()

    return pl.pallas_call(
        body, out_shape=[jax.ShapeDtypeStruct(a.shape, a.dtype) for a in parts],
        in_specs=[HBM_SPEC] * K, out_specs=[HBM_SPEC] * K,
        scratch_shapes=[pltpu.SemaphoreType.DMA((K, 3)), pltpu.SemaphoreType.DMA((K, 3)), pltpu.SemaphoreType.DMA((K,))],
        name=name)(*parts)


def sibling_join(halves, name):
    K = len(halves)

    def body(*refs):
        ins, outs = refs[:K], refs[K:2 * K]
        send_sems, recv_sems, local_sems = refs[2 * K:]
        x, y, c = _place()
        local = [pltpu.make_async_copy(ins[k], outs[k].at[c], local_sems.at[k]) for k in range(K)]
        sends = [_remote(ins[k], outs[k].at[c], send_sems.at[k], recv_sems.at[k], (x, y, 1 - c)) for k in range(K)]
        recvs = [_remote(ins[k], outs[k].at[1 - c], send_sems.at[k], recv_sems.at[k], (x, y, 1 - c)) for k in range(K)]
        for cp in local + sends:
            cp.start()
        for cp in recvs:
            cp.wait_recv()
        for cp in sends:
            cp.wait_send()
        for cp in local:
            cp.wait()

    return pl.pallas_call(
        body, out_shape=[jax.ShapeDtypeStruct((2,) + a.shape, a.dtype) for a in halves],
        in_specs=[HBM_SPEC] * K, out_specs=[HBM_SPEC] * K,
        scratch_shapes=[pltpu.SemaphoreType.DMA((K,)), pltpu.SemaphoreType.DMA((K,)), pltpu.SemaphoreType.DMA((K,))],
        name=name)(*halves)


def _slab_rows(n):
    return max(8, min(256, (1 << 18) // n // 8 * 8))


def add_pair(a, b, *, out_dtype, name):
    S, h, n = a.shape
    tr = _pick(h, tuple(t for t in (256, 128, 64, 32, 16, 8) if t <= _slab_rows(n)))
    spec = pl.BlockSpec((None, tr, n), lambda s, r: (s, r, 0))

    def body(a_ref, b_ref, o_ref):
        o_ref[...] = (a_ref[...] + b_ref[...]).astype(o_ref.dtype)

    return pl.pallas_call(body, out_shape=jax.ShapeDtypeStruct(a.shape, out_dtype), grid=(S, h // tr),
                          in_specs=[spec, spec], out_specs=spec, compiler_params=_params("parallel", "parallel"),
                          name=name)(a, b)


def sum_slabs(a, *, name):
    S, h, n = a.shape
    tr = _pick(h, tuple(t for t in (256, 128, 64, 32, 16, 8) if t <= _slab_rows(n)))

    def body(a_ref, o_ref):
        acc = a_ref[0].astype(F32)
        for s in range(1, S):
            acc = acc + a_ref[s].astype(F32)
        o_ref[...] = acc

    return pl.pallas_call(body, out_shape=jax.ShapeDtypeStruct((h, n), F32), grid=(h // tr,),
                          in_specs=[pl.BlockSpec((S, tr, n), lambda r: (0, r, 0))],
                          out_specs=pl.BlockSpec((tr, n), lambda r: (r, 0)), compiler_params=_params("parallel"),
                          name=name)(a)


def _adam_math(w, g, m, v):
    m = ADAM_B1 * m + (1.0 - ADAM_B1) * g
    v = ADAM_B2 * v + (1.0 - ADAM_B2) * (g * g)
    m_hat = m / (1.0 - ADAM_B1 ** ADAM_STEP)
    v_hat = v / (1.0 - ADAM_B2 ** ADAM_STEP)
    delta = -ADAM_LR * (m_hat / (jnp.sqrt(v_hat) + ADAM_EPS) + ADAM_WD * w)
    return delta, m, v


def adamw(w, g, m, v, *, name):
    R, n = w.shape
    tr = _pick(R, tuple(t for t in (256, 128, 64, 32, 16, 8) if t <= _slab_rows(n)))
    spec = pl.BlockSpec((tr, n), lambda r: (r, 0))

    def body(w_ref, g_ref, m_ref, v_ref, d_ref, nm_ref, nv_ref):
        d, nm, nv = _adam_math(w_ref[...], g_ref[...], m_ref[...], v_ref[...])
        d_ref[...] = d
        nm_ref[...] = nm
        nv_ref[...] = nv

    return pl.pallas_call(body, out_shape=[jax.ShapeDtypeStruct(w.shape, F32)] * 3, grid=(R // tr,),
                          in_specs=[spec] * 4, out_specs=[spec] * 3, compiler_params=_params("parallel"),
                          name=name)(w, g, m, v)


MOD_ROWS = 16


def mod_fwd(cact_in, w_mod, b_mod):
    nl, _, cols = w_mod.shape
    tn = 512

    def body(c_ref, w_ref, b_ref, o_ref):
        o_ref[...] = _dot(_silu(c_ref[...]).astype(MM_DTYPE), w_ref[...].astype(MM_DTYPE)) + b_ref[...]

    return pl.pallas_call(
        body, out_shape=jax.ShapeDtypeStruct((nl, MOD_ROWS, cols), F32), grid=(nl, cols // tn),
        in_specs=[pl.BlockSpec((MOD_ROWS, D_MODEL), lambda i, j: (0, 0)),
                  pl.BlockSpec((None, D_MODEL, tn), lambda i, j: (i, 0, j)),
                  pl.BlockSpec((None, 1, tn), lambda i, j: (i, 0, j))],
        out_specs=pl.BlockSpec((None, MOD_ROWS, tn), lambda i, j: (i, 0, j)),
        compiler_params=_params("parallel", "parallel"), name="mod_fwd")(cact_in, w_mod, b_mod)


def mod_update(c_rows, dmod, w, m, v):
    nl, _, cols = w.shape
    tr, tn = 256, 512
    wspec = pl.BlockSpec((None, tr, tn), lambda i, r, j: (i, r, j))

    def body(c_ref, d_ref, w_ref, m_ref, v_ref, g_ref, dl_ref, nm_ref, nv_ref):
        g = _dot_tn(_silu(c_ref[...]).astype(MM_DTYPE), d_ref[...].astype(MM_DTYPE))
        g_ref[...] = g
        dl, nm, nv = _adam_math(w_ref[...], g, m_ref[...], v_ref[...])
        dl_ref[...] = dl
        nm_ref[...] = nm
        nv_ref[...] = nv

    return pl.pallas_call(
        body, out_shape=[jax.ShapeDtypeStruct(w.shape, F32)] * 4, grid=(nl, D_MODEL // tr, cols // tn),
        in_specs=[pl.BlockSpec((MOD_ROWS, tr), lambda i, r, j: (0, r)),
                  pl.BlockSpec((None, MOD_ROWS, tn), lambda i, r, j: (i, 0, j)), wspec, wspec, wspec],
        out_specs=[wspec] * 4, compiler_params=_params("parallel", "parallel", "parallel"),
        name="mod_update")(c_rows, dmod, w, m, v)


def cctx_partial(dmod_ctx, w_mod, c_ctx):
    nl, _, cols = w_mod.shape
    tk = 512
    per = cols // tk
    part = _matmul(dmod_ctx, w_mod, pl.BlockSpec((8, tk), lambda i, j, k: (0, k)),
                   pl.BlockSpec((None, D_MODEL, tk), lambda i, j, k: (k // per, 0, k % per)),
                   pl.BlockSpec((8, D_MODEL), lambda i, j, k: (0, 0)),
                   jax.ShapeDtypeStruct((8, D_MODEL), F32), (1, 1, nl * per), tb=True, name="cctx_partial")

    def body(p_ref, c_ref, o_ref):
        o_ref[...] = 0.5 * jnp.sum(p_ref[...], axis=0, keepdims=True) * _silu_grad(c_ref[...])

    return pl.pallas_call(body, out_shape=jax.ShapeDtypeStruct((1, D_MODEL), F32), name="cctx_scale")(part, c_ctx)


WEIGHT_NAMES = ("c_ctx", "w_mod", "b_mod", "g_pre", "g_post", "w_in", "b_gate", "na_rpb", "pool_w", "pool_scale",
                "conv_w", "ssm_a_re", "ssm_a_im", "ssm_log_dt", "ssm_b_re", "ssm_b_im", "ssm_c_re", "ssm_c_im",
                "ssm_d", "glu_w", "w_br", "w_o")
SSM_NAMES = ("ssm_a_re", "ssm_a_im", "ssm_log_dt", "ssm_b_re", "ssm_b_im", "ssm_c_re", "ssm_c_im")
SMALL_NAMES = ("c_ctx", "b_mod", "g_pre", "g_post", "b_gate", "na_rpb", "pool_w", "pool_scale") + SSM_NAMES + ("ssm_d",)
BIG_NAMES = ("w_in", "glu_w", "w_br", "w_o")
FLAT_COLS = 1024


def _flat(parts):
    v = jnp.concatenate([p.reshape(-1) for p in parts])
    pad = -v.shape[0] % (64 * FLAT_COLS)
    return jnp.pad(v, (0, pad)).reshape(-1, FLAT_COLS)


def _unflat(flat, shapes):
    v = flat.reshape(-1)
    out, off = [], 0
    for s in shapes:
        n = math.prod(s)
        out.append(v[off:off + n].reshape(s))
        off += n
    return out


def kernel(x, c, ctx, c_ctx, w_mod, b_mod, g_pre, g_post, w_in, b_gate, na_rpb, pool_w, pool_scale, conv_w, ssm_a_re, ssm_a_im, ssm_log_dt, ssm_b_re, ssm_b_im, ssm_c_re, ssm_c_im, ssm_d, glu_w, w_br, w_o, loss_target, m_c_ctx, m_w_mod, m_b_mod, m_g_pre, m_g_post, m_w_in, m_b_gate, m_na_rpb, m_pool_w, m_pool_scale, m_conv_w, m_ssm_a_re, m_ssm_a_im, m_ssm_log_dt, m_ssm_b_re, m_ssm_b_im, m_ssm_c_re, m_ssm_c_im, m_ssm_d, m_glu_w, m_w_br, m_w_o, v_c_ctx, v_w_mod, v_b_mod, v_g_pre, v_g_post, v_w_in, v_b_gate, v_na_rpb, v_pool_w, v_pool_scale, v_conv_w, v_ssm_a_re, v_ssm_a_im, v_ssm_log_dt, v_ssm_b_re, v_ssm_b_im, v_ssm_c_re, v_ssm_c_im, v_ssm_d, v_glu_w, v_w_br, v_w_o):
    W = dict(c_ctx=c_ctx, w_mod=w_mod, b_mod=b_mod, g_pre=g_pre, g_post=g_post, w_in=w_in, b_gate=b_gate,
             na_rpb=na_rpb, pool_w=pool_w, pool_scale=pool_scale, conv_w=conv_w, ssm_a_re=ssm_a_re, ssm_a_im=ssm_a_im,
             ssm_log_dt=ssm_log_dt, ssm_b_re=ssm_b_re, ssm_b_im=ssm_b_im, ssm_c_re=ssm_c_re, ssm_c_im=ssm_c_im,
             ssm_d=ssm_d, glu_w=glu_w, w_br=w_br, w_o=w_o)
    M = dict(c_ctx=m_c_ctx, w_mod=m_w_mod, b_mod=m_b_mod, g_pre=m_g_pre, g_post=m_g_post, w_in=m_w_in, b_gate=m_b_gate,
             na_rpb=m_na_rpb, pool_w=m_pool_w, pool_scale=m_pool_scale, conv_w=m_conv_w, ssm_a_re=m_ssm_a_re,
             ssm_a_im=m_ssm_a_im, ssm_log_dt=m_ssm_log_dt, ssm_b_re=m_ssm_b_re, ssm_b_im=m_ssm_b_im,
             ssm_c_re=m_ssm_c_re, ssm_c_im=m_ssm_c_im, ssm_d=m_ssm_d, glu_w=m_glu_w, w_br=m_w_br, w_o=m_w_o)
    V = dict(c_ctx=v_c_ctx, w_mod=v_w_mod, b_mod=v_b_mod, g_pre=v_g_pre, g_post=v_g_post, w_in=v_w_in, b_gate=v_b_gate,
             na_rpb=v_na_rpb, pool_w=v_pool_w, pool_scale=v_pool_scale, conv_w=v_conv_w, ssm_a_re=v_ssm_a_re,
             ssm_a_im=v_ssm_a_im, ssm_log_dt=v_ssm_log_dt, ssm_b_re=v_ssm_b_re, ssm_b_im=v_ssm_b_im,
             ssm_c_re=v_ssm_c_re, ssm_c_im=v_ssm_c_im, ssm_d=v_ssm_d, glu_w=v_glu_w, w_br=v_w_br, w_o=v_w_o)
    nl = w_in.shape[0]
    xi, yi, ci = _place()
    chip = 2 * xi + yi
    example = 4 * xi + 2 * yi + ci
    mod_cols = w_mod.shape[2]

    def my_half(a):
        half = a.shape[0] // 2
        return lax.dynamic_slice_in_dim(a, ci * half, half, axis=0).astype(MM_DTYPE)

    gathered = []
    for i in range(nl):
        g = all_gather8([my_half(W[n][i]) for n in BIG_NAMES], name="gather_weights")
        gathered.append({n: a.reshape(N_SHARDS, -1, a.shape[-1]) for n, a in zip(BIG_NAMES, g)})

    c8 = jnp.pad(c, ((0, 7), (0, 0)))
    c_all, = all_gather8([c8], name="gather_c")
    c_rows = jnp.concatenate([c_all[:, 0], jnp.broadcast_to(c_ctx[None], (8, D_MODEL))], axis=0)
    b_cols = lax.dynamic_slice_in_dim(b_mod, chip * mod_cols, mod_cols, axis=1)[:, None]
    mod_part = mod_fwd(c_rows, w_mod, b_cols)
    conv_part = jnp.pad(conv_w.reshape(nl * 3, -1), ((0, 16 - nl * 3), (0, 0)))
    parts, conv_all = all_gather8([mod_part.reshape(nl * MOD_ROWS, mod_cols), conv_part], name="gather_mod")
    mod_full = jnp.concatenate([parts[2 * p].reshape(nl, MOD_ROWS, mod_cols) for p in range(N_SHARDS)], axis=-1)
    conv_full = jnp.concatenate([conv_all[2 * p][:nl * 3].reshape(nl, 3, -1) for p in range(N_SHARDS)], axis=-1)
    own = lax.dynamic_index_in_dim(mod_full, example, axis=1, keepdims=False)
    mods = [jnp.stack([own[i], mod_full[i, 8]]) for i in range(nl)]

    layers = []
    for i in range(nl):
        gw = gathered[i]
        layers.append(dict(
            w_in=gw["w_in"], w_br=gw["w_br"].reshape(D_MODEL, D_MODEL), w_o=gw["w_o"].reshape(D_MODEL, D_MODEL),
            glu_w=gw["glu_w"].transpose(1, 0, 2).reshape(BRANCH, 2 * BRANCH),
            conv_w=jnp.pad(conv_full[i], ((0, 5), (0, 0))), pool_w=pool_w[i], pool_scale=pool_scale[i][None],
            b_gate=b_gate[i][None], g_pre=g_pre[i][None], g_post=g_post[i][None], na_rpb=na_rpb[i],
            ssm_d=ssm_d[i][None], ssm_params=tuple(W[n][i] for n in SSM_NAMES)))

    loss_local, grad_x, grads = local_step(x[0], ctx[0], loss_target[0], mods, layers)
    loss = lax.psum(loss_local, ("x", "y", "c"))

    dmod_local = jnp.stack([g["mod"] for g in grads])
    dmod_all, = all_gather8([jnp.pad(dmod_local.reshape(nl * 2, -1), ((0, 8 - nl * 2), (0, 0)))], name="gather_dmod")
    dmod_all = dmod_all[:, :nl * 2].reshape(8, nl, 2, 3 * D_MODEL)
    dmod_rows = jnp.concatenate([dmod_all[:, :, 0], dmod_all[:, :, 1]], axis=0).transpose(1, 0, 2)
    dmod_cols = lax.dynamic_slice_in_dim(dmod_rows, chip * mod_cols, mod_cols, axis=2)
    g_w_mod, d_w_mod, nm_w_mod, nv_w_mod = mod_update(c_rows, dmod_cols, w_mod, m_w_mod, v_w_mod)
    dctx_cols = dmod_cols[:, 8:].transpose(1, 0, 2).reshape(8, nl * mod_cols)
    g_cctx_part = cctx_partial(dctx_cols, w_mod, c_ctx[None])[0]

    def small_grad(n):
        if n == "c_ctx":
            return g_cctx_part
        if n == "b_mod":
            return jnp.stack([g["mod"][0] + g["mod"][1] for g in grads])
        if n in SSM_NAMES:
            return jnp.stack([g["ssm"][SSM_NAMES.index(n)] for g in grads])
        return jnp.stack([g[n] for g in grads])

    conv_grad_full = jnp.stack([g["conv_w"] for g in grads])
    flat_g = _flat([small_grad(n) for n in SMALL_NAMES] + [conv_grad_full])
    flat_all, = all_gather8([flat_g], name="gather_small_grads")
    flat_sum = sum_slabs(flat_all, name="sum_small_grads")
    small_shapes = [W[n].shape for n in SMALL_NAMES]
    small_g = _unflat(flat_sum, small_shapes + [conv_grad_full.shape])
    conv_g = lax.dynamic_slice_in_dim(small_g[-1], chip * conv_w.shape[2], conv_w.shape[2], axis=2)
    adam_names = SMALL_NAMES + ("conv_w",)
    adam_shapes = small_shapes + [conv_w.shape]
    g_list = small_g[:-1] + [conv_g]
    upd = adamw(_flat([W[n] for n in adam_names]), _flat(g_list), _flat([M[n] for n in adam_names]),
                _flat([V[n] for n in adam_names]), name="adamw_small")
    G = dict(zip(adam_names, g_list))
    DL, NM, NV = (dict(zip(adam_names, _unflat(u, adam_shapes))) for u in upd)
    G["w_mod"], DL["w_mod"], NM["w_mod"], NV["w_mod"] = g_w_mod, d_w_mod, nm_w_mod, nv_w_mod

    big = {n: [] for n in BIG_NAMES}
    for i in range(nl):
        g = grads[i]
        local = [g["w_in"], g["glu_w"].reshape(BRANCH, N_SHARDS, -1).transpose(1, 0, 2),
                 g["w_br"].reshape(N_SHARDS, BRANCH, D_MODEL), g["w_o"].reshape(N_SHARDS, BRANCH, D_MODEL)]
        mine, theirs = sibling_split(local, name="grad_split")
        partial = [add_pair(a, b, out_dtype=MM_DTYPE, name=f"grad_pair_{n}") for n, a, b in zip(BIG_NAMES, mine, theirs)]
        arrived = chip_exchange(partial, name="grad_exchange")
        halves = [sum_slabs(a, name=f"grad_sum_{n}") for n, a in zip(BIG_NAMES, arrived)]
        full = sibling_join(halves, name="grad_join")
        for n, a in zip(BIG_NAMES, full):
            big[n].append(a.reshape(-1, a.shape[-1]))
    for n in BIG_NAMES:
        g = jnp.stack(big[n])
        rows = g.shape[0] * g.shape[1]
        d, nm, nv = adamw(W[n].reshape(rows, -1), g.reshape(rows, -1), M[n].reshape(rows, -1), V[n].reshape(rows, -1),
                          name=f"adamw_{n}")
        G[n], DL[n], NM[n], NV[n] = g, d.reshape(g.shape), nm.reshape(g.shape), nv.reshape(g.shape)

    out = [loss, grad_x[None]]
    for group in (G, DL, NM, NV):
        out += [group[n].reshape(W[n].shape) for n in WEIGHT_NAMES]
    return tuple(out)
```

```python
import functools
import math

import numpy as np
import jax
import jax.numpy as jnp
from jax import lax
from jax.experimental import pallas as pl
from jax.experimental.pallas import tpu as pltpu

F32 = jnp.float32
BF16 = jnp.bfloat16
MM_DTYPE = jnp.bfloat16

D_MODEL = 2048
BRANCH = 512
N_HEADS = 8
HEAD_DIM = 64
GRID_W = 64
WIN_ROWS = 8
WIN_COLS = 16
POOL_GROUPS = 4
POOL_DIM = 128
SSM_GROUPS = 32
SSM_GDIM = 16
SSM_STATE = 64
N_STATE = SSM_GROUPS * SSM_STATE
IN_TOTAL = 14336
RMS_EPS = 1e-6
NEG_INF = -1e30
COL = dict(q=0, k=512, v=1024, na_z=1536, pool_u=2048, pool_z=2560, conv_x=3072, conv_b=3584,
           conv_c=4096, conv_z=4608, ssm_u=5120, ssm_z=5632, merge=6144)
N_SHARDS = 4
W_IN_SHARD = IN_TOTAL // N_SHARDS
VMEM_LIMIT_BYTES = 48 * 1024 * 1024
ROW_TILE = 256

ADAM_LR = 0.001
ADAM_B1 = 0.9
ADAM_B2 = 0.999
ADAM_EPS = 1e-08
ADAM_WD = 0.01
ADAM_STEP = 10


def _params(*sem):
    return pltpu.CompilerParams(dimension_semantics=sem, vmem_limit_bytes=VMEM_LIMIT_BYTES)


def _sigmoid(x):
    return 1.0 / (1.0 + jnp.exp(-x))


def _matmul(a, b, a_spec, b_spec, o_spec, out_shape, grid, *, ta=False, tb=False, name):
    nk = grid[-1]
    kaxis = len(grid) - 1
    dims = (((0,) if ta else (1,), (1,) if tb else (0,)), ((), ()))

    def body(a_ref, b_ref, o_ref, *scratch):
        p = lax.dot_general(a_ref[...].astype(MM_DTYPE), b_ref[...].astype(MM_DTYPE), dims,
                            preferred_element_type=F32)
        if nk == 1:
            o_ref[...] = p.astype(o_ref.dtype)
            return
        acc_ref, = scratch
        k = pl.program_id(kaxis)

        @pl.when(k == 0)
        def _():
            acc_ref[...] = p

        @pl.when(k > 0)
        def _():
            acc_ref[...] += p

        @pl.when(k == nk - 1)
        def _():
            o_ref[...] = acc_ref[...].astype(o_ref.dtype)

    oblock = tuple(s for s in o_spec.block_shape if s is not None)
    scratch = [] if nk == 1 else [pltpu.VMEM(oblock, F32)]
    sem = ("parallel",) * (len(grid) - 1) + ("arbitrary",)
    return pl.pallas_call(body, out_shape=out_shape, grid=grid, in_specs=[a_spec, b_spec],
                          out_specs=o_spec, scratch_shapes=scratch, compiler_params=_params(*sem),
                          name=name)(a, b)


def _pick(n, cands):
    for c in cands:
        if n % c == 0:
            return c
    raise ValueError(f"no tile for {n}")


def _row_tile(T):
    return _pick(T, (544, 512, 256, 128))


def mm_nn(a, b, *, out_dtype, name, tn=512, a_rows=None, o_rows=None, a_cols=None):
    M = a.shape[0]
    c0, K = a_cols or (0, a.shape[1])
    N = b.shape[1]
    tm = ROW_TILE if (a_rows or o_rows) else _row_tile(M)
    tn = min(tn, N)
    tk = K if K <= 2048 else _pick(K, (2048, 1024, 512))
    kb0 = c0 // tk
    ar = a_rows or (lambda i: i)
    orr = o_rows or (lambda i: i)
    return _matmul(a, b, pl.BlockSpec((tm, tk), lambda i, j, k: (ar(i), kb0 + k)),
                   pl.BlockSpec((tk, tn), lambda i, j, k: (k, j)),
                   pl.BlockSpec((tm, tn), lambda i, j, k: (orr(i), j)),
                   jax.ShapeDtypeStruct((M, N), out_dtype), (M // tm, N // tn, K // tk), name=name)


def mm_nt(a, b, *, out_dtype, name, a_rows=None, o_rows=None):
    M, K = a.shape
    N = b.shape[0]
    tm = ROW_TILE if (a_rows or o_rows) else _row_tile(M)
    tn = min(N, 2048)
    tk = K if K <= 1024 else _pick(K, (1024, 512))
    ar = a_rows or (lambda i: i)
    orr = o_rows or (lambda i: i)
    return _matmul(a, b, pl.BlockSpec((tm, tk), lambda i, j, k: (ar(i), k)),
                   pl.BlockSpec((tn, tk), lambda i, j, k: (j, k)),
                   pl.BlockSpec((tm, tn), lambda i, j, k: (orr(i), j)),
                   jax.ShapeDtypeStruct((M, N), out_dtype), (M // tm, N // tn, K // tk), tb=True, name=name)


def mm_tn(a, b, *, out_dtype, name, a_rows=None, b_rows=None, tm=512, tn=1024, a_cols=None):
    K = a.shape[0]
    c0, M = a_cols or (0, a.shape[1])
    N = b.shape[1]
    tk = ROW_TILE if (a_rows or b_rows) else _row_tile(K)
    tm = min(tm, M)
    tn = min(tn, N)
    mb0 = c0 // tm
    ar = a_rows or (lambda k: k)
    br = b_rows or (lambda k: k)
    return _matmul(a, b, pl.BlockSpec((tk, tm), lambda i, j, k: (ar(k), mb0 + i)),
                   pl.BlockSpec((tk, tn), lambda i, j, k: (br(k), j)),
                   pl.BlockSpec((tm, tn), lambda i, j, k: (i, j)),
                   jax.ShapeDtypeStruct((M, N), out_dtype), (M // tm, N // tn, K // tk), ta=True, name=name)


def _ew(fn, ins, outs, colsums, *, T, L, name):
    tb = ROW_TILE
    nlat = L // tb
    seg = lambda i: jnp.where(i >= nlat, 1, 0)
    in_specs, arrays = [], []
    for arr, kind, cb, width in ins:
        arrays.append(arr)
        if kind == "row":
            in_specs.append(pl.BlockSpec((tb, width), lambda i, cb=cb: (i, cb)))
        elif kind == "bcast":
            in_specs.append(pl.BlockSpec((1, width), lambda i, cb=cb: (0, cb)))
        else:
            in_specs.append(pl.BlockSpec((None, 1, width), lambda i, cb=cb: (seg(i), 0, cb)))
    out_specs = [pl.BlockSpec((tb, w), lambda i: (i, 0)) for w, _ in outs]
    out_shapes = [jax.ShapeDtypeStruct((T, w), dt) for w, dt in outs]
    out_specs += [pl.BlockSpec((None, 1, w), lambda i: (seg(i), 0, 0)) for w in colsums]
    out_shapes += [jax.ShapeDtypeStruct((2, 1, w), F32) for w in colsums]
    n_in, n_out = len(ins), len(outs)

    def body(*refs):
        i = pl.program_id(0)
        res = fn(*[r[...] for r in refs[:n_in]])
        for r, v in zip(refs[n_in:n_in + n_out], res[:n_out]):
            r[...] = v.astype(r.dtype)
        first = (i == 0) | (i == nlat)
        for r, v in zip(refs[n_in + n_out:], res[n_out:]):
            s = jnp.sum(v, axis=0, keepdims=True)

            @pl.when(first)
            def _(r=r, s=s):
                r[...] = s

            @pl.when(jnp.logical_not(first))
            def _(r=r, s=s):
                r[...] += s

    res = pl.pallas_call(body, out_shape=out_shapes, grid=(T // tb,), in_specs=in_specs,
                         out_specs=out_specs, compiler_params=_params("arbitrary"), name=name)(*arrays)
    return res


def _rms(x):
    return lax.rsqrt(jnp.mean(x * x, axis=-1, keepdims=True) + RMS_EPS)


def prenorm_fwd(X, g, scale, shift, *, L):
    T = X.shape[0]

    def fn(x, g, sc, sh):
        return ((x * _rms(x)) * (g * (1.0 + sc)) + sh,)

    h, = _ew(fn, [(X, "row", 0, D_MODEL), (g, "bcast", 0, D_MODEL), (scale, "seg", 0, D_MODEL),
                  (shift, "seg", 0, D_MODEL)], [(D_MODEL, MM_DTYPE)], [], T=T, L=L, name="prenorm_fwd")
    return h


def prenorm_bwd(dh, X, g, scale, dres, *, L):
    T = X.shape[0]

    def fn(dh, x, g, sc, dres):
        r = _rms(x)
        xn = x * r
        dxn = dh * (g * (1.0 + sc))
        dx = r * (dxn - xn * jnp.mean(dxn * xn, axis=-1, keepdims=True))
        return dres + dx, dh, dh * xn

    return _ew(fn, [(dh, "row", 0, D_MODEL), (X, "row", 0, D_MODEL), (g, "bcast", 0, D_MODEL),
                    (scale, "seg", 0, D_MODEL), (dres, "row", 0, D_MODEL)],
               [(D_MODEL, F32)], [D_MODEL, D_MODEL], T=T, L=L, name="prenorm_bwd")


def postnorm_fwd(X, y, g, gate, *, L):
    T = X.shape[0]

    def fn(x, y, g, gate):
        return (x + gate * ((y * _rms(y)) * g),)

    out, = _ew(fn, [(X, "row", 0, D_MODEL), (y, "row", 0, D_MODEL), (g, "bcast", 0, D_MODEL),
                    (gate, "seg", 0, D_MODEL)], [(D_MODEL, F32)], [], T=T, L=L, name="postnorm_fwd")
    return out


def postnorm_bwd(dX, y, g, gate, *, L):
    T = dX.shape[0]

    def fn(dx, y, g, gate):
        r = _rms(y)
        yn = y * r
        dyn = dx * (gate * g)
        dy = r * (dyn - yn * jnp.mean(dyn * yn, axis=-1, keepdims=True))
        return dy, dx * yn

    return _ew(fn, [(dX, "row", 0, D_MODEL), (y, "row", 0, D_MODEL), (g, "bcast", 0, D_MODEL),
                    (gate, "seg", 0, D_MODEL)], [(D_MODEL, MM_DTYPE)], [D_MODEL], T=T, L=L, name="postnorm_bwd")


def loss_and_grad(X, target, *, L):
    T = X.shape[0]
    tb = ROW_TILE
    nlat = L // tb

    def body(x_ref, t_ref, dx_ref, part_ref):
        i = pl.program_id(0)

        @pl.when(i < nlat)
        def _():
            err = x_ref[...] - t_ref[...]
            dx_ref[...] = err * (1.0 / D_MODEL)
            part_ref[...] = jnp.full(part_ref.shape, 0.5 / D_MODEL * jnp.sum(err * err), F32)

        @pl.when(i >= nlat)
        def _():
            dx_ref[...] = jnp.zeros(dx_ref.shape, F32)
            part_ref[...] = jnp.zeros(part_ref.shape, F32)

    dx, part = pl.pallas_call(
        body, out_shape=[jax.ShapeDtypeStruct((T, D_MODEL), F32), jax.ShapeDtypeStruct((T // tb, 8, 128), F32)],
        grid=(T // tb,),
        in_specs=[pl.BlockSpec((tb, D_MODEL), lambda i: (i, 0)),
                  pl.BlockSpec((tb, D_MODEL), lambda i: (jnp.minimum(i, nlat - 1), 0))],
        out_specs=[pl.BlockSpec((tb, D_MODEL), lambda i: (i, 0)), pl.BlockSpec((None, 8, 128), lambda i: (i, 0, 0))],
        compiler_params=_params("parallel"), name="loss_and_grad")(X, target)
    return jnp.sum(part[:, 0, 0]), dx


Q_BLOCK = WIN_ROWS * GRID_W
BAND = 2 * WIN_ROWS * GRID_W


def _bias_constants(rows):
    col = np.arange(GRID_W)
    col_start = np.clip(col - WIN_COLS // 2, 0, GRID_W - WIN_COLS)
    in_win = (col[None, :] >= col_start[:, None]) & (col[None, :] < col_start[:, None] + WIN_COLS)
    dcol = np.clip(col[None, :] - col[:, None] + (WIN_COLS - 1), 0, 2 * WIN_COLS - 2)
    E = np.zeros((2 * WIN_COLS - 1, GRID_W, GRID_W), np.float32)
    for dc in range(2 * WIN_COLS - 1):
        E[dc] = (dcol == dc) & in_win
    sel = np.zeros((3, WIN_ROWS, 2 * WIN_ROWS, 2 * WIN_ROWS - 1), np.float32)
    valid = np.zeros((3, WIN_ROWS, GRID_W, 2 * WIN_ROWS, GRID_W), bool)
    for v, r0 in enumerate((0, WIN_ROWS, rows - WIN_ROWS)):
        kstart = int(np.clip(r0 - WIN_ROWS // 2, 0, rows - 2 * WIN_ROWS))
        for a in range(WIN_ROWS):
            qr = r0 + a
            wstart = int(np.clip(qr - WIN_ROWS // 2, 0, rows - WIN_ROWS))
            for b in range(2 * WIN_ROWS):
                kr = kstart + b
                if wstart <= kr < wstart + WIN_ROWS:
                    sel[v, a, b, kr - qr + WIN_ROWS - 1] = 1.0
                    valid[v, a, :, b, :] = in_win
    return E, sel, valid.reshape(3, Q_BLOCK, BAND)


def build_bias(rpb, rows):
    E, sel, valid = _bias_constants(rows)
    tiles = jnp.einsum("hrd,dqk->hrqk", rpb, E, precision=lax.Precision.HIGHEST)
    b = jnp.einsum("vabr,hrqk->vhaqbk", sel, tiles, precision=lax.Precision.HIGHEST)
    b = b.reshape(3, N_HEADS, Q_BLOCK, BAND)
    return jnp.where(valid[:, None], b, NEG_INF)


def _band_start(i, rows):
    return pl.multiple_of(jnp.clip(WIN_ROWS * i - WIN_ROWS // 2, 0, rows - 2 * WIN_ROWS) * GRID_W, 256)


def _variant(i, nq):
    return jnp.where(i == 0, 0, jnp.where(i == nq - 1, 2, 1))


def _dot_nt(a, b):
    return lax.dot_general(a, b, (((1,), (1,)), ((), ())), preferred_element_type=F32)


def _dot_tn(a, b):
    return lax.dot_general(a, b, (((0,), (0,)), ((), ())), preferred_element_type=F32)


def _dot(a, b):
    return jnp.dot(a, b, preferred_element_type=F32)


def attn_fwd(q, k, v, bias, *, L, N):
    H, T, _ = q.shape
    rows = L // GRID_W
    nq = L // Q_BLOCK
    scale = HEAD_DIM ** -0.5

    def body(q_ref, k_ref, v_ref, b_ref, o_ref, lse_ref):
        ks = _band_start(pl.program_id(1), rows)
        qv = q_ref[...]
        kb, vb = k_ref[pl.ds(ks, BAND), :], v_ref[pl.ds(ks, BAND), :]
        kc, vc = k_ref[pl.ds(L, N), :], v_ref[pl.ds(L, N), :]
        sb = _dot_nt(qv, kb) * scale + b_ref[...]
        sc = _dot_nt(qv, kc) * scale
        m = jnp.maximum(jnp.max(sb, axis=-1, keepdims=True), jnp.max(sc, axis=-1, keepdims=True))
        pb, pc = jnp.exp(sb - m), jnp.exp(sc - m)
        l = jnp.sum(pb, axis=-1, keepdims=True) + jnp.sum(pc, axis=-1, keepdims=True)
        o = _dot(pb.astype(MM_DTYPE), vb) + _dot(pc.astype(MM_DTYPE), vc)
        o_ref[...] = o / l
        lse_ref[...] = m + jnp.log(l)

    return pl.pallas_call(
        body, out_shape=[jax.ShapeDtypeStruct((H, L, HEAD_DIM), F32), jax.ShapeDtypeStruct((H, L, 1), F32)],
        grid=(H, nq),
        in_specs=[pl.BlockSpec((None, Q_BLOCK, HEAD_DIM), lambda h, i: (h, i, 0)),
                  pl.BlockSpec((None, T, HEAD_DIM), lambda h, i: (h, 0, 0)),
                  pl.BlockSpec((None, T, HEAD_DIM), lambda h, i: (h, 0, 0)),
                  pl.BlockSpec((None, None, Q_BLOCK, BAND), lambda h, i: (_variant(i, nq), h, 0, 0))],
        out_specs=[pl.BlockSpec((None, Q_BLOCK, HEAD_DIM), lambda h, i: (h, i, 0)),
                   pl.BlockSpec((None, Q_BLOCK, 1), lambda h, i: (h, i, 0))],
        compiler_params=_params("parallel", "arbitrary"), name="attn_fwd")(q, k, v, bias)


def attn_bwd(q, k, v, bias, o, do, lse, *, L, N):
    H, T, _ = q.shape
    rows = L // GRID_W
    nq = L // Q_BLOCK
    scale = HEAD_DIM ** -0.5

    def body(q_ref, k_ref, v_ref, b_ref, o_ref, do_ref, lse_ref, dq_ref, dk_ref, dv_ref, db_ref):
        i = pl.program_id(1)
        ks = _band_start(i, rows)

        @pl.when(i == 0)
        def _():
            dk_ref[...] = jnp.zeros(dk_ref.shape, F32)
            dv_ref[...] = jnp.zeros(dv_ref.shape, F32)

        @pl.when((i == 0) | (i == 1) | (i == nq - 1))
        def _():
            db_ref[...] = jnp.zeros(db_ref.shape, F32)

        qv = q_ref[...]
        kb, vb = k_ref[pl.ds(ks, BAND), :], v_ref[pl.ds(ks, BAND), :]
        kc, vc = k_ref[pl.ds(L, N), :], v_ref[pl.ds(L, N), :]
        lse = lse_ref[...]
        pb = jnp.exp(_dot_nt(qv, kb) * scale + b_ref[...] - lse)
        pc = jnp.exp(_dot_nt(qv, kc) * scale - lse)
        do_f = do_ref[...]
        delta = jnp.sum(do_f * o_ref[...], axis=-1, keepdims=True)
        dov = do_f.astype(MM_DTYPE)
        dsb = pb * (_dot_nt(dov, vb) - delta)
        dsc = pc * (_dot_nt(dov, vc) - delta)
        db_ref[...] += dsb
        dsb_s, dsc_s = (dsb * scale).astype(MM_DTYPE), (dsc * scale).astype(MM_DTYPE)
        dq_ref[...] = _dot(dsb_s, kb) + _dot(dsc_s, kc)
        dk_ref[pl.ds(ks, BAND), :] += _dot_tn(dsb_s, qv)
        dk_ref[pl.ds(L, N), :] += _dot_tn(dsc_s, qv)
        dv_ref[pl.ds(ks, BAND), :] += _dot_tn(pb.astype(MM_DTYPE), dov)
        dv_ref[pl.ds(L, N), :] += _dot_tn(pc.astype(MM_DTYPE), dov)

    qspec = pl.BlockSpec((None, Q_BLOCK, HEAD_DIM), lambda h, i: (h, i, 0))
    kspec = pl.BlockSpec((None, T, HEAD_DIM), lambda h, i: (h, 0, 0))
    bspec = pl.BlockSpec((None, None, Q_BLOCK, BAND), lambda h, i: (_variant(i, nq), h, 0, 0))
    return pl.pallas_call(
        body,
        out_shape=[jax.ShapeDtypeStruct((H, L, HEAD_DIM), F32), jax.ShapeDtypeStruct((H, T, HEAD_DIM), F32),
                   jax.ShapeDtypeStruct((H, T, HEAD_DIM), F32), jax.ShapeDtypeStruct((3, H, Q_BLOCK, BAND), F32)],
        grid=(H, nq),
        in_specs=[qspec, kspec, kspec, bspec, qspec, qspec, pl.BlockSpec((None, Q_BLOCK, 1), lambda h, i: (h, i, 0))],
        out_specs=[qspec, kspec, kspec, bspec],
        compiler_params=_params("parallel", "arbitrary"), name="attn_bwd")(q, k, v, bias, o, do, lse)


def cattn_fwd(q, k, v, *, L, N):
    H = q.shape[0]
    scale = HEAD_DIM ** -0.5
    cspec = pl.BlockSpec((None, N, HEAD_DIM), lambda h: (h, L // N, 0))

    def body(q_ref, k_ref, v_ref, o_ref, lse_ref):
        s = _dot_nt(q_ref[...], k_ref[...]) * scale
        m = jnp.max(s, axis=-1, keepdims=True)
        p = jnp.exp(s - m)
        l = jnp.sum(p, axis=-1, keepdims=True)
        o_ref[...] = _dot(p.astype(MM_DTYPE), v_ref[...]) / l
        lse_ref[...] = m + jnp.log(l)

    return pl.pallas_call(
        body, out_shape=[jax.ShapeDtypeStruct((H, N, HEAD_DIM), F32), jax.ShapeDtypeStruct((H, N, 1), F32)],
        grid=(H,), in_specs=[cspec, cspec, cspec],
        out_specs=[pl.BlockSpec((None, N, HEAD_DIM), lambda h: (h, 0, 0)), pl.BlockSpec((None, N, 1), lambda h: (h, 0, 0))],
        compiler_params=_params("parallel"), name="cattn_fwd")(q, k, v)


def cattn_bwd(q, k, v, o, do, lse, *, L, N):
    H = q.shape[0]
    scale = HEAD_DIM ** -0.5
    cspec = pl.BlockSpec((None, N, HEAD_DIM), lambda h: (h, L // N, 0))
    ospec = pl.BlockSpec((None, N, HEAD_DIM), lambda h: (h, 0, 0))

    def body(q_ref, k_ref, v_ref, o_ref, do_ref, lse_ref, dq_ref, dk_ref, dv_ref):
        qv, kv, vv = q_ref[...], k_ref[...], v_ref[...]
        p = jnp.exp(_dot_nt(qv, kv) * scale - lse_ref[...])
        do_f = do_ref[...]
        delta = jnp.sum(do_f * o_ref[...], axis=-1, keepdims=True)
        dov = do_f.astype(MM_DTYPE)
        ds = (p * (_dot_nt(dov, vv) - delta) * scale).astype(MM_DTYPE)
        dq_ref[...] = _dot(ds, kv)
        dk_ref[...] = _dot_tn(ds, qv)
        dv_ref[...] = _dot_tn(p.astype(MM_DTYPE), dov)

    return pl.pallas_call(
        body, out_shape=[jax.ShapeDtypeStruct((H, N, HEAD_DIM), F32)] * 3, grid=(H,),
        in_specs=[cspec, cspec, cspec, ospec, ospec, pl.BlockSpec((None, N, 1), lambda h: (h, 0, 0))],
        out_specs=[ospec, ospec, ospec], compiler_params=_params("parallel"), name="cattn_bwd")(q, k, v, o, do, lse)


def _heads(a):
    T = a.shape[0]
    return a.reshape(T, N_HEADS, HEAD_DIM).transpose(1, 0, 2)


def _unheads(a):
    return a.transpose(1, 0, 2).reshape(a.shape[1], N_HEADS * HEAD_DIM)


PAD = 16


def _row_ids(T):
    return lax.broadcasted_iota(jnp.int32, (T, POOL_DIM), 0)


def _same_segment(t, s, L, T):
    return (s >= 0) & (s < T) & ((t < L) == (s < L))


def _window_sum(buf_ref, x, half, *, L, T, transpose):
    buf_ref[pl.ds(PAD, T), :] = x
    t = _row_ids(T)
    acc = jnp.zeros((T, POOL_DIM), F32)
    for j in range(-8, 9):
        inside = ((j > -half) & (j <= half)) if transpose else ((j >= -half) & (j < half))
        ok = _same_segment(t, t + j, L, T) & inside
        acc = acc + jnp.where(ok, buf_ref[pl.ds(PAD + j, T), :], 0.0)
    return acc


def _window_count(half, *, L, T):
    t = _row_ids(T)
    pos = jnp.where(t < L, t, t - L)
    seg_len = jnp.where(t < L, L, T - L)
    return (jnp.minimum(pos + half, seg_len) - jnp.maximum(pos - half, 0)).astype(F32)


def _zero_pads(buf_ref, T):
    buf_ref[pl.ds(0, PAD), :] = jnp.zeros((PAD, POOL_DIM), F32)
    buf_ref[pl.ds(PAD + T, PAD), :] = jnp.zeros((PAD, POOL_DIM), F32)


def pool_fwd(proj, pool_w, pool_scale, *, L):
    T = proj.shape[0]
    cb0 = COL["pool_u"] // POOL_DIM

    def body(u_ref, w_ref, s_ref, o_ref, p_ref, buf_ref):
        half = jnp.left_shift(1, pl.program_id(0))
        _zero_pads(buf_ref, T)
        u = u_ref[...]
        pooled = _window_sum(buf_ref, u, half, L=L, T=T, transpose=False) / _window_count(half, L=L, T=T) - u
        pm = pooled.astype(MM_DTYPE)
        p_ref[...] = pm
        o_ref[...] = _dot(pm, w_ref[...].astype(MM_DTYPE)) * s_ref[...]

    cspec = pl.BlockSpec((T, POOL_DIM), lambda g: (0, g))
    return pl.pallas_call(
        body, out_shape=[jax.ShapeDtypeStruct((T, BRANCH), F32), jax.ShapeDtypeStruct((T, BRANCH), MM_DTYPE)],
        grid=(POOL_GROUPS,),
        in_specs=[pl.BlockSpec((T, POOL_DIM), lambda g: (0, cb0 + g)),
                  pl.BlockSpec((None, POOL_DIM, POOL_DIM), lambda g: (g, 0, 0)),
                  pl.BlockSpec((1, POOL_DIM), lambda g: (0, g))],
        out_specs=[cspec, cspec], scratch_shapes=[pltpu.VMEM((T + 2 * PAD, POOL_DIM), F32)],
        compiler_params=_params("parallel"), name="pool_fwd")(proj, pool_w, pool_scale)


def pool_bwd(do, pooled, pool_w, pool_scale, *, L):
    T = do.shape[0]

    def body(do_ref, p_ref, w_ref, s_ref, du_ref, dw_ref, ds_ref, buf_ref):
        half = jnp.left_shift(1, pl.program_id(0))
        _zero_pads(buf_ref, T)
        pm = p_ref[...]
        w = w_ref[...].astype(MM_DTYPE)
        mixed = _dot(pm, w)
        dov = do_ref[...]
        ds_ref[...] = jnp.broadcast_to(jnp.sum(dov * mixed, axis=0, keepdims=True), ds_ref.shape)
        dmixed = (dov * s_ref[...]).astype(MM_DTYPE)
        dw_ref[...] = _dot_tn(pm, dmixed)
        dpooled = _dot_nt(dmixed, w)
        scaled = dpooled / _window_count(half, L=L, T=T)
        du = _window_sum(buf_ref, scaled, half, L=L, T=T, transpose=True) - dpooled
        du_ref[...] = du.astype(du_ref.dtype)

    cspec = pl.BlockSpec((T, POOL_DIM), lambda g: (0, g))
    return pl.pallas_call(
        body, out_shape=[jax.ShapeDtypeStruct((T, BRANCH), MM_DTYPE),
                         jax.ShapeDtypeStruct((POOL_GROUPS, POOL_DIM, POOL_DIM), F32),
                         jax.ShapeDtypeStruct((8, BRANCH), F32)],
        grid=(POOL_GROUPS,),
        in_specs=[cspec, cspec, pl.BlockSpec((None, POOL_DIM, POOL_DIM), lambda g: (g, 0, 0)),
                  pl.BlockSpec((1, POOL_DIM), lambda g: (0, g))],
        out_specs=[cspec, pl.BlockSpec((None, POOL_DIM, POOL_DIM), lambda g: (g, 0, 0)),
                   pl.BlockSpec((8, POOL_DIM), lambda g: (0, g))],
        scratch_shapes=[pltpu.VMEM((T + 2 * PAD, POOL_DIM), F32)],
        compiler_params=_params("parallel"), name="pool_bwd")(do, pooled, pool_w, pool_scale)


def _shifted(buf_ref, x, j, *, L, T):
    buf_ref[pl.ds(PAD, T), :] = x
    t = _row_ids(T)
    return jnp.where(_same_segment(t, t + j, L, T), buf_ref[pl.ds(PAD + j, T), :], 0.0)


def conv_fwd(proj, conv_w, *, L):
    T = proj.shape[0]
    nb = BRANCH // POOL_DIM
    cx, cbb, cc = (COL[n] // POOL_DIM for n in ("conv_x", "conv_b", "conv_c"))

    def body(x_ref, b_ref, c_ref, w_ref, o_ref, buf_ref):
        _zero_pads(buf_ref, T)
        xc = c_ref[...] * x_ref[...]
        w = w_ref[...]
        conv = (w[0:1] * _shifted(buf_ref, xc, -1, L=L, T=T) + w[1:2] * xc
                + w[2:3] * _shifted(buf_ref, xc, 1, L=L, T=T))
        o_ref[...] = b_ref[...] * conv

    return pl.pallas_call(
        body, out_shape=jax.ShapeDtypeStruct((T, BRANCH), F32), grid=(nb,),
        in_specs=[pl.BlockSpec((T, POOL_DIM), lambda g: (0, cx + g)), pl.BlockSpec((T, POOL_DIM), lambda g: (0, cbb + g)),
                  pl.BlockSpec((T, POOL_DIM), lambda g: (0, cc + g)), pl.BlockSpec((8, POOL_DIM), lambda g: (0, g))],
        out_specs=pl.BlockSpec((T, POOL_DIM), lambda g: (0, g)),
        scratch_shapes=[pltpu.VMEM((T + 2 * PAD, POOL_DIM), F32)],
        compiler_params=_params("parallel"), name="conv_fwd")(proj, proj, proj, conv_w)


def conv_bwd(do, proj, conv_w, *, L):
    T = proj.shape[0]
    nb = BRANCH // POOL_DIM
    cx, cbb, cc = (COL[n] // POOL_DIM for n in ("conv_x", "conv_b", "conv_c"))

    def body(do_ref, x_ref, b_ref, c_ref, w_ref, dx_ref, db_ref, dc_ref, dw_ref, buf_ref):
        _zero_pads(buf_ref, T)
        xv, gb, gc = x_ref[...], b_ref[...], c_ref[...]
        xc = gc * xv
        w = w_ref[...]
        xm = _shifted(buf_ref, xc, -1, L=L, T=T)
        xp = _shifted(buf_ref, xc, 1, L=L, T=T)
        conv = w[0:1] * xm + w[1:2] * xc + w[2:3] * xp
        dov = do_ref[...]
        db_ref[...] = (dov * conv).astype(db_ref.dtype)
        dconv = dov * gb
        sums = [jnp.sum(dconv * a, axis=0, keepdims=True) for a in (xm, xc, xp)]
        dw_ref[...] = jnp.concatenate(sums + [jnp.zeros((5, POOL_DIM), F32)], axis=0)
        dxc = (w[0:1] * _shifted(buf_ref, dconv, 1, L=L, T=T) + w[1:2] * dconv
               + w[2:3] * _shifted(buf_ref, dconv, -1, L=L, T=T))
        dc_ref[...] = (dxc * xv).astype(dc_ref.dtype)
        dx_ref[...] = (dxc * gc).astype(dx_ref.dtype)

    ospec = lambda off: pl.BlockSpec((T, POOL_DIM), lambda g: (0, off + g))
    return pl.pallas_call(
        body, out_shape=[jax.ShapeDtypeStruct((T, BRANCH), MM_DTYPE)] * 3 + [jax.ShapeDtypeStruct((8, BRANCH), F32)],
        grid=(nb,),
        in_specs=[ospec(0), ospec(cx), ospec(cbb), ospec(cc), pl.BlockSpec((8, POOL_DIM), lambda g: (0, g))],
        out_specs=[ospec(0), ospec(0), ospec(0), pl.BlockSpec((8, POOL_DIM), lambda g: (0, g))],
        scratch_shapes=[pltpu.VMEM((T + 2 * PAD, POOL_DIM), F32)],
        compiler_params=_params("parallel"), name="conv_bwd")(do, proj, proj, proj, conv_w)


SCAN_COLS = 1024
SCAN_ROWS = 256


def ssm_operators(a_re, a_im, log_dt, b_re, b_im, c_re, c_im):
    dt = jnp.exp(log_dt)[..., None]
    mag = jnp.exp(a_re * dt)
    abar_re, abar_im = mag * jnp.cos(a_im * dt), mag * jnp.sin(a_im * dt)
    den = a_re * a_re + a_im * a_im
    num_re, num_im = abar_re - 1.0, abar_im
    f_re = (num_re * a_re + num_im * a_im) / den
    f_im = (num_im * a_re - num_re * a_im) / den
    bbar_re = f_re[..., None] * b_re - f_im[..., None] * b_im
    bbar_im = f_re[..., None] * b_im + f_im[..., None] * b_re
    gpb = SSM_GROUPS // SSM_BLOCKS
    eye = jnp.eye(gpb, dtype=bool)[None, None, :, None, :, None]

    def blocks(t):
        _, _, a, b = t.shape
        t = t.reshape(2, SSM_BLOCKS, gpb, a, 1, b)
        return jnp.where(eye, t, 0.0).reshape(2, SSM_BLOCKS, gpb * a, gpb * b)

    in_map = lambda bbar: blocks(bbar.transpose(0, 1, 3, 2))
    out_map = lambda c: blocks(c.transpose(0, 1, 3, 2))
    abar = jnp.concatenate([abar_re.reshape(2, 1, N_STATE), abar_im.reshape(2, 1, N_STATE)], axis=-1)
    bcat = jnp.concatenate([in_map(bbar_re), in_map(bbar_im)], axis=1)
    ccat = jnp.concatenate([out_map(c_re), -out_map(c_im)], axis=1)
    return abar, bcat, ccat


SSM_BLOCKS = 4
SSM_BCH = BRANCH // SSM_BLOCKS
SSM_BST = N_STATE // SSM_BLOCKS


def _ssm_rows(T, perm):
    tm = ROW_TILE if perm else _row_tile(T)
    return tm, (perm or (lambda i: i))


def ssm_in(u, bcat, *, ucol0, perm, name):
    T = u.shape[0]
    tm, rows = _ssm_rows(T, perm)
    ub0 = ucol0 // SSM_BCH
    return _matmul(u, bcat, pl.BlockSpec((tm, SSM_BCH), lambda i, n, k: (rows(i), ub0 + n % SSM_BLOCKS)),
                   pl.BlockSpec((None, SSM_BCH, SSM_BST), lambda i, n, k: (n, 0, 0)),
                   pl.BlockSpec((tm, SSM_BST), lambda i, n, k: (i, n)),
                   jax.ShapeDtypeStruct((T, 2 * N_STATE), F32), (T // tm, 2 * SSM_BLOCKS, 1), name=name)


def ssm_out(s, ccat, *, perm, name):
    T = s.shape[0]
    tm, rows = _ssm_rows(T, perm)
    return _matmul(s, ccat, pl.BlockSpec((tm, SSM_BST), lambda i, n, k: (i, k * SSM_BLOCKS + n)),
                   pl.BlockSpec((None, SSM_BST, SSM_BCH), lambda i, n, k: (k * SSM_BLOCKS + n, 0, 0)),
                   pl.BlockSpec((tm, SSM_BCH), lambda i, n, k: (rows(i), n)),
                   jax.ShapeDtypeStruct((T, BRANCH), F32), (T // tm, SSM_BLOCKS, 2), name=name)


def ssm_out_dx(dy, ccat, *, perm, name):
    T = dy.shape[0]
    tm, rows = _ssm_rows(T, perm)
    return _matmul(dy, ccat, pl.BlockSpec((tm, SSM_BCH), lambda i, n, k: (rows(i), n % SSM_BLOCKS)),
                   pl.BlockSpec((None, SSM_BST, SSM_BCH), lambda i, n, k: (n, 0, 0)),
                   pl.BlockSpec((tm, SSM_BST), lambda i, n, k: (i, n)),
                   jax.ShapeDtypeStruct((T, 2 * N_STATE), F32), (T // tm, 2 * SSM_BLOCKS, 1), tb=True, name=name)


def ssm_in_dx(lam, bcat, *, perm, name):
    T = lam.shape[0]
    tm, rows = _ssm_rows(T, perm)
    return _matmul(lam, bcat, pl.BlockSpec((tm, SSM_BST), lambda i, n, k: (i, k * SSM_BLOCKS + n)),
                   pl.BlockSpec((None, SSM_BCH, SSM_BST), lambda i, n, k: (k * SSM_BLOCKS + n, 0, 0)),
                   pl.BlockSpec((tm, SSM_BCH), lambda i, n, k: (rows(i), n)),
                   jax.ShapeDtypeStruct((T, BRANCH), F32), (T // tm, SSM_BLOCKS, 2), tb=True, name=name)


def ssm_in_dw(u, lam, *, ucol0, perm, name):
    T = u.shape[0]
    tk, rows = _ssm_rows(T, perm)
    ub0 = ucol0 // SSM_BCH
    return _matmul(u, lam, pl.BlockSpec((tk, SSM_BCH), lambda n, j, k: (rows(k), ub0 + n % SSM_BLOCKS)),
                   pl.BlockSpec((tk, SSM_BST), lambda n, j, k: (k, n)),
                   pl.BlockSpec((None, SSM_BCH, SSM_BST), lambda n, j, k: (n, 0, 0)),
                   jax.ShapeDtypeStruct((2 * SSM_BLOCKS, SSM_BCH, SSM_BST), F32), (2 * SSM_BLOCKS, 1, T // tk),
                   ta=True, name=name)


def ssm_out_dw(s, dy, *, perm, name):
    T = s.shape[0]
    tk, rows = _ssm_rows(T, perm)
    return _matmul(s, dy, pl.BlockSpec((tk, SSM_BST), lambda n, j, k: (k, n)),
                   pl.BlockSpec((tk, SSM_BCH), lambda n, j, k: (rows(k), n % SSM_BLOCKS)),
                   pl.BlockSpec((None, SSM_BST, SSM_BCH), lambda n, j, k: (n, 0, 0)),
                   jax.ShapeDtypeStruct((2 * SSM_BLOCKS, SSM_BST, SSM_BCH), F32), (2 * SSM_BLOCKS, 1, T // tk),
                   ta=True, name=name)


def _time_block(T, reverse):
    nt = T // SCAN_ROWS
    tix = (lambda i: nt - 1 - i) if reverse else (lambda i: i)
    return nt, pl.BlockSpec((SCAN_ROWS, 2 * N_STATE), lambda i: (tix(i), 0))


def ssm_scan(bu, abar, *, reverse):
    T = bu.shape[0]
    nt, tspec = _time_block(T, reverse)

    def body(b_ref, a_ref, s_ref, c_ref):
        @pl.when(pl.program_id(0) == 0)
        def _():
            c_ref[...] = jnp.zeros(c_ref.shape, F32)

        for c0 in range(0, N_STATE, SCAN_COLS):
            re, im = pl.ds(c0, SCAN_COLS), pl.ds(N_STATE + c0, SCAN_COLS)
            ar, ai = a_ref[:, re], a_ref[:, im]

            def step(n, carry, re=re, im=im, ar=ar, ai=ai):
                sr, si = carry
                t = (SCAN_ROWS - 1 - n) if reverse else n
                nr = ar * sr - ai * si + b_ref[pl.ds(t, 1), re]
                ni = ar * si + ai * sr + b_ref[pl.ds(t, 1), im]
                s_ref[pl.ds(t, 1), re] = nr
                s_ref[pl.ds(t, 1), im] = ni
                return nr, ni

            sr, si = lax.fori_loop(0, SCAN_ROWS, step, (c_ref[:, re], c_ref[:, im]))
            c_ref[:, re] = sr
            c_ref[:, im] = si

    return pl.pallas_call(
        body, out_shape=jax.ShapeDtypeStruct((T, 2 * N_STATE), F32), grid=(nt,),
        in_specs=[tspec, pl.BlockSpec((1, 2 * N_STATE), lambda i: (0, 0))], out_specs=tspec,
        scratch_shapes=[pltpu.VMEM((1, 2 * N_STATE), F32)],
        compiler_params=_params("arbitrary"), name="ssm_scan_rev" if reverse else "ssm_scan_fwd")(bu, abar)


def ssm_scan_bwd(g, s, abar, *, reverse):
    T = g.shape[0]
    nt, tspec = _time_block(T, not reverse)
    back = not reverse

    def body(g_ref, s_ref, a_ref, l_ref, da_ref, c_ref):
        @pl.when(pl.program_id(0) == 0)
        def _():
            c_ref[...] = jnp.zeros(c_ref.shape, F32)
            da_ref[...] = jnp.zeros(da_ref.shape, F32)

        for c0 in range(0, N_STATE, SCAN_COLS):
            re, im = pl.ds(c0, SCAN_COLS), pl.ds(N_STATE + c0, SCAN_COLS)
            ar, ai = a_ref[:, re], a_ref[:, im]

            def step(n, carry, re=re, im=im, ar=ar, ai=ai):
                lr, li, dr, di = carry
                t = (SCAN_ROWS - 1 - n) if back else n
                sr, si = s_ref[pl.ds(t, 1), re], s_ref[pl.ds(t, 1), im]
                dr = dr + sr * lr + si * li
                di = di + sr * li - si * lr
                nr = g_ref[pl.ds(t, 1), re] + ar * lr + ai * li
                ni = g_ref[pl.ds(t, 1), im] + ar * li - ai * lr
                l_ref[pl.ds(t, 1), re] = nr
                l_ref[pl.ds(t, 1), im] = ni
                return nr, ni, dr, di

            zero = jnp.zeros((1, SCAN_COLS), F32)
            lr, li, dr, di = lax.fori_loop(0, SCAN_ROWS, step, (c_ref[:, re], c_ref[:, im], zero, zero))
            c_ref[:, re] = lr
            c_ref[:, im] = li
            da_ref[:, re] += jnp.broadcast_to(dr, (8, SCAN_COLS))
            da_ref[:, im] += jnp.broadcast_to(di, (8, SCAN_COLS))

    return pl.pallas_call(
        body, out_shape=[jax.ShapeDtypeStruct((T, 2 * N_STATE), F32), jax.ShapeDtypeStruct((8, 2 * N_STATE), F32)],
        grid=(nt,), in_specs=[tspec, tspec, pl.BlockSpec((1, 2 * N_STATE), lambda i: (0, 0))],
        out_specs=[tspec, pl.BlockSpec((8, 2 * N_STATE), lambda i: (0, 0))],
        scratch_shapes=[pltpu.VMEM((1, 2 * N_STATE), F32)],
        compiler_params=_params("arbitrary"),
        name="ssm_scan_bwd_rev" if reverse else "ssm_scan_bwd_fwd")(g, s, abar)


def _gelu(x):
    return 0.5 * x * (1.0 + jnp.tanh(0.7978845608028654 * (x + 0.044715 * x * x * x)))


def _gelu_grad(x):
    t = jnp.tanh(0.7978845608028654 * (x + 0.044715 * x * x * x))
    return 0.5 * (1.0 + t) + 0.5 * x * (1.0 - t * t) * 0.7978845608028654 * (1.0 + 3 * 0.044715 * x * x)


def _silu(z):
    return z * _sigmoid(z)


def _silu_grad(z):
    s = _sigmoid(z)
    return s * (1.0 + z * (1.0 - s))


def ssm_fwd(proj, ops, dsk, glu_w, *, L):
    T = proj.shape[0]
    abar, bcat, ccat = ops
    nb, nlat = T // ROW_TILE, L // ROW_TILE
    to_f = lambda i: (i + nlat) % nb
    states, ys = [], []
    for d in (0, 1):
        perm = to_f if d == 0 else None
        bu = ssm_in(proj, bcat[d], ucol0=COL["ssm_u"], perm=perm, name=f"ssm_in{d}")
        s = ssm_scan(bu, abar[d], reverse=(d == 1))
        states.append(s)
        ys.append(ssm_out(s, ccat[d], perm=perm, name=f"ssm_out{d}"))

    def pre(u, yf, yr, dsk):
        y = dsk * u + yf + yr
        return y, _gelu(y)

    ypre, gy = _ew(pre, [(proj, "row", COL["ssm_u"] // BRANCH, BRANCH), (ys[0], "row", 0, BRANCH),
                         (ys[1], "row", 0, BRANCH), (dsk, "bcast", 0, BRANCH)],
                   [(BRANCH, F32), (BRANCH, MM_DTYPE)], [], T=T, L=L, name="ssm_pre")
    gg = mm_nn(gy, glu_w, out_dtype=F32, name="ssm_glu")

    def post(ga, gb):
        return (ga * _sigmoid(gb),)

    o, = _ew(post, [(gg, "row", 0, BRANCH), (gg, "row", 1, BRANCH)], [(BRANCH, F32)], [], T=T, L=L, name="ssm_post")
    return o, dict(states=states, ypre=ypre, gy=gy, gg=gg)


def ssm_bwd(do, proj, ops, dsk, glu_w, saved, *, L):
    T = proj.shape[0]
    abar, bcat, ccat = ops
    nb, nlat = T // ROW_TILE, L // ROW_TILE
    to_f = lambda i: (i + nlat) % nb
    gg, gy, ypre = saved["gg"], saved["gy"], saved["ypre"]

    def post_bwd(do, ga, gb):
        sg = _sigmoid(gb)
        return (jnp.concatenate([do * sg, do * ga * sg * (1.0 - sg)], axis=1),)

    dgg, = _ew(post_bwd, [(do, "row", 0, BRANCH), (gg, "row", 0, BRANCH), (gg, "row", 1, BRANCH)],
               [(2 * BRANCH, MM_DTYPE)], [], T=T, L=L, name="ssm_post_bwd")
    dgy = mm_nt(dgg, glu_w, out_dtype=F32, name="ssm_glu_dx")
    dglu = mm_tn(gy, dgg, out_dtype=F32, name="ssm_glu_dw")

    def pre_bwd(dgy, y, u, dsk):
        dy = dgy * _gelu_grad(y)
        return dy, dy * dsk, dy * u

    dy, du_skip, dd = _ew(pre_bwd, [(dgy, "row", 0, BRANCH), (ypre, "row", 0, BRANCH),
                                    (proj, "row", COL["ssm_u"] // BRANCH, BRANCH), (dsk, "bcast", 0, BRANCH)],
                          [(BRANCH, MM_DTYPE), (BRANCH, F32)], [BRANCH], T=T, L=L, name="ssm_pre_bwd")
    du = du_skip
    dabar, dbcat, dccat = [], [], []
    for d in (0, 1):
        perm = to_f if d == 0 else None
        s = saved["states"][d]
        g = ssm_out_dx(dy, ccat[d], perm=perm, name=f"ssm_out{d}_dx")
        lam, da = ssm_scan_bwd(g, s, abar[d], reverse=(d == 1))
        dabar.append(da[0:1])
        dccat.append(ssm_out_dw(s, dy, perm=perm, name=f"ssm_out{d}_dw"))
        du = du + ssm_in_dx(lam, bcat[d], perm=perm, name=f"ssm_in{d}_dx")
        dbcat.append(ssm_in_dw(proj, lam, ucol0=COL["ssm_u"], perm=perm, name=f"ssm_in{d}_dw"))
    d_ops = (jnp.stack(dabar), jnp.stack(dbcat), jnp.stack(dccat))
    return du, d_ops, dd[0, 0] + dd[1, 0], dglu


Z_COLS = tuple(COL[n] // BRANCH for n in ("na_z", "pool_z", "conv_z", "ssm_z"))


def gate_act(o, proj, *, L):
    T = o.shape[0]

    def fn(o, z0, z1, z2, z3):
        return (o * _silu(jnp.concatenate([z0, z1, z2, z3], axis=1)),)

    a, = _ew(fn, [(o, "row", 0, D_MODEL)] + [(proj, "row", c, BRANCH) for c in Z_COLS],
             [(D_MODEL, MM_DTYPE)], [], T=T, L=L, name="gate_act")
    return a


def gate_act_bwd(da, o, proj, *, L):
    T = o.shape[0]

    def fn(da, o, z0, z1, z2, z3):
        z = jnp.concatenate([z0, z1, z2, z3], axis=1)
        return da * _silu(z), da * o * _silu_grad(z)

    return _ew(fn, [(da, "row", 0, D_MODEL), (o, "row", 0, D_MODEL)] + [(proj, "row", c, BRANCH) for c in Z_COLS],
               [(D_MODEL, F32), (D_MODEL, MM_DTYPE)], [], T=T, L=L, name="gate_act_bwd")


MERGE_TN = 512


def merge_fwd(a, w_br, proj, b_gate):
    T = a.shape[0]
    tm, tn = _row_tile(T), MERGE_TN
    nn = D_MODEL // tn
    lb0 = COL["merge"] // tn

    def body(a_ref, w_ref, l_ref, b_ref, m_ref, br_ref, acc_ref):
        i = pl.program_id(2)
        br = _dot(a_ref[...].astype(MM_DTYPE), w_ref[...].astype(MM_DTYPE))
        br_ref[...] = br
        term = _sigmoid(l_ref[...] + b_ref[...]) * br

        @pl.when(i == 0)
        def _():
            acc_ref[...] = term

        @pl.when(i > 0)
        def _():
            acc_ref[...] += term

        @pl.when(i == 3)
        def _():
            m_ref[...] = acc_ref[...].astype(m_ref.dtype)

    return pl.pallas_call(
        body, out_shape=[jax.ShapeDtypeStruct((T, D_MODEL), MM_DTYPE), jax.ShapeDtypeStruct((T, 4 * D_MODEL), F32)],
        grid=(T // tm, nn, 4),
        in_specs=[pl.BlockSpec((tm, BRANCH), lambda m, n, i: (m, i)),
                  pl.BlockSpec((BRANCH, tn), lambda m, n, i: (i, n)),
                  pl.BlockSpec((tm, tn), lambda m, n, i: (m, lb0 + i * nn + n)),
                  pl.BlockSpec((1, tn), lambda m, n, i: (0, i * nn + n))],
        out_specs=[pl.BlockSpec((tm, tn), lambda m, n, i: (m, n)), pl.BlockSpec((tm, tn), lambda m, n, i: (m, i * nn + n))],
        scratch_shapes=[pltpu.VMEM((tm, tn), F32)],
        compiler_params=_params("parallel", "parallel", "arbitrary"), name="merge_fwd")(a, w_br, proj, b_gate)


def merge_bwd(dmerged, br, proj, b_gate):
    T = dmerged.shape[0]
    tb = ROW_TILE
    lb0 = COL["merge"] // D_MODEL

    def body(dm_ref, br_ref, l_ref, b_ref, dbr_ref, dl_ref, db_ref):
        dm = dm_ref[...]
        gates = _sigmoid(l_ref[...] + b_ref[...])
        dbr_ref[...] = (dm * gates).astype(dbr_ref.dtype)
        dl = dm * br_ref[...] * gates * (1.0 - gates)
        dl_ref[...] = dl.astype(dl_ref.dtype)
        s = jnp.broadcast_to(jnp.sum(dl, axis=0, keepdims=True), db_ref.shape)

        @pl.when(pl.program_id(1) == 0)
        def _():
            db_ref[...] = s

        @pl.when(pl.program_id(1) > 0)
        def _():
            db_ref[...] += s

    wide = pl.BlockSpec((tb, D_MODEL), lambda b, i: (i, b))
    return pl.pallas_call(
        body, out_shape=[jax.ShapeDtypeStruct((T, 4 * D_MODEL), MM_DTYPE)] * 2 + [jax.ShapeDtypeStruct((8, 4 * D_MODEL), F32)],
        grid=(4, T // tb),
        in_specs=[pl.BlockSpec((tb, D_MODEL), lambda b, i: (i, 0)), wide,
                  pl.BlockSpec((tb, D_MODEL), lambda b, i: (i, lb0 + b)), pl.BlockSpec((1, D_MODEL), lambda b, i: (0, b))],
        out_specs=[wide, wide, pl.BlockSpec((8, D_MODEL), lambda b, i: (0, b))],
        compiler_params=_params("parallel", "arbitrary"), name="merge_bwd")(dmerged, br, proj, b_gate)


def branch_dx(dbr, w_br):
    T = dbr.shape[0]
    tm, tk = _row_tile(T), 1024
    nk = D_MODEL // tk
    return _matmul(dbr, w_br, pl.BlockSpec((tm, tk), lambda m, i, k: (m, i * nk + k)),
                   pl.BlockSpec((BRANCH, tk), lambda m, i, k: (i, k)),
                   pl.BlockSpec((tm, BRANCH), lambda m, i, k: (m, i)),
                   jax.ShapeDtypeStruct((T, D_MODEL), F32), (T // tm, 4, nk), tb=True, name="branch_dx")


def branch_dw(a, dbr):
    T = a.shape[0]
    tk, tn = _row_tile(T), 1024
    nn = D_MODEL // tn
    return _matmul(a, dbr, pl.BlockSpec((tk, BRANCH), lambda i, n, k: (k, i)),
                   pl.BlockSpec((tk, tn), lambda i, n, k: (k, i * nn + n)),
                   pl.BlockSpec((BRANCH, tn), lambda i, n, k: (i, n)),
                   jax.ShapeDtypeStruct((D_MODEL, D_MODEL), F32), (4, nn, T // tk), ta=True, name="branch_dw")


def proj_fwd(h, w_in):
    T = h.shape[0]
    tm, tn = _row_tile(T), 1792
    per = W_IN_SHARD // tn
    return _matmul(h, w_in, pl.BlockSpec((tm, D_MODEL), lambda i, j, k: (i, 0)),
                   pl.BlockSpec((None, D_MODEL, tn), lambda i, j, k: (j // per, 0, j % per)),
                   pl.BlockSpec((tm, tn), lambda i, j, k: (i, j)),
                   jax.ShapeDtypeStruct((T, IN_TOTAL), F32), (T // tm, IN_TOTAL // tn, 1), name="proj_fwd")


def proj_dx(dproj, w_in):
    T = dproj.shape[0]
    tm, tk = _row_tile(T), 1792
    per = W_IN_SHARD // tk
    return _matmul(dproj, w_in, pl.BlockSpec((tm, tk), lambda i, j, k: (i, k)),
                   pl.BlockSpec((None, D_MODEL, tk), lambda i, j, k: (k // per, 0, k % per)),
                   pl.BlockSpec((tm, D_MODEL), lambda i, j, k: (i, 0)),
                   jax.ShapeDtypeStruct((T, D_MODEL), F32), (T // tm, 1, IN_TOTAL // tk), tb=True, name="proj_dx")


def proj_dw(h, dproj):
    T = h.shape[0]
    tk, tm, tn = _row_tile(T), 1024, 1792
    per = W_IN_SHARD // tn
    return _matmul(h, dproj, pl.BlockSpec((tk, tm), lambda i, j, k: (k, i)),
                   pl.BlockSpec((tk, tn), lambda i, j, k: (k, j)),
                   pl.BlockSpec((None, tm, tn), lambda i, j, k: (j // per, i, j % per)),
                   jax.ShapeDtypeStruct((N_SHARDS, D_MODEL, W_IN_SHARD), F32),
                   (D_MODEL // tm, IN_TOTAL // tn, T // tk), ta=True, name="proj_dw")


def layer_fwd(X, P, *, L):
    T = X.shape[0]
    N = T - L
    h = prenorm_fwd(X, P["g_pre"], P["scale"], P["shift"], L=L)
    proj = proj_fwd(h, P["w_in"])
    q, k, v = (_heads(proj[:, COL[n]:COL[n] + BRANCH]).astype(MM_DTYPE) for n in ("q", "k", "v"))
    o_lat, lse = attn_fwd(q, k, v, P["bias"], L=L, N=N)
    o_ctx, lse_c = cattn_fwd(q, k, v, L=L, N=N)
    o_na = jnp.concatenate([o_lat, o_ctx], axis=1)
    o_pool, pooled = pool_fwd(proj, P["pool_w"], P["pool_scale"], L=L)
    o_conv = conv_fwd(proj, P["conv_w"], L=L)
    o_ssm, ssm_saved = ssm_fwd(proj, P["ssm_ops"], P["ssm_d"], P["glu_w"], L=L)
    o = jnp.concatenate([_unheads(o_na), o_pool, o_conv, o_ssm], axis=1)
    a = gate_act(o, proj, L=L)
    merged, br = merge_fwd(a, P["w_br"], proj, P["b_gate"])
    y = mm_nn(merged, P["w_o"], out_dtype=F32, name="out_proj")
    Xn = postnorm_fwd(X, y, P["g_post"], P["gate"], L=L)
    saved = dict(X=X, h=h, proj=proj, q=q, k=k, v=v, o_na=o_na, lse=lse, lse_c=lse_c, pooled=pooled,
                 ssm=ssm_saved, o=o, a=a, merged=merged, br=br, y=y)
    return Xn, saved


def layer_bwd(dXn, P, S, bias_vjp, ops_vjp, *, L):
    T = dXn.shape[0]
    N = T - L
    proj = S["proj"]
    dy, cs_post = postnorm_bwd(dXn, S["y"], P["g_post"], P["gate"], L=L)
    dmerged = mm_nt(dy, P["w_o"], out_dtype=F32, name="out_proj_dx")
    d_w_o = mm_tn(S["merged"], dy, out_dtype=F32, name="out_proj_dw", tm=1024, tn=1024)
    dbr, dlogit, d_bgate = merge_bwd(dmerged, S["br"], proj, P["b_gate"])
    da = branch_dx(dbr, P["w_br"])
    d_w_br = branch_dw(S["a"], dbr)
    do, dz = gate_act_bwd(da, S["o"], proj, L=L)
    do_na = _heads(do[:, 0:BRANCH])
    dq_lat, dk, dv, dbias = attn_bwd(S["q"], S["k"], S["v"], P["bias"], S["o_na"][:, :L], do_na[:, :L], S["lse"], L=L, N=N)
    dq_c, dk_c, dv_c = cattn_bwd(S["q"], S["k"], S["v"], S["o_na"][:, L:], do_na[:, L:], S["lse_c"], L=L, N=N)
    dq = jnp.concatenate([dq_lat, dq_c], axis=1)
    dk = dk.at[:, L:].add(dk_c)
    dv = dv.at[:, L:].add(dv_c)
    d_rpb, = bias_vjp(dbias)
    dpool_u, d_pool_w, d_pool_scale = pool_bwd(do[:, BRANCH:2 * BRANCH], S["pooled"], P["pool_w"], P["pool_scale"], L=L)
    dcx, dcb, dcc, d_conv_w = conv_bwd(do[:, 2 * BRANCH:3 * BRANCH], proj, P["conv_w"], L=L)
    dssm_u, d_ops, d_ssm_d, d_glu = ssm_bwd(do[:, 3 * BRANCH:], proj, P["ssm_ops"], P["ssm_d"], P["glu_w"], S["ssm"], L=L)
    d_ssm = ops_vjp(d_ops)
    z = lambda i: dz[:, i * BRANCH:(i + 1) * BRANCH]
    cast = lambda t: t.astype(MM_DTYPE)
    dproj = jnp.concatenate([cast(_unheads(dq)), cast(_unheads(dk)), cast(_unheads(dv)), z(0), dpool_u, z(1),
                             dcx, dcb, dcc, z(2), cast(dssm_u), z(3), dlogit], axis=1)
    dh = proj_dx(dproj, P["w_in"])
    d_w_in = proj_dw(S["h"], dproj)
    dX, cs_h, cs_hx = prenorm_bwd(dh, S["X"], P["g_pre"], P["scale"], dXn, L=L)
    g_pre, g_post = P["g_pre"], P["g_post"]
    d_shift = cs_h
    d_scale = cs_hx * g_pre
    d_gate = cs_post * g_post
    d_g_pre = jnp.sum(cs_hx * (1.0 + P["scale"]), axis=0)[0]
    d_g_post = jnp.sum(cs_post * P["gate"], axis=0)[0]
    grads = dict(w_in=d_w_in, w_br=d_w_br, w_o=d_w_o, glu_w=d_glu, conv_w=d_conv_w[0:3], pool_w=d_pool_w,
                 pool_scale=d_pool_scale[0], b_gate=d_bgate[0], na_rpb=d_rpb, ssm=d_ssm, ssm_d=d_ssm_d,
                 g_pre=d_g_pre, g_post=d_g_post,
                 mod=jnp.concatenate([d_shift, d_scale, d_gate], axis=-1)[:, 0])
    return dX, grads


def local_step(x, ctx, target, mods, layers):
    L = x.shape[0]
    rows = L // GRID_W
    X = jnp.concatenate([x, ctx], axis=0)
    saved, Ps, vjps = [], [], []
    for i, lay in enumerate(layers):
        P = dict(lay)
        m = mods[i][:, None, :]
        P["shift"], P["scale"], P["gate"] = m[..., :D_MODEL], m[..., D_MODEL:2 * D_MODEL], m[..., 2 * D_MODEL:]
        P["bias"], bias_vjp = jax.vjp(lambda r: build_bias(r, rows), lay["na_rpb"])
        P["ssm_ops"], ops_vjp = jax.vjp(ssm_operators, *lay["ssm_params"])
        X, S = layer_fwd(X, P, L=L)
        saved.append(S)
        Ps.append(P)
        vjps.append((bias_vjp, ops_vjp))
    loss, dX = loss_and_grad(X, target, L=L)
    grads = [None] * len(layers)
    for i in reversed(range(len(layers))):
        dX, grads[i] = layer_bwd(dX, Ps[i], saved[i], *vjps[i], L=L)
    return loss, dX[:L], grads


MESH_ID = pl.DeviceIdType.MESH
HBM_SPEC = pl.BlockSpec(memory_space=pltpu.HBM)


def _place():
    return lax.axis_index("x"), lax.axis_index("y"), lax.axis_index("c")


def _other_chips(x, y):
    return [(1 - x, y), (x, 1 - y), (1 - x, 1 - y)]


def _remote(src, dst, send_sem, recv_sem, to):
    return pltpu.make_async_remote_copy(src_ref=src, dst_ref=dst, send_sem=send_sem, recv_sem=recv_sem,
                                        device_id=to, device_id_type=MESH_ID)


DMA_CHUNK_BYTES = 1 << 20


def _row_pieces(rows, row_bytes):
    n = max(1, min(64, rows * row_bytes // DMA_CHUNK_BYTES))
    while n > 1 and (rows % n or (rows // n) % 16):
        n -= 1
    return [(r * (rows // n), rows // n) for r in range(n)]


def _row_bytes(ref):
    return math.prod(ref.shape[1:]) * jnp.dtype(ref.dtype).itemsize


def _start_in_pieces(make, src, dst):
    for r0, nr in _row_pieces(src.shape[0], _row_bytes(src)):
        make(src.at[pl.ds(r0, nr)], dst.at[pl.ds(r0, nr)]).start()


def all_gather8(blocks, name, *, shard_major=False):
    K = len(blocks)

    def body(*refs):
        ins, outs = refs[:K], refs[K:2 * K]
        send_sems, recv_sems, local_sems = refs[2 * K:]
        x, y, c = _place()
        me, sibling = (x, y, c), (x, y, 1 - c)
        chips = _other_chips(x, y)

        def slot(k, block):
            px, py, pc = block
            if shard_major:
                h = ins[k].shape[0]
                return outs[k].at[2 * px + py, pl.ds(pl.multiple_of(pc * h, 16), h)]
            return outs[k].at[4 * px + 2 * py + pc]

        def copy(k, j, to):
            return lambda s, d: _remote(s, d, send_sems.at[k, j], recv_sems.at[k, j], to)

        for k in range(K):
            _start_in_pieces(lambda s, d, k=k: pltpu.make_async_copy(s, d, local_sems.at[k]), ins[k], slot(k, me))
            _start_in_pieces(copy(k, 0, sibling), ins[k], slot(k, me))
            for j, chip in enumerate(chips):
                _start_in_pieces(copy(k, 1 + j, (*chip, c)), ins[k], slot(k, me))
        for j, chip in enumerate(chips):
            for k in range(K):
                got = slot(k, (*chip, c))
                copy(k, 1 + j, me)(got, got).wait_recv()
                _start_in_pieces(copy(k, 4 + j, sibling), got, got)
        for k in range(K):
            sib = slot(k, sibling)
            copy(k, 0, me)(sib, sib).wait_recv()
            for j, chip in enumerate(chips):
                got = slot(k, (*chip, 1 - c))
                copy(k, 4 + j, me)(got, got).wait_recv()
        for k in range(K):
            own = slot(k, me)
            for j in range(4):
                copy(k, j, me)(ins[k], own).wait_send()
            for j, chip in enumerate(chips):
                got = slot(k, (*chip, c))
                copy(k, 4 + j, me)(got, got).wait_send()
            pltpu.make_async_copy(ins[k], own, local_sems.at[k]).wait()

    def out_shape(b):
        if shard_major:
            return jax.ShapeDtypeStruct((N_SHARDS, 2 * b.shape[0]) + b.shape[1:], b.dtype)
        return jax.ShapeDtypeStruct((8,) + b.shape, b.dtype)

    return pl.pallas_call(
        body, out_shape=[out_shape(b) for b in blocks], in_specs=[HBM_SPEC] * K, out_specs=[HBM_SPEC] * K,
        scratch_shapes=[pltpu.SemaphoreType.DMA((K, 7)), pltpu.SemaphoreType.DMA((K, 7)), pltpu.SemaphoreType.DMA((K,))],
        name=name)(*blocks)


def sibling_split(grads, name):
    K = len(grads)

    def body(*refs):
        ins, mine, theirs = refs[:K], refs[K:2 * K], refs[2 * K:3 * K]
        send_sems, recv_sems, local_sems = refs[3 * K:]
        x, y, c = _place()
        for k in range(K):
            half = ins[k].shape[1] // 2
            for q in range(ins[k].shape[0]):
                keep = ins[k].at[q, pl.ds(pl.multiple_of(c * half, 16), half)]
                give = ins[k].at[q, pl.ds(pl.multiple_of((1 - c) * half, 16), half)]
                _start_in_pieces(lambda s, d, k=k: pltpu.make_async_copy(s, d, local_sems.at[k]), keep, mine[k].at[q])
                _start_in_pieces(lambda s, d, k=k: _remote(s, d, send_sems.at[k], recv_sems.at[k], (x, y, 1 - c)),
                                 give, theirs[k].at[q])
        for k in range(K):
            _remote(mine[k], theirs[k], send_sems.at[k], recv_sems.at[k], (x, y, 1 - c)).wait()
            pltpu.make_async_copy(mine[k], mine[k], local_sems.at[k]).wait()

    shapes = [jax.ShapeDtypeStruct((g.shape[0], g.shape[1] // 2, g.shape[2]), g.dtype) for g in grads]
    res = pl.pallas_call(
        body, out_shape=shapes + shapes, in_specs=[HBM_SPEC] * K, out_specs=[HBM_SPEC] * (2 * K),
        scratch_shapes=[pltpu.SemaphoreType.DMA((K,)), pltpu.SemaphoreType.DMA((K,)), pltpu.SemaphoreType.DMA((K,))],
        name=name)(*grads)
    return res[:K], res[K:]


def chip_exchange(parts, name):
    K = len(parts)

    def body(*refs):
        ins, outs = refs[:K], refs[K:2 * K]
        send_sems, recv_sems, local_sems = refs[2 * K:]
        x, y, c = _place()
        p = 2 * x + y
        chips = _other_chips(x, y)
        waits = []
        for k in range(K):
            _start_in_pieces(lambda s, d, k=k: pltpu.make_async_copy(s, d, local_sems.at[k]), ins[k].at[p], outs[k].at[p])
            for j, (cx, cy) in enumerate(chips):
                q = 2 * cx + cy
                make = lambda s, d, k=k, j=j, to=(cx, cy, c): _remote(s, d, send_sems.at[k, j], recv_sems.at[k, j], to)
                _start_in_pieces(make, ins[k].at[q], outs[k].at[p])
                waits.append(make(ins[k].at[q], outs[k].at[q]))
        for cp in waits:
            cp.wait_recv()
        for cp in waits:
            cp.wait_send()
        for k in range(K):
            pltpu.make_async_copy(ins[k].at[p], outs[k].at[p], local_sems.at[k]).wait()

    return pl.pallas_call(
        body, out_shape=[jax.ShapeDtypeStruct(a.shape, a.dtype) for a in parts],
        in_specs=[HBM_SPEC] * K, out_specs=[HBM_SPEC] * K,
        scratch_shapes=[pltpu.SemaphoreType.DMA((K, 3)), pltpu.SemaphoreType.DMA((K, 3)), pltpu.SemaphoreType.DMA((K,))],
        name=name)(*parts)


def sibling_join(halves, name):
    K = len(halves)

    def body(*refs):
        ins, outs = refs[:K], refs[K:2 * K]
        send_sems, recv_sems, local_sems = refs[2 * K:]
        x, y, c = _place()
        for k in range(K):
            h = ins[k].shape[0]
            own = outs[k].at[pl.ds(pl.multiple_of(c * h, 16), h)]
            _start_in_pieces(lambda s, d, k=k: pltpu.make_async_copy(s, d, local_sems.at[k]), ins[k], own)
            _start_in_pieces(lambda s, d, k=k: _remote(s, d, send_sems.at[k], recv_sems.at[k], (x, y, 1 - c)), ins[k], own)
        for k in range(K):
            h = ins[k].shape[0]
            own = outs[k].at[pl.ds(pl.multiple_of(c * h, 16), h)]
            other = outs[k].at[pl.ds(pl.multiple_of((1 - c) * h, 16), h)]
            cp = _remote(ins[k], other, send_sems.at[k], recv_sems.at[k], (x, y, 1 - c))
            cp.wait_recv()
            cp.wait_send()
            pltpu.make_async_copy(ins[k], own, local_sems.at[k]).wait()

    return pl.pallas_call(
        body, out_shape=[jax.ShapeDtypeStruct((2 * a.shape[0],) + a.shape[1:], a.dtype) for a in halves],
        in_specs=[HBM_SPEC] * K, out_specs=[HBM_SPEC] * K,
        scratch_shapes=[pltpu.SemaphoreType.DMA((K,)), pltpu.SemaphoreType.DMA((K,)), pltpu.SemaphoreType.DMA((K,))],
        name=name)(*halves)


def _slab_rows(n):
    return max(8, min(256, (1 << 18) // n // 8 * 8))


def add_pair(a, b, *, out_dtype, name):
    S, h, n = a.shape
    tr = _pick(h, tuple(t for t in (256, 128, 64, 32, 16, 8) if t <= _slab_rows(n)))
    spec = pl.BlockSpec((None, tr, n), lambda s, r: (s, r, 0))

    def body(a_ref, b_ref, o_ref):
        o_ref[...] = (a_ref[...] + b_ref[...]).astype(o_ref.dtype)

    return pl.pallas_call(body, out_shape=jax.ShapeDtypeStruct(a.shape, out_dtype), grid=(S, h // tr),
                          in_specs=[spec, spec], out_specs=spec, compiler_params=_params("parallel", "parallel"),
                          name=name)(a, b)


def sum_slabs(a, *, name):
    S, h, n = a.shape
    tr = _pick(h, tuple(t for t in (256, 128, 64, 32, 16, 8) if t <= _slab_rows(n)))

    def body(a_ref, o_ref):
        acc = a_ref[0].astype(F32)
        for s in range(1, S):
            acc = acc + a_ref[s].astype(F32)
        o_ref[...] = acc

    return pl.pallas_call(body, out_shape=jax.ShapeDtypeStruct((h, n), F32), grid=(h // tr,),
                          in_specs=[pl.BlockSpec((S, tr, n), lambda r: (0, r, 0))],
                          out_specs=pl.BlockSpec((tr, n), lambda r: (r, 0)), compiler_params=_params("parallel"),
                          name=name)(a)


def _adam_math(w, g, m, v):
    m = ADAM_B1 * m + (1.0 - ADAM_B1) * g
    v = ADAM_B2 * v + (1.0 - ADAM_B2) * (g * g)
    m_hat = m / (1.0 - ADAM_B1 ** ADAM_STEP)
    v_hat = v / (1.0 - ADAM_B2 ** ADAM_STEP)
    delta = -ADAM_LR * (m_hat / (jnp.sqrt(v_hat) + ADAM_EPS) + ADAM_WD * w)
    return delta, m, v


def adamw(w, g, m, v, *, name):
    R, n = w.shape
    tr = _pick(R, tuple(t for t in (256, 128, 64, 32, 16, 8) if t <= _slab_rows(n)))
    spec = pl.BlockSpec((tr, n), lambda r: (r, 0))

    def body(w_ref, g_ref, m_ref, v_ref, d_ref, nm_ref, nv_ref):
        d, nm, nv = _adam_math(w_ref[...], g_ref[...], m_ref[...], v_ref[...])
        d_ref[...] = d
        nm_ref[...] = nm
        nv_ref[...] = nv

    return pl.pallas_call(body, out_shape=[jax.ShapeDtypeStruct(w.shape, F32)] * 3, grid=(R // tr,),
                          in_specs=[spec] * 4, out_specs=[spec] * 3, compiler_params=_params("parallel"),
                          name=name)(w, g, m, v)


MOD_ROWS = 16


def mod_fwd(cact_in, w_mod, b_mod):
    nl, _, cols = w_mod.shape
    tn = 512

    def body(c_ref, w_ref, b_ref, o_ref):
        o_ref[...] = _dot(_silu(c_ref[...]).astype(MM_DTYPE), w_ref[...].astype(MM_DTYPE)) + b_ref[...]

    return pl.pallas_call(
        body, out_shape=jax.ShapeDtypeStruct((nl, MOD_ROWS, cols), F32), grid=(nl, cols // tn),
        in_specs=[pl.BlockSpec((MOD_ROWS, D_MODEL), lambda i, j: (0, 0)),
                  pl.BlockSpec((None, D_MODEL, tn), lambda i, j: (i, 0, j)),
                  pl.BlockSpec((None, 1, tn), lambda i, j: (i, 0, j))],
        out_specs=pl.BlockSpec((None, MOD_ROWS, tn), lambda i, j: (i, 0, j)),
        compiler_params=_params("parallel", "parallel"), name="mod_fwd")(cact_in, w_mod, b_mod)


def mod_update(c_rows, dmod, w, m, v):
    nl, _, cols = w.shape
    tr, tn = 256, 512
    wspec = pl.BlockSpec((None, tr, tn), lambda i, r, j: (i, r, j))

    def body(c_ref, d_ref, w_ref, m_ref, v_ref, g_ref, dl_ref, nm_ref, nv_ref):
        g = _dot_tn(_silu(c_ref[...]).astype(MM_DTYPE), d_ref[...].astype(MM_DTYPE))
        g_ref[...] = g
        dl, nm, nv = _adam_math(w_ref[...], g, m_ref[...], v_ref[...])
        dl_ref[...] = dl
        nm_ref[...] = nm
        nv_ref[...] = nv

    return pl.pallas_call(
        body, out_shape=[jax.ShapeDtypeStruct(w.shape, F32)] * 4, grid=(nl, D_MODEL // tr, cols // tn),
        in_specs=[pl.BlockSpec((MOD_ROWS, tr), lambda i, r, j: (0, r)),
                  pl.BlockSpec((None, MOD_ROWS, tn), lambda i, r, j: (i, 0, j)), wspec, wspec, wspec],
        out_specs=[wspec] * 4, compiler_params=_params("parallel", "parallel", "parallel"),
        name="mod_update")(c_rows, dmod, w, m, v)


def cctx_partial(dmod_ctx, w_mod, c_ctx):
    nl, _, cols = w_mod.shape
    tk = 512
    per = cols // tk
    part = _matmul(dmod_ctx, w_mod, pl.BlockSpec((8, tk), lambda i, j, k: (0, k)),
                   pl.BlockSpec((None, D_MODEL, tk), lambda i, j, k: (k // per, 0, k % per)),
                   pl.BlockSpec((8, D_MODEL), lambda i, j, k: (0, 0)),
                   jax.ShapeDtypeStruct((8, D_MODEL), F32), (1, 1, nl * per), tb=True, name="cctx_partial")

    def body(p_ref, c_ref, o_ref):
        o_ref[...] = 0.5 * jnp.sum(p_ref[...], axis=0, keepdims=True) * _silu_grad(c_ref[...])

    return pl.pallas_call(body, out_shape=jax.ShapeDtypeStruct((1, D_MODEL), F32), name="cctx_scale")(part, c_ctx)


WEIGHT_NAMES = ("c_ctx", "w_mod", "b_mod", "g_pre", "g_post", "w_in", "b_gate", "na_rpb", "pool_w", "pool_scale",
                "conv_w", "ssm_a_re", "ssm_a_im", "ssm_log_dt", "ssm_b_re", "ssm_b_im", "ssm_c_re", "ssm_c_im",
                "ssm_d", "glu_w", "w_br", "w_o")
SSM_NAMES = ("ssm_a_re", "ssm_a_im", "ssm_log_dt", "ssm_b_re", "ssm_b_im", "ssm_c_re", "ssm_c_im")
SMALL_NAMES = ("c_ctx", "b_mod", "g_pre", "g_post", "b_gate", "na_rpb", "pool_w", "pool_scale") + SSM_NAMES + ("ssm_d",)
BIG_NAMES = ("w_in", "glu_w", "w_br", "w_o")
FLAT_COLS = 1024


def _flat(parts):
    v = jnp.concatenate([p.reshape(-1) for p in parts])
    pad = -v.shape[0] % (64 * FLAT_COLS)
    return jnp.pad(v, (0, pad)).reshape(-1, FLAT_COLS)


def _unflat(flat, shapes):
    v = flat.reshape(-1)
    out, off = [], 0
    for s in shapes:
        n = math.prod(s)
        out.append(v[off:off + n].reshape(s))
        off += n
    return out


def kernel(x, c, ctx, c_ctx, w_mod, b_mod, g_pre, g_post, w_in, b_gate, na_rpb, pool_w, pool_scale, conv_w, ssm_a_re, ssm_a_im, ssm_log_dt, ssm_b_re, ssm_b_im, ssm_c_re, ssm_c_im, ssm_d, glu_w, w_br, w_o, loss_target, m_c_ctx, m_w_mod, m_b_mod, m_g_pre, m_g_post, m_w_in, m_b_gate, m_na_rpb, m_pool_w, m_pool_scale, m_conv_w, m_ssm_a_re, m_ssm_a_im, m_ssm_log_dt, m_ssm_b_re, m_ssm_b_im, m_ssm_c_re, m_ssm_c_im, m_ssm_d, m_glu_w, m_w_br, m_w_o, v_c_ctx, v_w_mod, v_b_mod, v_g_pre, v_g_post, v_w_in, v_b_gate, v_na_rpb, v_pool_w, v_pool_scale, v_conv_w, v_ssm_a_re, v_ssm_a_im, v_ssm_log_dt, v_ssm_b_re, v_ssm_b_im, v_ssm_c_re, v_ssm_c_im, v_ssm_d, v_glu_w, v_w_br, v_w_o):
    W = dict(c_ctx=c_ctx, w_mod=w_mod, b_mod=b_mod, g_pre=g_pre, g_post=g_post, w_in=w_in, b_gate=b_gate,
             na_rpb=na_rpb, pool_w=pool_w, pool_scale=pool_scale, conv_w=conv_w, ssm_a_re=ssm_a_re, ssm_a_im=ssm_a_im,
             ssm_log_dt=ssm_log_dt, ssm_b_re=ssm_b_re, ssm_b_im=ssm_b_im, ssm_c_re=ssm_c_re, ssm_c_im=ssm_c_im,
             ssm_d=ssm_d, glu_w=glu_w, w_br=w_br, w_o=w_o)
    M = dict(c_ctx=m_c_ctx, w_mod=m_w_mod, b_mod=m_b_mod, g_pre=m_g_pre, g_post=m_g_post, w_in=m_w_in, b_gate=m_b_gate,
             na_rpb=m_na_rpb, pool_w=m_pool_w, pool_scale=m_pool_scale, conv_w=m_conv_w, ssm_a_re=m_ssm_a_re,
             ssm_a_im=m_ssm_a_im, ssm_log_dt=m_ssm_log_dt, ssm_b_re=m_ssm_b_re, ssm_b_im=m_ssm_b_im,
             ssm_c_re=m_ssm_c_re, ssm_c_im=m_ssm_c_im, ssm_d=m_ssm_d, glu_w=m_glu_w, w_br=m_w_br, w_o=m_w_o)
    V = dict(c_ctx=v_c_ctx, w_mod=v_w_mod, b_mod=v_b_mod, g_pre=v_g_pre, g_post=v_g_post, w_in=v_w_in, b_gate=v_b_gate,
             na_rpb=v_na_rpb, pool_w=v_pool_w, pool_scale=v_pool_scale, conv_w=v_conv_w, ssm_a_re=v_ssm_a_re,
             ssm_a_im=v_ssm_a_im, ssm_log_dt=v_ssm_log_dt, ssm_b_re=v_ssm_b_re, ssm_b_im=v_ssm_b_im,
             ssm_c_re=v_ssm_c_re, ssm_c_im=v_ssm_c_im, ssm_d=v_ssm_d, glu_w=v_glu_w, w_br=v_w_br, w_o=v_w_o)
    nl = w_in.shape[0]
    xi, yi, ci = _place()
    chip = 2 * xi + yi
    example = 4 * xi + 2 * yi + ci
    mod_cols = w_mod.shape[2]

    def my_half(a):
        half = a.shape[0] // 2
        return lax.dynamic_slice_in_dim(a, ci * half, half, axis=0).astype(MM_DTYPE)

    gathered = []
    for i in range(nl):
        g = all_gather8([my_half(W[n][i]) for n in BIG_NAMES], name="gather_weights", shard_major=True)
        gathered.append(dict(zip(BIG_NAMES, g)))

    c8 = jnp.pad(c, ((0, 7), (0, 0)))
    c_all, = all_gather8([c8], name="gather_c")
    c_rows = jnp.concatenate([c_all[:, 0], jnp.broadcast_to(c_ctx[None], (8, D_MODEL))], axis=0)
    b_cols = lax.dynamic_slice_in_dim(b_mod, chip * mod_cols, mod_cols, axis=1)[:, None]
    mod_part = mod_fwd(c_rows, w_mod, b_cols)
    conv_part = jnp.pad(conv_w.reshape(nl * 3, -1), ((0, 16 - nl * 3), (0, 0)))
    parts, conv_all = all_gather8([mod_part.reshape(nl * MOD_ROWS, mod_cols), conv_part], name="gather_mod")
    mod_full = jnp.concatenate([parts[2 * p].reshape(nl, MOD_ROWS, mod_cols) for p in range(N_SHARDS)], axis=-1)
    conv_full = jnp.concatenate([conv_all[2 * p][:nl * 3].reshape(nl, 3, -1) for p in range(N_SHARDS)], axis=-1)
    own = lax.dynamic_index_in_dim(mod_full, example, axis=1, keepdims=False)
    mods = [jnp.stack([own[i], mod_full[i, 8]]) for i in range(nl)]

    layers = []
    for i in range(nl):
        gw = gathered[i]
        layers.append(dict(
            w_in=gw["w_in"], w_br=gw["w_br"].reshape(D_MODEL, D_MODEL), w_o=gw["w_o"].reshape(D_MODEL, D_MODEL),
            glu_w=gw["glu_w"].transpose(1, 0, 2).reshape(BRANCH, 2 * BRANCH),
            conv_w=jnp.pad(conv_full[i], ((0, 5), (0, 0))), pool_w=pool_w[i], pool_scale=pool_scale[i][None],
            b_gate=b_gate[i][None], g_pre=g_pre[i][None], g_post=g_post[i][None], na_rpb=na_rpb[i],
            ssm_d=ssm_d[i][None], ssm_params=tuple(W[n][i] for n in SSM_NAMES)))

    loss_local, grad_x, grads = local_step(x[0], ctx[0], loss_target[0], mods, layers)
    loss = lax.psum(loss_local, ("x", "y", "c"))

    dmod_local = jnp.stack([g["mod"] for g in grads])
    dmod_all, = all_gather8([jnp.pad(dmod_local.reshape(nl * 2, -1), ((0, 8 - nl * 2), (0, 0)))], name="gather_dmod")
    dmod_all = dmod_all[:, :nl * 2].reshape(8, nl, 2, 3 * D_MODEL)
    dmod_rows = jnp.concatenate([dmod_all[:, :, 0], dmod_all[:, :, 1]], axis=0).transpose(1, 0, 2)
    dmod_cols = lax.dynamic_slice_in_dim(dmod_rows, chip * mod_cols, mod_cols, axis=2)
    g_w_mod, d_w_mod, nm_w_mod, nv_w_mod = mod_update(c_rows, dmod_cols, w_mod, m_w_mod, v_w_mod)
    dctx_cols = dmod_cols[:, 8:].transpose(1, 0, 2).reshape(8, nl * mod_cols)
    g_cctx_part = cctx_partial(dctx_cols, w_mod, c_ctx[None])[0]

    def small_grad(n):
        if n == "c_ctx":
            return g_cctx_part
        if n == "b_mod":
            return jnp.stack([g["mod"][0] + g["mod"][1] for g in grads])
        if n in SSM_NAMES:
            return jnp.stack([g["ssm"][SSM_NAMES.index(n)] for g in grads])
        return jnp.stack([g[n] for g in grads])

    conv_grad_full = jnp.stack([g["conv_w"] for g in grads])
    flat_g = _flat([small_grad(n) for n in SMALL_NAMES] + [conv_grad_full])
    flat_all, = all_gather8([flat_g], name="gather_small_grads")
    flat_sum = sum_slabs(flat_all, name="sum_small_grads")
    small_shapes = [W[n].shape for n in SMALL_NAMES]
    small_g = _unflat(flat_sum, small_shapes + [conv_grad_full.shape])
    conv_g = lax.dynamic_slice_in_dim(small_g[-1], chip * conv_w.shape[2], conv_w.shape[2], axis=2)
    adam_names = SMALL_NAMES + ("conv_w",)
    adam_shapes = small_shapes + [conv_w.shape]
    g_list = small_g[:-1] + [conv_g]
    upd = adamw(_flat([W[n] for n in adam_names]), _flat(g_list), _flat([M[n] for n in adam_names]),
                _flat([V[n] for n in adam_names]), name="adamw_small")
    G = dict(zip(adam_names, g_list))
    DL, NM, NV = (dict(zip(adam_names, _unflat(u, adam_shapes))) for u in upd)
    G["w_mod"], DL["w_mod"], NM["w_mod"], NV["w_mod"] = g_w_mod, d_w_mod, nm_w_mod, nv_w_mod

    big = {n: [] for n in BIG_NAMES}
    for i in range(nl):
        g = grads[i]
        local = [g["w_in"], g["glu_w"].reshape(BRANCH, N_SHARDS, -1).transpose(1, 0, 2),
                 g["w_br"].reshape(N_SHARDS, BRANCH, D_MODEL), g["w_o"].reshape(N_SHARDS, BRANCH, D_MODEL)]
        mine, theirs = sibling_split(local, name="grad_split")
        partial = [add_pair(a, b, out_dtype=MM_DTYPE, name=f"grad_pair_{n}") for n, a, b in zip(BIG_NAMES, mine, theirs)]
        arrived = chip_exchange(partial, name="grad_exchange")
        halves = [sum_slabs(a, name=f"grad_sum_{n}") for n, a in zip(BIG_NAMES, arrived)]
        full = sibling_join(halves, name="grad_join")
        for n, a in zip(BIG_NAMES, full):
            big[n].append(a)
    for n in BIG_NAMES:
        g = jnp.stack(big[n])
        rows = g.shape[0] * g.shape[1]
        d, nm, nv = adamw(W[n].reshape(rows, -1), g.reshape(rows, -1), M[n].reshape(rows, -1), V[n].reshape(rows, -1),
                          name=f"adamw_{n}")
        G[n], DL[n], NM[n], NV[n] = g, d.reshape(g.shape), nm.reshape(g.shape), nv.reshape(g.shape)

    out = [loss, grad_x[None]]
    for group in (G, DL, NM, NV):
        out += [group[n].reshape(W[n].shape) for n in WEIGHT_NAMES]
    return tuple(out)
```

```python
import functools
import math

import numpy as np
import jax
import jax.numpy as jnp
from jax import lax
from jax.experimental import pallas as pl
from jax.experimental.pallas import tpu as pltpu

F32 = jnp.float32
BF16 = jnp.bfloat16
MM_DTYPE = jnp.bfloat16

D_MODEL = 2048
BRANCH = 512
N_HEADS = 8
HEAD_DIM = 64
GRID_W = 64
WIN_ROWS = 8
WIN_COLS = 16
POOL_GROUPS = 4
POOL_DIM = 128
SSM_GROUPS = 32
SSM_GDIM = 16
SSM_STATE = 64
N_STATE = SSM_GROUPS * SSM_STATE
IN_TOTAL = 14336
RMS_EPS = 1e-6
NEG_INF = -1e30
COL = dict(q=0, k=512, v=1024, na_z=1536, pool_u=2048, pool_z=2560, conv_x=3072, conv_b=3584,
           conv_c=4096, conv_z=4608, ssm_u=5120, ssm_z=5632, merge=6144)
N_SHARDS = 4
W_IN_SHARD = IN_TOTAL // N_SHARDS
VMEM_LIMIT_BYTES = 48 * 1024 * 1024
ROW_TILE = 256

ADAM_LR = 0.001
ADAM_B1 = 0.9
ADAM_B2 = 0.999
ADAM_EPS = 1e-08
ADAM_WD = 0.01
ADAM_STEP = 10


def _params(*sem):
    return pltpu.CompilerParams(dimension_semantics=sem, vmem_limit_bytes=VMEM_LIMIT_BYTES)


def _sigmoid(x):
    return 1.0 / (1.0 + jnp.exp(-x))


def _matmul(a, b, a_spec, b_spec, o_spec, out_shape, grid, *, ta=False, tb=False, name):
    nk = grid[-1]
    kaxis = len(grid) - 1
    dims = (((0,) if ta else (1,), (1,) if tb else (0,)), ((), ()))

    def body(a_ref, b_ref, o_ref, *scratch):
        p = lax.dot_general(a_ref[...].astype(MM_DTYPE), b_ref[...].astype(MM_DTYPE), dims,
                            preferred_element_type=F32)
        if nk == 1:
            o_ref[...] = p.astype(o_ref.dtype)
            return
        acc_ref, = scratch
        k = pl.program_id(kaxis)

        @pl.when(k == 0)
        def _():
            acc_ref[...] = p

        @pl.when(k > 0)
        def _():
            acc_ref[...] += p

        @pl.when(k == nk - 1)
        def _():
            o_ref[...] = acc_ref[...].astype(o_ref.dtype)

    oblock = tuple(s for s in o_spec.block_shape if s is not None)
    scratch = [] if nk == 1 else [pltpu.VMEM(oblock, F32)]
    sem = ("parallel",) * (len(grid) - 1) + ("arbitrary",)
    return pl.pallas_call(body, out_shape=out_shape, grid=grid, in_specs=[a_spec, b_spec],
                          out_specs=o_spec, scratch_shapes=scratch, compiler_params=_params(*sem),
                          name=name)(a, b)


def _pick(n, cands):
    for c in cands:
        if n % c == 0:
            return c
    raise ValueError(f"no tile for {n}")


def _row_tile(T):
    return _pick(T, (544, 512, 256, 128))


def mm_nn(a, b, *, out_dtype, name, tn=512, a_rows=None, o_rows=None, a_cols=None):
    M = a.shape[0]
    c0, K = a_cols or (0, a.shape[1])
    N = b.shape[1]
    tm = ROW_TILE if (a_rows or o_rows) else _row_tile(M)
    tn = min(tn, N)
    tk = K if K <= 2048 else _pick(K, (2048, 1024, 512))
    kb0 = c0 // tk
    ar = a_rows or (lambda i: i)
    orr = o_rows or (lambda i: i)
    return _matmul(a, b, pl.BlockSpec((tm, tk), lambda i, j, k: (ar(i), kb0 + k)),
                   pl.BlockSpec((tk, tn), lambda i, j, k: (k, j)),
                   pl.BlockSpec((tm, tn), lambda i, j, k: (orr(i), j)),
                   jax.ShapeDtypeStruct((M, N), out_dtype), (M // tm, N // tn, K // tk), name=name)


def mm_nt(a, b, *, out_dtype, name, a_rows=None, o_rows=None):
    M, K = a.shape
    N = b.shape[0]
    tm = ROW_TILE if (a_rows or o_rows) else _row_tile(M)
    tn = min(N, 2048)
    tk = K if K <= 1024 else _pick(K, (1024, 512))
    ar = a_rows or (lambda i: i)
    orr = o_rows or (lambda i: i)
    return _matmul(a, b, pl.BlockSpec((tm, tk), lambda i, j, k: (ar(i), k)),
                   pl.BlockSpec((tn, tk), lambda i, j, k: (j, k)),
                   pl.BlockSpec((tm, tn), lambda i, j, k: (orr(i), j)),
                   jax.ShapeDtypeStruct((M, N), out_dtype), (M // tm, N // tn, K // tk), tb=True, name=name)


def mm_tn(a, b, *, out_dtype, name, a_rows=None, b_rows=None, tm=512, tn=1024, a_cols=None):
    K = a.shape[0]
    c0, M = a_cols or (0, a.shape[1])
    N = b.shape[1]
    tk = ROW_TILE if (a_rows or b_rows) else _row_tile(K)
    tm = min(tm, M)
    tn = min(tn, N)
    mb0 = c0 // tm
    ar = a_rows or (lambda k: k)
    br = b_rows or (lambda k: k)
    return _matmul(a, b, pl.BlockSpec((tk, tm), lambda i, j, k: (ar(k), mb0 + i)),
                   pl.BlockSpec((tk, tn), lambda i, j, k: (br(k), j)),
                   pl.BlockSpec((tm, tn), lambda i, j, k: (i, j)),
                   jax.ShapeDtypeStruct((M, N), out_dtype), (M // tm, N // tn, K // tk), ta=True, name=name)


def _ew(fn, ins, outs, colsums, *, T, L, name):
    tb = ROW_TILE
    nlat = L // tb
    seg = lambda i: jnp.where(i >= nlat, 1, 0)
    in_specs, arrays = [], []
    for arr, kind, cb, width in ins:
        arrays.append(arr)
        if kind == "row":
            in_specs.append(pl.BlockSpec((tb, width), lambda i, cb=cb: (i, cb)))
        elif kind == "bcast":
            in_specs.append(pl.BlockSpec((1, width), lambda i, cb=cb: (0, cb)))
        else:
            in_specs.append(pl.BlockSpec((None, 1, width), lambda i, cb=cb: (seg(i), 0, cb)))
    out_specs = [pl.BlockSpec((tb, w), lambda i: (i, 0)) for w, _ in outs]
    out_shapes = [jax.ShapeDtypeStruct((T, w), dt) for w, dt in outs]
    out_specs += [pl.BlockSpec((None, 1, w), lambda i: (seg(i), 0, 0)) for w in colsums]
    out_shapes += [jax.ShapeDtypeStruct((2, 1, w), F32) for w in colsums]
    n_in, n_out = len(ins), len(outs)

    def body(*refs):
        i = pl.program_id(0)
        res = fn(*[r[...] for r in refs[:n_in]])
        for r, v in zip(refs[n_in:n_in + n_out], res[:n_out]):
            r[...] = v.astype(r.dtype)
        first = (i == 0) | (i == nlat)
        for r, v in zip(refs[n_in + n_out:], res[n_out:]):
            s = jnp.sum(v, axis=0, keepdims=True)

            @pl.when(first)
            def _(r=r, s=s):
                r[...] = s

            @pl.when(jnp.logical_not(first))
            def _(r=r, s=s):
                r[...] += s

    res = pl.pallas_call(body, out_shape=out_shapes, grid=(T // tb,), in_specs=in_specs,
                         out_specs=out_specs, compiler_params=_params("arbitrary"), name=name)(*arrays)
    return res


def _rms(x):
    return lax.rsqrt(jnp.mean(x * x, axis=-1, keepdims=True) + RMS_EPS)


def prenorm_fwd(X, g, scale, shift, *, L):
    T = X.shape[0]

    def fn(x, g, sc, sh):
        return ((x * _rms(x)) * (g * (1.0 + sc)) + sh,)

    h, = _ew(fn, [(X, "row", 0, D_MODEL), (g, "bcast", 0, D_MODEL), (scale, "seg", 0, D_MODEL),
                  (shift, "seg", 0, D_MODEL)], [(D_MODEL, MM_DTYPE)], [], T=T, L=L, name="prenorm_fwd")
    return h


def prenorm_bwd(dh, X, g, scale, dres, *, L):
    T = X.shape[0]

    def fn(dh, x, g, sc, dres):
        r = _rms(x)
        xn = x * r
        dxn = dh * (g * (1.0 + sc))
        dx = r * (dxn - xn * jnp.mean(dxn * xn, axis=-1, keepdims=True))
        return dres + dx, dh, dh * xn

    return _ew(fn, [(dh, "row", 0, D_MODEL), (X, "row", 0, D_MODEL), (g, "bcast", 0, D_MODEL),
                    (scale, "seg", 0, D_MODEL), (dres, "row", 0, D_MODEL)],
               [(D_MODEL, F32)], [D_MODEL, D_MODEL], T=T, L=L, name="prenorm_bwd")


def postnorm_fwd(X, y, g, gate, *, L):
    T = X.shape[0]

    def fn(x, y, g, gate):
        return (x + gate * ((y * _rms(y)) * g),)

    out, = _ew(fn, [(X, "row", 0, D_MODEL), (y, "row", 0, D_MODEL), (g, "bcast", 0, D_MODEL),
                    (gate, "seg", 0, D_MODEL)], [(D_MODEL, F32)], [], T=T, L=L, name="postnorm_fwd")
    return out


def postnorm_bwd(dX, y, g, gate, *, L):
    T = dX.shape[0]

    def fn(dx, y, g, gate):
        r = _rms(y)
        yn = y * r
        dyn = dx * (gate * g)
        dy = r * (dyn - yn * jnp.mean(dyn * yn, axis=-1, keepdims=True))
        return dy, dx * yn

    return _ew(fn, [(dX, "row", 0, D_MODEL), (y, "row", 0, D_MODEL), (g, "bcast", 0, D_MODEL),
                    (gate, "seg", 0, D_MODEL)], [(D_MODEL, MM_DTYPE)], [D_MODEL], T=T, L=L, name="postnorm_bwd")


def loss_and_grad(X, target, *, L):
    T = X.shape[0]
    tb = ROW_TILE
    nlat = L // tb

    def body(x_ref, t_ref, dx_ref, part_ref):
        i = pl.program_id(0)

        @pl.when(i < nlat)
        def _():
            err = x_ref[...] - t_ref[...]
            dx_ref[...] = err * (1.0 / D_MODEL)
            part_ref[...] = jnp.full(part_ref.shape, 0.5 / D_MODEL * jnp.sum(err * err), F32)

        @pl.when(i >= nlat)
        def _():
            dx_ref[...] = jnp.zeros(dx_ref.shape, F32)
            part_ref[...] = jnp.zeros(part_ref.shape, F32)

    dx, part = pl.pallas_call(
        body, out_shape=[jax.ShapeDtypeStruct((T, D_MODEL), F32), jax.ShapeDtypeStruct((T // tb, 8, 128), F32)],
        grid=(T // tb,),
        in_specs=[pl.BlockSpec((tb, D_MODEL), lambda i: (i, 0)),
                  pl.BlockSpec((tb, D_MODEL), lambda i: (jnp.minimum(i, nlat - 1), 0))],
        out_specs=[pl.BlockSpec((tb, D_MODEL), lambda i: (i, 0)), pl.BlockSpec((None, 8, 128), lambda i: (i, 0, 0))],
        compiler_params=_params("parallel"), name="loss_and_grad")(X, target)
    return jnp.sum(part[:, 0, 0]), dx


Q_BLOCK = WIN_ROWS * GRID_W
BAND = 2 * WIN_ROWS * GRID_W


def _bias_constants(rows):
    col = np.arange(GRID_W)
    col_start = np.clip(col - WIN_COLS // 2, 0, GRID_W - WIN_COLS)
    in_win = (col[None, :] >= col_start[:, None]) & (col[None, :] < col_start[:, None] + WIN_COLS)
    dcol = np.clip(col[None, :] - col[:, None] + (WIN_COLS - 1), 0, 2 * WIN_COLS - 2)
    E = np.zeros((2 * WIN_COLS - 1, GRID_W, GRID_W), np.float32)
    for dc in range(2 * WIN_COLS - 1):
        E[dc] = (dcol == dc) & in_win
    sel = np.zeros((3, WIN_ROWS, 2 * WIN_ROWS, 2 * WIN_ROWS - 1), np.float32)
    valid = np.zeros((3, WIN_ROWS, GRID_W, 2 * WIN_ROWS, GRID_W), bool)
    for v, r0 in enumerate((0, WIN_ROWS, rows - WIN_ROWS)):
        kstart = int(np.clip(r0 - WIN_ROWS // 2, 0, rows - 2 * WIN_ROWS))
        for a in range(WIN_ROWS):
            qr = r0 + a
            wstart = int(np.clip(qr - WIN_ROWS // 2, 0, rows - WIN_ROWS))
            for b in range(2 * WIN_ROWS):
                kr = kstart + b
                if wstart <= kr < wstart + WIN_ROWS:
                    sel[v, a, b, kr - qr + WIN_ROWS - 1] = 1.0
                    valid[v, a, :, b, :] = in_win
    return E, sel, valid.reshape(3, Q_BLOCK, BAND)


def build_bias(rpb, rows):
    E, sel, valid = _bias_constants(rows)
    tiles = jnp.einsum("hrd,dqk->hrqk", rpb, E, precision=lax.Precision.HIGHEST)
    b = jnp.einsum("vabr,hrqk->vhaqbk", sel, tiles, precision=lax.Precision.HIGHEST)
    b = b.reshape(3, N_HEADS, Q_BLOCK, BAND)
    return jnp.where(valid[:, None], b, NEG_INF)


def _band_start(i, rows):
    return pl.multiple_of(jnp.clip(WIN_ROWS * i - WIN_ROWS // 2, 0, rows - 2 * WIN_ROWS) * GRID_W, 256)


def _variant(i, nq):
    return jnp.where(i == 0, 0, jnp.where(i == nq - 1, 2, 1))


def _dot_nt(a, b):
    return lax.dot_general(a, b, (((1,), (1,)), ((), ())), preferred_element_type=F32)


def _dot_tn(a, b):
    return lax.dot_general(a, b, (((0,), (0,)), ((), ())), preferred_element_type=F32)


def _dot(a, b):
    return jnp.dot(a, b, preferred_element_type=F32)


def attn_fwd(q, k, v, bias, *, L, N):
    H, T, _ = q.shape
    rows = L // GRID_W
    nq = L // Q_BLOCK
    scale = HEAD_DIM ** -0.5

    def body(q_ref, k_ref, v_ref, b_ref, o_ref, lse_ref):
        ks = _band_start(pl.program_id(1), rows)
        qv = q_ref[...]
        kb, vb = k_ref[pl.ds(ks, BAND), :], v_ref[pl.ds(ks, BAND), :]
        kc, vc = k_ref[pl.ds(L, N), :], v_ref[pl.ds(L, N), :]
        sb = _dot_nt(qv, kb) * scale + b_ref[...]
        sc = _dot_nt(qv, kc) * scale
        m = jnp.maximum(jnp.max(sb, axis=-1, keepdims=True), jnp.max(sc, axis=-1, keepdims=True))
        pb, pc = jnp.exp(sb - m), jnp.exp(sc - m)
        l = jnp.sum(pb, axis=-1, keepdims=True) + jnp.sum(pc, axis=-1, keepdims=True)
        o = _dot(pb.astype(MM_DTYPE), vb) + _dot(pc.astype(MM_DTYPE), vc)
        o_ref[...] = o / l
        lse_ref[...] = m + jnp.log(l)

    return pl.pallas_call(
        body, out_shape=[jax.ShapeDtypeStruct((H, L, HEAD_DIM), F32), jax.ShapeDtypeStruct((H, L, 1), F32)],
        grid=(H, nq),
        in_specs=[pl.BlockSpec((None, Q_BLOCK, HEAD_DIM), lambda h, i: (h, i, 0)),
                  pl.BlockSpec((None, T, HEAD_DIM), lambda h, i: (h, 0, 0)),
                  pl.BlockSpec((None, T, HEAD_DIM), lambda h, i: (h, 0, 0)),
                  pl.BlockSpec((None, None, Q_BLOCK, BAND), lambda h, i: (_variant(i, nq), h, 0, 0))],
        out_specs=[pl.BlockSpec((None, Q_BLOCK, HEAD_DIM), lambda h, i: (h, i, 0)),
                   pl.BlockSpec((None, Q_BLOCK, 1), lambda h, i: (h, i, 0))],
        compiler_params=_params("parallel", "arbitrary"), name="attn_fwd")(q, k, v, bias)


def attn_bwd(q, k, v, bias, o, do, lse, *, L, N):
    H, T, _ = q.shape
    rows = L // GRID_W
    nq = L // Q_BLOCK
    scale = HEAD_DIM ** -0.5

    def body(q_ref, k_ref, v_ref, b_ref, o_ref, do_ref, lse_ref, dq_ref, dk_ref, dv_ref, db_ref):
        i = pl.program_id(1)
        ks = _band_start(i, rows)

        @pl.when(i == 0)
        def _():
            dk_ref[...] = jnp.zeros(dk_ref.shape, F32)
            dv_ref[...] = jnp.zeros(dv_ref.shape, F32)

        @pl.when((i == 0) | (i == 1) | (i == nq - 1))
        def _():
            db_ref[...] = jnp.zeros(db_ref.shape, F32)

        qv = q_ref[...]
        kb, vb = k_ref[pl.ds(ks, BAND), :], v_ref[pl.ds(ks, BAND), :]
        kc, vc = k_ref[pl.ds(L, N), :], v_ref[pl.ds(L, N), :]
        lse = lse_ref[...]
        pb = jnp.exp(_dot_nt(qv, kb) * scale + b_ref[...] - lse)
        pc = jnp.exp(_dot_nt(qv, kc) * scale - lse)
        do_f = do_ref[...]
        delta = jnp.sum(do_f * o_ref[...], axis=-1, keepdims=True)
        dov = do_f.astype(MM_DTYPE)
        dsb = pb * (_dot_nt(dov, vb) - delta)
        dsc = pc * (_dot_nt(dov, vc) - delta)
        db_ref[...] += dsb
        dsb_s, dsc_s = (dsb * scale).astype(MM_DTYPE), (dsc * scale).astype(MM_DTYPE)
        dq_ref[...] = _dot(dsb_s, kb) + _dot(dsc_s, kc)
        dk_ref[pl.ds(ks, BAND), :] += _dot_tn(dsb_s, qv)
        dk_ref[pl.ds(L, N), :] += _dot_tn(dsc_s, qv)
        dv_ref[pl.ds(ks, BAND), :] += _dot_tn(pb.astype(MM_DTYPE), dov)
        dv_ref[pl.ds(L, N), :] += _dot_tn(pc.astype(MM_DTYPE), dov)

    qspec = pl.BlockSpec((None, Q_BLOCK, HEAD_DIM), lambda h, i: (h, i, 0))
    kspec = pl.BlockSpec((None, T, HEAD_DIM), lambda h, i: (h, 0, 0))
    bspec = pl.BlockSpec((None, None, Q_BLOCK, BAND), lambda h, i: (_variant(i, nq), h, 0, 0))
    return pl.pallas_call(
        body,
        out_shape=[jax.ShapeDtypeStruct((H, L, HEAD_DIM), F32), jax.ShapeDtypeStruct((H, T, HEAD_DIM), F32),
                   jax.ShapeDtypeStruct((H, T, HEAD_DIM), F32), jax.ShapeDtypeStruct((3, H, Q_BLOCK, BAND), F32)],
        grid=(H, nq),
        in_specs=[qspec, kspec, kspec, bspec, qspec, qspec, pl.BlockSpec((None, Q_BLOCK, 1), lambda h, i: (h, i, 0))],
        out_specs=[qspec, kspec, kspec, bspec],
        compiler_params=_params("parallel", "arbitrary"), name="attn_bwd")(q, k, v, bias, o, do, lse)


def cattn_fwd(q, k, v, *, L, N):
    H = q.shape[0]
    scale = HEAD_DIM ** -0.5
    cspec = pl.BlockSpec((None, N, HEAD_DIM), lambda h: (h, L // N, 0))

    def body(q_ref, k_ref, v_ref, o_ref, lse_ref):
        s = _dot_nt(q_ref[...], k_ref[...]) * scale
        m = jnp.max(s, axis=-1, keepdims=True)
        p = jnp.exp(s - m)
        l = jnp.sum(p, axis=-1, keepdims=True)
        o_ref[...] = _dot(p.astype(MM_DTYPE), v_ref[...]) / l
        lse_ref[...] = m + jnp.log(l)

    return pl.pallas_call(
        body, out_shape=[jax.ShapeDtypeStruct((H, N, HEAD_DIM), F32), jax.ShapeDtypeStruct((H, N, 1), F32)],
        grid=(H,), in_specs=[cspec, cspec, cspec],
        out_specs=[pl.BlockSpec((None, N, HEAD_DIM), lambda h: (h, 0, 0)), pl.BlockSpec((None, N, 1), lambda h: (h, 0, 0))],
        compiler_params=_params("parallel"), name="cattn_fwd")(q, k, v)


def cattn_bwd(q, k, v, o, do, lse, *, L, N):
    H = q.shape[0]
    scale = HEAD_DIM ** -0.5
    cspec = pl.BlockSpec((None, N, HEAD_DIM), lambda h: (h, L // N, 0))
    ospec = pl.BlockSpec((None, N, HEAD_DIM), lambda h: (h, 0, 0))

    def body(q_ref, k_ref, v_ref, o_ref, do_ref, lse_ref, dq_ref, dk_ref, dv_ref):
        qv, kv, vv = q_ref[...], k_ref[...], v_ref[...]
        p = jnp.exp(_dot_nt(qv, kv) * scale - lse_ref[...])
        do_f = do_ref[...]
        delta = jnp.sum(do_f * o_ref[...], axis=-1, keepdims=True)
        dov = do_f.astype(MM_DTYPE)
        ds = (p * (_dot_nt(dov, vv) - delta) * scale).astype(MM_DTYPE)
        dq_ref[...] = _dot(ds, kv)
        dk_ref[...] = _dot_tn(ds, qv)
        dv_ref[...] = _dot_tn(p.astype(MM_DTYPE), dov)

    return pl.pallas_call(
        body, out_shape=[jax.ShapeDtypeStruct((H, N, HEAD_DIM), F32)] * 3, grid=(H,),
        in_specs=[cspec, cspec, cspec, ospec, ospec, pl.BlockSpec((None, N, 1), lambda h: (h, 0, 0))],
        out_specs=[ospec, ospec, ospec], compiler_params=_params("parallel"), name="cattn_bwd")(q, k, v, o, do, lse)


def _heads(a):
    T = a.shape[0]
    return a.reshape(T, N_HEADS, HEAD_DIM).transpose(1, 0, 2)


def _unheads(a):
    return a.transpose(1, 0, 2).reshape(a.shape[1], N_HEADS * HEAD_DIM)


PAD = 16


def _row_ids(T):
    return lax.broadcasted_iota(jnp.int32, (T, POOL_DIM), 0)


def _same_segment(t, s, L, T):
    return (s >= 0) & (s < T) & ((t < L) == (s < L))


def _window_sum(buf_ref, x, half, *, L, T, transpose):
    buf_ref[pl.ds(PAD, T), :] = x
    t = _row_ids(T)
    acc = jnp.zeros((T, POOL_DIM), F32)
    for j in range(-8, 9):
        inside = ((j > -half) & (j <= half)) if transpose else ((j >= -half) & (j < half))
        ok = _same_segment(t, t + j, L, T) & inside
        acc = acc + jnp.where(ok, buf_ref[pl.ds(PAD + j, T), :], 0.0)
    return acc


def _window_count(half, *, L, T):
    t = _row_ids(T)
    pos = jnp.where(t < L, t, t - L)
    seg_len = jnp.where(t < L, L, T - L)
    return (jnp.minimum(pos + half, seg_len) - jnp.maximum(pos - half, 0)).astype(F32)


def _zero_pads(buf_ref, T):
    buf_ref[pl.ds(0, PAD), :] = jnp.zeros((PAD, POOL_DIM), F32)
    buf_ref[pl.ds(PAD + T, PAD), :] = jnp.zeros((PAD, POOL_DIM), F32)


def pool_fwd(proj, pool_w, pool_scale, *, L):
    T = proj.shape[0]
    cb0 = COL["pool_u"] // POOL_DIM

    def body(u_ref, w_ref, s_ref, o_ref, p_ref, buf_ref):
        half = jnp.left_shift(1, pl.program_id(0))
        _zero_pads(buf_ref, T)
        u = u_ref[...]
        pooled = _window_sum(buf_ref, u, half, L=L, T=T, transpose=False) / _window_count(half, L=L, T=T) - u
        pm = pooled.astype(MM_DTYPE)
        p_ref[...] = pm
        o_ref[...] = _dot(pm, w_ref[...].astype(MM_DTYPE)) * s_ref[...]

    cspec = pl.BlockSpec((T, POOL_DIM), lambda g: (0, g))
    return pl.pallas_call(
        body, out_shape=[jax.ShapeDtypeStruct((T, BRANCH), F32), jax.ShapeDtypeStruct((T, BRANCH), MM_DTYPE)],
        grid=(POOL_GROUPS,),
        in_specs=[pl.BlockSpec((T, POOL_DIM), lambda g: (0, cb0 + g)),
                  pl.BlockSpec((None, POOL_DIM, POOL_DIM), lambda g: (g, 0, 0)),
                  pl.BlockSpec((1, POOL_DIM), lambda g: (0, g))],
        out_specs=[cspec, cspec], scratch_shapes=[pltpu.VMEM((T + 2 * PAD, POOL_DIM), F32)],
        compiler_params=_params("parallel"), name="pool_fwd")(proj, pool_w, pool_scale)


def pool_bwd(do, pooled, pool_w, pool_scale, *, L):
    T = do.shape[0]

    def body(do_ref, p_ref, w_ref, s_ref, du_ref, dw_ref, ds_ref, buf_ref):
        half = jnp.left_shift(1, pl.program_id(0))
        _zero_pads(buf_ref, T)
        pm = p_ref[...]
        w = w_ref[...].astype(MM_DTYPE)
        mixed = _dot(pm, w)
        dov = do_ref[...]
        ds_ref[...] = jnp.broadcast_to(jnp.sum(dov * mixed, axis=0, keepdims=True), ds_ref.shape)
        dmixed = (dov * s_ref[...]).astype(MM_DTYPE)
        dw_ref[...] = _dot_tn(pm, dmixed)
        dpooled = _dot_nt(dmixed, w)
        scaled = dpooled / _window_count(half, L=L, T=T)
        du = _window_sum(buf_ref, scaled, half, L=L, T=T, transpose=True) - dpooled
        du_ref[...] = du.astype(du_ref.dtype)

    cspec = pl.BlockSpec((T, POOL_DIM), lambda g: (0, g))
    return pl.pallas_call(
        body, out_shape=[jax.ShapeDtypeStruct((T, BRANCH), MM_DTYPE),
                         jax.ShapeDtypeStruct((POOL_GROUPS, POOL_DIM, POOL_DIM), F32),
                         jax.ShapeDtypeStruct((8, BRANCH), F32)],
        grid=(POOL_GROUPS,),
        in_specs=[cspec, cspec, pl.BlockSpec((None, POOL_DIM, POOL_DIM), lambda g: (g, 0, 0)),
                  pl.BlockSpec((1, POOL_DIM), lambda g: (0, g))],
        out_specs=[cspec, pl.BlockSpec((None, POOL_DIM, POOL_DIM), lambda g: (g, 0, 0)),
                   pl.BlockSpec((8, POOL_DIM), lambda g: (0, g))],
        scratch_shapes=[pltpu.VMEM((T + 2 * PAD, POOL_DIM), F32)],
        compiler_params=_params("parallel"), name="pool_bwd")(do, pooled, pool_w, pool_scale)


def _shifted(buf_ref, x, j, *, L, T):
    buf_ref[pl.ds(PAD, T), :] = x
    t = _row_ids(T)
    return jnp.where(_same_segment(t, t + j, L, T), buf_ref[pl.ds(PAD + j, T), :], 0.0)


def conv_fwd(proj, conv_w, *, L):
    T = proj.shape[0]
    nb = BRANCH // POOL_DIM
    cx, cbb, cc = (COL[n] // POOL_DIM for n in ("conv_x", "conv_b", "conv_c"))

    def body(x_ref, b_ref, c_ref, w_ref, o_ref, buf_ref):
        _zero_pads(buf_ref, T)
        xc = c_ref[...] * x_ref[...]
        w = w_ref[...]
        conv = (w[0:1] * _shifted(buf_ref, xc, -1, L=L, T=T) + w[1:2] * xc
                + w[2:3] * _shifted(buf_ref, xc, 1, L=L, T=T))
        o_ref[...] = b_ref[...] * conv

    return pl.pallas_call(
        body, out_shape=jax.ShapeDtypeStruct((T, BRANCH), F32), grid=(nb,),
        in_specs=[pl.BlockSpec((T, POOL_DIM), lambda g: (0, cx + g)), pl.BlockSpec((T, POOL_DIM), lambda g: (0, cbb + g)),
                  pl.BlockSpec((T, POOL_DIM), lambda g: (0, cc + g)), pl.BlockSpec((8, POOL_DIM), lambda g: (0, g))],
        out_specs=pl.BlockSpec((T, POOL_DIM), lambda g: (0, g)),
        scratch_shapes=[pltpu.VMEM((T + 2 * PAD, POOL_DIM), F32)],
        compiler_params=_params("parallel"), name="conv_fwd")(proj, proj, proj, conv_w)


def conv_bwd(do, proj, conv_w, *, L):
    T = proj.shape[0]
    nb = BRANCH // POOL_DIM
    cx, cbb, cc = (COL[n] // POOL_DIM for n in ("conv_x", "conv_b", "conv_c"))

    def body(do_ref, x_ref, b_ref, c_ref, w_ref, dx_ref, db_ref, dc_ref, dw_ref, buf_ref):
        _zero_pads(buf_ref, T)
        xv, gb, gc = x_ref[...], b_ref[...], c_ref[...]
        xc = gc * xv
        w = w_ref[...]
        xm = _shifted(buf_ref, xc, -1, L=L, T=T)
        xp = _shifted(buf_ref, xc, 1, L=L, T=T)
        conv = w[0:1] * xm + w[1:2] * xc + w[2:3] * xp
        dov = do_ref[...]
        db_ref[...] = (dov * conv).astype(db_ref.dtype)
        dconv = dov * gb
        sums = [jnp.sum(dconv * a, axis=0, keepdims=True) for a in (xm, xc, xp)]
        dw_ref[...] = jnp.concatenate(sums + [jnp.zeros((5, POOL_DIM), F32)], axis=0)
        dxc = (w[0:1] * _shifted(buf_ref, dconv, 1, L=L, T=T) + w[1:2] * dconv
               + w[2:3] * _shifted(buf_ref, dconv, -1, L=L, T=T))
        dc_ref[...] = (dxc * xv).astype(dc_ref.dtype)
        dx_ref[...] = (dxc * gc).astype(dx_ref.dtype)

    ospec = lambda off: pl.BlockSpec((T, POOL_DIM), lambda g: (0, off + g))
    return pl.pallas_call(
        body, out_shape=[jax.ShapeDtypeStruct((T, BRANCH), MM_DTYPE)] * 3 + [jax.ShapeDtypeStruct((8, BRANCH), F32)],
        grid=(nb,),
        in_specs=[ospec(0), ospec(cx), ospec(cbb), ospec(cc), pl.BlockSpec((8, POOL_DIM), lambda g: (0, g))],
        out_specs=[ospec(0), ospec(0), ospec(0), pl.BlockSpec((8, POOL_DIM), lambda g: (0, g))],
        scratch_shapes=[pltpu.VMEM((T + 2 * PAD, POOL_DIM), F32)],
        compiler_params=_params("parallel"), name="conv_bwd")(do, proj, proj, proj, conv_w)


SCAN_COLS = 1024
SCAN_ROWS = 256


def ssm_operators(a_re, a_im, log_dt, b_re, b_im, c_re, c_im):
    dt = jnp.exp(log_dt)[..., None]
    mag = jnp.exp(a_re * dt)
    abar_re, abar_im = mag * jnp.cos(a_im * dt), mag * jnp.sin(a_im * dt)
    den = a_re * a_re + a_im * a_im
    num_re, num_im = abar_re - 1.0, abar_im
    f_re = (num_re * a_re + num_im * a_im) / den
    f_im = (num_im * a_re - num_re * a_im) / den
    bbar_re = f_re[..., None] * b_re - f_im[..., None] * b_im
    bbar_im = f_re[..., None] * b_im + f_im[..., None] * b_re
    gpb = SSM_GROUPS // SSM_BLOCKS
    eye = jnp.eye(gpb, dtype=bool)[None, None, :, None, :, None]

    def blocks(t):
        _, _, a, b = t.shape
        t = t.reshape(2, SSM_BLOCKS, gpb, a, 1, b)
        return jnp.where(eye, t, 0.0).reshape(2, SSM_BLOCKS, gpb * a, gpb * b)

    in_map = lambda bbar: blocks(bbar.transpose(0, 1, 3, 2))
    out_map = lambda c: blocks(c.transpose(0, 1, 3, 2))
    abar = jnp.concatenate([abar_re.reshape(2, 1, N_STATE), abar_im.reshape(2, 1, N_STATE)], axis=-1)
    bcat = jnp.concatenate([in_map(bbar_re), in_map(bbar_im)], axis=1)
    ccat = jnp.concatenate([out_map(c_re), -out_map(c_im)], axis=1)
    return abar, bcat, ccat


SSM_BLOCKS = 4
SSM_BCH = BRANCH // SSM_BLOCKS
SSM_BST = N_STATE // SSM_BLOCKS


def _ssm_rows(T, perm):
    tm = ROW_TILE if perm else _row_tile(T)
    return tm, (perm or (lambda i: i))


def ssm_in(u, bcat, *, ucol0, perm, name):
    T = u.shape[0]
    tm, rows = _ssm_rows(T, perm)
    ub0 = ucol0 // SSM_BCH
    return _matmul(u, bcat, pl.BlockSpec((tm, SSM_BCH), lambda i, n, k: (rows(i), ub0 + n % SSM_BLOCKS)),
                   pl.BlockSpec((None, SSM_BCH, SSM_BST), lambda i, n, k: (n, 0, 0)),
                   pl.BlockSpec((tm, SSM_BST), lambda i, n, k: (i, n)),
                   jax.ShapeDtypeStruct((T, 2 * N_STATE), F32), (T // tm, 2 * SSM_BLOCKS, 1), name=name)


def ssm_out(s, ccat, *, perm, name):
    T = s.shape[0]
    tm, rows = _ssm_rows(T, perm)
    return _matmul(s, ccat, pl.BlockSpec((tm, SSM_BST), lambda i, n, k: (i, k * SSM_BLOCKS + n)),
                   pl.BlockSpec((None, SSM_BST, SSM_BCH), lambda i, n, k: (k * SSM_BLOCKS + n, 0, 0)),
                   pl.BlockSpec((tm, SSM_BCH), lambda i, n, k: (rows(i), n)),
                   jax.ShapeDtypeStruct((T, BRANCH), F32), (T // tm, SSM_BLOCKS, 2), name=name)


def ssm_out_dx(dy, ccat, *, perm, name):
    T = dy.shape[0]
    tm, rows = _ssm_rows(T, perm)
    return _matmul(dy, ccat, pl.BlockSpec((tm, SSM_BCH), lambda i, n, k: (rows(i), n % SSM_BLOCKS)),
                   pl.BlockSpec((None, SSM_BST, SSM_BCH), lambda i, n, k: (n, 0, 0)),
                   pl.BlockSpec((tm, SSM_BST), lambda i, n, k: (i, n)),
                   jax.ShapeDtypeStruct((T, 2 * N_STATE), F32), (T // tm, 2 * SSM_BLOCKS, 1), tb=True, name=name)


def ssm_in_dx(lam, bcat, *, perm, name):
    T = lam.shape[0]
    tm, rows = _ssm_rows(T, perm)
    return _matmul(lam, bcat, pl.BlockSpec((tm, SSM_BST), lambda i, n, k: (i, k * SSM_BLOCKS + n)),
                   pl.BlockSpec((None, SSM_BCH, SSM_BST), lambda i, n, k: (k * SSM_BLOCKS + n, 0, 0)),
                   pl.BlockSpec((tm, SSM_BCH), lambda i, n, k: (rows(i), n)),
                   jax.ShapeDtypeStruct((T, BRANCH), F32), (T // tm, SSM_BLOCKS, 2), tb=True, name=name)


def ssm_in_dw(u, lam, *, ucol0, perm, name):
    T = u.shape[0]
    tk, rows = _ssm_rows(T, perm)
    ub0 = ucol0 // SSM_BCH
    return _matmul(u, lam, pl.BlockSpec((tk, SSM_BCH), lambda n, j, k: (rows(k), ub0 + n % SSM_BLOCKS)),
                   pl.BlockSpec((tk, SSM_BST), lambda n, j, k: (k, n)),
                   pl.BlockSpec((None, SSM_BCH, SSM_BST), lambda n, j, k: (n, 0, 0)),
                   jax.ShapeDtypeStruct((2 * SSM_BLOCKS, SSM_BCH, SSM_BST), F32), (2 * SSM_BLOCKS, 1, T // tk),
                   ta=True, name=name)


def ssm_out_dw(s, dy, *, perm, name):
    T = s.shape[0]
    tk, rows = _ssm_rows(T, perm)
    return _matmul(s, dy, pl.BlockSpec((tk, SSM_BST), lambda n, j, k: (k, n)),
                   pl.BlockSpec((tk, SSM_BCH), lambda n, j, k: (rows(k), n % SSM_BLOCKS)),
                   pl.BlockSpec((None, SSM_BST, SSM_BCH), lambda n, j, k: (n, 0, 0)),
                   jax.ShapeDtypeStruct((2 * SSM_BLOCKS, SSM_BST, SSM_BCH), F32), (2 * SSM_BLOCKS, 1, T // tk),
                   ta=True, name=name)


def _time_block(T, reverse):
    nt = T // SCAN_ROWS
    tix = (lambda i: nt - 1 - i) if reverse else (lambda i: i)
    return nt, pl.BlockSpec((SCAN_ROWS, 2 * N_STATE), lambda i: (tix(i), 0))


def ssm_scan(bu, abar, *, reverse):
    T = bu.shape[0]
    nt, tspec = _time_block(T, reverse)

    def body(b_ref, a_ref, s_ref, c_ref):
        @pl.when(pl.program_id(0) == 0)
        def _():
            c_ref[...] = jnp.zeros(c_ref.shape, F32)

        for c0 in range(0, N_STATE, SCAN_COLS):
            re, im = pl.ds(c0, SCAN_COLS), pl.ds(N_STATE + c0, SCAN_COLS)
            ar, ai = a_ref[:, re], a_ref[:, im]

            def step(n, carry, re=re, im=im, ar=ar, ai=ai):
                sr, si = carry
                t = (SCAN_ROWS - 1 - n) if reverse else n
                nr = ar * sr - ai * si + b_ref[pl.ds(t, 1), re]
                ni = ar * si + ai * sr + b_ref[pl.ds(t, 1), im]
                s_ref[pl.ds(t, 1), re] = nr
                s_ref[pl.ds(t, 1), im] = ni
                return nr, ni

            sr, si = lax.fori_loop(0, SCAN_ROWS, step, (c_ref[:, re], c_ref[:, im]))
            c_ref[:, re] = sr
            c_ref[:, im] = si

    return pl.pallas_call(
        body, out_shape=jax.ShapeDtypeStruct((T, 2 * N_STATE), F32), grid=(nt,),
        in_specs=[tspec, pl.BlockSpec((1, 2 * N_STATE), lambda i: (0, 0))], out_specs=tspec,
        scratch_shapes=[pltpu.VMEM((1, 2 * N_STATE), F32)],
        compiler_params=_params("arbitrary"), name="ssm_scan_rev" if reverse else "ssm_scan_fwd")(bu, abar)


def ssm_scan_bwd(g, s, abar, *, reverse):
    T = g.shape[0]
    nt, tspec = _time_block(T, not reverse)
    back = not reverse

    def body(g_ref, s_ref, a_ref, l_ref, da_ref, c_ref):
        @pl.when(pl.program_id(0) == 0)
        def _():
            c_ref[...] = jnp.zeros(c_ref.shape, F32)
            da_ref[...] = jnp.zeros(da_ref.shape, F32)

        for c0 in range(0, N_STATE, SCAN_COLS):
            re, im = pl.ds(c0, SCAN_COLS), pl.ds(N_STATE + c0, SCAN_COLS)
            ar, ai = a_ref[:, re], a_ref[:, im]

            def step(n, carry, re=re, im=im, ar=ar, ai=ai):
                lr, li, dr, di = carry
                t = (SCAN_ROWS - 1 - n) if back else n
                sr, si = s_ref[pl.ds(t, 1), re], s_ref[pl.ds(t, 1), im]
                dr = dr + sr * lr + si * li
                di = di + sr * li - si * lr
                nr = g_ref[pl.ds(t, 1), re] + ar * lr + ai * li
                ni = g_ref[pl.ds(t, 1), im] + ar * li - ai * lr
                l_ref[pl.ds(t, 1), re] = nr
                l_ref[pl.ds(t, 1), im] = ni
                return nr, ni, dr, di

            zero = jnp.zeros((1, SCAN_COLS), F32)
            lr, li, dr, di = lax.fori_loop(0, SCAN_ROWS, step, (c_ref[:, re], c_ref[:, im], zero, zero))
            c_ref[:, re] = lr
            c_ref[:, im] = li
            da_ref[:, re] += jnp.broadcast_to(dr, (8, SCAN_COLS))
            da_ref[:, im] += jnp.broadcast_to(di, (8, SCAN_COLS))

    return pl.pallas_call(
        body, out_shape=[jax.ShapeDtypeStruct((T, 2 * N_STATE), F32), jax.ShapeDtypeStruct((8, 2 * N_STATE), F32)],
        grid=(nt,), in_specs=[tspec, tspec, pl.BlockSpec((1, 2 * N_STATE), lambda i: (0, 0))],
        out_specs=[tspec, pl.BlockSpec((8, 2 * N_STATE), lambda i: (0, 0))],
        scratch_shapes=[pltpu.VMEM((1, 2 * N_STATE), F32)],
        compiler_params=_params("arbitrary"),
        name="ssm_scan_bwd_rev" if reverse else "ssm_scan_bwd_fwd")(g, s, abar)


def _gelu(x):
    return 0.5 * x * (1.0 + jnp.tanh(0.7978845608028654 * (x + 0.044715 * x * x * x)))


def _gelu_grad(x):
    t = jnp.tanh(0.7978845608028654 * (x + 0.044715 * x * x * x))
    return 0.5 * (1.0 + t) + 0.5 * x * (1.0 - t * t) * 0.7978845608028654 * (1.0 + 3 * 0.044715 * x * x)


def _silu(z):
    return z * _sigmoid(z)


def _silu_grad(z):
    s = _sigmoid(z)
    return s * (1.0 + z * (1.0 - s))


def ssm_fwd(proj, ops, dsk, glu_w, *, L):
    T = proj.shape[0]
    abar, bcat, ccat = ops
    nb, nlat = T // ROW_TILE, L // ROW_TILE
    to_f = lambda i: (i + nlat) % nb
    states, ys = [], []
    for d in (0, 1):
        perm = to_f if d == 0 else None
        bu = ssm_in(proj, bcat[d], ucol0=COL["ssm_u"], perm=perm, name=f"ssm_in{d}")
        s = ssm_scan(bu, abar[d], reverse=(d == 1))
        states.append(s)
        ys.append(ssm_out(s, ccat[d], perm=perm, name=f"ssm_out{d}"))

    def pre(u, yf, yr, dsk):
        y = dsk * u + yf + yr
        return y, _gelu(y)

    ypre, gy = _ew(pre, [(proj, "row", COL["ssm_u"] // BRANCH, BRANCH), (ys[0], "row", 0, BRANCH),
                         (ys[1], "row", 0, BRANCH), (dsk, "bcast", 0, BRANCH)],
                   [(BRANCH, F32), (BRANCH, MM_DTYPE)], [], T=T, L=L, name="ssm_pre")
    gg = mm_nn(gy, glu_w, out_dtype=F32, name="ssm_glu")

    def post(ga, gb):
        return (ga * _sigmoid(gb),)

    o, = _ew(post, [(gg, "row", 0, BRANCH), (gg, "row", 1, BRANCH)], [(BRANCH, F32)], [], T=T, L=L, name="ssm_post")
    return o, dict(states=states, ypre=ypre, gy=gy, gg=gg)


def ssm_bwd(do, proj, ops, dsk, glu_w, saved, *, L):
    T = proj.shape[0]
    abar, bcat, ccat = ops
    nb, nlat = T // ROW_TILE, L // ROW_TILE
    to_f = lambda i: (i + nlat) % nb
    gg, gy, ypre = saved["gg"], saved["gy"], saved["ypre"]

    def post_bwd(do, ga, gb):
        sg = _sigmoid(gb)
        return (jnp.concatenate([do * sg, do * ga * sg * (1.0 - sg)], axis=1),)

    dgg, = _ew(post_bwd, [(do, "row", 0, BRANCH), (gg, "row", 0, BRANCH), (gg, "row", 1, BRANCH)],
               [(2 * BRANCH, MM_DTYPE)], [], T=T, L=L, name="ssm_post_bwd")
    dgy = mm_nt(dgg, glu_w, out_dtype=F32, name="ssm_glu_dx")
    dglu = mm_tn(gy, dgg, out_dtype=F32, name="ssm_glu_dw")

    def pre_bwd(dgy, y, u, dsk):
        dy = dgy * _gelu_grad(y)
        return dy, dy * dsk, dy * u

    dy, du_skip, dd = _ew(pre_bwd, [(dgy, "row", 0, BRANCH), (ypre, "row", 0, BRANCH),
                                    (proj, "row", COL["ssm_u"] // BRANCH, BRANCH), (dsk, "bcast", 0, BRANCH)],
                          [(BRANCH, MM_DTYPE), (BRANCH, F32)], [BRANCH], T=T, L=L, name="ssm_pre_bwd")
    du = du_skip
    dabar, dbcat, dccat = [], [], []
    for d in (0, 1):
        perm = to_f if d == 0 else None
        s = saved["states"][d]
        g = ssm_out_dx(dy, ccat[d], perm=perm, name=f"ssm_out{d}_dx")
        lam, da = ssm_scan_bwd(g, s, abar[d], reverse=(d == 1))
        dabar.append(da[0:1])
        dccat.append(ssm_out_dw(s, dy, perm=perm, name=f"ssm_out{d}_dw"))
        du = du + ssm_in_dx(lam, bcat[d], perm=perm, name=f"ssm_in{d}_dx")
        dbcat.append(ssm_in_dw(proj, lam, ucol0=COL["ssm_u"], perm=perm, name=f"ssm_in{d}_dw"))
    d_ops = (jnp.stack(dabar), jnp.stack(dbcat), jnp.stack(dccat))
    return du, d_ops, dd[0, 0] + dd[1, 0], dglu


Z_COLS = tuple(COL[n] // BRANCH for n in ("na_z", "pool_z", "conv_z", "ssm_z"))


def gate_act(o, proj, *, L):
    T = o.shape[0]

    def fn(o, z0, z1, z2, z3):
        return (o * _silu(jnp.concatenate([z0, z1, z2, z3], axis=1)),)

    a, = _ew(fn, [(o, "row", 0, D_MODEL)] + [(proj, "row", c, BRANCH) for c in Z_COLS],
             [(D_MODEL, MM_DTYPE)], [], T=T, L=L, name="gate_act")
    return a


def gate_act_bwd(da, o, proj, *, L):
    T = o.shape[0]

    def fn(da, o, z0, z1, z2, z3):
        z = jnp.concatenate([z0, z1, z2, z3], axis=1)
        return da * _silu(z), da * o * _silu_grad(z)

    return _ew(fn, [(da, "row", 0, D_MODEL), (o, "row", 0, D_MODEL)] + [(proj, "row", c, BRANCH) for c in Z_COLS],
               [(D_MODEL, F32), (D_MODEL, MM_DTYPE)], [], T=T, L=L, name="gate_act_bwd")


MERGE_TN = 512


def merge_fwd(a, w_br, proj, b_gate):
    T = a.shape[0]
    tm, tn = _row_tile(T), MERGE_TN
    nn = D_MODEL // tn
    lb0 = COL["merge"] // tn

    def body(a_ref, w_ref, l_ref, b_ref, m_ref, br_ref, acc_ref):
        i = pl.program_id(2)
        br = _dot(a_ref[...].astype(MM_DTYPE), w_ref[...].astype(MM_DTYPE))
        br_ref[...] = br
        term = _sigmoid(l_ref[...] + b_ref[...]) * br

        @pl.when(i == 0)
        def _():
            acc_ref[...] = term

        @pl.when(i > 0)
        def _():
            acc_ref[...] += term

        @pl.when(i == 3)
        def _():
            m_ref[...] = acc_ref[...].astype(m_ref.dtype)

    return pl.pallas_call(
        body, out_shape=[jax.ShapeDtypeStruct((T, D_MODEL), MM_DTYPE), jax.ShapeDtypeStruct((T, 4 * D_MODEL), F32)],
        grid=(T // tm, nn, 4),
        in_specs=[pl.BlockSpec((tm, BRANCH), lambda m, n, i: (m, i)),
                  pl.BlockSpec((BRANCH, tn), lambda m, n, i: (i, n)),
                  pl.BlockSpec((tm, tn), lambda m, n, i: (m, lb0 + i * nn + n)),
                  pl.BlockSpec((1, tn), lambda m, n, i: (0, i * nn + n))],
        out_specs=[pl.BlockSpec((tm, tn), lambda m, n, i: (m, n)), pl.BlockSpec((tm, tn), lambda m, n, i: (m, i * nn + n))],
        scratch_shapes=[pltpu.VMEM((tm, tn), F32)],
        compiler_params=_params("parallel", "parallel", "arbitrary"), name="merge_fwd")(a, w_br, proj, b_gate)


def merge_bwd(dmerged, br, proj, b_gate):
    T = dmerged.shape[0]
    tb = ROW_TILE
    lb0 = COL["merge"] // D_MODEL

    def body(dm_ref, br_ref, l_ref, b_ref, dbr_ref, dl_ref, db_ref):
        dm = dm_ref[...]
        gates = _sigmoid(l_ref[...] + b_ref[...])
        dbr_ref[...] = (dm * gates).astype(dbr_ref.dtype)
        dl = dm * br_ref[...] * gates * (1.0 - gates)
        dl_ref[...] = dl.astype(dl_ref.dtype)
        s = jnp.broadcast_to(jnp.sum(dl, axis=0, keepdims=True), db_ref.shape)

        @pl.when(pl.program_id(1) == 0)
        def _():
            db_ref[...] = s

        @pl.when(pl.program_id(1) > 0)
        def _():
            db_ref[...] += s

    wide = pl.BlockSpec((tb, D_MODEL), lambda b, i: (i, b))
    return pl.pallas_call(
        body, out_shape=[jax.ShapeDtypeStruct((T, 4 * D_MODEL), MM_DTYPE)] * 2 + [jax.ShapeDtypeStruct((8, 4 * D_MODEL), F32)],
        grid=(4, T // tb),
        in_specs=[pl.BlockSpec((tb, D_MODEL), lambda b, i: (i, 0)), wide,
                  pl.BlockSpec((tb, D_MODEL), lambda b, i: (i, lb0 + b)), pl.BlockSpec((1, D_MODEL), lambda b, i: (0, b))],
        out_specs=[wide, wide, pl.BlockSpec((8, D_MODEL), lambda b, i: (0, b))],
        compiler_params=_params("parallel", "arbitrary"), name="merge_bwd")(dmerged, br, proj, b_gate)


def branch_dx(dbr, w_br):
    T = dbr.shape[0]
    tm, tk = _row_tile(T), 1024
    nk = D_MODEL // tk
    return _matmul(dbr, w_br, pl.BlockSpec((tm, tk), lambda m, i, k: (m, i * nk + k)),
                   pl.BlockSpec((BRANCH, tk), lambda m, i, k: (i, k)),
                   pl.BlockSpec((tm, BRANCH), lambda m, i, k: (m, i)),
                   jax.ShapeDtypeStruct((T, D_MODEL), F32), (T // tm, 4, nk), tb=True, name="branch_dx")


def branch_dw(a, dbr):
    T = a.shape[0]
    tk, tn = _row_tile(T), 1024
    nn = D_MODEL // tn
    return _matmul(a, dbr, pl.BlockSpec((tk, BRANCH), lambda i, n, k: (k, i)),
                   pl.BlockSpec((tk, tn), lambda i, n, k: (k, i * nn + n)),
                   pl.BlockSpec((BRANCH, tn), lambda i, n, k: (i, n)),
                   jax.ShapeDtypeStruct((D_MODEL, D_MODEL), F32), (4, nn, T // tk), ta=True, name="branch_dw")


def proj_fwd(h, w_in):
    T = h.shape[0]
    tm, tn = _row_tile(T), 1792
    per = W_IN_SHARD // tn
    return _matmul(h, w_in, pl.BlockSpec((tm, D_MODEL), lambda i, j, k: (i, 0)),
                   pl.BlockSpec((None, D_MODEL, tn), lambda i, j, k: (j // per, 0, j % per)),
                   pl.BlockSpec((tm, tn), lambda i, j, k: (i, j)),
                   jax.ShapeDtypeStruct((T, IN_TOTAL), F32), (T // tm, IN_TOTAL // tn, 1), name="proj_fwd")


def proj_dx(dproj, w_in):
    T = dproj.shape[0]
    tm, tk = _row_tile(T), 1792
    per = W_IN_SHARD // tk
    return _matmul(dproj, w_in, pl.BlockSpec((tm, tk), lambda i, j, k: (i, k)),
                   pl.BlockSpec((None, D_MODEL, tk), lambda i, j, k: (k // per, 0, k % per)),
                   pl.BlockSpec((tm, D_MODEL), lambda i, j, k: (i, 0)),
                   jax.ShapeDtypeStruct((T, D_MODEL), F32), (T // tm, 1, IN_TOTAL // tk), tb=True, name="proj_dx")


def proj_dw(h, dproj):
    T = h.shape[0]
    tk, tm, tn = _row_tile(T), 1024, 1792
    per = W_IN_SHARD // tn
    return _matmul(h, dproj, pl.BlockSpec((tk, tm), lambda i, j, k: (k, i)),
                   pl.BlockSpec((tk, tn), lambda i, j, k: (k, j)),
                   pl.BlockSpec((None, tm, tn), lambda i, j, k: (j // per, i, j % per)),
                   jax.ShapeDtypeStruct((N_SHARDS, D_MODEL, W_IN_SHARD), F32),
                   (D_MODEL // tm, IN_TOTAL // tn, T // tk), ta=True, name="proj_dw")


def layer_fwd(X, P, *, L):
    T = X.shape[0]
    N = T - L
    h = prenorm_fwd(X, P["g_pre"], P["scale"], P["shift"], L=L)
    proj = proj_fwd(h, P["w_in"])
    q, k, v = (_heads(proj[:, COL[n]:COL[n] + BRANCH]).astype(MM_DTYPE) for n in ("q", "k", "v"))
    o_lat, lse = attn_fwd(q, k, v, P["bias"], L=L, N=N)
    o_ctx, lse_c = cattn_fwd(q, k, v, L=L, N=N)
    o_na = jnp.concatenate([o_lat, o_ctx], axis=1)
    o_pool, pooled = pool_fwd(proj, P["pool_w"], P["pool_scale"], L=L)
    o_conv = conv_fwd(proj, P["conv_w"], L=L)
    o_ssm, ssm_saved = ssm_fwd(proj, P["ssm_ops"], P["ssm_d"], P["glu_w"], L=L)
    o = jnp.concatenate([_unheads(o_na), o_pool, o_conv, o_ssm], axis=1)
    a = gate_act(o, proj, L=L)
    merged, br = merge_fwd(a, P["w_br"], proj, P["b_gate"])
    y = mm_nn(merged, P["w_o"], out_dtype=F32, name="out_proj")
    Xn = postnorm_fwd(X, y, P["g_post"], P["gate"], L=L)
    saved = dict(X=X, h=h, proj=proj, q=q, k=k, v=v, o_na=o_na, lse=lse, lse_c=lse_c, pooled=pooled,
                 ssm=ssm_saved, o=o, a=a, merged=merged, br=br, y=y)
    return Xn, saved


def layer_bwd(dXn, P, S, bias_vjp, ops_vjp, *, L):
    T = dXn.shape[0]
    N = T - L
    proj = S["proj"]
    dy, cs_post = postnorm_bwd(dXn, S["y"], P["g_post"], P["gate"], L=L)
    dmerged = mm_nt(dy, P["w_o"], out_dtype=F32, name="out_proj_dx")
    d_w_o = mm_tn(S["merged"], dy, out_dtype=F32, name="out_proj_dw", tm=1024, tn=1024)
    dbr, dlogit, d_bgate = merge_bwd(dmerged, S["br"], proj, P["b_gate"])
    da = branch_dx(dbr, P["w_br"])
    d_w_br = branch_dw(S["a"], dbr)
    do, dz = gate_act_bwd(da, S["o"], proj, L=L)
    do_na = _heads(do[:, 0:BRANCH])
    dq_lat, dk, dv, dbias = attn_bwd(S["q"], S["k"], S["v"], P["bias"], S["o_na"][:, :L], do_na[:, :L], S["lse"], L=L, N=N)
    dq_c, dk_c, dv_c = cattn_bwd(S["q"], S["k"], S["v"], S["o_na"][:, L:], do_na[:, L:], S["lse_c"], L=L, N=N)
    dq = jnp.concatenate([dq_lat, dq_c], axis=1)
    dk = dk.at[:, L:].add(dk_c)
    dv = dv.at[:, L:].add(dv_c)
    d_rpb, = bias_vjp(dbias)
    dpool_u, d_pool_w, d_pool_scale = pool_bwd(do[:, BRANCH:2 * BRANCH], S["pooled"], P["pool_w"], P["pool_scale"], L=L)
    dcx, dcb, dcc, d_conv_w = conv_bwd(do[:, 2 * BRANCH:3 * BRANCH], proj, P["conv_w"], L=L)
    dssm_u, d_ops, d_ssm_d, d_glu = ssm_bwd(do[:, 3 * BRANCH:], proj, P["ssm_ops"], P["ssm_d"], P["glu_w"], S["ssm"], L=L)
    d_ssm = ops_vjp(d_ops)
    z = lambda i: dz[:, i * BRANCH:(i + 1) * BRANCH]
    cast = lambda t: t.astype(MM_DTYPE)
    dproj = jnp.concatenate([cast(_unheads(dq)), cast(_unheads(dk)), cast(_unheads(dv)), z(0), dpool_u, z(1),
                             dcx, dcb, dcc, z(2), cast(dssm_u), z(3), dlogit], axis=1)
    dh = proj_dx(dproj, P["w_in"])
    d_w_in = proj_dw(S["h"], dproj)
    dX, cs_h, cs_hx = prenorm_bwd(dh, S["X"], P["g_pre"], P["scale"], dXn, L=L)
    g_pre, g_post = P["g_pre"], P["g_post"]
    d_shift = cs_h
    d_scale = cs_hx * g_pre
    d_gate = cs_post * g_post
    d_g_pre = jnp.sum(cs_hx * (1.0 + P["scale"]), axis=0)[0]
    d_g_post = jnp.sum(cs_post * P["gate"], axis=0)[0]
    grads = dict(w_in=d_w_in, w_br=d_w_br, w_o=d_w_o, glu_w=d_glu, conv_w=d_conv_w[0:3], pool_w=d_pool_w,
                 pool_scale=d_pool_scale[0], b_gate=d_bgate[0], na_rpb=d_rpb, ssm=d_ssm, ssm_d=d_ssm_d,
                 g_pre=d_g_pre, g_post=d_g_post,
                 mod=jnp.concatenate([d_shift, d_scale, d_gate], axis=-1)[:, 0])
    return dX, grads


def local_step(x, ctx, target, mods, layers):
    L = x.shape[0]
    rows = L // GRID_W
    X = jnp.concatenate([x, ctx], axis=0)
    saved, Ps, vjps = [], [], []
    for i, lay in enumerate(layers):
        P = dict(lay)
        m = mods[i][:, None, :]
        P["shift"], P["scale"], P["gate"] = m[..., :D_MODEL], m[..., D_MODEL:2 * D_MODEL], m[..., 2 * D_MODEL:]
        P["bias"], bias_vjp = jax.vjp(lambda r: build_bias(r, rows), lay["na_rpb"])
        P["ssm_ops"], ops_vjp = jax.vjp(ssm_operators, *lay["ssm_params"])
        X, S = layer_fwd(X, P, L=L)
        saved.append(S)
        Ps.append(P)
        vjps.append((bias_vjp, ops_vjp))
    loss, dX = loss_and_grad(X, target, L=L)
    grads = [None] * len(layers)
    for i in reversed(range(len(layers))):
        dX, grads[i] = layer_bwd(dX, Ps[i], saved[i], *vjps[i], L=L)
    return loss, dX[:L], grads


MESH_ID = pl.DeviceIdType.MESH
HBM_SPEC = pl.BlockSpec(memory_space=pltpu.HBM)


def _place():
    return lax.axis_index("x"), lax.axis_index("y"), lax.axis_index("c")


def _other_chips(x, y):
    return [(1 - x, y), (x, 1 - y), (1 - x, 1 - y)]


def _remote(src, dst, send_sem, recv_sem, to):
    return pltpu.make_async_remote_copy(src_ref=src, dst_ref=dst, send_sem=send_sem, recv_sem=recv_sem,
                                        device_id=to, device_id_type=MESH_ID)


DMA_CHUNK_BYTES = 1 << 20


def _row_pieces(rows, row_bytes):
    n = max(1, min(64, rows * row_bytes // DMA_CHUNK_BYTES))
    while n > 1 and (rows % n or (rows // n) % 16):
        n -= 1
    return [(r * (rows // n), rows // n) for r in range(n)]


def _row_bytes(ref):
    return math.prod(ref.shape[1:]) * jnp.dtype(ref.dtype).itemsize


def _start_in_pieces(make, src, dst):
    for r0, nr in _row_pieces(src.shape[0], _row_bytes(src)):
        make(src.at[pl.ds(r0, nr)], dst.at[pl.ds(r0, nr)]).start()


def all_gather8(blocks, name, *, shard_major=False):
    K = len(blocks)

    def body(*refs):
        ins, outs = refs[:K], refs[K:2 * K]
        send_sems, recv_sems, local_sems = refs[2 * K:]
        x, y, c = _place()
        me, sibling = (x, y, c), (x, y, 1 - c)
        chips = _other_chips(x, y)

        def slot(k, block):
            px, py, pc = block
            if shard_major:
                h = ins[k].shape[0]
                return outs[k].at[2 * px + py, pl.ds(pl.multiple_of(pc * h, 16), h)]
            return outs[k].at[4 * px + 2 * py + pc]

        def copy(k, j, to):
            return lambda s, d: _remote(s, d, send_sems.at[k, j], recv_sems.at[k, j], to)

        for k in range(K):
            _start_in_pieces(lambda s, d, k=k: pltpu.make_async_copy(s, d, local_sems.at[k]), ins[k], slot(k, me))
            _start_in_pieces(copy(k, 0, sibling), ins[k], slot(k, me))
            for j, chip in enumerate(chips):
                _start_in_pieces(copy(k, 1 + j, (*chip, c)), ins[k], slot(k, me))
        for j, chip in enumerate(chips):
            for k in range(K):
                got = slot(k, (*chip, c))
                copy(k, 1 + j, me)(got, got).wait_recv()
                _start_in_pieces(copy(k, 4 + j, sibling), got, got)
        for k in range(K):
            sib = slot(k, sibling)
            copy(k, 0, me)(sib, sib).wait_recv()
            for j, chip in enumerate(chips):
                got = slot(k, (*chip, 1 - c))
                copy(k, 4 + j, me)(got, got).wait_recv()
        for k in range(K):
            own = slot(k, me)
            for j in range(4):
                copy(k, j, me)(ins[k], own).wait_send()
            for j, chip in enumerate(chips):
                got = slot(k, (*chip, c))
                copy(k, 4 + j, me)(got, got).wait_send()
            pltpu.make_async_copy(ins[k], own, local_sems.at[k]).wait()

    def out_shape(b):
        if shard_major:
            return jax.ShapeDtypeStruct((N_SHARDS, 2 * b.shape[0]) + b.shape[1:], b.dtype)
        return jax.ShapeDtypeStruct((8,) + b.shape, b.dtype)

    return pl.pallas_call(
        body, out_shape=[out_shape(b) for b in blocks], in_specs=[HBM_SPEC] * K, out_specs=[HBM_SPEC] * K,
        scratch_shapes=[pltpu.SemaphoreType.DMA((K, 7)), pltpu.SemaphoreType.DMA((K, 7)), pltpu.SemaphoreType.DMA((K,))],
        name=name)(*blocks)


def chip_exchange(parts, name):
    K = len(parts)

    def body(*refs):
        ins, outs = refs[:K], refs[K:2 * K]
        send_sems, recv_sems, local_sems = refs[2 * K:]
        x, y, c = _place()
        p = 2 * x + y
        chips = _other_chips(x, y)
        waits = []
        for k in range(K):
            _start_in_pieces(lambda s, d, k=k: pltpu.make_async_copy(s, d, local_sems.at[k]), ins[k].at[p], outs[k].at[p])
            for j, (cx, cy) in enumerate(chips):
                q = 2 * cx + cy
                make = lambda s, d, k=k, j=j, to=(cx, cy, c): _remote(s, d, send_sems.at[k, j], recv_sems.at[k, j], to)
                _start_in_pieces(make, ins[k].at[q], outs[k].at[p])
                waits.append(make(ins[k].at[q], outs[k].at[q]))
        for cp in waits:
            cp.wait_recv()
        for cp in waits:
            cp.wait_send()
        for k in range(K):
            pltpu.make_async_copy(ins[k].at[p], outs[k].at[p], local_sems.at[k]).wait()

    return pl.pallas_call(
        body, out_shape=[jax.ShapeDtypeStruct(a.shape, a.dtype) for a in parts],
        in_specs=[HBM_SPEC] * K, out_specs=[HBM_SPEC] * K,
        scratch_shapes=[pltpu.SemaphoreType.DMA((K, 3)), pltpu.SemaphoreType.DMA((K, 3)), pltpu.SemaphoreType.DMA((K,))],
        name=name)(*parts)


PAIR_TILE_BYTES = 2 << 20


def _pair_rows(h, n):
    return _pick(h, tuple(t for t in (512, 256, 128, 64, 32, 16) if t * n * 4 <= PAIR_TILE_BYTES))


def _core_index():
    return jnp.reshape(lax.axis_index("c"), (1,)).astype(jnp.int32)


def pair_reduce(g, *, out_dtype, name):
    S, R, n = g.shape
    h = R // 2
    tr = _pair_rows(h, n)
    nt = h // tr

    def body(c_ref, keep_ref, give_ref, o_ref, recv_ref, send_sems, recv_sems):
        x, y, c = _place()
        slot = (pl.program_id(0) * nt + pl.program_id(1)) % 2
        cp = _remote(give_ref, recv_ref.at[slot], send_sems.at[slot], recv_sems.at[slot], (x, y, 1 - c))
        cp.start()
        cp.wait_recv()
        o_ref[...] = (keep_ref[...] + recv_ref[slot]).astype(o_ref.dtype)
        cp.wait_send()

    g2 = g.reshape(S * R, n)
    grid_spec = pltpu.PrefetchScalarGridSpec(
        num_scalar_prefetch=1, grid=(S, nt),
        in_specs=[pl.BlockSpec((tr, n), lambda q, i, c: ((2 * q + c[0]) * nt + i, 0)),
                  pl.BlockSpec((tr, n), lambda q, i, c: ((2 * q + 1 - c[0]) * nt + i, 0))],
        out_specs=pl.BlockSpec((tr, n), lambda q, i, c: (q * nt + i, 0)),
        scratch_shapes=[pltpu.VMEM((2, tr, n), F32), pltpu.SemaphoreType.DMA((2,)), pltpu.SemaphoreType.DMA((2,))])
    out = pl.pallas_call(body, out_shape=jax.ShapeDtypeStruct((S * h, n), out_dtype), grid_spec=grid_spec,
                         compiler_params=_params("arbitrary", "arbitrary"), name=name)(_core_index(), g2, g2)
    return out.reshape(S, h, n)


def pair_finish(parts, *, name):
    S, h, n = parts.shape
    tr = _pair_rows(h, n)
    nt = h // tr

    def body(c_ref, p_ref, o_ref, recv_ref, send_sem, recv_sem):
        x, y, c = _place()
        phase, i = pl.program_id(0), pl.program_id(1)
        rows = pl.ds(pl.multiple_of(i * tr, 16), tr)

        @pl.when(phase == 0)
        def _():
            acc = p_ref[0].astype(F32)
            for s in range(1, S):
                acc = acc + p_ref[s].astype(F32)
            o_ref[...] = acc
            cp = _remote(o_ref, recv_ref.at[rows], send_sem, recv_sem, (x, y, 1 - c))
            cp.start()
            cp.wait_send()

        @pl.when((phase == 1) & (i == 0))
        def _():
            _remote(recv_ref, recv_ref, send_sem, recv_sem, (x, y, 1 - c)).wait_recv()

        @pl.when(phase == 1)
        def _():
            o_ref[...] = recv_ref[rows]

    grid_spec = pltpu.PrefetchScalarGridSpec(
        num_scalar_prefetch=1, grid=(2, nt),
        in_specs=[pl.BlockSpec((S, tr, n), lambda ph, i, c: (0, jnp.where(ph == 0, i, nt - 1), 0))],
        out_specs=pl.BlockSpec((tr, n), lambda ph, i, c: (jnp.where(ph == 0, c[0], 1 - c[0]) * nt + i, 0)),
        scratch_shapes=[pltpu.VMEM((h, n), F32), pltpu.SemaphoreType.DMA(()), pltpu.SemaphoreType.DMA(())])
    return pl.pallas_call(body, out_shape=jax.ShapeDtypeStruct((2 * h, n), F32), grid_spec=grid_spec,
                          compiler_params=_params("arbitrary", "arbitrary"), name=name)(_core_index(), parts)


def _slab_rows(n):
    return max(8, min(256, (1 << 18) // n // 8 * 8))


def sum_slabs(a, *, name):
    S, h, n = a.shape
    tr = _pick(h, tuple(t for t in (256, 128, 64, 32, 16, 8) if t <= _slab_rows(n)))

    def body(a_ref, o_ref):
        acc = a_ref[0].astype(F32)
        for s in range(1, S):
            acc = acc + a_ref[s].astype(F32)
        o_ref[...] = acc

    return pl.pallas_call(body, out_shape=jax.ShapeDtypeStruct((h, n), F32), grid=(h // tr,),
                          in_specs=[pl.BlockSpec((S, tr, n), lambda r: (0, r, 0))],
                          out_specs=pl.BlockSpec((tr, n), lambda r: (r, 0)), compiler_params=_params("parallel"),
                          name=name)(a)


def _adam_math(w, g, m, v):
    m = ADAM_B1 * m + (1.0 - ADAM_B1) * g
    v = ADAM_B2 * v + (1.0 - ADAM_B2) * (g * g)
    m_hat = m / (1.0 - ADAM_B1 ** ADAM_STEP)
    v_hat = v / (1.0 - ADAM_B2 ** ADAM_STEP)
    delta = -ADAM_LR * (m_hat / (jnp.sqrt(v_hat) + ADAM_EPS) + ADAM_WD * w)
    return delta, m, v


def adamw(w, g, m, v, *, name):
    R, n = w.shape
    tr = _pick(R, tuple(t for t in (256, 128, 64, 32, 16, 8) if t <= _slab_rows(n)))
    spec = pl.BlockSpec((tr, n), lambda r: (r, 0))

    def body(w_ref, g_ref, m_ref, v_ref, d_ref, nm_ref, nv_ref):
        d, nm, nv = _adam_math(w_ref[...], g_ref[...], m_ref[...], v_ref[...])
        d_ref[...] = d
        nm_ref[...] = nm
        nv_ref[...] = nv

    return pl.pallas_call(body, out_shape=[jax.ShapeDtypeStruct(w.shape, F32)] * 3, grid=(R // tr,),
                          in_specs=[spec] * 4, out_specs=[spec] * 3, compiler_params=_params("parallel"),
                          name=name)(w, g, m, v)


MOD_ROWS = 16


def mod_fwd(cact_in, w_mod, b_mod):
    nl, _, cols = w_mod.shape
    tn = 512

    def body(c_ref, w_ref, b_ref, o_ref):
        o_ref[...] = _dot(_silu(c_ref[...]).astype(MM_DTYPE), w_ref[...].astype(MM_DTYPE)) + b_ref[...]

    return pl.pallas_call(
        body, out_shape=jax.ShapeDtypeStruct((nl, MOD_ROWS, cols), F32), grid=(nl, cols // tn),
        in_specs=[pl.BlockSpec((MOD_ROWS, D_MODEL), lambda i, j: (0, 0)),
                  pl.BlockSpec((None, D_MODEL, tn), lambda i, j: (i, 0, j)),
                  pl.BlockSpec((None, 1, tn), lambda i, j: (i, 0, j))],
        out_specs=pl.BlockSpec((None, MOD_ROWS, tn), lambda i, j: (i, 0, j)),
        compiler_params=_params("parallel", "parallel"), name="mod_fwd")(cact_in, w_mod, b_mod)


def mod_update(c_rows, dmod, w, m, v):
    nl, _, cols = w.shape
    tr, tn = 256, 512
    wspec = pl.BlockSpec((None, tr, tn), lambda i, r, j: (i, r, j))

    def body(c_ref, d_ref, w_ref, m_ref, v_ref, g_ref, dl_ref, nm_ref, nv_ref):
        g = _dot_tn(_silu(c_ref[...]).astype(MM_DTYPE), d_ref[...].astype(MM_DTYPE))
        g_ref[...] = g
        dl, nm, nv = _adam_math(w_ref[...], g, m_ref[...], v_ref[...])
        dl_ref[...] = dl
        nm_ref[...] = nm
        nv_ref[...] = nv

    return pl.pallas_call(
        body, out_shape=[jax.ShapeDtypeStruct(w.shape, F32)] * 4, grid=(nl, D_MODEL // tr, cols // tn),
        in_specs=[pl.BlockSpec((MOD_ROWS, tr), lambda i, r, j: (0, r)),
                  pl.BlockSpec((None, MOD_ROWS, tn), lambda i, r, j: (i, 0, j)), wspec, wspec, wspec],
        out_specs=[wspec] * 4, compiler_params=_params("parallel", "parallel", "parallel"),
        name="mod_update")(c_rows, dmod, w, m, v)


def cctx_partial(dmod_ctx, w_mod, c_ctx):
    nl, _, cols = w_mod.shape
    tk = 512
    per = cols // tk
    part = _matmul(dmod_ctx, w_mod, pl.BlockSpec((8, tk), lambda i, j, k: (0, k)),
                   pl.BlockSpec((None, D_MODEL, tk), lambda i, j, k: (k // per, 0, k % per)),
                   pl.BlockSpec((8, D_MODEL), lambda i, j, k: (0, 0)),
                   jax.ShapeDtypeStruct((8, D_MODEL), F32), (1, 1, nl * per), tb=True, name="cctx_partial")

    def body(p_ref, c_ref, o_ref):
        o_ref[...] = 0.5 * jnp.sum(p_ref[...], axis=0, keepdims=True) * _silu_grad(c_ref[...])

    return pl.pallas_call(body, out_shape=jax.ShapeDtypeStruct((1, D_MODEL), F32), name="cctx_scale")(part, c_ctx)


WEIGHT_NAMES = ("c_ctx", "w_mod", "b_mod", "g_pre", "g_post", "w_in", "b_gate", "na_rpb", "pool_w", "pool_scale",
                "conv_w", "ssm_a_re", "ssm_a_im", "ssm_log_dt", "ssm_b_re", "ssm_b_im", "ssm_c_re", "ssm_c_im",
                "ssm_d", "glu_w", "w_br", "w_o")
SSM_NAMES = ("ssm_a_re", "ssm_a_im", "ssm_log_dt", "ssm_b_re", "ssm_b_im", "ssm_c_re", "ssm_c_im")
SMALL_NAMES = ("c_ctx", "b_mod", "g_pre", "g_post", "b_gate", "na_rpb", "pool_w", "pool_scale") + SSM_NAMES + ("ssm_d",)
BIG_NAMES = ("w_in", "glu_w", "w_br", "w_o")
FLAT_COLS = 1024


def _flat(parts):
    v = jnp.concatenate([p.reshape(-1) for p in parts])
    pad = -v.shape[0] % (64 * FLAT_COLS)
    return jnp.pad(v, (0, pad)).reshape(-1, FLAT_COLS)


def _unflat(flat, shapes):
    v = flat.reshape(-1)
    out, off = [], 0
    for s in shapes:
        n = math.prod(s)
        out.append(v[off:off + n].reshape(s))
        off += n
    return out


def kernel(x, c, ctx, c_ctx, w_mod, b_mod, g_pre, g_post, w_in, b_gate, na_rpb, pool_w, pool_scale, conv_w, ssm_a_re, ssm_a_im, ssm_log_dt, ssm_b_re, ssm_b_im, ssm_c_re, ssm_c_im, ssm_d, glu_w, w_br, w_o, loss_target, m_c_ctx, m_w_mod, m_b_mod, m_g_pre, m_g_post, m_w_in, m_b_gate, m_na_rpb, m_pool_w, m_pool_scale, m_conv_w, m_ssm_a_re, m_ssm_a_im, m_ssm_log_dt, m_ssm_b_re, m_ssm_b_im, m_ssm_c_re, m_ssm_c_im, m_ssm_d, m_glu_w, m_w_br, m_w_o, v_c_ctx, v_w_mod, v_b_mod, v_g_pre, v_g_post, v_w_in, v_b_gate, v_na_rpb, v_pool_w, v_pool_scale, v_conv_w, v_ssm_a_re, v_ssm_a_im, v_ssm_log_dt, v_ssm_b_re, v_ssm_b_im, v_ssm_c_re, v_ssm_c_im, v_ssm_d, v_glu_w, v_w_br, v_w_o):
    W = dict(c_ctx=c_ctx, w_mod=w_mod, b_mod=b_mod, g_pre=g_pre, g_post=g_post, w_in=w_in, b_gate=b_gate,
             na_rpb=na_rpb, pool_w=pool_w, pool_scale=pool_scale, conv_w=conv_w, ssm_a_re=ssm_a_re, ssm_a_im=ssm_a_im,
             ssm_log_dt=ssm_log_dt, ssm_b_re=ssm_b_re, ssm_b_im=ssm_b_im, ssm_c_re=ssm_c_re, ssm_c_im=ssm_c_im,
             ssm_d=ssm_d, glu_w=glu_w, w_br=w_br, w_o=w_o)
    M = dict(c_ctx=m_c_ctx, w_mod=m_w_mod, b_mod=m_b_mod, g_pre=m_g_pre, g_post=m_g_post, w_in=m_w_in, b_gate=m_b_gate,
             na_rpb=m_na_rpb, pool_w=m_pool_w, pool_scale=m_pool_scale, conv_w=m_conv_w, ssm_a_re=m_ssm_a_re,
             ssm_a_im=m_ssm_a_im, ssm_log_dt=m_ssm_log_dt, ssm_b_re=m_ssm_b_re, ssm_b_im=m_ssm_b_im,
             ssm_c_re=m_ssm_c_re, ssm_c_im=m_ssm_c_im, ssm_d=m_ssm_d, glu_w=m_glu_w, w_br=m_w_br, w_o=m_w_o)
    V = dict(c_ctx=v_c_ctx, w_mod=v_w_mod, b_mod=v_b_mod, g_pre=v_g_pre, g_post=v_g_post, w_in=v_w_in, b_gate=v_b_gate,
             na_rpb=v_na_rpb, pool_w=v_pool_w, pool_scale=v_pool_scale, conv_w=v_conv_w, ssm_a_re=v_ssm_a_re,
             ssm_a_im=v_ssm_a_im, ssm_log_dt=v_ssm_log_dt, ssm_b_re=v_ssm_b_re, ssm_b_im=v_ssm_b_im,
             ssm_c_re=v_ssm_c_re, ssm_c_im=v_ssm_c_im, ssm_d=v_ssm_d, glu_w=v_glu_w, w_br=v_w_br, w_o=v_w_o)
    nl = w_in.shape[0]
    xi, yi, ci = _place()
    chip = 2 * xi + yi
    example = 4 * xi + 2 * yi + ci
    mod_cols = w_mod.shape[2]

    def my_half(a):
        half = a.shape[0] // 2
        return lax.dynamic_slice_in_dim(a, ci * half, half, axis=0).astype(MM_DTYPE)

    gathered = []
    for i in range(nl):
        g = all_gather8([my_half(W[n][i]) for n in BIG_NAMES], name="gather_weights", shard_major=True)
        gathered.append(dict(zip(BIG_NAMES, g)))

    c8 = jnp.pad(c, ((0, 7), (0, 0)))
    c_all, = all_gather8([c8], name="gather_c")
    c_rows = jnp.concatenate([c_all[:, 0], jnp.broadcast_to(c_ctx[None], (8, D_MODEL))], axis=0)
    b_cols = lax.dynamic_slice_in_dim(b_mod, chip * mod_cols, mod_cols, axis=1)[:, None]
    mod_part = mod_fwd(c_rows, w_mod, b_cols)
    conv_part = jnp.pad(conv_w.reshape(nl * 3, -1), ((0, 16 - nl * 3), (0, 0)))
    parts, conv_all = all_gather8([mod_part.reshape(nl * MOD_ROWS, mod_cols), conv_part], name="gather_mod")
    mod_full = jnp.concatenate([parts[2 * p].reshape(nl, MOD_ROWS, mod_cols) for p in range(N_SHARDS)], axis=-1)
    conv_full = jnp.concatenate([conv_all[2 * p][:nl * 3].reshape(nl, 3, -1) for p in range(N_SHARDS)], axis=-1)
    own = lax.dynamic_index_in_dim(mod_full, example, axis=1, keepdims=False)
    mods = [jnp.stack([own[i], mod_full[i, 8]]) for i in range(nl)]

    layers = []
    for i in range(nl):
        gw = gathered[i]
        layers.append(dict(
            w_in=gw["w_in"], w_br=gw["w_br"].reshape(D_MODEL, D_MODEL), w_o=gw["w_o"].reshape(D_MODEL, D_MODEL),
            glu_w=gw["glu_w"].transpose(1, 0, 2).reshape(BRANCH, 2 * BRANCH),
            conv_w=jnp.pad(conv_full[i], ((0, 5), (0, 0))), pool_w=pool_w[i], pool_scale=pool_scale[i][None],
            b_gate=b_gate[i][None], g_pre=g_pre[i][None], g_post=g_post[i][None], na_rpb=na_rpb[i],
            ssm_d=ssm_d[i][None], ssm_params=tuple(W[n][i] for n in SSM_NAMES)))

    loss_local, grad_x, grads = local_step(x[0], ctx[0], loss_target[0], mods, layers)
    loss = lax.psum(loss_local, ("x", "y", "c"))

    dmod_local = jnp.stack([g["mod"] for g in grads])
    dmod_all, = all_gather8([jnp.pad(dmod_local.reshape(nl * 2, -1), ((0, 8 - nl * 2), (0, 0)))], name="gather_dmod")
    dmod_all = dmod_all[:, :nl * 2].reshape(8, nl, 2, 3 * D_MODEL)
    dmod_rows = jnp.concatenate([dmod_all[:, :, 0], dmod_all[:, :, 1]], axis=0).transpose(1, 0, 2)
    dmod_cols = lax.dynamic_slice_in_dim(dmod_rows, chip * mod_cols, mod_cols, axis=2)
    g_w_mod, d_w_mod, nm_w_mod, nv_w_mod = mod_update(c_rows, dmod_cols, w_mod, m_w_mod, v_w_mod)
    dctx_cols = dmod_cols[:, 8:].transpose(1, 0, 2).reshape(8, nl * mod_cols)
    g_cctx_part = cctx_partial(dctx_cols, w_mod, c_ctx[None])[0]

    def small_grad(n):
        if n == "c_ctx":
            return g_cctx_part
        if n == "b_mod":
            return jnp.stack([g["mod"][0] + g["mod"][1] for g in grads])
        if n in SSM_NAMES:
            return jnp.stack([g["ssm"][SSM_NAMES.index(n)] for g in grads])
        return jnp.stack([g[n] for g in grads])

    conv_grad_full = jnp.stack([g["conv_w"] for g in grads])
    flat_g = _flat([small_grad(n) for n in SMALL_NAMES] + [conv_grad_full])
    flat_all, = all_gather8([flat_g], name="gather_small_grads")
    flat_sum = sum_slabs(flat_all, name="sum_small_grads")
    small_shapes = [W[n].shape for n in SMALL_NAMES]
    small_g = _unflat(flat_sum, small_shapes + [conv_grad_full.shape])
    conv_g = lax.dynamic_slice_in_dim(small_g[-1], chip * conv_w.shape[2], conv_w.shape[2], axis=2)
    adam_names = SMALL_NAMES + ("conv_w",)
    adam_shapes = small_shapes + [conv_w.shape]
    g_list = small_g[:-1] + [conv_g]
    upd = adamw(_flat([W[n] for n in adam_names]), _flat(g_list), _flat([M[n] for n in adam_names]),
                _flat([V[n] for n in adam_names]), name="adamw_small")
    G = dict(zip(adam_names, g_list))
    DL, NM, NV = (dict(zip(adam_names, _unflat(u, adam_shapes))) for u in upd)
    G["w_mod"], DL["w_mod"], NM["w_mod"], NV["w_mod"] = g_w_mod, d_w_mod, nm_w_mod, nv_w_mod

    big = {n: [] for n in BIG_NAMES}
    for i in range(nl):
        g = grads[i]
        local = [g["w_in"], g["glu_w"].reshape(BRANCH, N_SHARDS, -1).transpose(1, 0, 2),
                 g["w_br"].reshape(N_SHARDS, BRANCH, D_MODEL), g["w_o"].reshape(N_SHARDS, BRANCH, D_MODEL)]
        partial = [pair_reduce(a, out_dtype=MM_DTYPE, name=f"grad_pair_{n}") for n, a in zip(BIG_NAMES, local)]
        arrived = chip_exchange(partial, name="grad_exchange")
        full = [pair_finish(a, name=f"grad_finish_{n}") for n, a in zip(BIG_NAMES, arrived)]
        for n, a in zip(BIG_NAMES, full):
            big[n].append(a)
    for n in BIG_NAMES:
        g = jnp.stack(big[n])
        rows = g.shape[0] * g.shape[1]
        d, nm, nv = adamw(W[n].reshape(rows, -1), g.reshape(rows, -1), M[n].reshape(rows, -1), V[n].reshape(rows, -1),
                          name=f"adamw_{n}")
        G[n], DL[n], NM[n], NV[n] = g, d.reshape(g.shape), nm.reshape(g.shape), nv.reshape(g.shape)

    out = [loss, grad_x[None]]
    for group in (G, DL, NM, NV):
        out += [group[n].reshape(W[n].shape) for n in WEIGHT_NAMES]
    return tuple(out)
```

```python
import functools
import math

import numpy as np
import jax
import jax.numpy as jnp
from jax import lax
from jax.experimental import pallas as pl
from jax.experimental.pallas import tpu as pltpu

F32 = jnp.float32
BF16 = jnp.bfloat16
MM_DTYPE = jnp.bfloat16

D_MODEL = 2048
BRANCH = 512
N_HEADS = 8
HEAD_DIM = 64
GRID_W = 64
WIN_ROWS = 8
WIN_COLS = 16
POOL_GROUPS = 4
POOL_DIM = 128
SSM_GROUPS = 32
SSM_GDIM = 16
SSM_STATE = 64
N_STATE = SSM_GROUPS * SSM_STATE
IN_TOTAL = 14336
RMS_EPS = 1e-6
NEG_INF = -1e30
COL = dict(q=0, k=512, v=1024, na_z=1536, pool_u=2048, pool_z=2560, conv_x=3072, conv_b=3584,
           conv_c=4096, conv_z=4608, ssm_u=5120, ssm_z=5632, merge=6144)
N_SHARDS = 4
W_IN_SHARD = IN_TOTAL // N_SHARDS
VMEM_LIMIT_BYTES = 48 * 1024 * 1024
ROW_TILE = 256

ADAM_LR = 0.001
ADAM_B1 = 0.9
ADAM_B2 = 0.999
ADAM_EPS = 1e-08
ADAM_WD = 0.01
ADAM_STEP = 10


def _params(*sem):
    return pltpu.CompilerParams(dimension_semantics=sem, vmem_limit_bytes=VMEM_LIMIT_BYTES)


def _sigmoid(x):
    return 1.0 / (1.0 + jnp.exp(-x))


def _matmul(a, b, a_spec, b_spec, o_spec, out_shape, grid, *, ta=False, tb=False, name):
    nk = grid[-1]
    kaxis = len(grid) - 1
    dims = (((0,) if ta else (1,), (1,) if tb else (0,)), ((), ()))

    def body(a_ref, b_ref, o_ref, *scratch):
        p = lax.dot_general(a_ref[...].astype(MM_DTYPE), b_ref[...].astype(MM_DTYPE), dims,
                            preferred_element_type=F32)
        if nk == 1:
            o_ref[...] = p.astype(o_ref.dtype)
            return
        acc_ref, = scratch
        k = pl.program_id(kaxis)

        @pl.when(k == 0)
        def _():
            acc_ref[...] = p

        @pl.when(k > 0)
        def _():
            acc_ref[...] += p

        @pl.when(k == nk - 1)
        def _():
            o_ref[...] = acc_ref[...].astype(o_ref.dtype)

    oblock = tuple(s for s in o_spec.block_shape if s is not None)
    scratch = [] if nk == 1 else [pltpu.VMEM(oblock, F32)]
    sem = ("parallel",) * (len(grid) - 1) + ("arbitrary",)
    return pl.pallas_call(body, out_shape=out_shape, grid=grid, in_specs=[a_spec, b_spec],
                          out_specs=o_spec, scratch_shapes=scratch, compiler_params=_params(*sem),
                          name=name)(a, b)


def _pick(n, cands):
    for c in cands:
        if n % c == 0:
            return c
    raise ValueError(f"no tile for {n}")


def _row_tile(T):
    return _pick(T, (544, 512, 256, 128))


def mm_nn(a, b, *, out_dtype, name, tn=512, a_rows=None, o_rows=None, a_cols=None):
    M = a.shape[0]
    c0, K = a_cols or (0, a.shape[1])
    N = b.shape[1]
    tm = ROW_TILE if (a_rows or o_rows) else _row_tile(M)
    tn = min(tn, N)
    tk = K if K <= 2048 else _pick(K, (2048, 1024, 512))
    kb0 = c0 // tk
    ar = a_rows or (lambda i: i)
    orr = o_rows or (lambda i: i)
    return _matmul(a, b, pl.BlockSpec((tm, tk), lambda i, j, k: (ar(i), kb0 + k)),
                   pl.BlockSpec((tk, tn), lambda i, j, k: (k, j)),
                   pl.BlockSpec((tm, tn), lambda i, j, k: (orr(i), j)),
                   jax.ShapeDtypeStruct((M, N), out_dtype), (M // tm, N // tn, K // tk), name=name)


def mm_nt(a, b, *, out_dtype, name, a_rows=None, o_rows=None):
    M, K = a.shape
    N = b.shape[0]
    tm = ROW_TILE if (a_rows or o_rows) else _row_tile(M)
    tn = min(N, 2048)
    tk = K if K <= 1024 else _pick(K, (1024, 512))
    ar = a_rows or (lambda i: i)
    orr = o_rows or (lambda i: i)
    return _matmul(a, b, pl.BlockSpec((tm, tk), lambda i, j, k: (ar(i), k)),
                   pl.BlockSpec((tn, tk), lambda i, j, k: (j, k)),
                   pl.BlockSpec((tm, tn), lambda i, j, k: (orr(i), j)),
                   jax.ShapeDtypeStruct((M, N), out_dtype), (M // tm, N // tn, K // tk), tb=True, name=name)


def mm_tn(a, b, *, out_dtype, name, a_rows=None, b_rows=None, tm=512, tn=1024, a_cols=None):
    K = a.shape[0]
    c0, M = a_cols or (0, a.shape[1])
    N = b.shape[1]
    tk = ROW_TILE if (a_rows or b_rows) else K
    tm = min(tm, M)
    tn = min(tn, N)
    mb0 = c0 // tm
    ar = a_rows or (lambda k: k)
    br = b_rows or (lambda k: k)
    return _matmul(a, b, pl.BlockSpec((tk, tm), lambda i, j, k: (ar(k), mb0 + i)),
                   pl.BlockSpec((tk, tn), lambda i, j, k: (br(k), j)),
                   pl.BlockSpec((tm, tn), lambda i, j, k: (i, j)),
                   jax.ShapeDtypeStruct((M, N), out_dtype), (M // tm, N // tn, K // tk), ta=True, name=name)


def _ew(fn, ins, outs, colsums, *, T, L, name):
    tb = ROW_TILE
    nlat = L // tb
    seg = lambda i: jnp.where(i >= nlat, 1, 0)
    in_specs, arrays = [], []
    for arr, kind, cb, width in ins:
        arrays.append(arr)
        if kind == "row":
            in_specs.append(pl.BlockSpec((tb, width), lambda i, cb=cb: (i, cb)))
        elif kind == "bcast":
            in_specs.append(pl.BlockSpec((1, width), lambda i, cb=cb: (0, cb)))
        else:
            in_specs.append(pl.BlockSpec((None, 1, width), lambda i, cb=cb: (seg(i), 0, cb)))
    out_specs = [pl.BlockSpec((tb, w), lambda i: (i, 0)) for w, _ in outs]
    out_shapes = [jax.ShapeDtypeStruct((T, w), dt) for w, dt in outs]
    out_specs += [pl.BlockSpec((None, 1, w), lambda i: (seg(i), 0, 0)) for w in colsums]
    out_shapes += [jax.ShapeDtypeStruct((2, 1, w), F32) for w in colsums]
    n_in, n_out = len(ins), len(outs)

    def body(*refs):
        i = pl.program_id(0)
        res = fn(*[r[...] for r in refs[:n_in]])
        for r, v in zip(refs[n_in:n_in + n_out], res[:n_out]):
            r[...] = v.astype(r.dtype)
        first = (i == 0) | (i == nlat)
        for r, v in zip(refs[n_in + n_out:], res[n_out:]):
            s = jnp.sum(v, axis=0, keepdims=True)

            @pl.when(first)
            def _(r=r, s=s):
                r[...] = s

            @pl.when(jnp.logical_not(first))
            def _(r=r, s=s):
                r[...] += s

    res = pl.pallas_call(body, out_shape=out_shapes, grid=(T // tb,), in_specs=in_specs,
                         out_specs=out_specs, compiler_params=_params("arbitrary"), name=name)(*arrays)
    return res


def _rms(x):
    return lax.rsqrt(jnp.mean(x * x, axis=-1, keepdims=True) + RMS_EPS)


def prenorm_fwd(X, g, scale, shift, *, L):
    T = X.shape[0]

    def fn(x, g, sc, sh):
        return ((x * _rms(x)) * (g * (1.0 + sc)) + sh,)

    h, = _ew(fn, [(X, "row", 0, D_MODEL), (g, "bcast", 0, D_MODEL), (scale, "seg", 0, D_MODEL),
                  (shift, "seg", 0, D_MODEL)], [(D_MODEL, MM_DTYPE)], [], T=T, L=L, name="prenorm_fwd")
    return h


def prenorm_bwd(dh, X, g, scale, dres, *, L):
    T = X.shape[0]

    def fn(dh, x, g, sc, dres):
        r = _rms(x)
        xn = x * r
        dxn = dh * (g * (1.0 + sc))
        dx = r * (dxn - xn * jnp.mean(dxn * xn, axis=-1, keepdims=True))
        return dres + dx, dh, dh * xn

    return _ew(fn, [(dh, "row", 0, D_MODEL), (X, "row", 0, D_MODEL), (g, "bcast", 0, D_MODEL),
                    (scale, "seg", 0, D_MODEL), (dres, "row", 0, D_MODEL)],
               [(D_MODEL, F32)], [D_MODEL, D_MODEL], T=T, L=L, name="prenorm_bwd")


def postnorm_fwd(X, y, g, gate, *, L):
    T = X.shape[0]

    def fn(x, y, g, gate):
        return (x + gate * ((y * _rms(y)) * g),)

    out, = _ew(fn, [(X, "row", 0, D_MODEL), (y, "row", 0, D_MODEL), (g, "bcast", 0, D_MODEL),
                    (gate, "seg", 0, D_MODEL)], [(D_MODEL, F32)], [], T=T, L=L, name="postnorm_fwd")
    return out


def postnorm_bwd(dX, y, g, gate, *, L):
    T = dX.shape[0]

    def fn(dx, y, g, gate):
        r = _rms(y)
        yn = y * r
        dyn = dx * (gate * g)
        dy = r * (dyn - yn * jnp.mean(dyn * yn, axis=-1, keepdims=True))
        return dy, dx * yn

    return _ew(fn, [(dX, "row", 0, D_MODEL), (y, "row", 0, D_MODEL), (g, "bcast", 0, D_MODEL),
                    (gate, "seg", 0, D_MODEL)], [(D_MODEL, MM_DTYPE)], [D_MODEL], T=T, L=L, name="postnorm_bwd")


def loss_and_grad(X, target, *, L):
    T = X.shape[0]
    tb = ROW_TILE
    nlat = L // tb

    def body(x_ref, t_ref, dx_ref, part_ref):
        i = pl.program_id(0)

        @pl.when(i < nlat)
        def _():
            err = x_ref[...] - t_ref[...]
            dx_ref[...] = err * (1.0 / D_MODEL)
            part_ref[...] = jnp.full(part_ref.shape, 0.5 / D_MODEL * jnp.sum(err * err), F32)

        @pl.when(i >= nlat)
        def _():
            dx_ref[...] = jnp.zeros(dx_ref.shape, F32)
            part_ref[...] = jnp.zeros(part_ref.shape, F32)

    dx, part = pl.pallas_call(
        body, out_shape=[jax.ShapeDtypeStruct((T, D_MODEL), F32), jax.ShapeDtypeStruct((T // tb, 8, 128), F32)],
        grid=(T // tb,),
        in_specs=[pl.BlockSpec((tb, D_MODEL), lambda i: (i, 0)),
                  pl.BlockSpec((tb, D_MODEL), lambda i: (jnp.minimum(i, nlat - 1), 0))],
        out_specs=[pl.BlockSpec((tb, D_MODEL), lambda i: (i, 0)), pl.BlockSpec((None, 8, 128), lambda i: (i, 0, 0))],
        compiler_params=_params("parallel"), name="loss_and_grad")(X, target)
    return jnp.sum(part[:, 0, 0]), dx


Q_BLOCK = WIN_ROWS * GRID_W
BAND = 2 * WIN_ROWS * GRID_W


PAIR_TILES = 2 * WIN_ROWS
HEAD_PAIRS = N_HEADS // 2
LANES = 2 * HEAD_DIM
ROW_SHIFT = GRID_W.bit_length() - 1


def bias_pair_tiles(rpb):
    col = np.arange(GRID_W)
    col_start = np.clip(col - WIN_COLS // 2, 0, GRID_W - WIN_COLS)
    in_win = (col[None, :] >= col_start[:, None]) & (col[None, :] < col_start[:, None] + WIN_COLS)
    dcol = np.clip(col[None, :] - col[:, None] + (WIN_COLS - 1), 0, 2 * WIN_COLS - 2)
    E = np.stack([(dcol == dc) & in_win for dc in range(2 * WIN_COLS - 1)]).astype(np.float32)
    tiles = jnp.einsum("hrd,dqk->hrqk", rpb, E, precision=lax.Precision.HIGHEST)
    z = jnp.zeros((N_HEADS, 1, GRID_W, GRID_W), F32)
    return jnp.concatenate([jnp.concatenate([z, tiles], axis=1), jnp.concatenate([tiles, z], axis=1)], axis=-1)


def _band_row(i, rows):
    return jnp.clip(WIN_ROWS * i - WIN_ROWS // 2, 0, rows - 2 * WIN_ROWS)


def _band_start(i, rows):
    return pl.multiple_of(_band_row(i, rows) * GRID_W, 256)


def _window_mask(i, rows):
    r = lax.broadcasted_iota(jnp.int32, (Q_BLOCK, BAND), 0)
    k = lax.broadcasted_iota(jnp.int32, (Q_BLOCK, BAND), 1)
    qr, qc = WIN_ROWS * i + (r >> ROW_SHIFT), r & (GRID_W - 1)
    kr, kc = _band_row(i, rows) + (k >> ROW_SHIFT), k & (GRID_W - 1)
    ws = jnp.clip(qr - WIN_ROWS // 2, 0, rows - WIN_ROWS)
    cs = jnp.clip(qc - WIN_COLS // 2, 0, GRID_W - WIN_COLS)
    return (kr >= ws) & (kr < ws + WIN_ROWS) & (kc >= cs) & (kc < cs + WIN_COLS)


def _pair_index(i, rows, a, j):
    off = _band_row(i, rows) - WIN_ROWS * i
    return jnp.clip(2 * j - a + WIN_ROWS + off, 0, PAIR_TILES - 1)


def _band_bias(p_ref, hh, i, rows):
    bands = [jnp.concatenate([p_ref[hh, _pair_index(i, rows, a, j)] for j in range(WIN_ROWS)], axis=1)
             for a in range(WIN_ROWS)]
    return jnp.concatenate(bands, axis=0)


def _dot_nt(a, b):
    return lax.dot_general(a, b, (((1,), (1,)), ((), ())), preferred_element_type=F32)


def _dot_tn(a, b):
    return lax.dot_general(a, b, (((0,), (0,)), ((), ())), preferred_element_type=F32)


def _dot(a, b):
    return jnp.dot(a, b, preferred_element_type=F32)


QKV_BLOCKS = tuple(COL[n] // LANES for n in ("q", "k", "v"))
SCALE = HEAD_DIM ** -0.5


def _head(x, hh):
    return x[:, hh * HEAD_DIM:(hh + 1) * HEAD_DIM]


def _both(fn):
    res = [fn(0), fn(1)]
    return [jnp.concatenate([a, b], axis=1) for a, b in zip(*res)]


def attn_fwd(proj, ptiles, *, L):
    T = proj.shape[0]
    N = T - L
    rows, nq = L // GRID_W, L // Q_BLOCK
    qb, kb, vb = QKV_BLOCKS

    def body(q_ref, k_ref, v_ref, p_ref, o_ref, lse_ref):
        i = pl.program_id(1)
        ks = _band_start(i, rows)
        mask = _window_mask(i, rows)
        qv = q_ref[...].astype(MM_DTYPE)
        kband, vband = k_ref[pl.ds(ks, BAND), :].astype(MM_DTYPE), v_ref[pl.ds(ks, BAND), :].astype(MM_DTYPE)
        kctx, vctx = k_ref[pl.ds(L, N), :].astype(MM_DTYPE), v_ref[pl.ds(L, N), :].astype(MM_DTYPE)

        def head(hh):
            q = _head(qv, hh)
            sb = _dot_nt(q, _head(kband, hh)) * SCALE + jnp.where(mask, _band_bias(p_ref, hh, i, rows), NEG_INF)
            sc = _dot_nt(q, _head(kctx, hh)) * SCALE
            m = jnp.maximum(jnp.max(sb, axis=-1, keepdims=True), jnp.max(sc, axis=-1, keepdims=True))
            pb, pc = jnp.exp(sb - m), jnp.exp(sc - m)
            l = jnp.sum(pb, axis=-1, keepdims=True) + jnp.sum(pc, axis=-1, keepdims=True)
            o = _dot(pb.astype(MM_DTYPE), _head(vband, hh)) + _dot(pc.astype(MM_DTYPE), _head(vctx, hh))
            return o / l, jnp.broadcast_to(m + jnp.log(l), (Q_BLOCK, HEAD_DIM))

        o_ref[...], lse_ref[...] = _both(head)

    qspec = lambda b0: pl.BlockSpec((Q_BLOCK, LANES), lambda hp, i: (i, b0 + hp))
    kspec = lambda b0: pl.BlockSpec((T, LANES), lambda hp, i: (0, b0 + hp))
    return pl.pallas_call(
        body, out_shape=[jax.ShapeDtypeStruct((T, BRANCH), F32), jax.ShapeDtypeStruct((L, BRANCH), F32)],
        grid=(HEAD_PAIRS, nq),
        in_specs=[qspec(qb), kspec(kb), kspec(vb),
                  pl.BlockSpec((2, PAIR_TILES, GRID_W, LANES), lambda hp, i: (hp, 0, 0, 0))],
        out_specs=[qspec(0), qspec(0)],
        compiler_params=_params("parallel", "arbitrary"), name="attn_fwd")(proj, proj, proj, ptiles)


def attn_bwd(proj, ptiles, o, do, lse, *, L):
    T = proj.shape[0]
    N = T - L
    rows, nq = L // GRID_W, L // Q_BLOCK
    qb, kb, vb = QKV_BLOCKS

    def body(q_ref, k_ref, v_ref, p_ref, o_ref, do_ref, lse_ref, dq_ref, dk_ref, dv_ref, dp_ref):
        i = pl.program_id(1)
        ks = _band_start(i, rows)

        @pl.when(i == 0)
        def _():
            dk_ref[...] = jnp.zeros(dk_ref.shape, F32)
            dv_ref[...] = jnp.zeros(dv_ref.shape, F32)
            dp_ref[...] = jnp.zeros(dp_ref.shape, F32)

        mask = _window_mask(i, rows)
        qv = q_ref[...].astype(MM_DTYPE)
        kband, vband = k_ref[pl.ds(ks, BAND), :].astype(MM_DTYPE), v_ref[pl.ds(ks, BAND), :].astype(MM_DTYPE)
        kctx, vctx = k_ref[pl.ds(L, N), :].astype(MM_DTYPE), v_ref[pl.ds(L, N), :].astype(MM_DTYPE)
        ov, dof, lsev = o_ref[...], do_ref[...], lse_ref[...]

        def head(hh):
            q, kb_h, kc_h, vb_h, vc_h = (_head(t, hh) for t in (qv, kband, kctx, vband, vctx))
            lse = _head(lsev, hh)[:, 0:1]
            pb = jnp.exp(_dot_nt(q, kb_h) * SCALE + jnp.where(mask, _band_bias(p_ref, hh, i, rows), NEG_INF) - lse)
            pc = jnp.exp(_dot_nt(q, kc_h) * SCALE - lse)
            do_h = _head(dof, hh)
            delta = jnp.sum(do_h * _head(ov, hh), axis=-1, keepdims=True)
            dov = do_h.astype(MM_DTYPE)
            dsb = pb * (_dot_nt(dov, vb_h) - delta)
            dsc = pc * (_dot_nt(dov, vc_h) - delta)
            for a in range(WIN_ROWS):
                for j in range(WIN_ROWS):
                    dp_ref[hh, _pair_index(i, rows, a, j)] += dsb[a * GRID_W:(a + 1) * GRID_W, j * LANES:(j + 1) * LANES]
            dsb_s, dsc_s = (dsb * SCALE).astype(MM_DTYPE), (dsc * SCALE).astype(MM_DTYPE)
            dq = _dot(dsb_s, kb_h) + _dot(dsc_s, kc_h)
            return (dq, _dot_tn(dsb_s, q), _dot_tn(dsc_s, q), _dot_tn(pb.astype(MM_DTYPE), dov),
                    _dot_tn(pc.astype(MM_DTYPE), dov))

        dq, dkb, dkc, dvb, dvc = _both(head)
        dq_ref[...] = dq
        dk_ref[pl.ds(ks, BAND), :] += dkb
        dk_ref[pl.ds(L, N), :] += dkc
        dv_ref[pl.ds(ks, BAND), :] += dvb
        dv_ref[pl.ds(L, N), :] += dvc

    qspec = lambda b0: pl.BlockSpec((Q_BLOCK, LANES), lambda hp, i: (i, b0 + hp))
    kspec = lambda b0: pl.BlockSpec((T, LANES), lambda hp, i: (0, b0 + hp))
    pspec = pl.BlockSpec((2, PAIR_TILES, GRID_W, LANES), lambda hp, i: (hp, 0, 0, 0))
    return pl.pallas_call(
        body,
        out_shape=[jax.ShapeDtypeStruct((T, BRANCH), F32)] * 3 + [jax.ShapeDtypeStruct(ptiles.shape, F32)],
        grid=(HEAD_PAIRS, nq),
        in_specs=[qspec(qb), kspec(kb), kspec(vb), pspec, qspec(0), qspec(0), qspec(0)],
        out_specs=[qspec(0), kspec(0), kspec(0), pspec],
        compiler_params=_params("parallel", "arbitrary"), name="attn_bwd")(proj, proj, proj, ptiles, o, do, lse)


ANY_SPEC = pl.BlockSpec(memory_space=pl.ANY)


def cattn_fwd(proj, o, *, L):
    T = proj.shape[0]
    N = T - L
    qb, kb, vb = QKV_BLOCKS
    cspec = lambda b0: pl.BlockSpec((N, LANES), lambda hp: (L // N, b0 + hp))

    def body(q_ref, k_ref, v_ref, o_in, o_ref, lse_ref):
        qv, kv, vv = (r[...].astype(MM_DTYPE) for r in (q_ref, k_ref, v_ref))

        def head(hh):
            s = _dot_nt(_head(qv, hh), _head(kv, hh)) * SCALE
            m = jnp.max(s, axis=-1, keepdims=True)
            p = jnp.exp(s - m)
            l = jnp.sum(p, axis=-1, keepdims=True)
            return _dot(p.astype(MM_DTYPE), _head(vv, hh)) / l, jnp.broadcast_to(m + jnp.log(l), (N, HEAD_DIM))

        o_ref[...], lse_ref[...] = _both(head)

    return pl.pallas_call(
        body, out_shape=[jax.ShapeDtypeStruct(o.shape, F32), jax.ShapeDtypeStruct((N, BRANCH), F32)],
        grid=(HEAD_PAIRS,), in_specs=[cspec(qb), cspec(kb), cspec(vb), ANY_SPEC],
        out_specs=[cspec(0), pl.BlockSpec((N, LANES), lambda hp: (0, hp))], input_output_aliases={3: 0},
        compiler_params=_params("parallel"), name="cattn_fwd")(proj, proj, proj, o)


def cattn_bwd(proj, o, do, lse, dq, dk, dv, *, L):
    T = proj.shape[0]
    N = T - L
    qb, kb, vb = QKV_BLOCKS
    cspec = lambda b0: pl.BlockSpec((N, LANES), lambda hp: (L // N, b0 + hp))

    def body(q_ref, k_ref, v_ref, o_ref, do_ref, lse_ref, dq_in, dk_in, dv_in, dq_ref, dk_ref, dv_ref):
        qv, kv, vv = (r[...].astype(MM_DTYPE) for r in (q_ref, k_ref, v_ref))
        ov, dof, lsev = o_ref[...], do_ref[...], lse_ref[...]

        def head(hh):
            q, k, v = _head(qv, hh), _head(kv, hh), _head(vv, hh)
            p = jnp.exp(_dot_nt(q, k) * SCALE - _head(lsev, hh)[:, 0:1])
            do_h = _head(dof, hh)
            delta = jnp.sum(do_h * _head(ov, hh), axis=-1, keepdims=True)
            dov = do_h.astype(MM_DTYPE)
            ds = (p * (_dot_nt(dov, v) - delta) * SCALE).astype(MM_DTYPE)
            return _dot(ds, k), _dot_tn(ds, q), _dot_tn(p.astype(MM_DTYPE), dov)

        dq_c, dk_c, dv_c = _both(head)
        dq_ref[...] = dq_c
        dk_ref[...] = dk_in[...] + dk_c
        dv_ref[...] = dv_in[...] + dv_c

    return pl.pallas_call(
        body, out_shape=[jax.ShapeDtypeStruct(dq.shape, F32)] * 3, grid=(HEAD_PAIRS,),
        in_specs=[cspec(qb), cspec(kb), cspec(vb), cspec(0), cspec(0), pl.BlockSpec((N, LANES), lambda hp: (0, hp)),
                  ANY_SPEC, cspec(0), cspec(0)],
        out_specs=[cspec(0)] * 3, input_output_aliases={6: 0, 7: 1, 8: 2},
        compiler_params=_params("parallel"), name="cattn_bwd")(proj, proj, proj, o, do, lse, dq, dk, dv)


PAD = 16


def _row_ids(T):
    return lax.broadcasted_iota(jnp.int32, (T, POOL_DIM), 0)


def _same_segment(t, s, L, T):
    return (s >= 0) & (s < T) & ((t < L) == (s < L))


def _window_sum(buf_ref, x, half, *, L, T, transpose):
    buf_ref[pl.ds(PAD, T), :] = x
    t = _row_ids(T)
    acc = jnp.zeros((T, POOL_DIM), F32)
    for j in range(-8, 9):
        inside = ((j > -half) & (j <= half)) if transpose else ((j >= -half) & (j < half))
        ok = _same_segment(t, t + j, L, T) & inside
        acc = acc + jnp.where(ok, buf_ref[pl.ds(PAD + j, T), :], 0.0)
    return acc


def _window_count(half, *, L, T):
    t = _row_ids(T)
    pos = jnp.where(t < L, t, t - L)
    seg_len = jnp.where(t < L, L, T - L)
    return (jnp.minimum(pos + half, seg_len) - jnp.maximum(pos - half, 0)).astype(F32)


def _zero_pads(buf_ref, T):
    buf_ref[pl.ds(0, PAD), :] = jnp.zeros((PAD, POOL_DIM), F32)
    buf_ref[pl.ds(PAD + T, PAD), :] = jnp.zeros((PAD, POOL_DIM), F32)


def pool_fwd(proj, pool_w, pool_scale, *, L):
    T = proj.shape[0]
    cb0 = COL["pool_u"] // POOL_DIM

    def body(u_ref, w_ref, s_ref, o_ref, p_ref, buf_ref):
        half = jnp.left_shift(1, pl.program_id(0))
        _zero_pads(buf_ref, T)
        u = u_ref[...]
        pooled = _window_sum(buf_ref, u, half, L=L, T=T, transpose=False) / _window_count(half, L=L, T=T) - u
        pm = pooled.astype(MM_DTYPE)
        p_ref[...] = pm
        o_ref[...] = _dot(pm, w_ref[...].astype(MM_DTYPE)) * s_ref[...]

    cspec = pl.BlockSpec((T, POOL_DIM), lambda g: (0, g))
    return pl.pallas_call(
        body, out_shape=[jax.ShapeDtypeStruct((T, BRANCH), F32), jax.ShapeDtypeStruct((T, BRANCH), MM_DTYPE)],
        grid=(POOL_GROUPS,),
        in_specs=[pl.BlockSpec((T, POOL_DIM), lambda g: (0, cb0 + g)),
                  pl.BlockSpec((None, POOL_DIM, POOL_DIM), lambda g: (g, 0, 0)),
                  pl.BlockSpec((1, POOL_DIM), lambda g: (0, g))],
        out_specs=[cspec, cspec], scratch_shapes=[pltpu.VMEM((T + 2 * PAD, POOL_DIM), F32)],
        compiler_params=_params("parallel"), name="pool_fwd")(proj, pool_w, pool_scale)


def pool_bwd(do, pooled, pool_w, pool_scale, *, L):
    T = do.shape[0]

    def body(do_ref, p_ref, w_ref, s_ref, du_ref, dw_ref, ds_ref, buf_ref):
        half = jnp.left_shift(1, pl.program_id(0))
        _zero_pads(buf_ref, T)
        pm = p_ref[...]
        w = w_ref[...].astype(MM_DTYPE)
        mixed = _dot(pm, w)
        dov = do_ref[...]
        ds_ref[...] = jnp.broadcast_to(jnp.sum(dov * mixed, axis=0, keepdims=True), ds_ref.shape)
        dmixed = (dov * s_ref[...]).astype(MM_DTYPE)
        dw_ref[...] = _dot_tn(pm, dmixed)
        dpooled = _dot_nt(dmixed, w)
        scaled = dpooled / _window_count(half, L=L, T=T)
        du = _window_sum(buf_ref, scaled, half, L=L, T=T, transpose=True) - dpooled
        du_ref[...] = du.astype(du_ref.dtype)

    cspec = pl.BlockSpec((T, POOL_DIM), lambda g: (0, g))
    return pl.pallas_call(
        body, out_shape=[jax.ShapeDtypeStruct((T, BRANCH), MM_DTYPE),
                         jax.ShapeDtypeStruct((POOL_GROUPS, POOL_DIM, POOL_DIM), F32),
                         jax.ShapeDtypeStruct((8, BRANCH), F32)],
        grid=(POOL_GROUPS,),
        in_specs=[cspec, cspec, pl.BlockSpec((None, POOL_DIM, POOL_DIM), lambda g: (g, 0, 0)),
                  pl.BlockSpec((1, POOL_DIM), lambda g: (0, g))],
        out_specs=[cspec, pl.BlockSpec((None, POOL_DIM, POOL_DIM), lambda g: (g, 0, 0)),
                   pl.BlockSpec((8, POOL_DIM), lambda g: (0, g))],
        scratch_shapes=[pltpu.VMEM((T + 2 * PAD, POOL_DIM), F32)],
        compiler_params=_params("parallel"), name="pool_bwd")(do, pooled, pool_w, pool_scale)


def _shifted(buf_ref, x, j, *, L, T):
    buf_ref[pl.ds(PAD, T), :] = x
    t = _row_ids(T)
    return jnp.where(_same_segment(t, t + j, L, T), buf_ref[pl.ds(PAD + j, T), :], 0.0)


def conv_fwd(proj, conv_w, *, L):
    T = proj.shape[0]
    nb = BRANCH // POOL_DIM
    cx, cbb, cc = (COL[n] // POOL_DIM for n in ("conv_x", "conv_b", "conv_c"))

    def body(x_ref, b_ref, c_ref, w_ref, o_ref, buf_ref):
        _zero_pads(buf_ref, T)
        xc = c_ref[...] * x_ref[...]
        w = w_ref[...]
        conv = (w[0:1] * _shifted(buf_ref, xc, -1, L=L, T=T) + w[1:2] * xc
                + w[2:3] * _shifted(buf_ref, xc, 1, L=L, T=T))
        o_ref[...] = b_ref[...] * conv

    return pl.pallas_call(
        body, out_shape=jax.ShapeDtypeStruct((T, BRANCH), F32), grid=(nb,),
        in_specs=[pl.BlockSpec((T, POOL_DIM), lambda g: (0, cx + g)), pl.BlockSpec((T, POOL_DIM), lambda g: (0, cbb + g)),
                  pl.BlockSpec((T, POOL_DIM), lambda g: (0, cc + g)), pl.BlockSpec((8, POOL_DIM), lambda g: (0, g))],
        out_specs=pl.BlockSpec((T, POOL_DIM), lambda g: (0, g)),
        scratch_shapes=[pltpu.VMEM((T + 2 * PAD, POOL_DIM), F32)],
        compiler_params=_params("parallel"), name="conv_fwd")(proj, proj, proj, conv_w)


def conv_bwd(do, proj, conv_w, *, L):
    T = proj.shape[0]
    nb = BRANCH // POOL_DIM
    cx, cbb, cc = (COL[n] // POOL_DIM for n in ("conv_x", "conv_b", "conv_c"))

    def body(do_ref, x_ref, b_ref, c_ref, w_ref, dx_ref, db_ref, dc_ref, dw_ref, buf_ref):
        _zero_pads(buf_ref, T)
        xv, gb, gc = x_ref[...], b_ref[...], c_ref[...]
        xc = gc * xv
        w = w_ref[...]
        xm = _shifted(buf_ref, xc, -1, L=L, T=T)
        xp = _shifted(buf_ref, xc, 1, L=L, T=T)
        conv = w[0:1] * xm + w[1:2] * xc + w[2:3] * xp
        dov = do_ref[...]
        db_ref[...] = (dov * conv).astype(db_ref.dtype)
        dconv = dov * gb
        sums = [jnp.sum(dconv * a, axis=0, keepdims=True) for a in (xm, xc, xp)]
        dw_ref[...] = jnp.concatenate(sums + [jnp.zeros((5, POOL_DIM), F32)], axis=0)
        dxc = (w[0:1] * _shifted(buf_ref, dconv, 1, L=L, T=T) + w[1:2] * dconv
               + w[2:3] * _shifted(buf_ref, dconv, -1, L=L, T=T))
        dc_ref[...] = (dxc * xv).astype(dc_ref.dtype)
        dx_ref[...] = (dxc * gc).astype(dx_ref.dtype)

    ospec = lambda off: pl.BlockSpec((T, POOL_DIM), lambda g: (0, off + g))
    return pl.pallas_call(
        body, out_shape=[jax.ShapeDtypeStruct((T, BRANCH), MM_DTYPE)] * 3 + [jax.ShapeDtypeStruct((8, BRANCH), F32)],
        grid=(nb,),
        in_specs=[ospec(0), ospec(cx), ospec(cbb), ospec(cc), pl.BlockSpec((8, POOL_DIM), lambda g: (0, g))],
        out_specs=[ospec(0), ospec(0), ospec(0), pl.BlockSpec((8, POOL_DIM), lambda g: (0, g))],
        scratch_shapes=[pltpu.VMEM((T + 2 * PAD, POOL_DIM), F32)],
        compiler_params=_params("parallel"), name="conv_bwd")(do, proj, proj, proj, conv_w)


SCAN_COLS = 1024
SCAN_ROWS = 256


def ssm_operators(a_re, a_im, log_dt, b_re, b_im, c_re, c_im):
    dt = jnp.exp(log_dt)[..., None]
    mag = jnp.exp(a_re * dt)
    abar_re, abar_im = mag * jnp.cos(a_im * dt), mag * jnp.sin(a_im * dt)
    den = a_re * a_re + a_im * a_im
    num_re, num_im = abar_re - 1.0, abar_im
    f_re = (num_re * a_re + num_im * a_im) / den
    f_im = (num_im * a_re - num_re * a_im) / den
    bbar_re = f_re[..., None] * b_re - f_im[..., None] * b_im
    bbar_im = f_re[..., None] * b_im + f_im[..., None] * b_re
    gpb = SSM_GROUPS // SSM_BLOCKS
    eye = jnp.eye(gpb, dtype=bool)[None, None, :, None, :, None]

    def blocks(t):
        _, _, a, b = t.shape
        t = t.reshape(2, SSM_BLOCKS, gpb, a, 1, b)
        return jnp.where(eye, t, 0.0).reshape(2, SSM_BLOCKS, gpb * a, gpb * b)

    in_map = lambda bbar: blocks(bbar.transpose(0, 1, 3, 2))
    out_map = lambda c: blocks(c.transpose(0, 1, 3, 2))
    abar = jnp.concatenate([abar_re.reshape(2, 1, N_STATE), abar_im.reshape(2, 1, N_STATE)], axis=-1)
    bcat = jnp.concatenate([in_map(bbar_re), in_map(bbar_im)], axis=1)
    ccat = jnp.concatenate([out_map(c_re), -out_map(c_im)], axis=1)
    return abar, bcat, ccat


SSM_BLOCKS = 4
SSM_BCH = BRANCH // SSM_BLOCKS
SSM_BST = N_STATE // SSM_BLOCKS


def _ssm_rows(T, perm):
    tm = ROW_TILE if perm else _row_tile(T)
    return tm, (perm or (lambda i: i))


def _lanes(x, n, width):
    return x[:, n * width:(n + 1) * width]


def _ssm_specs(T, perm, ucol0=None):
    tm, rows = _ssm_rows(T, perm)
    chan = pl.BlockSpec((tm, BRANCH), lambda i: (rows(i), 0 if ucol0 is None else ucol0 // BRANCH))
    state = pl.BlockSpec((tm, 2 * N_STATE), lambda i: (i, 0))
    bspec = pl.BlockSpec((2 * SSM_BLOCKS, SSM_BCH, SSM_BST), lambda i: (0, 0, 0))
    cspec = pl.BlockSpec((2 * SSM_BLOCKS, SSM_BST, SSM_BCH), lambda i: (0, 0, 0))
    return T // tm, chan, state, bspec, cspec


def ssm_in(u, bcat, *, ucol0, perm, name):
    T = u.shape[0]
    steps, chan, state, bspec, _ = _ssm_specs(T, perm, ucol0)

    def body(u_ref, b_ref, o_ref):
        uv = u_ref[...].astype(MM_DTYPE)
        for n in range(2 * SSM_BLOCKS):
            o_ref[:, n * SSM_BST:(n + 1) * SSM_BST] = _dot(_lanes(uv, n % SSM_BLOCKS, SSM_BCH), b_ref[n].astype(MM_DTYPE))

    return pl.pallas_call(body, out_shape=jax.ShapeDtypeStruct((T, 2 * N_STATE), F32), grid=(steps,),
                          in_specs=[chan, bspec], out_specs=state, compiler_params=_params("parallel"), name=name)(u, bcat)


def ssm_out(s, ccat, *, perm, name):
    T = s.shape[0]
    steps, chan, state, _, cspec = _ssm_specs(T, perm)

    def body(s_ref, c_ref, o_ref):
        sv = s_ref[...].astype(MM_DTYPE)
        o_ref[...] = jnp.concatenate(
            [_dot(_lanes(sv, j, SSM_BST), c_ref[j].astype(MM_DTYPE))
             + _dot(_lanes(sv, SSM_BLOCKS + j, SSM_BST), c_ref[SSM_BLOCKS + j].astype(MM_DTYPE))
             for j in range(SSM_BLOCKS)], axis=1)

    return pl.pallas_call(body, out_shape=jax.ShapeDtypeStruct((T, BRANCH), F32), grid=(steps,),
                          in_specs=[state, cspec], out_specs=chan, compiler_params=_params("parallel"), name=name)(s, ccat)


def ssm_out_dx(dy, ccat, *, perm, name):
    T = dy.shape[0]
    steps, chan, state, _, cspec = _ssm_specs(T, perm)

    def body(d_ref, c_ref, o_ref):
        dv = d_ref[...].astype(MM_DTYPE)
        for n in range(2 * SSM_BLOCKS):
            o_ref[:, n * SSM_BST:(n + 1) * SSM_BST] = _dot_nt(_lanes(dv, n % SSM_BLOCKS, SSM_BCH), c_ref[n].astype(MM_DTYPE))

    return pl.pallas_call(body, out_shape=jax.ShapeDtypeStruct((T, 2 * N_STATE), F32), grid=(steps,),
                          in_specs=[chan, cspec], out_specs=state, compiler_params=_params("parallel"), name=name)(dy, ccat)


def ssm_in_dx(lam, bcat, *, perm, name):
    T = lam.shape[0]
    steps, chan, state, bspec, _ = _ssm_specs(T, perm)

    def body(l_ref, b_ref, o_ref):
        lv = l_ref[...].astype(MM_DTYPE)
        o_ref[...] = jnp.concatenate(
            [_dot_nt(_lanes(lv, j, SSM_BST), b_ref[j].astype(MM_DTYPE))
             + _dot_nt(_lanes(lv, SSM_BLOCKS + j, SSM_BST), b_ref[SSM_BLOCKS + j].astype(MM_DTYPE))
             for j in range(SSM_BLOCKS)], axis=1)

    return pl.pallas_call(body, out_shape=jax.ShapeDtypeStruct((T, BRANCH), F32), grid=(steps,),
                          in_specs=[state, bspec], out_specs=chan, compiler_params=_params("parallel"), name=name)(lam, bcat)


def _ssm_dw(chan_arr, state_arr, chan_spec, state_spec, out_block, steps, chan_first, name):
    def body(c_ref, s_ref, o_ref):
        @pl.when(pl.program_id(0) == 0)
        def _():
            o_ref[...] = jnp.zeros(o_ref.shape, F32)

        cv, sv = c_ref[...].astype(MM_DTYPE), s_ref[...].astype(MM_DTYPE)
        for n in range(2 * SSM_BLOCKS):
            c, s = _lanes(cv, n % SSM_BLOCKS, SSM_BCH), _lanes(sv, n, SSM_BST)
            o_ref[n] += _dot_tn(c, s) if chan_first else _dot_tn(s, c)

    shape = (2 * SSM_BLOCKS,) + out_block
    return pl.pallas_call(body, out_shape=jax.ShapeDtypeStruct(shape, F32), grid=(steps,),
                          in_specs=[chan_spec, state_spec], out_specs=pl.BlockSpec(shape, lambda k: (0, 0, 0)),
                          compiler_params=_params("arbitrary"), name=name)(chan_arr, state_arr)


def ssm_in_dw(u, lam, *, ucol0, perm, name):
    steps, chan, state, _, _ = _ssm_specs(u.shape[0], perm, ucol0)
    return _ssm_dw(u, lam, chan, state, (SSM_BCH, SSM_BST), steps, True, name)


def ssm_out_dw(s, dy, *, perm, name):
    steps, chan, state, _, _ = _ssm_specs(s.shape[0], perm)
    return _ssm_dw(dy, s, chan, state, (SSM_BST, SSM_BCH), steps, False, name)


def _time_block(T, reverse):
    nt = T // SCAN_ROWS
    tix = (lambda i: nt - 1 - i) if reverse else (lambda i: i)
    return nt, pl.BlockSpec((SCAN_ROWS, 2 * N_STATE), lambda i: (tix(i), 0))


def ssm_scan(bu, abar, *, reverse):
    T = bu.shape[0]
    nt, tspec = _time_block(T, reverse)

    def body(b_ref, a_ref, s_ref, c_ref):
        @pl.when(pl.program_id(0) == 0)
        def _():
            c_ref[...] = jnp.zeros(c_ref.shape, F32)

        for c0 in range(0, N_STATE, SCAN_COLS):
            re, im = pl.ds(c0, SCAN_COLS), pl.ds(N_STATE + c0, SCAN_COLS)
            ar, ai = a_ref[:, re], a_ref[:, im]

            def step(n, carry, re=re, im=im, ar=ar, ai=ai):
                sr, si = carry
                t = (SCAN_ROWS - 1 - n) if reverse else n
                nr = ar * sr - ai * si + b_ref[pl.ds(t, 1), re]
                ni = ar * si + ai * sr + b_ref[pl.ds(t, 1), im]
                s_ref[pl.ds(t, 1), re] = nr
                s_ref[pl.ds(t, 1), im] = ni
                return nr, ni

            sr, si = lax.fori_loop(0, SCAN_ROWS, step, (c_ref[:, re], c_ref[:, im]))
            c_ref[:, re] = sr
            c_ref[:, im] = si

    return pl.pallas_call(
        body, out_shape=jax.ShapeDtypeStruct((T, 2 * N_STATE), F32), grid=(nt,),
        in_specs=[tspec, pl.BlockSpec((1, 2 * N_STATE), lambda i: (0, 0))], out_specs=tspec,
        scratch_shapes=[pltpu.VMEM((1, 2 * N_STATE), F32)],
        compiler_params=_params("arbitrary"), name="ssm_scan_rev" if reverse else "ssm_scan_fwd")(bu, abar)


def ssm_scan_bwd(g, s, abar, *, reverse):
    T = g.shape[0]
    nt, tspec = _time_block(T, not reverse)
    back = not reverse

    def body(g_ref, s_ref, a_ref, l_ref, da_ref, c_ref):
        @pl.when(pl.program_id(0) == 0)
        def _():
            c_ref[...] = jnp.zeros(c_ref.shape, F32)
            da_ref[...] = jnp.zeros(da_ref.shape, F32)

        for c0 in range(0, N_STATE, SCAN_COLS):
            re, im = pl.ds(c0, SCAN_COLS), pl.ds(N_STATE + c0, SCAN_COLS)
            ar, ai = a_ref[:, re], a_ref[:, im]

            def step(n, carry, re=re, im=im, ar=ar, ai=ai):
                lr, li, dr, di = carry
                t = (SCAN_ROWS - 1 - n) if back else n
                sr, si = s_ref[pl.ds(t, 1), re], s_ref[pl.ds(t, 1), im]
                dr = dr + sr * lr + si * li
                di = di + sr * li - si * lr
                nr = g_ref[pl.ds(t, 1), re] + ar * lr + ai * li
                ni = g_ref[pl.ds(t, 1), im] + ar * li - ai * lr
                l_ref[pl.ds(t, 1), re] = nr
                l_ref[pl.ds(t, 1), im] = ni
                return nr, ni, dr, di

            zero = jnp.zeros((1, SCAN_COLS), F32)
            lr, li, dr, di = lax.fori_loop(0, SCAN_ROWS, step, (c_ref[:, re], c_ref[:, im], zero, zero))
            c_ref[:, re] = lr
            c_ref[:, im] = li
            da_ref[:, re] += jnp.broadcast_to(dr, (8, SCAN_COLS))
            da_ref[:, im] += jnp.broadcast_to(di, (8, SCAN_COLS))

    return pl.pallas_call(
        body, out_shape=[jax.ShapeDtypeStruct((T, 2 * N_STATE), F32), jax.ShapeDtypeStruct((8, 2 * N_STATE), F32)],
        grid=(nt,), in_specs=[tspec, tspec, pl.BlockSpec((1, 2 * N_STATE), lambda i: (0, 0))],
        out_specs=[tspec, pl.BlockSpec((8, 2 * N_STATE), lambda i: (0, 0))],
        scratch_shapes=[pltpu.VMEM((1, 2 * N_STATE), F32)],
        compiler_params=_params("arbitrary"),
        name="ssm_scan_bwd_rev" if reverse else "ssm_scan_bwd_fwd")(g, s, abar)


def _gelu(x):
    return 0.5 * x * (1.0 + jnp.tanh(0.7978845608028654 * (x + 0.044715 * x * x * x)))


def _gelu_grad(x):
    t = jnp.tanh(0.7978845608028654 * (x + 0.044715 * x * x * x))
    return 0.5 * (1.0 + t) + 0.5 * x * (1.0 - t * t) * 0.7978845608028654 * (1.0 + 3 * 0.044715 * x * x)


def _silu(z):
    return z * _sigmoid(z)


def _silu_grad(z):
    s = _sigmoid(z)
    return s * (1.0 + z * (1.0 - s))


def ssm_fwd(proj, ops, dsk, glu_w, *, L):
    T = proj.shape[0]
    abar, bcat, ccat = ops
    nb, nlat = T // ROW_TILE, L // ROW_TILE
    to_f = lambda i: (i + nlat) % nb
    states, ys = [], []
    for d in (0, 1):
        perm = to_f if d == 0 else None
        bu = ssm_in(proj, bcat[d], ucol0=COL["ssm_u"], perm=perm, name=f"ssm_in{d}")
        s = ssm_scan(bu, abar[d], reverse=(d == 1))
        states.append(s)
        ys.append(ssm_out(s, ccat[d], perm=perm, name=f"ssm_out{d}"))

    def pre(u, yf, yr, dsk):
        y = dsk * u + yf + yr
        return y, _gelu(y)

    ypre, gy = _ew(pre, [(proj, "row", COL["ssm_u"] // BRANCH, BRANCH), (ys[0], "row", 0, BRANCH),
                         (ys[1], "row", 0, BRANCH), (dsk, "bcast", 0, BRANCH)],
                   [(BRANCH, F32), (BRANCH, MM_DTYPE)], [], T=T, L=L, name="ssm_pre")
    gg = mm_nn(gy, glu_w, out_dtype=F32, name="ssm_glu")

    def post(ga, gb):
        return (ga * _sigmoid(gb),)

    o, = _ew(post, [(gg, "row", 0, BRANCH), (gg, "row", 1, BRANCH)], [(BRANCH, F32)], [], T=T, L=L, name="ssm_post")
    return o, dict(states=states, ypre=ypre, gy=gy, gg=gg)


def ssm_bwd(do, proj, ops, dsk, glu_w, saved, *, L):
    T = proj.shape[0]
    abar, bcat, ccat = ops
    nb, nlat = T // ROW_TILE, L // ROW_TILE
    to_f = lambda i: (i + nlat) % nb
    gg, gy, ypre = saved["gg"], saved["gy"], saved["ypre"]

    def post_bwd(do, ga, gb):
        sg = _sigmoid(gb)
        return (jnp.concatenate([do * sg, do * ga * sg * (1.0 - sg)], axis=1),)

    dgg, = _ew(post_bwd, [(do, "row", 0, BRANCH), (gg, "row", 0, BRANCH), (gg, "row", 1, BRANCH)],
               [(2 * BRANCH, MM_DTYPE)], [], T=T, L=L, name="ssm_post_bwd")
    dgy = mm_nt(dgg, glu_w, out_dtype=F32, name="ssm_glu_dx")
    dglu = mm_tn(gy, dgg, out_dtype=F32, name="ssm_glu_dw")

    def pre_bwd(dgy, y, u, dsk):
        dy = dgy * _gelu_grad(y)
        return dy, dy * dsk, dy * u

    dy, du_skip, dd = _ew(pre_bwd, [(dgy, "row", 0, BRANCH), (ypre, "row", 0, BRANCH),
                                    (proj, "row", COL["ssm_u"] // BRANCH, BRANCH), (dsk, "bcast", 0, BRANCH)],
                          [(BRANCH, MM_DTYPE), (BRANCH, F32)], [BRANCH], T=T, L=L, name="ssm_pre_bwd")
    du = du_skip
    dabar, dbcat, dccat = [], [], []
    for d in (0, 1):
        perm = to_f if d == 0 else None
        s = saved["states"][d]
        g = ssm_out_dx(dy, ccat[d], perm=perm, name=f"ssm_out{d}_dx")
        lam, da = ssm_scan_bwd(g, s, abar[d], reverse=(d == 1))
        dabar.append(da[0:1])
        dccat.append(ssm_out_dw(s, dy, perm=perm, name=f"ssm_out{d}_dw"))
        du = du + ssm_in_dx(lam, bcat[d], perm=perm, name=f"ssm_in{d}_dx")
        dbcat.append(ssm_in_dw(proj, lam, ucol0=COL["ssm_u"], perm=perm, name=f"ssm_in{d}_dw"))
    d_ops = (jnp.stack(dabar), jnp.stack(dbcat), jnp.stack(dccat))
    return du, d_ops, dd[0, 0] + dd[1, 0], dglu


Z_COLS = tuple(COL[n] // BRANCH for n in ("na_z", "pool_z", "conv_z", "ssm_z"))


def gate_act(o, proj, *, L):
    T = o.shape[0]

    def fn(o, z0, z1, z2, z3):
        return (o * _silu(jnp.concatenate([z0, z1, z2, z3], axis=1)),)

    a, = _ew(fn, [(o, "row", 0, D_MODEL)] + [(proj, "row", c, BRANCH) for c in Z_COLS],
             [(D_MODEL, MM_DTYPE)], [], T=T, L=L, name="gate_act")
    return a


def gate_act_bwd(da, o, proj, *, L):
    T = o.shape[0]

    def fn(da, o, z0, z1, z2, z3):
        z = jnp.concatenate([z0, z1, z2, z3], axis=1)
        return da * _silu(z), da * o * _silu_grad(z)

    return _ew(fn, [(da, "row", 0, D_MODEL), (o, "row", 0, D_MODEL)] + [(proj, "row", c, BRANCH) for c in Z_COLS],
               [(D_MODEL, F32), (D_MODEL, MM_DTYPE)], [], T=T, L=L, name="gate_act_bwd")


MERGE_TN = 512


def merge_fwd(a, w_br, proj, b_gate):
    T = a.shape[0]
    tm, tn = _row_tile(T), MERGE_TN
    nn = D_MODEL // tn
    lb0 = COL["merge"] // tn

    def body(a_ref, w_ref, l_ref, b_ref, m_ref, br_ref, acc_ref):
        i = pl.program_id(2)
        br = _dot(a_ref[...].astype(MM_DTYPE), w_ref[...].astype(MM_DTYPE))
        br_ref[...] = br
        term = _sigmoid(l_ref[...] + b_ref[...]) * br

        @pl.when(i == 0)
        def _():
            acc_ref[...] = term

        @pl.when(i > 0)
        def _():
            acc_ref[...] += term

        @pl.when(i == 3)
        def _():
            m_ref[...] = acc_ref[...].astype(m_ref.dtype)

    return pl.pallas_call(
        body, out_shape=[jax.ShapeDtypeStruct((T, D_MODEL), MM_DTYPE), jax.ShapeDtypeStruct((T, 4 * D_MODEL), F32)],
        grid=(T // tm, nn, 4),
        in_specs=[pl.BlockSpec((tm, BRANCH), lambda m, n, i: (m, i)),
                  pl.BlockSpec((BRANCH, tn), lambda m, n, i: (i, n)),
                  pl.BlockSpec((tm, tn), lambda m, n, i: (m, lb0 + i * nn + n)),
                  pl.BlockSpec((1, tn), lambda m, n, i: (0, i * nn + n))],
        out_specs=[pl.BlockSpec((tm, tn), lambda m, n, i: (m, n)), pl.BlockSpec((tm, tn), lambda m, n, i: (m, i * nn + n))],
        scratch_shapes=[pltpu.VMEM((tm, tn), F32)],
        compiler_params=_params("parallel", "parallel", "arbitrary"), name="merge_fwd")(a, w_br, proj, b_gate)


def merge_bwd(dmerged, br, proj, b_gate):
    T = dmerged.shape[0]
    tb = ROW_TILE
    lb0 = COL["merge"] // D_MODEL

    def body(dm_ref, br_ref, l_ref, b_ref, dbr_ref, dl_ref, db_ref):
        dm = dm_ref[...]
        gates = _sigmoid(l_ref[...] + b_ref[...])
        dbr_ref[...] = (dm * gates).astype(dbr_ref.dtype)
        dl = dm * br_ref[...] * gates * (1.0 - gates)
        dl_ref[...] = dl.astype(dl_ref.dtype)
        s = jnp.broadcast_to(jnp.sum(dl, axis=0, keepdims=True), db_ref.shape)

        @pl.when(pl.program_id(1) == 0)
        def _():
            db_ref[...] = s

        @pl.when(pl.program_id(1) > 0)
        def _():
            db_ref[...] += s

    wide = pl.BlockSpec((tb, D_MODEL), lambda b, i: (i, b))
    return pl.pallas_call(
        body, out_shape=[jax.ShapeDtypeStruct((T, 4 * D_MODEL), MM_DTYPE)] * 2 + [jax.ShapeDtypeStruct((8, 4 * D_MODEL), F32)],
        grid=(4, T // tb),
        in_specs=[pl.BlockSpec((tb, D_MODEL), lambda b, i: (i, 0)), wide,
                  pl.BlockSpec((tb, D_MODEL), lambda b, i: (i, lb0 + b)), pl.BlockSpec((1, D_MODEL), lambda b, i: (0, b))],
        out_specs=[wide, wide, pl.BlockSpec((8, D_MODEL), lambda b, i: (0, b))],
        compiler_params=_params("parallel", "arbitrary"), name="merge_bwd")(dmerged, br, proj, b_gate)


def branch_dx(dbr, w_br):
    T = dbr.shape[0]
    tm, tk = _row_tile(T), 1024
    nk = D_MODEL // tk
    return _matmul(dbr, w_br, pl.BlockSpec((tm, tk), lambda m, i, k: (m, i * nk + k)),
                   pl.BlockSpec((BRANCH, tk), lambda m, i, k: (i, k)),
                   pl.BlockSpec((tm, BRANCH), lambda m, i, k: (m, i)),
                   jax.ShapeDtypeStruct((T, D_MODEL), F32), (T // tm, 4, nk), tb=True, name="branch_dx")


def branch_dw(a, dbr):
    T = a.shape[0]
    tk, tn = T, 1024
    nn = D_MODEL // tn
    return _matmul(a, dbr, pl.BlockSpec((tk, BRANCH), lambda i, n, k: (k, i)),
                   pl.BlockSpec((tk, tn), lambda i, n, k: (k, i * nn + n)),
                   pl.BlockSpec((BRANCH, tn), lambda i, n, k: (i, n)),
                   jax.ShapeDtypeStruct((D_MODEL, D_MODEL), F32), (4, nn, T // tk), ta=True, name="branch_dw")


def proj_fwd(h, w_in):
    T = h.shape[0]
    tm, tn = _row_tile(T), 1792
    per = W_IN_SHARD // tn
    return _matmul(h, w_in, pl.BlockSpec((tm, D_MODEL), lambda i, j, k: (i, 0)),
                   pl.BlockSpec((None, D_MODEL, tn), lambda i, j, k: (j // per, 0, j % per)),
                   pl.BlockSpec((tm, tn), lambda i, j, k: (i, j)),
                   jax.ShapeDtypeStruct((T, IN_TOTAL), F32), (T // tm, IN_TOTAL // tn, 1), name="proj_fwd")


def proj_dx(dproj, w_in):
    T = dproj.shape[0]
    tm, tk = _row_tile(T), 1792
    per = W_IN_SHARD // tk
    return _matmul(dproj, w_in, pl.BlockSpec((tm, tk), lambda i, j, k: (i, k)),
                   pl.BlockSpec((None, D_MODEL, tk), lambda i, j, k: (k // per, 0, k % per)),
                   pl.BlockSpec((tm, D_MODEL), lambda i, j, k: (i, 0)),
                   jax.ShapeDtypeStruct((T, D_MODEL), F32), (T // tm, 1, IN_TOTAL // tk), tb=True, name="proj_dx")


def proj_dw(h, dproj):
    T = h.shape[0]
    tk, tm, tn = T, 512, 512
    per = W_IN_SHARD // tn
    return _matmul(h, dproj, pl.BlockSpec((tk, tm), lambda i, j, k: (k, i)),
                   pl.BlockSpec((tk, tn), lambda i, j, k: (k, j)),
                   pl.BlockSpec((None, tm, tn), lambda i, j, k: (j // per, i, j % per)),
                   jax.ShapeDtypeStruct((N_SHARDS, D_MODEL, W_IN_SHARD), F32),
                   (D_MODEL // tm, IN_TOTAL // tn, T // tk), ta=True, name="proj_dw")


def layer_fwd(X, P, *, L):
    h = prenorm_fwd(X, P["g_pre"], P["scale"], P["shift"], L=L)
    proj = proj_fwd(h, P["w_in"])
    o_att, lse = attn_fwd(proj, P["ptiles"], L=L)
    o_att, lse_c = cattn_fwd(proj, o_att, L=L)
    o_pool, pooled = pool_fwd(proj, P["pool_w"], P["pool_scale"], L=L)
    o_conv = conv_fwd(proj, P["conv_w"], L=L)
    o_ssm, ssm_saved = ssm_fwd(proj, P["ssm_ops"], P["ssm_d"], P["glu_w"], L=L)
    o = jnp.concatenate([o_att, o_pool, o_conv, o_ssm], axis=1)
    a = gate_act(o, proj, L=L)
    merged, br = merge_fwd(a, P["w_br"], proj, P["b_gate"])
    y = mm_nn(merged, P["w_o"], out_dtype=F32, name="out_proj")
    Xn = postnorm_fwd(X, y, P["g_post"], P["gate"], L=L)
    saved = dict(X=X, h=h, proj=proj, lse=lse, lse_c=lse_c, pooled=pooled, ssm=ssm_saved, o=o, a=a, merged=merged,
                 br=br, y=y)
    return Xn, saved


def layer_bwd(dXn, P, S, bias_vjp, ops_vjp, *, L):
    proj = S["proj"]
    dy, cs_post = postnorm_bwd(dXn, S["y"], P["g_post"], P["gate"], L=L)
    dmerged = mm_nt(dy, P["w_o"], out_dtype=F32, name="out_proj_dx")
    d_w_o = mm_tn(S["merged"], dy, out_dtype=F32, name="out_proj_dw", tm=512, tn=1024)
    dbr, dlogit, d_bgate = merge_bwd(dmerged, S["br"], proj, P["b_gate"])
    da = branch_dx(dbr, P["w_br"])
    d_w_br = branch_dw(S["a"], dbr)
    do, dz = gate_act_bwd(da, S["o"], proj, L=L)
    dq, dk, dv, dptiles = attn_bwd(proj, P["ptiles"], S["o"], do, S["lse"], L=L)
    dq, dk, dv = cattn_bwd(proj, S["o"], do, S["lse_c"], dq, dk, dv, L=L)
    d_rpb, = bias_vjp(dptiles)
    dpool_u, d_pool_w, d_pool_scale = pool_bwd(do[:, BRANCH:2 * BRANCH], S["pooled"], P["pool_w"], P["pool_scale"], L=L)
    dcx, dcb, dcc, d_conv_w = conv_bwd(do[:, 2 * BRANCH:3 * BRANCH], proj, P["conv_w"], L=L)
    dssm_u, d_ops, d_ssm_d, d_glu = ssm_bwd(do[:, 3 * BRANCH:], proj, P["ssm_ops"], P["ssm_d"], P["glu_w"], S["ssm"], L=L)
    d_ssm = ops_vjp(d_ops)
    z = lambda i: dz[:, i * BRANCH:(i + 1) * BRANCH]
    cast = lambda t: t.astype(MM_DTYPE)
    dproj = jnp.concatenate([cast(dq), cast(dk), cast(dv), z(0), dpool_u, z(1),
                             dcx, dcb, dcc, z(2), cast(dssm_u), z(3), dlogit], axis=1)
    dh = proj_dx(dproj, P["w_in"])
    d_w_in = proj_dw(S["h"], dproj)
    dX, cs_h, cs_hx = prenorm_bwd(dh, S["X"], P["g_pre"], P["scale"], dXn, L=L)
    g_pre, g_post = P["g_pre"], P["g_post"]
    d_shift = cs_h
    d_scale = cs_hx * g_pre
    d_gate = cs_post * g_post
    d_g_pre = jnp.sum(cs_hx * (1.0 + P["scale"]), axis=0)[0]
    d_g_post = jnp.sum(cs_post * P["gate"], axis=0)[0]
    grads = dict(w_in=d_w_in, w_br=d_w_br, w_o=d_w_o, glu_w=d_glu, conv_w=d_conv_w[0:3], pool_w=d_pool_w,
                 pool_scale=d_pool_scale[0], b_gate=d_bgate[0], na_rpb=d_rpb, ssm=d_ssm, ssm_d=d_ssm_d,
                 g_pre=d_g_pre, g_post=d_g_post,
                 mod=jnp.concatenate([d_shift, d_scale, d_gate], axis=-1)[:, 0])
    return dX, grads


def local_step(x, ctx, target, mods, layers):
    L = x.shape[0]
    X = jnp.concatenate([x, ctx], axis=0)
    saved, Ps, vjps = [], [], []
    for i, lay in enumerate(layers):
        P = dict(lay)
        m = mods[i][:, None, :]
        P["shift"], P["scale"], P["gate"] = m[..., :D_MODEL], m[..., D_MODEL:2 * D_MODEL], m[..., 2 * D_MODEL:]
        P["ptiles"], bias_vjp = jax.vjp(bias_pair_tiles, lay["na_rpb"])
        P["ssm_ops"], ops_vjp = jax.vjp(ssm_operators, *lay["ssm_params"])
        X, S = layer_fwd(X, P, L=L)
        saved.append(S)
        Ps.append(P)
        vjps.append((bias_vjp, ops_vjp))
    loss, dX = loss_and_grad(X, target, L=L)
    grads = [None] * len(layers)
    for i in reversed(range(len(layers))):
        dX, grads[i] = layer_bwd(dX, Ps[i], saved[i], *vjps[i], L=L)
    return loss, dX[:L], grads


MESH_ID = pl.DeviceIdType.MESH
HBM_SPEC = pl.BlockSpec(memory_space=pltpu.HBM)


def _place():
    return lax.axis_index("x"), lax.axis_index("y"), lax.axis_index("c")


def _other_chips(x, y):
    return [(1 - x, y), (x, 1 - y), (1 - x, 1 - y)]


def _remote(src, dst, send_sem, recv_sem, to):
    return pltpu.make_async_remote_copy(src_ref=src, dst_ref=dst, send_sem=send_sem, recv_sem=recv_sem,
                                        device_id=to, device_id_type=MESH_ID)


DMA_CHUNK_BYTES = 1 << 20


def _row_pieces(rows, row_bytes):
    n = max(1, min(64, rows * row_bytes // DMA_CHUNK_BYTES))
    while n > 1 and (rows % n or (rows // n) % 16):
        n -= 1
    return [(r * (rows // n), rows // n) for r in range(n)]


def _row_bytes(ref):
    return math.prod(ref.shape[1:]) * jnp.dtype(ref.dtype).itemsize


def _start_in_pieces(make, src, dst):
    for r0, nr in _row_pieces(src.shape[0], _row_bytes(src)):
        make(src.at[pl.ds(r0, nr)], dst.at[pl.ds(r0, nr)]).start()


def all_gather8(blocks, name, *, shard_major=False):
    K = len(blocks)

    def body(*refs):
        ins, outs = refs[:K], refs[K:2 * K]
        send_sems, recv_sems, local_sems = refs[2 * K:]
        x, y, c = _place()
        me, sibling = (x, y, c), (x, y, 1 - c)
        chips = _other_chips(x, y)

        def slot(k, block):
            px, py, pc = block
            if shard_major:
                h = ins[k].shape[0]
                return outs[k].at[2 * px + py, pl.ds(pl.multiple_of(pc * h, 16), h)]
            return outs[k].at[4 * px + 2 * py + pc]

        def copy(k, j, to):
            return lambda s, d: _remote(s, d, send_sems.at[k, j], recv_sems.at[k, j], to)

        for k in range(K):
            _start_in_pieces(lambda s, d, k=k: pltpu.make_async_copy(s, d, local_sems.at[k]), ins[k], slot(k, me))
            _start_in_pieces(copy(k, 0, sibling), ins[k], slot(k, me))
            for j, chip in enumerate(chips):
                _start_in_pieces(copy(k, 1 + j, (*chip, c)), ins[k], slot(k, me))
        for j, chip in enumerate(chips):
            for k in range(K):
                got = slot(k, (*chip, c))
                copy(k, 1 + j, me)(got, got).wait_recv()
                _start_in_pieces(copy(k, 4 + j, sibling), got, got)
        for k in range(K):
            sib = slot(k, sibling)
            copy(k, 0, me)(sib, sib).wait_recv()
            for j, chip in enumerate(chips):
                got = slot(k, (*chip, 1 - c))
                copy(k, 4 + j, me)(got, got).wait_recv()
        for k in range(K):
            own = slot(k, me)
            for j in range(4):
                copy(k, j, me)(ins[k], own).wait_send()
            for j, chip in enumerate(chips):
                got = slot(k, (*chip, c))
                copy(k, 4 + j, me)(got, got).wait_send()
            pltpu.make_async_copy(ins[k], own, local_sems.at[k]).wait()

    def out_shape(b):
        if shard_major:
            return jax.ShapeDtypeStruct((N_SHARDS, 2 * b.shape[0]) + b.shape[1:], b.dtype)
        return jax.ShapeDtypeStruct((8,) + b.shape, b.dtype)

    return pl.pallas_call(
        body, out_shape=[out_shape(b) for b in blocks], in_specs=[HBM_SPEC] * K, out_specs=[HBM_SPEC] * K,
        scratch_shapes=[pltpu.SemaphoreType.DMA((K, 7)), pltpu.SemaphoreType.DMA((K, 7)), pltpu.SemaphoreType.DMA((K,))],
        name=name)(*blocks)


def chip_exchange(parts, name):
    K = len(parts)

    def body(*refs):
        ins, outs = refs[:K], refs[K:2 * K]
        send_sems, recv_sems, local_sems = refs[2 * K:]
        x, y, c = _place()
        p = 2 * x + y
        chips = _other_chips(x, y)
        waits = []
        for k in range(K):
            _start_in_pieces(lambda s, d, k=k: pltpu.make_async_copy(s, d, local_sems.at[k]), ins[k].at[p], outs[k].at[p])
            for j, (cx, cy) in enumerate(chips):
                q = 2 * cx + cy
                make = lambda s, d, k=k, j=j, to=(cx, cy, c): _remote(s, d, send_sems.at[k, j], recv_sems.at[k, j], to)
                _start_in_pieces(make, ins[k].at[q], outs[k].at[p])
                waits.append(make(ins[k].at[q], outs[k].at[q]))
        for cp in waits:
            cp.wait_recv()
        for cp in waits:
            cp.wait_send()
        for k in range(K):
            pltpu.make_async_copy(ins[k].at[p], outs[k].at[p], local_sems.at[k]).wait()

    return pl.pallas_call(
        body, out_shape=[jax.ShapeDtypeStruct(a.shape, a.dtype) for a in parts],
        in_specs=[HBM_SPEC] * K, out_specs=[HBM_SPEC] * K,
        scratch_shapes=[pltpu.SemaphoreType.DMA((K, 3)), pltpu.SemaphoreType.DMA((K, 3)), pltpu.SemaphoreType.DMA((K,))],
        name=name)(*parts)


PAIR_TILE_BYTES = 2 << 20


def _pair_rows(h, n):
    return _pick(h, tuple(t for t in (512, 256, 128, 64, 32, 16) if t * n * 4 <= PAIR_TILE_BYTES))


def _core_index():
    return jnp.reshape(lax.axis_index("c"), (1,)).astype(jnp.int32)


def pair_reduce(g, *, out_dtype, name):
    S, R, n = g.shape
    h = R // 2
    tr = _pair_rows(h, n)
    nt = h // tr

    def body(c_ref, keep_ref, give_ref, o_ref, recv_ref, send_sems, recv_sems):
        x, y, c = _place()
        slot = (pl.program_id(0) * nt + pl.program_id(1)) % 2
        cp = _remote(give_ref, recv_ref.at[slot], send_sems.at[slot], recv_sems.at[slot], (x, y, 1 - c))
        cp.start()
        cp.wait_recv()
        o_ref[...] = (keep_ref[...] + recv_ref[slot]).astype(o_ref.dtype)
        cp.wait_send()

    g2 = g.reshape(S * R, n)
    grid_spec = pltpu.PrefetchScalarGridSpec(
        num_scalar_prefetch=1, grid=(S, nt),
        in_specs=[pl.BlockSpec((tr, n), lambda q, i, c: ((2 * q + c[0]) * nt + i, 0)),
                  pl.BlockSpec((tr, n), lambda q, i, c: ((2 * q + 1 - c[0]) * nt + i, 0))],
        out_specs=pl.BlockSpec((tr, n), lambda q, i, c: (q * nt + i, 0)),
        scratch_shapes=[pltpu.VMEM((2, tr, n), F32), pltpu.SemaphoreType.DMA((2,)), pltpu.SemaphoreType.DMA((2,))])
    out = pl.pallas_call(body, out_shape=jax.ShapeDtypeStruct((S * h, n), out_dtype), grid_spec=grid_spec,
                         compiler_params=_params("arbitrary", "arbitrary"), name=name)(_core_index(), g2, g2)
    return out.reshape(S, h, n)


def pair_finish(parts, *, name):
    S, h, n = parts.shape
    tr = _pair_rows(h, n)
    nt = h // tr

    def body(c_ref, p_ref, o_ref, recv_ref, send_sem, recv_sem):
        x, y, c = _place()
        phase, i = pl.program_id(0), pl.program_id(1)
        rows = pl.ds(pl.multiple_of(i * tr, 16), tr)

        @pl.when(phase == 0)
        def _():
            acc = p_ref[0].astype(F32)
            for s in range(1, S):
                acc = acc + p_ref[s].astype(F32)
            o_ref[...] = acc
            cp = _remote(o_ref, recv_ref.at[rows], send_sem, recv_sem, (x, y, 1 - c))
            cp.start()
            cp.wait_send()

        @pl.when((phase == 1) & (i == 0))
        def _():
            _remote(recv_ref, recv_ref, send_sem, recv_sem, (x, y, 1 - c)).wait_recv()

        @pl.when(phase == 1)
        def _():
            o_ref[...] = recv_ref[rows]

    grid_spec = pltpu.PrefetchScalarGridSpec(
        num_scalar_prefetch=1, grid=(2, nt),
        in_specs=[pl.BlockSpec((S, tr, n), lambda ph, i, c: (0, jnp.where(ph == 0, i, nt - 1), 0))],
        out_specs=pl.BlockSpec((tr, n), lambda ph, i, c: (jnp.where(ph == 0, c[0], 1 - c[0]) * nt + i, 0)),
        scratch_shapes=[pltpu.VMEM((h, n), F32), pltpu.SemaphoreType.DMA(()), pltpu.SemaphoreType.DMA(())])
    return pl.pallas_call(body, out_shape=jax.ShapeDtypeStruct((2 * h, n), F32), grid_spec=grid_spec,
                          compiler_params=_params("arbitrary", "arbitrary"), name=name)(_core_index(), parts)


def _slab_rows(n):
    return max(8, min(256, (1 << 18) // n // 8 * 8))


def sum_slabs(a, *, name):
    S, h, n = a.shape
    tr = _pick(h, tuple(t for t in (256, 128, 64, 32, 16, 8) if t <= _slab_rows(n)))

    def body(a_ref, o_ref):
        acc = a_ref[0].astype(F32)
        for s in range(1, S):
            acc = acc + a_ref[s].astype(F32)
        o_ref[...] = acc

    return pl.pallas_call(body, out_shape=jax.ShapeDtypeStruct((h, n), F32), grid=(h // tr,),
                          in_specs=[pl.BlockSpec((S, tr, n), lambda r: (0, r, 0))],
                          out_specs=pl.BlockSpec((tr, n), lambda r: (r, 0)), compiler_params=_params("parallel"),
                          name=name)(a)


def _adam_math(w, g, m, v):
    m = ADAM_B1 * m + (1.0 - ADAM_B1) * g
    v = ADAM_B2 * v + (1.0 - ADAM_B2) * (g * g)
    m_hat = m / (1.0 - ADAM_B1 ** ADAM_STEP)
    v_hat = v / (1.0 - ADAM_B2 ** ADAM_STEP)
    delta = -ADAM_LR * (m_hat / (jnp.sqrt(v_hat) + ADAM_EPS) + ADAM_WD * w)
    return delta, m, v


def adamw(w, g, m, v, *, name):
    R, n = w.shape
    tr = _pick(R, tuple(t for t in (256, 128, 64, 32, 16, 8) if t <= _slab_rows(n)))
    spec = pl.BlockSpec((tr, n), lambda r: (r, 0))

    def body(w_ref, g_ref, m_ref, v_ref, d_ref, nm_ref, nv_ref):
        d, nm, nv = _adam_math(w_ref[...], g_ref[...], m_ref[...], v_ref[...])
        d_ref[...] = d
        nm_ref[...] = nm
        nv_ref[...] = nv

    return pl.pallas_call(body, out_shape=[jax.ShapeDtypeStruct(w.shape, F32)] * 3, grid=(R // tr,),
                          in_specs=[spec] * 4, out_specs=[spec] * 3, compiler_params=_params("parallel"),
                          name=name)(w, g, m, v)


MOD_ROWS = 16


def mod_fwd(cact_in, w_mod, b_mod):
    nl, _, cols = w_mod.shape
    tn = 512

    def body(c_ref, w_ref, b_ref, o_ref):
        o_ref[...] = _dot(_silu(c_ref[...]).astype(MM_DTYPE), w_ref[...].astype(MM_DTYPE)) + b_ref[...]

    return pl.pallas_call(
        body, out_shape=jax.ShapeDtypeStruct((nl, MOD_ROWS, cols), F32), grid=(nl, cols // tn),
        in_specs=[pl.BlockSpec((MOD_ROWS, D_MODEL), lambda i, j: (0, 0)),
                  pl.BlockSpec((None, D_MODEL, tn), lambda i, j: (i, 0, j)),
                  pl.BlockSpec((None, 1, tn), lambda i, j: (i, 0, j))],
        out_specs=pl.BlockSpec((None, MOD_ROWS, tn), lambda i, j: (i, 0, j)),
        compiler_params=_params("parallel", "parallel"), name="mod_fwd")(cact_in, w_mod, b_mod)


def mod_update(c_rows, dmod, w, m, v):
    nl, _, cols = w.shape
    tr, tn = 256, 512
    wspec = pl.BlockSpec((None, tr, tn), lambda i, r, j: (i, r, j))

    def body(c_ref, d_ref, w_ref, m_ref, v_ref, g_ref, dl_ref, nm_ref, nv_ref):
        g = _dot_tn(_silu(c_ref[...]).astype(MM_DTYPE), d_ref[...].astype(MM_DTYPE))
        g_ref[...] = g
        dl, nm, nv = _adam_math(w_ref[...], g, m_ref[...], v_ref[...])
        dl_ref[...] = dl
        nm_ref[...] = nm
        nv_ref[...] = nv

    return pl.pallas_call(
        body, out_shape=[jax.ShapeDtypeStruct(w.shape, F32)] * 4, grid=(nl, D_MODEL // tr, cols // tn),
        in_specs=[pl.BlockSpec((MOD_ROWS, tr), lambda i, r, j: (0, r)),
                  pl.BlockSpec((None, MOD_ROWS, tn), lambda i, r, j: (i, 0, j)), wspec, wspec, wspec],
        out_specs=[wspec] * 4, compiler_params=_params("parallel", "parallel", "parallel"),
        name="mod_update")(c_rows, dmod, w, m, v)


def cctx_partial(dmod_ctx, w_mod, c_ctx):
    nl, _, cols = w_mod.shape
    tk = 512
    per = cols // tk
    part = _matmul(dmod_ctx, w_mod, pl.BlockSpec((8, tk), lambda i, j, k: (0, k)),
                   pl.BlockSpec((None, D_MODEL, tk), lambda i, j, k: (k // per, 0, k % per)),
                   pl.BlockSpec((8, D_MODEL), lambda i, j, k: (0, 0)),
                   jax.ShapeDtypeStruct((8, D_MODEL), F32), (1, 1, nl * per), tb=True, name="cctx_partial")

    def body(p_ref, c_ref, o_ref):
        o_ref[...] = 0.5 * jnp.sum(p_ref[...], axis=0, keepdims=True) * _silu_grad(c_ref[...])

    return pl.pallas_call(body, out_shape=jax.ShapeDtypeStruct((1, D_MODEL), F32), name="cctx_scale")(part, c_ctx)


WEIGHT_NAMES = ("c_ctx", "w_mod", "b_mod", "g_pre", "g_post", "w_in", "b_gate", "na_rpb", "pool_w", "pool_scale",
                "conv_w", "ssm_a_re", "ssm_a_im", "ssm_log_dt", "ssm_b_re", "ssm_b_im", "ssm_c_re", "ssm_c_im",
                "ssm_d", "glu_w", "w_br", "w_o")
SSM_NAMES = ("ssm_a_re", "ssm_a_im", "ssm_log_dt", "ssm_b_re", "ssm_b_im", "ssm_c_re", "ssm_c_im")
SMALL_NAMES = ("c_ctx", "b_mod", "g_pre", "g_post", "b_gate", "na_rpb", "pool_w", "pool_scale") + SSM_NAMES + ("ssm_d",)
BIG_NAMES = ("w_in", "glu_w", "w_br", "w_o")
FLAT_COLS = 1024


def _flat(parts):
    v = jnp.concatenate([p.reshape(-1) for p in parts])
    pad = -v.shape[0] % (64 * FLAT_COLS)
    return jnp.pad(v, (0, pad)).reshape(-1, FLAT_COLS)


def _unflat(flat, shapes):
    v = flat.reshape(-1)
    out, off = [], 0
    for s in shapes:
        n = math.prod(s)
        out.append(v[off:off + n].reshape(s))
        off += n
    return out


def kernel(x, c, ctx, c_ctx, w_mod, b_mod, g_pre, g_post, w_in, b_gate, na_rpb, pool_w, pool_scale, conv_w, ssm_a_re, ssm_a_im, ssm_log_dt, ssm_b_re, ssm_b_im, ssm_c_re, ssm_c_im, ssm_d, glu_w, w_br, w_o, loss_target, m_c_ctx, m_w_mod, m_b_mod, m_g_pre, m_g_post, m_w_in, m_b_gate, m_na_rpb, m_pool_w, m_pool_scale, m_conv_w, m_ssm_a_re, m_ssm_a_im, m_ssm_log_dt, m_ssm_b_re, m_ssm_b_im, m_ssm_c_re, m_ssm_c_im, m_ssm_d, m_glu_w, m_w_br, m_w_o, v_c_ctx, v_w_mod, v_b_mod, v_g_pre, v_g_post, v_w_in, v_b_gate, v_na_rpb, v_pool_w, v_pool_scale, v_conv_w, v_ssm_a_re, v_ssm_a_im, v_ssm_log_dt, v_ssm_b_re, v_ssm_b_im, v_ssm_c_re, v_ssm_c_im, v_ssm_d, v_glu_w, v_w_br, v_w_o):
    W = dict(c_ctx=c_ctx, w_mod=w_mod, b_mod=b_mod, g_pre=g_pre, g_post=g_post, w_in=w_in, b_gate=b_gate,
             na_rpb=na_rpb, pool_w=pool_w, pool_scale=pool_scale, conv_w=conv_w, ssm_a_re=ssm_a_re, ssm_a_im=ssm_a_im,
             ssm_log_dt=ssm_log_dt, ssm_b_re=ssm_b_re, ssm_b_im=ssm_b_im, ssm_c_re=ssm_c_re, ssm_c_im=ssm_c_im,
             ssm_d=ssm_d, glu_w=glu_w, w_br=w_br, w_o=w_o)
    M = dict(c_ctx=m_c_ctx, w_mod=m_w_mod, b_mod=m_b_mod, g_pre=m_g_pre, g_post=m_g_post, w_in=m_w_in, b_gate=m_b_gate,
             na_rpb=m_na_rpb, pool_w=m_pool_w, pool_scale=m_pool_scale, conv_w=m_conv_w, ssm_a_re=m_ssm_a_re,
             ssm_a_im=m_ssm_a_im, ssm_log_dt=m_ssm_log_dt, ssm_b_re=m_ssm_b_re, ssm_b_im=m_ssm_b_im,
             ssm_c_re=m_ssm_c_re, ssm_c_im=m_ssm_c_im, ssm_d=m_ssm_d, glu_w=m_glu_w, w_br=m_w_br, w_o=m_w_o)
    V = dict(c_ctx=v_c_ctx, w_mod=v_w_mod, b_mod=v_b_mod, g_pre=v_g_pre, g_post=v_g_post, w_in=v_w_in, b_gate=v_b_gate,
             na_rpb=v_na_rpb, pool_w=v_pool_w, pool_scale=v_pool_scale, conv_w=v_conv_w, ssm_a_re=v_ssm_a_re,
             ssm_a_im=v_ssm_a_im, ssm_log_dt=v_ssm_log_dt, ssm_b_re=v_ssm_b_re, ssm_b_im=v_ssm_b_im,
             ssm_c_re=v_ssm_c_re, ssm_c_im=v_ssm_c_im, ssm_d=v_ssm_d, glu_w=v_glu_w, w_br=v_w_br, w_o=v_w_o)
    nl = w_in.shape[0]
    xi, yi, ci = _place()
    chip = 2 * xi + yi
    example = 4 * xi + 2 * yi + ci
    mod_cols = w_mod.shape[2]

    def my_half(a):
        half = a.shape[0] // 2
        return lax.dynamic_slice_in_dim(a, ci * half, half, axis=0).astype(MM_DTYPE)

    gathered = []
    for i in range(nl):
        g = all_gather8([my_half(W[n][i]) for n in BIG_NAMES], name="gather_weights", shard_major=True)
        gathered.append(dict(zip(BIG_NAMES, g)))

    c8 = jnp.pad(c, ((0, 7), (0, 0)))
    c_all, = all_gather8([c8], name="gather_c")
    c_rows = jnp.concatenate([c_all[:, 0], jnp.broadcast_to(c_ctx[None], (8, D_MODEL))], axis=0)
    b_cols = lax.dynamic_slice_in_dim(b_mod, chip * mod_cols, mod_cols, axis=1)[:, None]
    mod_part = mod_fwd(c_rows, w_mod, b_cols)
    conv_part = jnp.pad(conv_w.reshape(nl * 3, -1), ((0, 16 - nl * 3), (0, 0)))
    parts, conv_all = all_gather8([mod_part.reshape(nl * MOD_ROWS, mod_cols), conv_part], name="gather_mod")
    mod_full = jnp.concatenate([parts[2 * p].reshape(nl, MOD_ROWS, mod_cols) for p in range(N_SHARDS)], axis=-1)
    conv_full = jnp.concatenate([conv_all[2 * p][:nl * 3].reshape(nl, 3, -1) for p in range(N_SHARDS)], axis=-1)
    own = lax.dynamic_index_in_dim(mod_full, example, axis=1, keepdims=False)
    mods = [jnp.stack([own[i], mod_full[i, 8]]) for i in range(nl)]

    layers = []
    for i in range(nl):
        gw = gathered[i]
        layers.append(dict(
            w_in=gw["w_in"], w_br=gw["w_br"].reshape(D_MODEL, D_MODEL), w_o=gw["w_o"].reshape(D_MODEL, D_MODEL),
            glu_w=gw["glu_w"].transpose(1, 0, 2).reshape(BRANCH, 2 * BRANCH),
            conv_w=jnp.pad(conv_full[i], ((0, 5), (0, 0))), pool_w=pool_w[i], pool_scale=pool_scale[i][None],
            b_gate=b_gate[i][None], g_pre=g_pre[i][None], g_post=g_post[i][None], na_rpb=na_rpb[i],
            ssm_d=ssm_d[i][None], ssm_params=tuple(W[n][i] for n in SSM_NAMES)))

    loss_local, grad_x, grads = local_step(x[0], ctx[0], loss_target[0], mods, layers)
    loss = lax.psum(loss_local, ("x", "y", "c"))

    dmod_local = jnp.stack([g["mod"] for g in grads])
    dmod_all, = all_gather8([jnp.pad(dmod_local.reshape(nl * 2, -1), ((0, 8 - nl * 2), (0, 0)))], name="gather_dmod")
    dmod_all = dmod_all[:, :nl * 2].reshape(8, nl, 2, 3 * D_MODEL)
    dmod_rows = jnp.concatenate([dmod_all[:, :, 0], dmod_all[:, :, 1]], axis=0).transpose(1, 0, 2)
    dmod_cols = lax.dynamic_slice_in_dim(dmod_rows, chip * mod_cols, mod_cols, axis=2)
    g_w_mod, d_w_mod, nm_w_mod, nv_w_mod = mod_update(c_rows, dmod_cols, w_mod, m_w_mod, v_w_mod)
    dctx_cols = dmod_cols[:, 8:].transpose(1, 0, 2).reshape(8, nl * mod_cols)
    g_cctx_part = cctx_partial(dctx_cols, w_mod, c_ctx[None])[0]

    def small_grad(n):
        if n == "c_ctx":
            return g_cctx_part
        if n == "b_mod":
            return jnp.stack([g["mod"][0] + g["mod"][1] for g in grads])
        if n in SSM_NAMES:
            return jnp.stack([g["ssm"][SSM_NAMES.index(n)] for g in grads])
        return jnp.stack([g[n] for g in grads])

    conv_grad_full = jnp.stack([g["conv_w"] for g in grads])
    flat_g = _flat([small_grad(n) for n in SMALL_NAMES] + [conv_grad_full])
    flat_all, = all_gather8([flat_g], name="gather_small_grads")
    flat_sum = sum_slabs(flat_all, name="sum_small_grads")
    small_shapes = [W[n].shape for n in SMALL_NAMES]
    small_g = _unflat(flat_sum, small_shapes + [conv_grad_full.shape])
    conv_g = lax.dynamic_slice_in_dim(small_g[-1], chip * conv_w.shape[2], conv_w.shape[2], axis=2)
    adam_names = SMALL_NAMES + ("conv_w",)
    adam_shapes = small_shapes + [conv_w.shape]
    g_list = small_g[:-1] + [conv_g]
    upd = adamw(_flat([W[n] for n in adam_names]), _flat(g_list), _flat([M[n] for n in adam_names]),
                _flat([V[n] for n in adam_names]), name="adamw_small")
    G = dict(zip(adam_names, g_list))
    DL, NM, NV = (dict(zip(adam_names, _unflat(u, adam_shapes))) for u in upd)
    G["w_mod"], DL["w_mod"], NM["w_mod"], NV["w_mod"] = g_w_mod, d_w_mod, nm_w_mod, nv_w_mod

    big = {n: [] for n in BIG_NAMES}
    for i in range(nl):
        g = grads[i]
        local = [g["w_in"], g["glu_w"].reshape(BRANCH, N_SHARDS, -1).transpose(1, 0, 2),
                 g["w_br"].reshape(N_SHARDS, BRANCH, D_MODEL), g["w_o"].reshape(N_SHARDS, BRANCH, D_MODEL)]
        partial = [pair_reduce(a, out_dtype=MM_DTYPE, name=f"grad_pair_{n}") for n, a in zip(BIG_NAMES, local)]
        arrived = chip_exchange(partial, name="grad_exchange")
        full = [pair_finish(a, name=f"grad_finish_{n}") for n, a in zip(BIG_NAMES, arrived)]
        for n, a in zip(BIG_NAMES, full):
            big[n].append(a)
    for n in BIG_NAMES:
        g = jnp.stack(big[n])
        rows = g.shape[0] * g.shape[1]
        d, nm, nv = adamw(W[n].reshape(rows, -1), g.reshape(rows, -1), M[n].reshape(rows, -1), V[n].reshape(rows, -1),
                          name=f"adamw_{n}")
        G[n], DL[n], NM[n], NV[n] = g, d.reshape(g.shape), nm.reshape(g.shape), nv.reshape(g.shape)

    out = [loss, grad_x[None]]
    for group in (G, DL, NM, NV):
        out += [group[n].reshape(W[n].shape) for n in WEIGHT_NAMES]
    return tuple(out)
```

```python
import functools
import math

import numpy as np
import jax
import jax.numpy as jnp
from jax import lax
from jax.experimental import pallas as pl
from jax.experimental.pallas import tpu as pltpu

F32 = jnp.float32
BF16 = jnp.bfloat16
MM_DTYPE = jnp.bfloat16

D_MODEL = 2048
BRANCH = 512
N_HEADS = 8
HEAD_DIM = 64
GRID_W = 64
WIN_ROWS = 8
WIN_COLS = 16
POOL_GROUPS = 4
POOL_DIM = 128
SSM_GROUPS = 32
SSM_GDIM = 16
SSM_STATE = 64
N_STATE = SSM_GROUPS * SSM_STATE
IN_TOTAL = 14336
RMS_EPS = 1e-6
NEG_INF = -1e30
COL = dict(q=0, k=512, v=1024, na_z=1536, pool_u=2048, pool_z=2560, conv_x=3072, conv_b=3584,
           conv_c=4096, conv_z=4608, ssm_u=5120, ssm_z=5632, merge=6144)
N_SHARDS = 4
W_IN_SHARD = IN_TOTAL // N_SHARDS
VMEM_LIMIT_BYTES = 48 * 1024 * 1024
ROW_TILE = 256

ADAM_LR = 0.001
ADAM_B1 = 0.9
ADAM_B2 = 0.999
ADAM_EPS = 1e-08
ADAM_WD = 0.01
ADAM_STEP = 10


def _params(*sem):
    return pltpu.CompilerParams(dimension_semantics=sem, vmem_limit_bytes=VMEM_LIMIT_BYTES)


def _sigmoid(x):
    return 1.0 / (1.0 + jnp.exp(-x))


def _matmul(a, b, a_spec, b_spec, o_spec, out_shape, grid, *, ta=False, tb=False, name, carried=None):
    nk = grid[-1]
    kaxis = len(grid) - 1
    dims = (((0,) if ta else (1,), (1,) if tb else (0,)), ((), ()))
    n_acc = 0 if nk == 1 else 1

    def compute(a_ref, b_ref, o_ref, acc):
        p = lax.dot_general(a_ref[...].astype(MM_DTYPE), b_ref[...].astype(MM_DTYPE), dims,
                            preferred_element_type=F32)
        if nk == 1:
            o_ref[...] = p.astype(o_ref.dtype)
            return
        acc_ref, = acc
        k = pl.program_id(kaxis)

        @pl.when(k == 0)
        def _():
            acc_ref[...] = p

        @pl.when(k > 0)
        def _():
            acc_ref[...] += p

        @pl.when(k == nk - 1)
        def _():
            o_ref[...] = acc_ref[...].astype(o_ref.dtype)

    oblock = tuple(s for s in o_spec.block_shape if s is not None)
    scratch = [] if nk == 1 else [pltpu.VMEM(oblock, F32)]
    if carried is None:
        def body(a_ref, b_ref, o_ref, *acc):
            compute(a_ref, b_ref, o_ref, acc)

        sem = ("parallel",) * (len(grid) - 1) + ("arbitrary",)
        return pl.pallas_call(body, out_shape=out_shape, grid=grid, in_specs=[a_spec, b_spec],
                              out_specs=o_spec, scratch_shapes=scratch, compiler_params=_params(*sem),
                              name=name)(a, b)

    n_in, n_out = len(carried.ins), len(carried.out_shapes)
    steps = math.prod(grid)

    def body(a_ref, b_ref, *rest):
        c_ins, o_ref, c_outs = rest[:n_in], rest[n_in], rest[n_in + 1:n_in + 1 + n_out]
        acc, sems = rest[n_in + 1 + n_out:][:n_acc], rest[n_in + 1 + n_out + n_acc:]
        step = pl.program_id(0)
        for ax in range(1, len(grid)):
            step = step * grid[ax] + pl.program_id(ax)

        @pl.when(step == 0)
        def _():
            carried.start(c_ins, c_outs, sems)

        compute(a_ref, b_ref, o_ref, acc)

        @pl.when(step == steps // 2)
        def _():
            carried.mid(c_ins, c_outs, sems)

        @pl.when(step == steps - 1)
        def _():
            carried.finish(c_ins, c_outs, sems)

    res = pl.pallas_call(body, out_shape=[out_shape] + list(carried.out_shapes), grid=grid,
                         in_specs=[a_spec, b_spec] + [HBM_SPEC] * n_in, out_specs=[o_spec] + [HBM_SPEC] * n_out,
                         scratch_shapes=scratch + list(carried.sem_shapes),
                         compiler_params=_params(*(("arbitrary",) * len(grid))), name=name)(a, b, *carried.ins)
    return res[0], res[1:]


def _pick(n, cands):
    for c in cands:
        if n % c == 0:
            return c
    raise ValueError(f"no tile for {n}")


def _row_tile(T):
    return _pick(T, (544, 512, 256, 128))


def mm_nn(a, b, *, out_dtype, name, tn=512, a_rows=None, o_rows=None, a_cols=None):
    M = a.shape[0]
    c0, K = a_cols or (0, a.shape[1])
    N = b.shape[1]
    tm = ROW_TILE if (a_rows or o_rows) else _row_tile(M)
    tn = min(tn, N)
    tk = K if K <= 2048 else _pick(K, (2048, 1024, 512))
    kb0 = c0 // tk
    ar = a_rows or (lambda i: i)
    orr = o_rows or (lambda i: i)
    return _matmul(a, b, pl.BlockSpec((tm, tk), lambda i, j, k: (ar(i), kb0 + k)),
                   pl.BlockSpec((tk, tn), lambda i, j, k: (k, j)),
                   pl.BlockSpec((tm, tn), lambda i, j, k: (orr(i), j)),
                   jax.ShapeDtypeStruct((M, N), out_dtype), (M // tm, N // tn, K // tk), name=name)


def mm_nt(a, b, *, out_dtype, name, a_rows=None, o_rows=None):
    M, K = a.shape
    N = b.shape[0]
    tm = ROW_TILE if (a_rows or o_rows) else _row_tile(M)
    tn = min(N, 2048)
    tk = K if K <= 1024 else _pick(K, (1024, 512))
    ar = a_rows or (lambda i: i)
    orr = o_rows or (lambda i: i)
    return _matmul(a, b, pl.BlockSpec((tm, tk), lambda i, j, k: (ar(i), k)),
                   pl.BlockSpec((tn, tk), lambda i, j, k: (j, k)),
                   pl.BlockSpec((tm, tn), lambda i, j, k: (orr(i), j)),
                   jax.ShapeDtypeStruct((M, N), out_dtype), (M // tm, N // tn, K // tk), tb=True, name=name)


def mm_tn(a, b, *, out_dtype, name, a_rows=None, b_rows=None, tm=512, tn=1024, a_cols=None):
    K = a.shape[0]
    c0, M = a_cols or (0, a.shape[1])
    N = b.shape[1]
    tk = ROW_TILE if (a_rows or b_rows) else K
    tm = min(tm, M)
    tn = min(tn, N)
    mb0 = c0 // tm
    ar = a_rows or (lambda k: k)
    br = b_rows or (lambda k: k)
    return _matmul(a, b, pl.BlockSpec((tk, tm), lambda i, j, k: (ar(k), mb0 + i)),
                   pl.BlockSpec((tk, tn), lambda i, j, k: (br(k), j)),
                   pl.BlockSpec((tm, tn), lambda i, j, k: (i, j)),
                   jax.ShapeDtypeStruct((M, N), out_dtype), (M // tm, N // tn, K // tk), ta=True, name=name)


def _ew(fn, ins, outs, colsums, *, T, L, name):
    tb = ROW_TILE
    nlat = L // tb
    seg = lambda i: jnp.where(i >= nlat, 1, 0)
    in_specs, arrays = [], []
    for arr, kind, cb, width in ins:
        arrays.append(arr)
        if kind == "row":
            in_specs.append(pl.BlockSpec((tb, width), lambda i, cb=cb: (i, cb)))
        elif kind == "bcast":
            in_specs.append(pl.BlockSpec((1, width), lambda i, cb=cb: (0, cb)))
        else:
            in_specs.append(pl.BlockSpec((None, 1, width), lambda i, cb=cb: (seg(i), 0, cb)))
    out_specs = [pl.BlockSpec((tb, w), lambda i: (i, 0)) for w, _ in outs]
    out_shapes = [jax.ShapeDtypeStruct((T, w), dt) for w, dt in outs]
    out_specs += [pl.BlockSpec((None, 1, w), lambda i: (seg(i), 0, 0)) for w in colsums]
    out_shapes += [jax.ShapeDtypeStruct((2, 1, w), F32) for w in colsums]
    n_in, n_out = len(ins), len(outs)

    def body(*refs):
        i = pl.program_id(0)
        res = fn(*[r[...] for r in refs[:n_in]])
        for r, v in zip(refs[n_in:n_in + n_out], res[:n_out]):
            r[...] = v.astype(r.dtype)
        first = (i == 0) | (i == nlat)
        for r, v in zip(refs[n_in + n_out:], res[n_out:]):
            s = jnp.sum(v, axis=0, keepdims=True)

            @pl.when(first)
            def _(r=r, s=s):
                r[...] = s

            @pl.when(jnp.logical_not(first))
            def _(r=r, s=s):
                r[...] += s

    res = pl.pallas_call(body, out_shape=out_shapes, grid=(T // tb,), in_specs=in_specs,
                         out_specs=out_specs, compiler_params=_params("arbitrary"), name=name)(*arrays)
    return res


def _rms(x):
    return lax.rsqrt(jnp.mean(x * x, axis=-1, keepdims=True) + RMS_EPS)


def prenorm_fwd(X, g, scale, shift, *, L):
    T = X.shape[0]

    def fn(x, g, sc, sh):
        return ((x * _rms(x)) * (g * (1.0 + sc)) + sh,)

    h, = _ew(fn, [(X, "row", 0, D_MODEL), (g, "bcast", 0, D_MODEL), (scale, "seg", 0, D_MODEL),
                  (shift, "seg", 0, D_MODEL)], [(D_MODEL, MM_DTYPE)], [], T=T, L=L, name="prenorm_fwd")
    return h


def prenorm_bwd(dh, X, g, scale, dres, *, L):
    T = X.shape[0]

    def fn(dh, x, g, sc, dres):
        r = _rms(x)
        xn = x * r
        dxn = dh * (g * (1.0 + sc))
        dx = r * (dxn - xn * jnp.mean(dxn * xn, axis=-1, keepdims=True))
        return dres + dx, dh, dh * xn

    return _ew(fn, [(dh, "row", 0, D_MODEL), (X, "row", 0, D_MODEL), (g, "bcast", 0, D_MODEL),
                    (scale, "seg", 0, D_MODEL), (dres, "row", 0, D_MODEL)],
               [(D_MODEL, F32)], [D_MODEL, D_MODEL], T=T, L=L, name="prenorm_bwd")


def postnorm_fwd(X, y, g, gate, *, L):
    T = X.shape[0]

    def fn(x, y, g, gate):
        return (x + gate * ((y * _rms(y)) * g),)

    out, = _ew(fn, [(X, "row", 0, D_MODEL), (y, "row", 0, D_MODEL), (g, "bcast", 0, D_MODEL),
                    (gate, "seg", 0, D_MODEL)], [(D_MODEL, F32)], [], T=T, L=L, name="postnorm_fwd")
    return out


def postnorm_bwd(dX, y, g, gate, *, L):
    T = dX.shape[0]

    def fn(dx, y, g, gate):
        r = _rms(y)
        yn = y * r
        dyn = dx * (gate * g)
        dy = r * (dyn - yn * jnp.mean(dyn * yn, axis=-1, keepdims=True))
        return dy, dx * yn

    return _ew(fn, [(dX, "row", 0, D_MODEL), (y, "row", 0, D_MODEL), (g, "bcast", 0, D_MODEL),
                    (gate, "seg", 0, D_MODEL)], [(D_MODEL, MM_DTYPE)], [D_MODEL], T=T, L=L, name="postnorm_bwd")


def loss_and_grad(X, target, *, L):
    T = X.shape[0]
    tb = ROW_TILE
    nlat = L // tb

    def body(x_ref, t_ref, dx_ref, part_ref):
        i = pl.program_id(0)

        @pl.when(i < nlat)
        def _():
            err = x_ref[...] - t_ref[...]
            dx_ref[...] = err * (1.0 / D_MODEL)
            part_ref[...] = jnp.full(part_ref.shape, 0.5 / D_MODEL * jnp.sum(err * err), F32)

        @pl.when(i >= nlat)
        def _():
            dx_ref[...] = jnp.zeros(dx_ref.shape, F32)
            part_ref[...] = jnp.zeros(part_ref.shape, F32)

    dx, part = pl.pallas_call(
        body, out_shape=[jax.ShapeDtypeStruct((T, D_MODEL), F32), jax.ShapeDtypeStruct((T // tb, 8, 128), F32)],
        grid=(T // tb,),
        in_specs=[pl.BlockSpec((tb, D_MODEL), lambda i: (i, 0)),
                  pl.BlockSpec((tb, D_MODEL), lambda i: (jnp.minimum(i, nlat - 1), 0))],
        out_specs=[pl.BlockSpec((tb, D_MODEL), lambda i: (i, 0)), pl.BlockSpec((None, 8, 128), lambda i: (i, 0, 0))],
        compiler_params=_params("parallel"), name="loss_and_grad")(X, target)
    return jnp.sum(part[:, 0, 0]), dx


Q_BLOCK = WIN_ROWS * GRID_W
BAND = 2 * WIN_ROWS * GRID_W


PAIR_TILES = 2 * WIN_ROWS
HEAD_PAIRS = N_HEADS // 2
LANES = 2 * HEAD_DIM
ROW_SHIFT = GRID_W.bit_length() - 1


def bias_pair_tiles(rpb):
    col = np.arange(GRID_W)
    col_start = np.clip(col - WIN_COLS // 2, 0, GRID_W - WIN_COLS)
    in_win = (col[None, :] >= col_start[:, None]) & (col[None, :] < col_start[:, None] + WIN_COLS)
    dcol = np.clip(col[None, :] - col[:, None] + (WIN_COLS - 1), 0, 2 * WIN_COLS - 2)
    E = np.stack([(dcol == dc) & in_win for dc in range(2 * WIN_COLS - 1)]).astype(np.float32)
    tiles = jnp.einsum("hrd,dqk->hrqk", rpb, E, precision=lax.Precision.HIGHEST)
    z = jnp.zeros((N_HEADS, 1, GRID_W, GRID_W), F32)
    return jnp.concatenate([jnp.concatenate([z, tiles], axis=1), jnp.concatenate([tiles, z], axis=1)], axis=-1)


def _band_row(i, rows):
    return jnp.clip(WIN_ROWS * i - WIN_ROWS // 2, 0, rows - 2 * WIN_ROWS)


def _band_start(i, rows):
    return pl.multiple_of(_band_row(i, rows) * GRID_W, 256)


def _window_mask(i, rows):
    r = lax.broadcasted_iota(jnp.int32, (Q_BLOCK, BAND), 0)
    k = lax.broadcasted_iota(jnp.int32, (Q_BLOCK, BAND), 1)
    qr, qc = WIN_ROWS * i + (r >> ROW_SHIFT), r & (GRID_W - 1)
    kr, kc = _band_row(i, rows) + (k >> ROW_SHIFT), k & (GRID_W - 1)
    ws = jnp.clip(qr - WIN_ROWS // 2, 0, rows - WIN_ROWS)
    cs = jnp.clip(qc - WIN_COLS // 2, 0, GRID_W - WIN_COLS)
    return (kr >= ws) & (kr < ws + WIN_ROWS) & (kc >= cs) & (kc < cs + WIN_COLS)


def _pair_index(i, rows, a, j):
    off = _band_row(i, rows) - WIN_ROWS * i
    return jnp.clip(2 * j - a + WIN_ROWS + off, 0, PAIR_TILES - 1)


def _band_bias(p_ref, hh, i, rows):
    bands = [jnp.concatenate([p_ref[hh, _pair_index(i, rows, a, j)] for j in range(WIN_ROWS)], axis=1)
             for a in range(WIN_ROWS)]
    return jnp.concatenate(bands, axis=0)


def _dot_nt(a, b):
    return lax.dot_general(a, b, (((1,), (1,)), ((), ())), preferred_element_type=F32)


def _dot_tn(a, b):
    return lax.dot_general(a, b, (((0,), (0,)), ((), ())), preferred_element_type=F32)


def _dot(a, b):
    return jnp.dot(a, b, preferred_element_type=F32)


QKV_BLOCKS = tuple(COL[n] // LANES for n in ("q", "k", "v"))
SCALE = HEAD_DIM ** -0.5


def _head(x, hh):
    return x[:, hh * HEAD_DIM:(hh + 1) * HEAD_DIM]


def _both(fn):
    res = [fn(0), fn(1)]
    return [jnp.concatenate([a, b], axis=1) for a, b in zip(*res)]


def attn_fwd(proj, ptiles, *, L):
    T = proj.shape[0]
    N = T - L
    rows, nq = L // GRID_W, L // Q_BLOCK
    qb, kb, vb = QKV_BLOCKS

    def body(q_ref, k_ref, v_ref, p_ref, o_ref, lse_ref):
        i = pl.program_id(1)
        ks = _band_start(i, rows)
        mask = _window_mask(i, rows)
        qv = q_ref[...].astype(MM_DTYPE)
        kband, vband = k_ref[pl.ds(ks, BAND), :].astype(MM_DTYPE), v_ref[pl.ds(ks, BAND), :].astype(MM_DTYPE)
        kctx, vctx = k_ref[pl.ds(L, N), :].astype(MM_DTYPE), v_ref[pl.ds(L, N), :].astype(MM_DTYPE)

        def head(hh):
            q = _head(qv, hh)
            sb = _dot_nt(q, _head(kband, hh)) * SCALE + jnp.where(mask, _band_bias(p_ref, hh, i, rows), NEG_INF)
            sc = _dot_nt(q, _head(kctx, hh)) * SCALE
            m = jnp.maximum(jnp.max(sb, axis=-1, keepdims=True), jnp.max(sc, axis=-1, keepdims=True))
            pb, pc = jnp.exp(sb - m), jnp.exp(sc - m)
            l = jnp.sum(pb, axis=-1, keepdims=True) + jnp.sum(pc, axis=-1, keepdims=True)
            o = _dot(pb.astype(MM_DTYPE), _head(vband, hh)) + _dot(pc.astype(MM_DTYPE), _head(vctx, hh))
            return o / l, jnp.broadcast_to(m + jnp.log(l), (Q_BLOCK, HEAD_DIM))

        o_ref[...], lse_ref[...] = _both(head)

    qspec = lambda b0: pl.BlockSpec((Q_BLOCK, LANES), lambda hp, i: (i, b0 + hp))
    kspec = lambda b0: pl.BlockSpec((T, LANES), lambda hp, i: (0, b0 + hp))
    return pl.pallas_call(
        body, out_shape=[jax.ShapeDtypeStruct((T, BRANCH), F32), jax.ShapeDtypeStruct((L, BRANCH), F32)],
        grid=(HEAD_PAIRS, nq),
        in_specs=[qspec(qb), kspec(kb), kspec(vb),
                  pl.BlockSpec((2, PAIR_TILES, GRID_W, LANES), lambda hp, i: (hp, 0, 0, 0))],
        out_specs=[qspec(0), qspec(0)],
        compiler_params=_params("parallel", "arbitrary"), name="attn_fwd")(proj, proj, proj, ptiles)


def attn_bwd(proj, ptiles, o, do, lse, *, L):
    T = proj.shape[0]
    N = T - L
    rows, nq = L // GRID_W, L // Q_BLOCK
    qb, kb, vb = QKV_BLOCKS

    def body(q_ref, k_ref, v_ref, p_ref, o_ref, do_ref, lse_ref, dq_ref, dk_ref, dv_ref, dp_ref):
        i = pl.program_id(1)
        ks = _band_start(i, rows)

        @pl.when(i == 0)
        def _():
            dk_ref[...] = jnp.zeros(dk_ref.shape, F32)
            dv_ref[...] = jnp.zeros(dv_ref.shape, F32)
            dp_ref[...] = jnp.zeros(dp_ref.shape, F32)

        mask = _window_mask(i, rows)
        qv = q_ref[...].astype(MM_DTYPE)
        kband, vband = k_ref[pl.ds(ks, BAND), :].astype(MM_DTYPE), v_ref[pl.ds(ks, BAND), :].astype(MM_DTYPE)
        kctx, vctx = k_ref[pl.ds(L, N), :].astype(MM_DTYPE), v_ref[pl.ds(L, N), :].astype(MM_DTYPE)
        ov, dof, lsev = o_ref[...], do_ref[...], lse_ref[...]

        def head(hh):
            q, kb_h, kc_h, vb_h, vc_h = (_head(t, hh) for t in (qv, kband, kctx, vband, vctx))
            lse = _head(lsev, hh)[:, 0:1]
            pb = jnp.exp(_dot_nt(q, kb_h) * SCALE + jnp.where(mask, _band_bias(p_ref, hh, i, rows), NEG_INF) - lse)
            pc = jnp.exp(_dot_nt(q, kc_h) * SCALE - lse)
            do_h = _head(dof, hh)
            delta = jnp.sum(do_h * _head(ov, hh), axis=-1, keepdims=True)
            dov = do_h.astype(MM_DTYPE)
            dsb = pb * (_dot_nt(dov, vb_h) - delta)
            dsc = pc * (_dot_nt(dov, vc_h) - delta)
            for a in range(WIN_ROWS):
                for j in range(WIN_ROWS):
                    dp_ref[hh, _pair_index(i, rows, a, j)] += dsb[a * GRID_W:(a + 1) * GRID_W, j * LANES:(j + 1) * LANES]
            dsb_s, dsc_s = (dsb * SCALE).astype(MM_DTYPE), (dsc * SCALE).astype(MM_DTYPE)
            dq = _dot(dsb_s, kb_h) + _dot(dsc_s, kc_h)
            return (dq, _dot_tn(dsb_s, q), _dot_tn(dsc_s, q), _dot_tn(pb.astype(MM_DTYPE), dov),
                    _dot_tn(pc.astype(MM_DTYPE), dov))

        dq, dkb, dkc, dvb, dvc = _both(head)
        dq_ref[...] = dq
        dk_ref[pl.ds(ks, BAND), :] += dkb
        dk_ref[pl.ds(L, N), :] += dkc
        dv_ref[pl.ds(ks, BAND), :] += dvb
        dv_ref[pl.ds(L, N), :] += dvc

    qspec = lambda b0: pl.BlockSpec((Q_BLOCK, LANES), lambda hp, i: (i, b0 + hp))
    kspec = lambda b0: pl.BlockSpec((T, LANES), lambda hp, i: (0, b0 + hp))
    pspec = pl.BlockSpec((2, PAIR_TILES, GRID_W, LANES), lambda hp, i: (hp, 0, 0, 0))
    return pl.pallas_call(
        body,
        out_shape=[jax.ShapeDtypeStruct((T, BRANCH), F32)] * 3 + [jax.ShapeDtypeStruct(ptiles.shape, F32)],
        grid=(HEAD_PAIRS, nq),
        in_specs=[qspec(qb), kspec(kb), kspec(vb), pspec, qspec(0), qspec(0), qspec(0)],
        out_specs=[qspec(0), kspec(0), kspec(0), pspec],
        compiler_params=_params("parallel", "arbitrary"), name="attn_bwd")(proj, proj, proj, ptiles, o, do, lse)


ANY_SPEC = pl.BlockSpec(memory_space=pl.ANY)


def cattn_fwd(proj, o, *, L):
    T = proj.shape[0]
    N = T - L
    qb, kb, vb = QKV_BLOCKS
    cspec = lambda b0: pl.BlockSpec((N, LANES), lambda hp: (L // N, b0 + hp))

    def body(q_ref, k_ref, v_ref, o_in, o_ref, lse_ref):
        qv, kv, vv = (r[...].astype(MM_DTYPE) for r in (q_ref, k_ref, v_ref))

        def head(hh):
            s = _dot_nt(_head(qv, hh), _head(kv, hh)) * SCALE
            m = jnp.max(s, axis=-1, keepdims=True)
            p = jnp.exp(s - m)
            l = jnp.sum(p, axis=-1, keepdims=True)
            return _dot(p.astype(MM_DTYPE), _head(vv, hh)) / l, jnp.broadcast_to(m + jnp.log(l), (N, HEAD_DIM))

        o_ref[...], lse_ref[...] = _both(head)

    return pl.pallas_call(
        body, out_shape=[jax.ShapeDtypeStruct(o.shape, F32), jax.ShapeDtypeStruct((N, BRANCH), F32)],
        grid=(HEAD_PAIRS,), in_specs=[cspec(qb), cspec(kb), cspec(vb), ANY_SPEC],
        out_specs=[cspec(0), pl.BlockSpec((N, LANES), lambda hp: (0, hp))], input_output_aliases={3: 0},
        compiler_params=_params("parallel"), name="cattn_fwd")(proj, proj, proj, o)


def cattn_bwd(proj, o, do, lse, dq, dk, dv, *, L):
    T = proj.shape[0]
    N = T - L
    qb, kb, vb = QKV_BLOCKS
    cspec = lambda b0: pl.BlockSpec((N, LANES), lambda hp: (L // N, b0 + hp))

    def body(q_ref, k_ref, v_ref, o_ref, do_ref, lse_ref, dq_in, dk_in, dv_in, dq_ref, dk_ref, dv_ref):
        qv, kv, vv = (r[...].astype(MM_DTYPE) for r in (q_ref, k_ref, v_ref))
        ov, dof, lsev = o_ref[...], do_ref[...], lse_ref[...]

        def head(hh):
            q, k, v = _head(qv, hh), _head(kv, hh), _head(vv, hh)
            p = jnp.exp(_dot_nt(q, k) * SCALE - _head(lsev, hh)[:, 0:1])
            do_h = _head(dof, hh)
            delta = jnp.sum(do_h * _head(ov, hh), axis=-1, keepdims=True)
            dov = do_h.astype(MM_DTYPE)
            ds = (p * (_dot_nt(dov, v) - delta) * SCALE).astype(MM_DTYPE)
            return _dot(ds, k), _dot_tn(ds, q), _dot_tn(p.astype(MM_DTYPE), dov)

        dq_c, dk_c, dv_c = _both(head)
        dq_ref[...] = dq_c
        dk_ref[...] = dk_in[...] + dk_c
        dv_ref[...] = dv_in[...] + dv_c

    return pl.pallas_call(
        body, out_shape=[jax.ShapeDtypeStruct(dq.shape, F32)] * 3, grid=(HEAD_PAIRS,),
        in_specs=[cspec(qb), cspec(kb), cspec(vb), cspec(0), cspec(0), pl.BlockSpec((N, LANES), lambda hp: (0, hp)),
                  ANY_SPEC, cspec(0), cspec(0)],
        out_specs=[cspec(0)] * 3, input_output_aliases={6: 0, 7: 1, 8: 2},
        compiler_params=_params("parallel"), name="cattn_bwd")(proj, proj, proj, o, do, lse, dq, dk, dv)


PAD = 16


def _row_ids(T):
    return lax.broadcasted_iota(jnp.int32, (T, POOL_DIM), 0)


def _same_segment(t, s, L, T):
    return (s >= 0) & (s < T) & ((t < L) == (s < L))


def _window_sum(buf_ref, x, half, *, L, T, transpose):
    buf_ref[pl.ds(PAD, T), :] = x
    t = _row_ids(T)
    acc = jnp.zeros((T, POOL_DIM), F32)
    for j in range(-8, 9):
        inside = ((j > -half) & (j <= half)) if transpose else ((j >= -half) & (j < half))
        ok = _same_segment(t, t + j, L, T) & inside
        acc = acc + jnp.where(ok, buf_ref[pl.ds(PAD + j, T), :], 0.0)
    return acc


def _window_count(half, *, L, T):
    t = _row_ids(T)
    pos = jnp.where(t < L, t, t - L)
    seg_len = jnp.where(t < L, L, T - L)
    return (jnp.minimum(pos + half, seg_len) - jnp.maximum(pos - half, 0)).astype(F32)


def _zero_pads(buf_ref, T):
    buf_ref[pl.ds(0, PAD), :] = jnp.zeros((PAD, POOL_DIM), F32)
    buf_ref[pl.ds(PAD + T, PAD), :] = jnp.zeros((PAD, POOL_DIM), F32)


def pool_fwd(proj, pool_w, pool_scale, *, L):
    T = proj.shape[0]
    cb0 = COL["pool_u"] // POOL_DIM

    def body(u_ref, w_ref, s_ref, o_ref, p_ref, buf_ref):
        half = jnp.left_shift(1, pl.program_id(0))
        _zero_pads(buf_ref, T)
        u = u_ref[...]
        pooled = _window_sum(buf_ref, u, half, L=L, T=T, transpose=False) / _window_count(half, L=L, T=T) - u
        pm = pooled.astype(MM_DTYPE)
        p_ref[...] = pm
        o_ref[...] = _dot(pm, w_ref[...].astype(MM_DTYPE)) * s_ref[...]

    cspec = pl.BlockSpec((T, POOL_DIM), lambda g: (0, g))
    return pl.pallas_call(
        body, out_shape=[jax.ShapeDtypeStruct((T, BRANCH), F32), jax.ShapeDtypeStruct((T, BRANCH), MM_DTYPE)],
        grid=(POOL_GROUPS,),
        in_specs=[pl.BlockSpec((T, POOL_DIM), lambda g: (0, cb0 + g)),
                  pl.BlockSpec((None, POOL_DIM, POOL_DIM), lambda g: (g, 0, 0)),
                  pl.BlockSpec((1, POOL_DIM), lambda g: (0, g))],
        out_specs=[cspec, cspec], scratch_shapes=[pltpu.VMEM((T + 2 * PAD, POOL_DIM), F32)],
        compiler_params=_params("parallel"), name="pool_fwd")(proj, pool_w, pool_scale)


def pool_bwd(do, pooled, pool_w, pool_scale, *, L):
    T = do.shape[0]

    def body(do_ref, p_ref, w_ref, s_ref, du_ref, dw_ref, ds_ref, buf_ref):
        half = jnp.left_shift(1, pl.program_id(0))
        _zero_pads(buf_ref, T)
        pm = p_ref[...]
        w = w_ref[...].astype(MM_DTYPE)
        mixed = _dot(pm, w)
        dov = do_ref[...]
        ds_ref[...] = jnp.broadcast_to(jnp.sum(dov * mixed, axis=0, keepdims=True), ds_ref.shape)
        dmixed = (dov * s_ref[...]).astype(MM_DTYPE)
        dw_ref[...] = _dot_tn(pm, dmixed)
        dpooled = _dot_nt(dmixed, w)
        scaled = dpooled / _window_count(half, L=L, T=T)
        du = _window_sum(buf_ref, scaled, half, L=L, T=T, transpose=True) - dpooled
        du_ref[...] = du.astype(du_ref.dtype)

    cspec = pl.BlockSpec((T, POOL_DIM), lambda g: (0, g))
    return pl.pallas_call(
        body, out_shape=[jax.ShapeDtypeStruct((T, BRANCH), MM_DTYPE),
                         jax.ShapeDtypeStruct((POOL_GROUPS, POOL_DIM, POOL_DIM), F32),
                         jax.ShapeDtypeStruct((8, BRANCH), F32)],
        grid=(POOL_GROUPS,),
        in_specs=[cspec, cspec, pl.BlockSpec((None, POOL_DIM, POOL_DIM), lambda g: (g, 0, 0)),
                  pl.BlockSpec((1, POOL_DIM), lambda g: (0, g))],
        out_specs=[cspec, pl.BlockSpec((None, POOL_DIM, POOL_DIM), lambda g: (g, 0, 0)),
                   pl.BlockSpec((8, POOL_DIM), lambda g: (0, g))],
        scratch_shapes=[pltpu.VMEM((T + 2 * PAD, POOL_DIM), F32)],
        compiler_params=_params("parallel"), name="pool_bwd")(do, pooled, pool_w, pool_scale)


def _shifted(buf_ref, x, j, *, L, T):
    buf_ref[pl.ds(PAD, T), :] = x
    t = _row_ids(T)
    return jnp.where(_same_segment(t, t + j, L, T), buf_ref[pl.ds(PAD + j, T), :], 0.0)


def conv_fwd(proj, conv_w, *, L):
    T = proj.shape[0]
    nb = BRANCH // POOL_DIM
    cx, cbb, cc = (COL[n] // POOL_DIM for n in ("conv_x", "conv_b", "conv_c"))

    def body(x_ref, b_ref, c_ref, w_ref, o_ref, buf_ref):
        _zero_pads(buf_ref, T)
        xc = c_ref[...] * x_ref[...]
        w = w_ref[...]
        conv = (w[0:1] * _shifted(buf_ref, xc, -1, L=L, T=T) + w[1:2] * xc
                + w[2:3] * _shifted(buf_ref, xc, 1, L=L, T=T))
        o_ref[...] = b_ref[...] * conv

    return pl.pallas_call(
        body, out_shape=jax.ShapeDtypeStruct((T, BRANCH), F32), grid=(nb,),
        in_specs=[pl.BlockSpec((T, POOL_DIM), lambda g: (0, cx + g)), pl.BlockSpec((T, POOL_DIM), lambda g: (0, cbb + g)),
                  pl.BlockSpec((T, POOL_DIM), lambda g: (0, cc + g)), pl.BlockSpec((8, POOL_DIM), lambda g: (0, g))],
        out_specs=pl.BlockSpec((T, POOL_DIM), lambda g: (0, g)),
        scratch_shapes=[pltpu.VMEM((T + 2 * PAD, POOL_DIM), F32)],
        compiler_params=_params("parallel"), name="conv_fwd")(proj, proj, proj, conv_w)


def conv_bwd(do, proj, conv_w, *, L):
    T = proj.shape[0]
    nb = BRANCH // POOL_DIM
    cx, cbb, cc = (COL[n] // POOL_DIM for n in ("conv_x", "conv_b", "conv_c"))

    def body(do_ref, x_ref, b_ref, c_ref, w_ref, dx_ref, db_ref, dc_ref, dw_ref, buf_ref):
        _zero_pads(buf_ref, T)
        xv, gb, gc = x_ref[...], b_ref[...], c_ref[...]
        xc = gc * xv
        w = w_ref[...]
        xm = _shifted(buf_ref, xc, -1, L=L, T=T)
        xp = _shifted(buf_ref, xc, 1, L=L, T=T)
        conv = w[0:1] * xm + w[1:2] * xc + w[2:3] * xp
        dov = do_ref[...]
        db_ref[...] = (dov * conv).astype(db_ref.dtype)
        dconv = dov * gb
        sums = [jnp.sum(dconv * a, axis=0, keepdims=True) for a in (xm, xc, xp)]
        dw_ref[...] = jnp.concatenate(sums + [jnp.zeros((5, POOL_DIM), F32)], axis=0)
        dxc = (w[0:1] * _shifted(buf_ref, dconv, 1, L=L, T=T) + w[1:2] * dconv
               + w[2:3] * _shifted(buf_ref, dconv, -1, L=L, T=T))
        dc_ref[...] = (dxc * xv).astype(dc_ref.dtype)
        dx_ref[...] = (dxc * gc).astype(dx_ref.dtype)

    ospec = lambda off: pl.BlockSpec((T, POOL_DIM), lambda g: (0, off + g))
    return pl.pallas_call(
        body, out_shape=[jax.ShapeDtypeStruct((T, BRANCH), MM_DTYPE)] * 3 + [jax.ShapeDtypeStruct((8, BRANCH), F32)],
        grid=(nb,),
        in_specs=[ospec(0), ospec(cx), ospec(cbb), ospec(cc), pl.BlockSpec((8, POOL_DIM), lambda g: (0, g))],
        out_specs=[ospec(0), ospec(0), ospec(0), pl.BlockSpec((8, POOL_DIM), lambda g: (0, g))],
        scratch_shapes=[pltpu.VMEM((T + 2 * PAD, POOL_DIM), F32)],
        compiler_params=_params("parallel"), name="conv_bwd")(do, proj, proj, proj, conv_w)


SCAN_COLS = 1024
SCAN_ROWS = 256


def ssm_operators(a_re, a_im, log_dt, b_re, b_im, c_re, c_im):
    dt = jnp.exp(log_dt)[..., None]
    mag = jnp.exp(a_re * dt)
    abar_re, abar_im = mag * jnp.cos(a_im * dt), mag * jnp.sin(a_im * dt)
    den = a_re * a_re + a_im * a_im
    num_re, num_im = abar_re - 1.0, abar_im
    f_re = (num_re * a_re + num_im * a_im) / den
    f_im = (num_im * a_re - num_re * a_im) / den
    bbar_re = f_re[..., None] * b_re - f_im[..., None] * b_im
    bbar_im = f_re[..., None] * b_im + f_im[..., None] * b_re
    gpb = SSM_GROUPS // SSM_BLOCKS
    eye = jnp.eye(gpb, dtype=bool)[None, None, :, None, :, None]

    def blocks(t):
        _, _, a, b = t.shape
        t = t.reshape(2, SSM_BLOCKS, gpb, a, 1, b)
        return jnp.where(eye, t, 0.0).reshape(2, SSM_BLOCKS, gpb * a, gpb * b)

    in_map = lambda bbar: blocks(bbar.transpose(0, 1, 3, 2))
    out_map = lambda c: blocks(c.transpose(0, 1, 3, 2))
    abar = jnp.concatenate([abar_re.reshape(2, 1, N_STATE), abar_im.reshape(2, 1, N_STATE)], axis=-1)
    bcat = jnp.concatenate([in_map(bbar_re), in_map(bbar_im)], axis=1)
    ccat = jnp.concatenate([out_map(c_re), -out_map(c_im)], axis=1)
    return abar, bcat, ccat


SSM_BLOCKS = 4
SSM_BCH = BRANCH // SSM_BLOCKS
SSM_BST = N_STATE // SSM_BLOCKS


def _ssm_rows(T, perm):
    tm = ROW_TILE if perm else _row_tile(T)
    return tm, (perm or (lambda i: i))


def _lanes(x, n, width):
    return x[:, n * width:(n + 1) * width]


def _ssm_specs(T, perm, ucol0=None):
    tm, rows = _ssm_rows(T, perm)
    chan = pl.BlockSpec((tm, BRANCH), lambda i: (rows(i), 0 if ucol0 is None else ucol0 // BRANCH))
    state = pl.BlockSpec((tm, 2 * N_STATE), lambda i: (i, 0))
    bspec = pl.BlockSpec((2 * SSM_BLOCKS, SSM_BCH, SSM_BST), lambda i: (0, 0, 0))
    cspec = pl.BlockSpec((2 * SSM_BLOCKS, SSM_BST, SSM_BCH), lambda i: (0, 0, 0))
    return T // tm, chan, state, bspec, cspec


def ssm_in(u, bcat, *, ucol0, perm, name):
    T = u.shape[0]
    steps, chan, state, bspec, _ = _ssm_specs(T, perm, ucol0)

    def body(u_ref, b_ref, o_ref):
        uv = u_ref[...].astype(MM_DTYPE)
        for n in range(2 * SSM_BLOCKS):
            o_ref[:, n * SSM_BST:(n + 1) * SSM_BST] = _dot(_lanes(uv, n % SSM_BLOCKS, SSM_BCH), b_ref[n].astype(MM_DTYPE))

    return pl.pallas_call(body, out_shape=jax.ShapeDtypeStruct((T, 2 * N_STATE), F32), grid=(steps,),
                          in_specs=[chan, bspec], out_specs=state, compiler_params=_params("parallel"), name=name)(u, bcat)


def ssm_out(s, ccat, *, perm, name):
    T = s.shape[0]
    steps, chan, state, _, cspec = _ssm_specs(T, perm)

    def body(s_ref, c_ref, o_ref):
        sv = s_ref[...].astype(MM_DTYPE)
        o_ref[...] = jnp.concatenate(
            [_dot(_lanes(sv, j, SSM_BST), c_ref[j].astype(MM_DTYPE))
             + _dot(_lanes(sv, SSM_BLOCKS + j, SSM_BST), c_ref[SSM_BLOCKS + j].astype(MM_DTYPE))
             for j in range(SSM_BLOCKS)], axis=1)

    return pl.pallas_call(body, out_shape=jax.ShapeDtypeStruct((T, BRANCH), F32), grid=(steps,),
                          in_specs=[state, cspec], out_specs=chan, compiler_params=_params("parallel"), name=name)(s, ccat)


def ssm_out_dx(dy, ccat, *, perm, name):
    T = dy.shape[0]
    steps, chan, state, _, cspec = _ssm_specs(T, perm)

    def body(d_ref, c_ref, o_ref):
        dv = d_ref[...].astype(MM_DTYPE)
        for n in range(2 * SSM_BLOCKS):
            o_ref[:, n * SSM_BST:(n + 1) * SSM_BST] = _dot_nt(_lanes(dv, n % SSM_BLOCKS, SSM_BCH), c_ref[n].astype(MM_DTYPE))

    return pl.pallas_call(body, out_shape=jax.ShapeDtypeStruct((T, 2 * N_STATE), F32), grid=(steps,),
                          in_specs=[chan, cspec], out_specs=state, compiler_params=_params("parallel"), name=name)(dy, ccat)


def ssm_in_dx(lam, bcat, *, perm, name):
    T = lam.shape[0]
    steps, chan, state, bspec, _ = _ssm_specs(T, perm)

    def body(l_ref, b_ref, o_ref):
        lv = l_ref[...].astype(MM_DTYPE)
        o_ref[...] = jnp.concatenate(
            [_dot_nt(_lanes(lv, j, SSM_BST), b_ref[j].astype(MM_DTYPE))
             + _dot_nt(_lanes(lv, SSM_BLOCKS + j, SSM_BST), b_ref[SSM_BLOCKS + j].astype(MM_DTYPE))
             for j in range(SSM_BLOCKS)], axis=1)

    return pl.pallas_call(body, out_shape=jax.ShapeDtypeStruct((T, BRANCH), F32), grid=(steps,),
                          in_specs=[state, bspec], out_specs=chan, compiler_params=_params("parallel"), name=name)(lam, bcat)


def _ssm_dw(chan_arr, state_arr, chan_spec, state_spec, out_block, steps, chan_first, name):
    def body(c_ref, s_ref, o_ref):
        @pl.when(pl.program_id(0) == 0)
        def _():
            o_ref[...] = jnp.zeros(o_ref.shape, F32)

        cv, sv = c_ref[...].astype(MM_DTYPE), s_ref[...].astype(MM_DTYPE)
        for n in range(2 * SSM_BLOCKS):
            c, s = _lanes(cv, n % SSM_BLOCKS, SSM_BCH), _lanes(sv, n, SSM_BST)
            o_ref[n] += _dot_tn(c, s) if chan_first else _dot_tn(s, c)

    shape = (2 * SSM_BLOCKS,) + out_block
    return pl.pallas_call(body, out_shape=jax.ShapeDtypeStruct(shape, F32), grid=(steps,),
                          in_specs=[chan_spec, state_spec], out_specs=pl.BlockSpec(shape, lambda k: (0, 0, 0)),
                          compiler_params=_params("arbitrary"), name=name)(chan_arr, state_arr)


def ssm_in_dw(u, lam, *, ucol0, perm, name):
    steps, chan, state, _, _ = _ssm_specs(u.shape[0], perm, ucol0)
    return _ssm_dw(u, lam, chan, state, (SSM_BCH, SSM_BST), steps, True, name)


def ssm_out_dw(s, dy, *, perm, name):
    steps, chan, state, _, _ = _ssm_specs(s.shape[0], perm)
    return _ssm_dw(dy, s, chan, state, (SSM_BST, SSM_BCH), steps, False, name)


def _time_block(T, reverse):
    nt = T // SCAN_ROWS
    tix = (lambda i: nt - 1 - i) if reverse else (lambda i: i)
    return nt, pl.BlockSpec((SCAN_ROWS, 2 * N_STATE), lambda i: (tix(i), 0))


def ssm_scan(bu, abar, *, reverse):
    T = bu.shape[0]
    nt, tspec = _time_block(T, reverse)

    def body(b_ref, a_ref, s_ref, c_ref):
        @pl.when(pl.program_id(0) == 0)
        def _():
            c_ref[...] = jnp.zeros(c_ref.shape, F32)

        for c0 in range(0, N_STATE, SCAN_COLS):
            re, im = pl.ds(c0, SCAN_COLS), pl.ds(N_STATE + c0, SCAN_COLS)
            ar, ai = a_ref[:, re], a_ref[:, im]

            def step(n, carry, re=re, im=im, ar=ar, ai=ai):
                sr, si = carry
                t = (SCAN_ROWS - 1 - n) if reverse else n
                nr = ar * sr - ai * si + b_ref[pl.ds(t, 1), re]
                ni = ar * si + ai * sr + b_ref[pl.ds(t, 1), im]
                s_ref[pl.ds(t, 1), re] = nr
                s_ref[pl.ds(t, 1), im] = ni
                return nr, ni

            sr, si = lax.fori_loop(0, SCAN_ROWS, step, (c_ref[:, re], c_ref[:, im]))
            c_ref[:, re] = sr
            c_ref[:, im] = si

    return pl.pallas_call(
        body, out_shape=jax.ShapeDtypeStruct((T, 2 * N_STATE), F32), grid=(nt,),
        in_specs=[tspec, pl.BlockSpec((1, 2 * N_STATE), lambda i: (0, 0))], out_specs=tspec,
        scratch_shapes=[pltpu.VMEM((1, 2 * N_STATE), F32)],
        compiler_params=_params("arbitrary"), name="ssm_scan_rev" if reverse else "ssm_scan_fwd")(bu, abar)


def ssm_scan_bwd(g, s, abar, *, reverse):
    T = g.shape[0]
    nt, tspec = _time_block(T, not reverse)
    back = not reverse

    def body(g_ref, s_ref, a_ref, l_ref, da_ref, c_ref):
        @pl.when(pl.program_id(0) == 0)
        def _():
            c_ref[...] = jnp.zeros(c_ref.shape, F32)
            da_ref[...] = jnp.zeros(da_ref.shape, F32)

        for c0 in range(0, N_STATE, SCAN_COLS):
            re, im = pl.ds(c0, SCAN_COLS), pl.ds(N_STATE + c0, SCAN_COLS)
            ar, ai = a_ref[:, re], a_ref[:, im]

            def step(n, carry, re=re, im=im, ar=ar, ai=ai):
                lr, li, dr, di = carry
                t = (SCAN_ROWS - 1 - n) if back else n
                sr, si = s_ref[pl.ds(t, 1), re], s_ref[pl.ds(t, 1), im]
                dr = dr + sr * lr + si * li
                di = di + sr * li - si * lr
                nr = g_ref[pl.ds(t, 1), re] + ar * lr + ai * li
                ni = g_ref[pl.ds(t, 1), im] + ar * li - ai * lr
                l_ref[pl.ds(t, 1), re] = nr
                l_ref[pl.ds(t, 1), im] = ni
                return nr, ni, dr, di

            zero = jnp.zeros((1, SCAN_COLS), F32)
            lr, li, dr, di = lax.fori_loop(0, SCAN_ROWS, step, (c_ref[:, re], c_ref[:, im], zero, zero))
            c_ref[:, re] = lr
            c_ref[:, im] = li
            da_ref[:, re] += jnp.broadcast_to(dr, (8, SCAN_COLS))
            da_ref[:, im] += jnp.broadcast_to(di, (8, SCAN_COLS))

    return pl.pallas_call(
        body, out_shape=[jax.ShapeDtypeStruct((T, 2 * N_STATE), F32), jax.ShapeDtypeStruct((8, 2 * N_STATE), F32)],
        grid=(nt,), in_specs=[tspec, tspec, pl.BlockSpec((1, 2 * N_STATE), lambda i: (0, 0))],
        out_specs=[tspec, pl.BlockSpec((8, 2 * N_STATE), lambda i: (0, 0))],
        scratch_shapes=[pltpu.VMEM((1, 2 * N_STATE), F32)],
        compiler_params=_params("arbitrary"),
        name="ssm_scan_bwd_rev" if reverse else "ssm_scan_bwd_fwd")(g, s, abar)


def _gelu(x):
    return 0.5 * x * (1.0 + jnp.tanh(0.7978845608028654 * (x + 0.044715 * x * x * x)))


def _gelu_grad(x):
    t = jnp.tanh(0.7978845608028654 * (x + 0.044715 * x * x * x))
    return 0.5 * (1.0 + t) + 0.5 * x * (1.0 - t * t) * 0.7978845608028654 * (1.0 + 3 * 0.044715 * x * x)


def _silu(z):
    return z * _sigmoid(z)


def _silu_grad(z):
    s = _sigmoid(z)
    return s * (1.0 + z * (1.0 - s))


def ssm_fwd(proj, ops, dsk, glu_w, *, L):
    T = proj.shape[0]
    abar, bcat, ccat = ops
    nb, nlat = T // ROW_TILE, L // ROW_TILE
    to_f = lambda i: (i + nlat) % nb
    states, ys = [], []
    for d in (0, 1):
        perm = to_f if d == 0 else None
        bu = ssm_in(proj, bcat[d], ucol0=COL["ssm_u"], perm=perm, name=f"ssm_in{d}")
        s = ssm_scan(bu, abar[d], reverse=(d == 1))
        states.append(s)
        ys.append(ssm_out(s, ccat[d], perm=perm, name=f"ssm_out{d}"))

    def pre(u, yf, yr, dsk):
        y = dsk * u + yf + yr
        return y, _gelu(y)

    ypre, gy = _ew(pre, [(proj, "row", COL["ssm_u"] // BRANCH, BRANCH), (ys[0], "row", 0, BRANCH),
                         (ys[1], "row", 0, BRANCH), (dsk, "bcast", 0, BRANCH)],
                   [(BRANCH, F32), (BRANCH, MM_DTYPE)], [], T=T, L=L, name="ssm_pre")
    gg = mm_nn(gy, glu_w, out_dtype=F32, name="ssm_glu")

    def post(ga, gb):
        return (ga * _sigmoid(gb),)

    o, = _ew(post, [(gg, "row", 0, BRANCH), (gg, "row", 1, BRANCH)], [(BRANCH, F32)], [], T=T, L=L, name="ssm_post")
    return o, dict(states=states, ypre=ypre, gy=gy, gg=gg)


def ssm_bwd(do, proj, ops, dsk, glu_w, saved, *, L):
    T = proj.shape[0]
    abar, bcat, ccat = ops
    nb, nlat = T // ROW_TILE, L // ROW_TILE
    to_f = lambda i: (i + nlat) % nb
    gg, gy, ypre = saved["gg"], saved["gy"], saved["ypre"]

    def post_bwd(do, ga, gb):
        sg = _sigmoid(gb)
        return (jnp.concatenate([do * sg, do * ga * sg * (1.0 - sg)], axis=1),)

    dgg, = _ew(post_bwd, [(do, "row", 0, BRANCH), (gg, "row", 0, BRANCH), (gg, "row", 1, BRANCH)],
               [(2 * BRANCH, MM_DTYPE)], [], T=T, L=L, name="ssm_post_bwd")
    dgy = mm_nt(dgg, glu_w, out_dtype=F32, name="ssm_glu_dx")
    dglu = mm_tn(gy, dgg, out_dtype=F32, name="ssm_glu_dw")

    def pre_bwd(dgy, y, u, dsk):
        dy = dgy * _gelu_grad(y)
        return dy, dy * dsk, dy * u

    dy, du_skip, dd = _ew(pre_bwd, [(dgy, "row", 0, BRANCH), (ypre, "row", 0, BRANCH),
                                    (proj, "row", COL["ssm_u"] // BRANCH, BRANCH), (dsk, "bcast", 0, BRANCH)],
                          [(BRANCH, MM_DTYPE), (BRANCH, F32)], [BRANCH], T=T, L=L, name="ssm_pre_bwd")
    du = du_skip
    dabar, dbcat, dccat = [], [], []
    for d in (0, 1):
        perm = to_f if d == 0 else None
        s = saved["states"][d]
        g = ssm_out_dx(dy, ccat[d], perm=perm, name=f"ssm_out{d}_dx")
        lam, da = ssm_scan_bwd(g, s, abar[d], reverse=(d == 1))
        dabar.append(da[0:1])
        dccat.append(ssm_out_dw(s, dy, perm=perm, name=f"ssm_out{d}_dw"))
        du = du + ssm_in_dx(lam, bcat[d], perm=perm, name=f"ssm_in{d}_dx")
        dbcat.append(ssm_in_dw(proj, lam, ucol0=COL["ssm_u"], perm=perm, name=f"ssm_in{d}_dw"))
    d_ops = (jnp.stack(dabar), jnp.stack(dbcat), jnp.stack(dccat))
    return du, d_ops, dd[0, 0] + dd[1, 0], dglu


Z_COLS = tuple(COL[n] // BRANCH for n in ("na_z", "pool_z", "conv_z", "ssm_z"))


def gate_act(o, proj, *, L):
    T = o.shape[0]

    def fn(o, z0, z1, z2, z3):
        return (o * _silu(jnp.concatenate([z0, z1, z2, z3], axis=1)),)

    a, = _ew(fn, [(o, "row", 0, D_MODEL)] + [(proj, "row", c, BRANCH) for c in Z_COLS],
             [(D_MODEL, MM_DTYPE)], [], T=T, L=L, name="gate_act")
    return a


def gate_act_bwd(da, o, proj, *, L):
    T = o.shape[0]

    def fn(da, o, z0, z1, z2, z3):
        z = jnp.concatenate([z0, z1, z2, z3], axis=1)
        return da * _silu(z), da * o * _silu_grad(z)

    return _ew(fn, [(da, "row", 0, D_MODEL), (o, "row", 0, D_MODEL)] + [(proj, "row", c, BRANCH) for c in Z_COLS],
               [(D_MODEL, F32), (D_MODEL, MM_DTYPE)], [], T=T, L=L, name="gate_act_bwd")


MERGE_TN = 512


def merge_fwd(a, w_br, proj, b_gate):
    T = a.shape[0]
    tm, tn = _row_tile(T), MERGE_TN
    nn = D_MODEL // tn
    lb0 = COL["merge"] // tn

    def body(a_ref, w_ref, l_ref, b_ref, m_ref, br_ref, acc_ref):
        i = pl.program_id(2)
        br = _dot(a_ref[...].astype(MM_DTYPE), w_ref[...].astype(MM_DTYPE))
        br_ref[...] = br
        term = _sigmoid(l_ref[...] + b_ref[...]) * br

        @pl.when(i == 0)
        def _():
            acc_ref[...] = term

        @pl.when(i > 0)
        def _():
            acc_ref[...] += term

        @pl.when(i == 3)
        def _():
            m_ref[...] = acc_ref[...].astype(m_ref.dtype)

    return pl.pallas_call(
        body, out_shape=[jax.ShapeDtypeStruct((T, D_MODEL), MM_DTYPE), jax.ShapeDtypeStruct((T, 4 * D_MODEL), F32)],
        grid=(T // tm, nn, 4),
        in_specs=[pl.BlockSpec((tm, BRANCH), lambda m, n, i: (m, i)),
                  pl.BlockSpec((BRANCH, tn), lambda m, n, i: (i, n)),
                  pl.BlockSpec((tm, tn), lambda m, n, i: (m, lb0 + i * nn + n)),
                  pl.BlockSpec((1, tn), lambda m, n, i: (0, i * nn + n))],
        out_specs=[pl.BlockSpec((tm, tn), lambda m, n, i: (m, n)), pl.BlockSpec((tm, tn), lambda m, n, i: (m, i * nn + n))],
        scratch_shapes=[pltpu.VMEM((tm, tn), F32)],
        compiler_params=_params("parallel", "parallel", "arbitrary"), name="merge_fwd")(a, w_br, proj, b_gate)


def merge_bwd(dmerged, br, proj, b_gate):
    T = dmerged.shape[0]
    tb = ROW_TILE
    lb0 = COL["merge"] // D_MODEL

    def body(dm_ref, br_ref, l_ref, b_ref, dbr_ref, dl_ref, db_ref):
        dm = dm_ref[...]
        gates = _sigmoid(l_ref[...] + b_ref[...])
        dbr_ref[...] = (dm * gates).astype(dbr_ref.dtype)
        dl = dm * br_ref[...] * gates * (1.0 - gates)
        dl_ref[...] = dl.astype(dl_ref.dtype)
        s = jnp.broadcast_to(jnp.sum(dl, axis=0, keepdims=True), db_ref.shape)

        @pl.when(pl.program_id(1) == 0)
        def _():
            db_ref[...] = s

        @pl.when(pl.program_id(1) > 0)
        def _():
            db_ref[...] += s

    wide = pl.BlockSpec((tb, D_MODEL), lambda b, i: (i, b))
    return pl.pallas_call(
        body, out_shape=[jax.ShapeDtypeStruct((T, 4 * D_MODEL), MM_DTYPE)] * 2 + [jax.ShapeDtypeStruct((8, 4 * D_MODEL), F32)],
        grid=(4, T // tb),
        in_specs=[pl.BlockSpec((tb, D_MODEL), lambda b, i: (i, 0)), wide,
                  pl.BlockSpec((tb, D_MODEL), lambda b, i: (i, lb0 + b)), pl.BlockSpec((1, D_MODEL), lambda b, i: (0, b))],
        out_specs=[wide, wide, pl.BlockSpec((8, D_MODEL), lambda b, i: (0, b))],
        compiler_params=_params("parallel", "arbitrary"), name="merge_bwd")(dmerged, br, proj, b_gate)


def branch_dx(dbr, w_br):
    T = dbr.shape[0]
    tm, tk = _row_tile(T), 1024
    nk = D_MODEL // tk
    return _matmul(dbr, w_br, pl.BlockSpec((tm, tk), lambda m, i, k: (m, i * nk + k)),
                   pl.BlockSpec((BRANCH, tk), lambda m, i, k: (i, k)),
                   pl.BlockSpec((tm, BRANCH), lambda m, i, k: (m, i)),
                   jax.ShapeDtypeStruct((T, D_MODEL), F32), (T // tm, 4, nk), tb=True, name="branch_dx")


def branch_dw(a, dbr):
    T = a.shape[0]
    tk, tn = T, 1024
    nn = D_MODEL // tn
    return _matmul(a, dbr, pl.BlockSpec((tk, BRANCH), lambda i, n, k: (k, i)),
                   pl.BlockSpec((tk, tn), lambda i, n, k: (k, i * nn + n)),
                   pl.BlockSpec((BRANCH, tn), lambda i, n, k: (i, n)),
                   jax.ShapeDtypeStruct((D_MODEL, D_MODEL), F32), (4, nn, T // tk), ta=True, name="branch_dw")


def proj_fwd(h, w_in, carried=None):
    T = h.shape[0]
    tm, tn = _row_tile(T), 1792
    per = W_IN_SHARD // tn
    return _matmul(h, w_in, pl.BlockSpec((tm, D_MODEL), lambda i, j, k: (i, 0)),
                   pl.BlockSpec((None, D_MODEL, tn), lambda i, j, k: (j // per, 0, j % per)),
                   pl.BlockSpec((tm, tn), lambda i, j, k: (i, j)),
                   jax.ShapeDtypeStruct((T, IN_TOTAL), F32), (T // tm, IN_TOTAL // tn, 1),
                   name="proj_fwd" if carried is None else "proj_fwd_gather", carried=carried)


def proj_dx(dproj, w_in, carried=None):
    T = dproj.shape[0]
    tm, tk = _row_tile(T), 1792
    per = W_IN_SHARD // tk
    return _matmul(dproj, w_in, pl.BlockSpec((tm, tk), lambda i, j, k: (i, k)),
                   pl.BlockSpec((None, D_MODEL, tk), lambda i, j, k: (k // per, 0, k % per)),
                   pl.BlockSpec((tm, D_MODEL), lambda i, j, k: (i, 0)),
                   jax.ShapeDtypeStruct((T, D_MODEL), F32), (T // tm, 1, IN_TOTAL // tk), tb=True,
                   name="proj_dx" if carried is None else "proj_dx_exchange", carried=carried)


def proj_dw(h, dproj):
    T = h.shape[0]
    tk, tm, tn = T, 512, 512
    per = W_IN_SHARD // tn
    return _matmul(h, dproj, pl.BlockSpec((tk, tm), lambda i, j, k: (k, i)),
                   pl.BlockSpec((tk, tn), lambda i, j, k: (k, j)),
                   pl.BlockSpec((None, tm, tn), lambda i, j, k: (j // per, i, j % per)),
                   jax.ShapeDtypeStruct((N_SHARDS, D_MODEL, W_IN_SHARD), F32),
                   (D_MODEL // tm, IN_TOTAL // tn, T // tk), ta=True, name="proj_dw")


def layer_fwd(X, P, *, L, carried=None):
    h = prenorm_fwd(X, P["g_pre"], P["scale"], P["shift"], L=L)
    proj, extras = proj_fwd(h, P["w_in"], carried) if carried is not None else (proj_fwd(h, P["w_in"]), None)
    o_att, lse = attn_fwd(proj, P["ptiles"], L=L)
    o_att, lse_c = cattn_fwd(proj, o_att, L=L)
    o_pool, pooled = pool_fwd(proj, P["pool_w"], P["pool_scale"], L=L)
    o_conv = conv_fwd(proj, P["conv_w"], L=L)
    o_ssm, ssm_saved = ssm_fwd(proj, P["ssm_ops"], P["ssm_d"], P["glu_w"], L=L)
    o = jnp.concatenate([o_att, o_pool, o_conv, o_ssm], axis=1)
    a = gate_act(o, proj, L=L)
    merged, br = merge_fwd(a, P["w_br"], proj, P["b_gate"])
    y = mm_nn(merged, P["w_o"], out_dtype=F32, name="out_proj")
    Xn = postnorm_fwd(X, y, P["g_post"], P["gate"], L=L)
    saved = dict(X=X, h=h, proj=proj, lse=lse, lse_c=lse_c, pooled=pooled, ssm=ssm_saved, o=o, a=a, merged=merged,
                 br=br, y=y)
    return Xn, saved, extras


def layer_bwd(dXn, P, S, bias_vjp, ops_vjp, *, L, carried=None):
    proj = S["proj"]
    dy, cs_post = postnorm_bwd(dXn, S["y"], P["g_post"], P["gate"], L=L)
    dmerged = mm_nt(dy, P["w_o"], out_dtype=F32, name="out_proj_dx")
    d_w_o = mm_tn(S["merged"], dy, out_dtype=F32, name="out_proj_dw", tm=512, tn=1024)
    dbr, dlogit, d_bgate = merge_bwd(dmerged, S["br"], proj, P["b_gate"])
    da = branch_dx(dbr, P["w_br"])
    d_w_br = branch_dw(S["a"], dbr)
    do, dz = gate_act_bwd(da, S["o"], proj, L=L)
    dq, dk, dv, dptiles = attn_bwd(proj, P["ptiles"], S["o"], do, S["lse"], L=L)
    dq, dk, dv = cattn_bwd(proj, S["o"], do, S["lse_c"], dq, dk, dv, L=L)
    d_rpb, = bias_vjp(dptiles)
    dpool_u, d_pool_w, d_pool_scale = pool_bwd(do[:, BRANCH:2 * BRANCH], S["pooled"], P["pool_w"], P["pool_scale"], L=L)
    dcx, dcb, dcc, d_conv_w = conv_bwd(do[:, 2 * BRANCH:3 * BRANCH], proj, P["conv_w"], L=L)
    dssm_u, d_ops, d_ssm_d, d_glu = ssm_bwd(do[:, 3 * BRANCH:], proj, P["ssm_ops"], P["ssm_d"], P["glu_w"], S["ssm"], L=L)
    d_ssm = ops_vjp(d_ops)
    z = lambda i: dz[:, i * BRANCH:(i + 1) * BRANCH]
    cast = lambda t: t.astype(MM_DTYPE)
    dproj = jnp.concatenate([cast(dq), cast(dk), cast(dv), z(0), dpool_u, z(1),
                             dcx, dcb, dcc, z(2), cast(dssm_u), z(3), dlogit], axis=1)
    dh, extras = proj_dx(dproj, P["w_in"], carried) if carried is not None else (proj_dx(dproj, P["w_in"]), None)
    d_w_in = proj_dw(S["h"], dproj)
    dX, cs_h, cs_hx = prenorm_bwd(dh, S["X"], P["g_pre"], P["scale"], dXn, L=L)
    g_pre, g_post = P["g_pre"], P["g_post"]
    d_shift = cs_h
    d_scale = cs_hx * g_pre
    d_gate = cs_post * g_post
    d_g_pre = jnp.sum(cs_hx * (1.0 + P["scale"]), axis=0)[0]
    d_g_post = jnp.sum(cs_post * P["gate"], axis=0)[0]
    grads = dict(w_in=d_w_in, w_br=d_w_br, w_o=d_w_o, glu_w=d_glu, conv_w=d_conv_w[0:3], pool_w=d_pool_w,
                 pool_scale=d_pool_scale[0], b_gate=d_bgate[0], na_rpb=d_rpb, ssm=d_ssm, ssm_d=d_ssm_d,
                 g_pre=d_g_pre, g_post=d_g_post,
                 mod=jnp.concatenate([d_shift, d_scale, d_gate], axis=-1)[:, 0])
    return dX, grads, extras


BIG_NAMES = ("w_in", "glu_w", "w_br", "w_o")


def gathered_weights(g):
    w_in, glu, w_br, w_o = g
    return dict(w_in=w_in, glu_w=glu.transpose(1, 0, 2).reshape(BRANCH, 2 * BRANCH),
                w_br=w_br.reshape(D_MODEL, D_MODEL), w_o=w_o.reshape(D_MODEL, D_MODEL))


def shard_major_grads(g):
    return [g["w_in"], g["glu_w"].reshape(BRANCH, N_SHARDS, -1).transpose(1, 0, 2),
            g["w_br"].reshape(N_SHARDS, BRANCH, D_MODEL), g["w_o"].reshape(N_SHARDS, BRANCH, D_MODEL)]


def local_step(x, ctx, target, mods, layers, *, shard_halves=None, reduce=False):
    L = x.shape[0]
    nl = len(layers)
    X = jnp.concatenate([x, ctx], axis=0)
    saved, Ps, vjps = [], [], []
    arrived = None
    for i, lay in enumerate(layers):
        P = dict(lay)
        if arrived is not None:
            P.update(gathered_weights(arrived))
        m = mods[i][:, None, :]
        P["shift"], P["scale"], P["gate"] = m[..., :D_MODEL], m[..., D_MODEL:2 * D_MODEL], m[..., 2 * D_MODEL:]
        P["ptiles"], bias_vjp = jax.vjp(bias_pair_tiles, lay["na_rpb"])
        P["ssm_ops"], ops_vjp = jax.vjp(ssm_operators, *lay["ssm_params"])
        ahead = shard_halves is not None and i + 1 < nl
        X, S, arrived = layer_fwd(X, P, L=L, carried=gather_plan(shard_halves[i + 1], shard_major=True) if ahead else None)
        saved.append(S)
        Ps.append(P)
        vjps.append((bias_vjp, ops_vjp))
    loss, dX = loss_and_grad(X, target, L=L)
    grads, reduced = [None] * nl, [None] * nl
    finish = lambda parts: [pair_finish(a, name=f"grad_finish_{n}") for n, a in zip(BIG_NAMES, parts)]
    pending = None
    for i in reversed(range(nl)):
        dX, grads[i], arrived = layer_bwd(dX, Ps[i], saved[i], *vjps[i], L=L,
                                          carried=exchange_plan(pending) if pending is not None else None)
        if pending is not None:
            reduced[i + 1] = finish(arrived)
        if reduce:
            pending = [pair_reduce(a, out_dtype=MM_DTYPE, name=f"grad_pair_{n}")
                       for n, a in zip(BIG_NAMES, shard_major_grads(grads[i]))]
    if reduce:
        reduced[0] = finish(chip_exchange(pending, name="grad_exchange"))
    return loss, dX[:L], grads, reduced


MESH_ID = pl.DeviceIdType.MESH
HBM_SPEC = pl.BlockSpec(memory_space=pltpu.HBM)


def _place():
    return lax.axis_index("x"), lax.axis_index("y"), lax.axis_index("c")


def _other_chips(x, y):
    return [(1 - x, y), (x, 1 - y), (1 - x, 1 - y)]


def _remote(src, dst, send_sem, recv_sem, to):
    return pltpu.make_async_remote_copy(src_ref=src, dst_ref=dst, send_sem=send_sem, recv_sem=recv_sem,
                                        device_id=to, device_id_type=MESH_ID)


DMA_CHUNK_BYTES = 1 << 20


def _row_pieces(rows, row_bytes):
    n = max(1, min(64, rows * row_bytes // DMA_CHUNK_BYTES))
    while n > 1 and (rows % n or (rows // n) % 16):
        n -= 1
    return [(r * (rows // n), rows // n) for r in range(n)]


def _row_bytes(ref):
    return math.prod(ref.shape[1:]) * jnp.dtype(ref.dtype).itemsize


def _start_in_pieces(make, src, dst):
    for r0, nr in _row_pieces(src.shape[0], _row_bytes(src)):
        make(src.at[pl.ds(r0, nr)], dst.at[pl.ds(r0, nr)]).start()


class Carried:
    def __init__(self, ins, out_shapes, sem_shapes, start, mid, finish):
        self.ins, self.out_shapes, self.sem_shapes = list(ins), list(out_shapes), list(sem_shapes)
        self.start, self.mid, self.finish = start, mid, finish


def run_carried(plan, name):
    n_in, n_out = len(plan.ins), len(plan.out_shapes)

    def body(*refs):
        parts = refs[:n_in], refs[n_in:n_in + n_out], refs[n_in + n_out:]
        plan.start(*parts)
        plan.mid(*parts)
        plan.finish(*parts)

    return pl.pallas_call(body, out_shape=plan.out_shapes, in_specs=[HBM_SPEC] * n_in, out_specs=[HBM_SPEC] * n_out,
                          scratch_shapes=plan.sem_shapes, name=name)(*plan.ins)


def gather_plan(blocks, *, shard_major=False):
    K = len(blocks)

    def tools(ins, outs, sems):
        send_sems, recv_sems, local_sems = sems
        x, y, c = _place()

        def slot(k, block):
            px, py, pc = block
            if shard_major:
                h = ins[k].shape[0]
                return outs[k].at[2 * px + py, pl.ds(pl.multiple_of(pc * h, 16), h)]
            return outs[k].at[4 * px + 2 * py + pc]

        def copy(k, j, to):
            return lambda s, d: _remote(s, d, send_sems.at[k, j], recv_sems.at[k, j], to)

        return (x, y, c), (x, y, 1 - c), _other_chips(x, y), c, slot, copy, local_sems

    def start(ins, outs, sems):
        me, sibling, chips, c, slot, copy, local_sems = tools(ins, outs, sems)
        for k in range(K):
            _start_in_pieces(lambda s, d, k=k: pltpu.make_async_copy(s, d, local_sems.at[k]), ins[k], slot(k, me))
            _start_in_pieces(copy(k, 0, sibling), ins[k], slot(k, me))
            for j, chip in enumerate(chips):
                _start_in_pieces(copy(k, 1 + j, (*chip, c)), ins[k], slot(k, me))

    def mid(ins, outs, sems):
        me, sibling, chips, c, slot, copy, _ = tools(ins, outs, sems)
        for j, chip in enumerate(chips):
            for k in range(K):
                got = slot(k, (*chip, c))
                copy(k, 1 + j, me)(got, got).wait_recv()
                _start_in_pieces(copy(k, 4 + j, sibling), got, got)

    def finish(ins, outs, sems):
        me, sibling, chips, c, slot, copy, local_sems = tools(ins, outs, sems)
        for k in range(K):
            sib = slot(k, sibling)
            copy(k, 0, me)(sib, sib).wait_recv()
            for j, chip in enumerate(chips):
                got = slot(k, (*chip, 1 - c))
                copy(k, 4 + j, me)(got, got).wait_recv()
        for k in range(K):
            own = slot(k, me)
            for j in range(4):
                copy(k, j, me)(ins[k], own).wait_send()
            for j, chip in enumerate(chips):
                got = slot(k, (*chip, c))
                copy(k, 4 + j, me)(got, got).wait_send()
            pltpu.make_async_copy(ins[k], own, local_sems.at[k]).wait()

    def out_shape(b):
        if shard_major:
            return jax.ShapeDtypeStruct((N_SHARDS, 2 * b.shape[0]) + b.shape[1:], b.dtype)
        return jax.ShapeDtypeStruct((8,) + b.shape, b.dtype)

    sems = [pltpu.SemaphoreType.DMA((K, 7)), pltpu.SemaphoreType.DMA((K, 7)), pltpu.SemaphoreType.DMA((K,))]
    return Carried(blocks, [out_shape(b) for b in blocks], sems, start, mid, finish)


def all_gather8(blocks, name, *, shard_major=False):
    return run_carried(gather_plan(blocks, shard_major=shard_major), name)


def exchange_plan(parts):
    K = len(parts)

    def copies(ins, outs, sems):
        send_sems, recv_sems, local_sems = sems
        x, y, c = _place()
        p = 2 * x + y
        makes = []
        for k in range(K):
            for j, (cx, cy) in enumerate(_other_chips(x, y)):
                make = lambda s, d, k=k, j=j, to=(cx, cy, c): _remote(s, d, send_sems.at[k, j], recv_sems.at[k, j], to)
                makes.append((k, 2 * cx + cy, make))
        return p, makes, local_sems

    def start(ins, outs, sems):
        p, makes, local_sems = copies(ins, outs, sems)
        for k in range(K):
            _start_in_pieces(lambda s, d, k=k: pltpu.make_async_copy(s, d, local_sems.at[k]), ins[k].at[p], outs[k].at[p])
        for k, q, make in makes:
            _start_in_pieces(make, ins[k].at[q], outs[k].at[p])

    def mid(ins, outs, sems):
        pass

    def finish(ins, outs, sems):
        p, makes, local_sems = copies(ins, outs, sems)
        waits = [make(ins[k].at[q], outs[k].at[q]) for k, q, make in makes]
        for cp in waits:
            cp.wait_recv()
        for cp in waits:
            cp.wait_send()
        for k in range(K):
            pltpu.make_async_copy(ins[k].at[p], outs[k].at[p], local_sems.at[k]).wait()

    sems = [pltpu.SemaphoreType.DMA((K, 3)), pltpu.SemaphoreType.DMA((K, 3)), pltpu.SemaphoreType.DMA((K,))]
    return Carried(parts, [jax.ShapeDtypeStruct(a.shape, a.dtype) for a in parts], sems, start, mid, finish)


def chip_exchange(parts, name):
    return run_carried(exchange_plan(parts), name)


PAIR_TILE_BYTES = 2 << 20


def _pair_rows(h, n):
    return _pick(h, tuple(t for t in (512, 256, 128, 64, 32, 16) if t * n * 4 <= PAIR_TILE_BYTES))


def _core_index():
    return jnp.reshape(lax.axis_index("c"), (1,)).astype(jnp.int32)


def pair_reduce(g, *, out_dtype, name):
    S, R, n = g.shape
    h = R // 2
    tr = _pair_rows(h, n)
    nt = h // tr

    def body(c_ref, keep_ref, give_ref, o_ref, recv_ref, send_sems, recv_sems):
        x, y, c = _place()
        slot = (pl.program_id(0) * nt + pl.program_id(1)) % 2
        cp = _remote(give_ref, recv_ref.at[slot], send_sems.at[slot], recv_sems.at[slot], (x, y, 1 - c))
        cp.start()
        cp.wait_recv()
        o_ref[...] = (keep_ref[...] + recv_ref[slot]).astype(o_ref.dtype)
        cp.wait_send()

    g2 = g.reshape(S * R, n)
    grid_spec = pltpu.PrefetchScalarGridSpec(
        num_scalar_prefetch=1, grid=(S, nt),
        in_specs=[pl.BlockSpec((tr, n), lambda q, i, c: ((2 * q + c[0]) * nt + i, 0)),
                  pl.BlockSpec((tr, n), lambda q, i, c: ((2 * q + 1 - c[0]) * nt + i, 0))],
        out_specs=pl.BlockSpec((tr, n), lambda q, i, c: (q * nt + i, 0)),
        scratch_shapes=[pltpu.VMEM((2, tr, n), F32), pltpu.SemaphoreType.DMA((2,)), pltpu.SemaphoreType.DMA((2,))])
    out = pl.pallas_call(body, out_shape=jax.ShapeDtypeStruct((S * h, n), out_dtype), grid_spec=grid_spec,
                         compiler_params=_params("arbitrary", "arbitrary"), name=name)(_core_index(), g2, g2)
    return out.reshape(S, h, n)


def pair_finish(parts, *, name):
    S, h, n = parts.shape
    tr = _pair_rows(h, n)
    nt = h // tr

    def body(c_ref, p_ref, o_ref, recv_ref, send_sem, recv_sem):
        x, y, c = _place()
        phase, i = pl.program_id(0), pl.program_id(1)
        rows = pl.ds(pl.multiple_of(i * tr, 16), tr)

        @pl.when(phase == 0)
        def _():
            acc = p_ref[0].astype(F32)
            for s in range(1, S):
                acc = acc + p_ref[s].astype(F32)
            o_ref[...] = acc
            cp = _remote(o_ref, recv_ref.at[rows], send_sem, recv_sem, (x, y, 1 - c))
            cp.start()
            cp.wait_send()

        @pl.when((phase == 1) & (i == 0))
        def _():
            _remote(recv_ref, recv_ref, send_sem, recv_sem, (x, y, 1 - c)).wait_recv()

        @pl.when(phase == 1)
        def _():
            o_ref[...] = recv_ref[rows]

    grid_spec = pltpu.PrefetchScalarGridSpec(
        num_scalar_prefetch=1, grid=(2, nt),
        in_specs=[pl.BlockSpec((S, tr, n), lambda ph, i, c: (0, jnp.where(ph == 0, i, nt - 1), 0))],
        out_specs=pl.BlockSpec((tr, n), lambda ph, i, c: (jnp.where(ph == 0, c[0], 1 - c[0]) * nt + i, 0)),
        scratch_shapes=[pltpu.VMEM((h, n), F32), pltpu.SemaphoreType.DMA(()), pltpu.SemaphoreType.DMA(())])
    return pl.pallas_call(body, out_shape=jax.ShapeDtypeStruct((2 * h, n), F32), grid_spec=grid_spec,
                          compiler_params=_params("arbitrary", "arbitrary"), name=name)(_core_index(), parts)


def _slab_rows(n):
    return max(8, min(256, (1 << 18) // n // 8 * 8))


def sum_slabs(a, *, name):
    S, h, n = a.shape
    tr = _pick(h, tuple(t for t in (256, 128, 64, 32, 16, 8) if t <= _slab_rows(n)))

    def body(a_ref, o_ref):
        acc = a_ref[0].astype(F32)
        for s in range(1, S):
            acc = acc + a_ref[s].astype(F32)
        o_ref[...] = acc

    return pl.pallas_call(body, out_shape=jax.ShapeDtypeStruct((h, n), F32), grid=(h // tr,),
                          in_specs=[pl.BlockSpec((S, tr, n), lambda r: (0, r, 0))],
                          out_specs=pl.BlockSpec((tr, n), lambda r: (r, 0)), compiler_params=_params("parallel"),
                          name=name)(a)


def _adam_math(w, g, m, v):
    m = ADAM_B1 * m + (1.0 - ADAM_B1) * g
    v = ADAM_B2 * v + (1.0 - ADAM_B2) * (g * g)
    m_hat = m / (1.0 - ADAM_B1 ** ADAM_STEP)
    v_hat = v / (1.0 - ADAM_B2 ** ADAM_STEP)
    delta = -ADAM_LR * (m_hat / (jnp.sqrt(v_hat) + ADAM_EPS) + ADAM_WD * w)
    return delta, m, v


def adamw(w, g, m, v, *, name):
    R, n = w.shape
    tr = _pick(R, tuple(t for t in (256, 128, 64, 32, 16, 8) if t <= _slab_rows(n)))
    spec = pl.BlockSpec((tr, n), lambda r: (r, 0))

    def body(w_ref, g_ref, m_ref, v_ref, d_ref, nm_ref, nv_ref):
        d, nm, nv = _adam_math(w_ref[...], g_ref[...], m_ref[...], v_ref[...])
        d_ref[...] = d
        nm_ref[...] = nm
        nv_ref[...] = nv

    return pl.pallas_call(body, out_shape=[jax.ShapeDtypeStruct(w.shape, F32)] * 3, grid=(R // tr,),
                          in_specs=[spec] * 4, out_specs=[spec] * 3, compiler_params=_params("parallel"),
                          name=name)(w, g, m, v)


MOD_ROWS = 16


def mod_fwd(cact_in, w_mod, b_mod):
    nl, _, cols = w_mod.shape
    tn = 512

    def body(c_ref, w_ref, b_ref, o_ref):
        o_ref[...] = _dot(_silu(c_ref[...]).astype(MM_DTYPE), w_ref[...].astype(MM_DTYPE)) + b_ref[...]

    return pl.pallas_call(
        body, out_shape=jax.ShapeDtypeStruct((nl, MOD_ROWS, cols), F32), grid=(nl, cols // tn),
        in_specs=[pl.BlockSpec((MOD_ROWS, D_MODEL), lambda i, j: (0, 0)),
                  pl.BlockSpec((None, D_MODEL, tn), lambda i, j: (i, 0, j)),
                  pl.BlockSpec((None, 1, tn), lambda i, j: (i, 0, j))],
        out_specs=pl.BlockSpec((None, MOD_ROWS, tn), lambda i, j: (i, 0, j)),
        compiler_params=_params("parallel", "parallel"), name="mod_fwd")(cact_in, w_mod, b_mod)


def mod_update(c_rows, dmod, w, m, v):
    nl, _, cols = w.shape
    tr, tn = 256, 512
    wspec = pl.BlockSpec((None, tr, tn), lambda i, r, j: (i, r, j))

    def body(c_ref, d_ref, w_ref, m_ref, v_ref, g_ref, dl_ref, nm_ref, nv_ref):
        g = _dot_tn(_silu(c_ref[...]).astype(MM_DTYPE), d_ref[...].astype(MM_DTYPE))
        g_ref[...] = g
        dl, nm, nv = _adam_math(w_ref[...], g, m_ref[...], v_ref[...])
        dl_ref[...] = dl
        nm_ref[...] = nm
        nv_ref[...] = nv

    return pl.pallas_call(
        body, out_shape=[jax.ShapeDtypeStruct(w.shape, F32)] * 4, grid=(nl, D_MODEL // tr, cols // tn),
        in_specs=[pl.BlockSpec((MOD_ROWS, tr), lambda i, r, j: (0, r)),
                  pl.BlockSpec((None, MOD_ROWS, tn), lambda i, r, j: (i, 0, j)), wspec, wspec, wspec],
        out_specs=[wspec] * 4, compiler_params=_params("parallel", "parallel", "parallel"),
        name="mod_update")(c_rows, dmod, w, m, v)


def cctx_partial(dmod_ctx, w_mod, c_ctx):
    nl, _, cols = w_mod.shape
    tk = 512
    per = cols // tk
    part = _matmul(dmod_ctx, w_mod, pl.BlockSpec((8, tk), lambda i, j, k: (0, k)),
                   pl.BlockSpec((None, D_MODEL, tk), lambda i, j, k: (k // per, 0, k % per)),
                   pl.BlockSpec((8, D_MODEL), lambda i, j, k: (0, 0)),
                   jax.ShapeDtypeStruct((8, D_MODEL), F32), (1, 1, nl * per), tb=True, name="cctx_partial")

    def body(p_ref, c_ref, o_ref):
        o_ref[...] = 0.5 * jnp.sum(p_ref[...], axis=0, keepdims=True) * _silu_grad(c_ref[...])

    return pl.pallas_call(body, out_shape=jax.ShapeDtypeStruct((1, D_MODEL), F32), name="cctx_scale")(part, c_ctx)


WEIGHT_NAMES = ("c_ctx", "w_mod", "b_mod", "g_pre", "g_post", "w_in", "b_gate", "na_rpb", "pool_w", "pool_scale",
                "conv_w", "ssm_a_re", "ssm_a_im", "ssm_log_dt", "ssm_b_re", "ssm_b_im", "ssm_c_re", "ssm_c_im",
                "ssm_d", "glu_w", "w_br", "w_o")
SSM_NAMES = ("ssm_a_re", "ssm_a_im", "ssm_log_dt", "ssm_b_re", "ssm_b_im", "ssm_c_re", "ssm_c_im")
SMALL_NAMES = ("c_ctx", "b_mod", "g_pre", "g_post", "b_gate", "na_rpb", "pool_w", "pool_scale") + SSM_NAMES + ("ssm_d",)
FLAT_COLS = 1024


def _flat(parts):
    v = jnp.concatenate([p.reshape(-1) for p in parts])
    pad = -v.shape[0] % (64 * FLAT_COLS)
    return jnp.pad(v, (0, pad)).reshape(-1, FLAT_COLS)


def _unflat(flat, shapes):
    v = flat.reshape(-1)
    out, off = [], 0
    for s in shapes:
        n = math.prod(s)
        out.append(v[off:off + n].reshape(s))
        off += n
    return out


def kernel(x, c, ctx, c_ctx, w_mod, b_mod, g_pre, g_post, w_in, b_gate, na_rpb, pool_w, pool_scale, conv_w, ssm_a_re, ssm_a_im, ssm_log_dt, ssm_b_re, ssm_b_im, ssm_c_re, ssm_c_im, ssm_d, glu_w, w_br, w_o, loss_target, m_c_ctx, m_w_mod, m_b_mod, m_g_pre, m_g_post, m_w_in, m_b_gate, m_na_rpb, m_pool_w, m_pool_scale, m_conv_w, m_ssm_a_re, m_ssm_a_im, m_ssm_log_dt, m_ssm_b_re, m_ssm_b_im, m_ssm_c_re, m_ssm_c_im, m_ssm_d, m_glu_w, m_w_br, m_w_o, v_c_ctx, v_w_mod, v_b_mod, v_g_pre, v_g_post, v_w_in, v_b_gate, v_na_rpb, v_pool_w, v_pool_scale, v_conv_w, v_ssm_a_re, v_ssm_a_im, v_ssm_log_dt, v_ssm_b_re, v_ssm_b_im, v_ssm_c_re, v_ssm_c_im, v_ssm_d, v_glu_w, v_w_br, v_w_o):
    W = dict(c_ctx=c_ctx, w_mod=w_mod, b_mod=b_mod, g_pre=g_pre, g_post=g_post, w_in=w_in, b_gate=b_gate,
             na_rpb=na_rpb, pool_w=pool_w, pool_scale=pool_scale, conv_w=conv_w, ssm_a_re=ssm_a_re, ssm_a_im=ssm_a_im,
             ssm_log_dt=ssm_log_dt, ssm_b_re=ssm_b_re, ssm_b_im=ssm_b_im, ssm_c_re=ssm_c_re, ssm_c_im=ssm_c_im,
             ssm_d=ssm_d, glu_w=glu_w, w_br=w_br, w_o=w_o)
    M = dict(c_ctx=m_c_ctx, w_mod=m_w_mod, b_mod=m_b_mod, g_pre=m_g_pre, g_post=m_g_post, w_in=m_w_in, b_gate=m_b_gate,
             na_rpb=m_na_rpb, pool_w=m_pool_w, pool_scale=m_pool_scale, conv_w=m_conv_w, ssm_a_re=m_ssm_a_re,
             ssm_a_im=m_ssm_a_im, ssm_log_dt=m_ssm_log_dt, ssm_b_re=m_ssm_b_re, ssm_b_im=m_ssm_b_im,
             ssm_c_re=m_ssm_c_re, ssm_c_im=m_ssm_c_im, ssm_d=m_ssm_d, glu_w=m_glu_w, w_br=m_w_br, w_o=m_w_o)
    V = dict(c_ctx=v_c_ctx, w_mod=v_w_mod, b_mod=v_b_mod, g_pre=v_g_pre, g_post=v_g_post, w_in=v_w_in, b_gate=v_b_gate,
             na_rpb=v_na_rpb, pool_w=v_pool_w, pool_scale=v_pool_scale, conv_w=v_conv_w, ssm_a_re=v_ssm_a_re,
             ssm_a_im=v_ssm_a_im, ssm_log_dt=v_ssm_log_dt, ssm_b_re=v_ssm_b_re, ssm_b_im=v_ssm_b_im,
             ssm_c_re=v_ssm_c_re, ssm_c_im=v_ssm_c_im, ssm_d=v_ssm_d, glu_w=v_glu_w, w_br=v_w_br, w_o=v_w_o)
    nl = w_in.shape[0]
    xi, yi, ci = _place()
    chip = 2 * xi + yi
    example = 4 * xi + 2 * yi + ci
    mod_cols = w_mod.shape[2]

    def my_half(a):
        half = a.shape[0] // 2
        return lax.dynamic_slice_in_dim(a, ci * half, half, axis=0).astype(MM_DTYPE)

    halves = [[my_half(W[n][i]) for n in BIG_NAMES] for i in range(nl)]
    first = gathered_weights(all_gather8(halves[0], name="gather_weights", shard_major=True))

    c8 = jnp.pad(c, ((0, 7), (0, 0)))
    c_all, = all_gather8([c8], name="gather_c")
    c_rows = jnp.concatenate([c_all[:, 0], jnp.broadcast_to(c_ctx[None], (8, D_MODEL))], axis=0)
    b_cols = lax.dynamic_slice_in_dim(b_mod, chip * mod_cols, mod_cols, axis=1)[:, None]
    mod_part = mod_fwd(c_rows, w_mod, b_cols)
    conv_part = jnp.pad(conv_w.reshape(nl * 3, -1), ((0, 16 - nl * 3), (0, 0)))
    parts, conv_all = all_gather8([mod_part.reshape(nl * MOD_ROWS, mod_cols), conv_part], name="gather_mod")
    mod_full = jnp.concatenate([parts[2 * p].reshape(nl, MOD_ROWS, mod_cols) for p in range(N_SHARDS)], axis=-1)
    conv_full = jnp.concatenate([conv_all[2 * p][:nl * 3].reshape(nl, 3, -1) for p in range(N_SHARDS)], axis=-1)
    own = lax.dynamic_index_in_dim(mod_full, example, axis=1, keepdims=False)
    mods = [jnp.stack([own[i], mod_full[i, 8]]) for i in range(nl)]

    layers = []
    for i in range(nl):
        layers.append(dict(
            conv_w=jnp.pad(conv_full[i], ((0, 5), (0, 0))), pool_w=pool_w[i], pool_scale=pool_scale[i][None],
            b_gate=b_gate[i][None], g_pre=g_pre[i][None], g_post=g_post[i][None], na_rpb=na_rpb[i],
            ssm_d=ssm_d[i][None], ssm_params=tuple(W[n][i] for n in SSM_NAMES)))
    layers[0].update(first)

    loss_local, grad_x, grads, reduced = local_step(x[0], ctx[0], loss_target[0], mods, layers,
                                                    shard_halves=halves, reduce=True)
    loss = lax.psum(loss_local, ("x", "y", "c"))

    dmod_local = jnp.stack([g["mod"] for g in grads])
    dmod_all, = all_gather8([jnp.pad(dmod_local.reshape(nl * 2, -1), ((0, 8 - nl * 2), (0, 0)))], name="gather_dmod")
    dmod_all = dmod_all[:, :nl * 2].reshape(8, nl, 2, 3 * D_MODEL)
    dmod_rows = jnp.concatenate([dmod_all[:, :, 0], dmod_all[:, :, 1]], axis=0).transpose(1, 0, 2)
    dmod_cols = lax.dynamic_slice_in_dim(dmod_rows, chip * mod_cols, mod_cols, axis=2)
    g_w_mod, d_w_mod, nm_w_mod, nv_w_mod = mod_update(c_rows, dmod_cols, w_mod, m_w_mod, v_w_mod)
    dctx_cols = dmod_cols[:, 8:].transpose(1, 0, 2).reshape(8, nl * mod_cols)
    g_cctx_part = cctx_partial(dctx_cols, w_mod, c_ctx[None])[0]

    def small_grad(n):
        if n == "c_ctx":
            return g_cctx_part
        if n == "b_mod":
            return jnp.stack([g["mod"][0] + g["mod"][1] for g in grads])
        if n in SSM_NAMES:
            return jnp.stack([g["ssm"][SSM_NAMES.index(n)] for g in grads])
        return jnp.stack([g[n] for g in grads])

    conv_grad_full = jnp.stack([g["conv_w"] for g in grads])
    flat_g = _flat([small_grad(n) for n in SMALL_NAMES] + [conv_grad_full])
    flat_all, = all_gather8([flat_g], name="gather_small_grads")
    flat_sum = sum_slabs(flat_all, name="sum_small_grads")
    small_shapes = [W[n].shape for n in SMALL_NAMES]
    small_g = _unflat(flat_sum, small_shapes + [conv_grad_full.shape])
    conv_g = lax.dynamic_slice_in_dim(small_g[-1], chip * conv_w.shape[2], conv_w.shape[2], axis=2)
    adam_names = SMALL_NAMES + ("conv_w",)
    adam_shapes = small_shapes + [conv_w.shape]
    g_list = small_g[:-1] + [conv_g]
    upd = adamw(_flat([W[n] for n in adam_names]), _flat(g_list), _flat([M[n] for n in adam_names]),
                _flat([V[n] for n in adam_names]), name="adamw_small")
    G = dict(zip(adam_names, g_list))
    DL, NM, NV = (dict(zip(adam_names, _unflat(u, adam_shapes))) for u in upd)
    G["w_mod"], DL["w_mod"], NM["w_mod"], NV["w_mod"] = g_w_mod, d_w_mod, nm_w_mod, nv_w_mod

    for k, n in enumerate(BIG_NAMES):
        g = jnp.stack([reduced[i][k] for i in range(nl)])
        rows = g.shape[0] * g.shape[1]
        d, nm, nv = adamw(W[n].reshape(rows, -1), g.reshape(rows, -1), M[n].reshape(rows, -1), V[n].reshape(rows, -1),
                          name=f"adamw_{n}")
        G[n], DL[n], NM[n], NV[n] = g, d.reshape(g.shape), nm.reshape(g.shape), nv.reshape(g.shape)

    out = [loss, grad_x[None]]
    for group in (G, DL, NM, NV):
        out += [group[n].reshape(W[n].shape) for n in WEIGHT_NAMES]
    return tuple(out)
```

```python
import functools
import math

import numpy as np
import jax
import jax.numpy as jnp
from jax import lax
from jax.experimental import pallas as pl
from jax.experimental.pallas import tpu as pltpu

F32 = jnp.float32
BF16 = jnp.bfloat16
MM_DTYPE = jnp.bfloat16

D_MODEL = 2048
BRANCH = 512
N_HEADS = 8
HEAD_DIM = 64
GRID_W = 64
WIN_ROWS = 8
WIN_COLS = 16
POOL_GROUPS = 4
POOL_DIM = 128
SSM_GROUPS = 32
SSM_GDIM = 16
SSM_STATE = 64
N_STATE = SSM_GROUPS * SSM_STATE
IN_TOTAL = 14336
RMS_EPS = 1e-6
NEG_INF = -1e30
COL = dict(q=0, k=512, v=1024, na_z=1536, pool_u=2048, pool_z=2560, conv_x=3072, conv_b=3584,
           conv_c=4096, conv_z=4608, ssm_u=5120, ssm_z=5632, merge=6144)
N_SHARDS = 4
W_IN_SHARD = IN_TOTAL // N_SHARDS
VMEM_LIMIT_BYTES = 48 * 1024 * 1024
ROW_TILE = 256

ADAM_LR = 0.001
ADAM_B1 = 0.9
ADAM_B2 = 0.999
ADAM_EPS = 1e-08
ADAM_WD = 0.01
ADAM_STEP = 10


def _params(*sem):
    return pltpu.CompilerParams(dimension_semantics=sem, vmem_limit_bytes=VMEM_LIMIT_BYTES)


def _sigmoid(x):
    return 1.0 / (1.0 + jnp.exp(-x))


def _matmul(a, b, a_spec, b_spec, o_spec, out_shape, grid, *, ta=False, tb=False, name, carried=None):
    nk = grid[-1]
    kaxis = len(grid) - 1
    dims = (((0,) if ta else (1,), (1,) if tb else (0,)), ((), ()))
    n_acc = 0 if nk == 1 else 1

    def compute(a_ref, b_ref, o_ref, acc):
        p = lax.dot_general(a_ref[...].astype(MM_DTYPE), b_ref[...].astype(MM_DTYPE), dims,
                            preferred_element_type=F32)
        if nk == 1:
            o_ref[...] = p.astype(o_ref.dtype)
            return
        acc_ref, = acc
        k = pl.program_id(kaxis)

        @pl.when(k == 0)
        def _():
            acc_ref[...] = p

        @pl.when(k > 0)
        def _():
            acc_ref[...] += p

        @pl.when(k == nk - 1)
        def _():
            o_ref[...] = acc_ref[...].astype(o_ref.dtype)

    oblock = tuple(s for s in o_spec.block_shape if s is not None)
    scratch = [] if nk == 1 else [pltpu.VMEM(oblock, F32)]
    if carried is None:
        def body(a_ref, b_ref, o_ref, *acc):
            compute(a_ref, b_ref, o_ref, acc)

        sem = ("parallel",) * (len(grid) - 1) + ("arbitrary",)
        return pl.pallas_call(body, out_shape=out_shape, grid=grid, in_specs=[a_spec, b_spec],
                              out_specs=o_spec, scratch_shapes=scratch, compiler_params=_params(*sem),
                              name=name)(a, b)

    n_in, n_out = len(carried.ins), len(carried.out_shapes)
    steps = math.prod(grid)

    def body(a_ref, b_ref, *rest):
        c_ins, o_ref, c_outs = rest[:n_in], rest[n_in], rest[n_in + 1:n_in + 1 + n_out]
        acc, sems = rest[n_in + 1 + n_out:][:n_acc], rest[n_in + 1 + n_out + n_acc:]
        step = pl.program_id(0)
        for ax in range(1, len(grid)):
            step = step * grid[ax] + pl.program_id(ax)

        @pl.when(step == 0)
        def _():
            carried.start(c_ins, c_outs, sems)

        compute(a_ref, b_ref, o_ref, acc)

        @pl.when(step == steps // 2)
        def _():
            carried.mid(c_ins, c_outs, sems)

        @pl.when(step == steps - 1)
        def _():
            carried.finish(c_ins, c_outs, sems)

    res = pl.pallas_call(body, out_shape=[out_shape] + list(carried.out_shapes), grid=grid,
                         in_specs=[a_spec, b_spec] + [HBM_SPEC] * n_in, out_specs=[o_spec] + [HBM_SPEC] * n_out,
                         scratch_shapes=scratch + list(carried.sem_shapes),
                         compiler_params=_params(*(("arbitrary",) * len(grid))), name=name)(a, b, *carried.ins)
    return res[0], res[1:]


def _pick(n, cands):
    for c in cands:
        if n % c == 0:
            return c
    raise ValueError(f"no tile for {n}")


def _row_tile(T):
    return _pick(T, (544, 512, 256, 128))


def mm_nn(a, b, *, out_dtype, name, tn=512, a_rows=None, o_rows=None, a_cols=None):
    M = a.shape[0]
    c0, K = a_cols or (0, a.shape[1])
    N = b.shape[1]
    tm = ROW_TILE if (a_rows or o_rows) else _row_tile(M)
    tn = min(tn, N)
    tk = K if K <= 2048 else _pick(K, (2048, 1024, 512))
    kb0 = c0 // tk
    ar = a_rows or (lambda i: i)
    orr = o_rows or (lambda i: i)
    return _matmul(a, b, pl.BlockSpec((tm, tk), lambda i, j, k: (ar(i), kb0 + k)),
                   pl.BlockSpec((tk, tn), lambda i, j, k: (k, j)),
                   pl.BlockSpec((tm, tn), lambda i, j, k: (orr(i), j)),
                   jax.ShapeDtypeStruct((M, N), out_dtype), (M // tm, N // tn, K // tk), name=name)


def mm_nt(a, b, *, out_dtype, name, a_rows=None, o_rows=None):
    M, K = a.shape
    N = b.shape[0]
    tm = ROW_TILE if (a_rows or o_rows) else _row_tile(M)
    tn = min(N, 2048)
    tk = K if K <= 1024 else _pick(K, (1024, 512))
    ar = a_rows or (lambda i: i)
    orr = o_rows or (lambda i: i)
    return _matmul(a, b, pl.BlockSpec((tm, tk), lambda i, j, k: (ar(i), k)),
                   pl.BlockSpec((tn, tk), lambda i, j, k: (j, k)),
                   pl.BlockSpec((tm, tn), lambda i, j, k: (orr(i), j)),
                   jax.ShapeDtypeStruct((M, N), out_dtype), (M // tm, N // tn, K // tk), tb=True, name=name)


def mm_tn(a, b, *, out_dtype, name, a_rows=None, b_rows=None, tm=512, tn=1024, a_cols=None):
    K = a.shape[0]
    c0, M = a_cols or (0, a.shape[1])
    N = b.shape[1]
    tk = ROW_TILE if (a_rows or b_rows) else K
    tm = min(tm, M)
    tn = min(tn, N)
    mb0 = c0 // tm
    ar = a_rows or (lambda k: k)
    br = b_rows or (lambda k: k)
    return _matmul(a, b, pl.BlockSpec((tk, tm), lambda i, j, k: (ar(k), mb0 + i)),
                   pl.BlockSpec((tk, tn), lambda i, j, k: (br(k), j)),
                   pl.BlockSpec((tm, tn), lambda i, j, k: (i, j)),
                   jax.ShapeDtypeStruct((M, N), out_dtype), (M // tm, N // tn, K // tk), ta=True, name=name)


def _ew(fn, ins, outs, colsums, *, T, L, name):
    tb = ROW_TILE
    nlat = L // tb
    seg = lambda i: jnp.where(i >= nlat, 1, 0)
    in_specs, arrays = [], []
    for arr, kind, cb, width in ins:
        arrays.append(arr)
        if kind == "row":
            in_specs.append(pl.BlockSpec((tb, width), lambda i, cb=cb: (i, cb)))
        elif kind == "bcast":
            in_specs.append(pl.BlockSpec((1, width), lambda i, cb=cb: (0, cb)))
        else:
            in_specs.append(pl.BlockSpec((None, 1, width), lambda i, cb=cb: (seg(i), 0, cb)))
    out_specs = [pl.BlockSpec((tb, w), lambda i: (i, 0)) for w, _ in outs]
    out_shapes = [jax.ShapeDtypeStruct((T, w), dt) for w, dt in outs]
    out_specs += [pl.BlockSpec((None, 1, w), lambda i: (seg(i), 0, 0)) for w in colsums]
    out_shapes += [jax.ShapeDtypeStruct((2, 1, w), F32) for w in colsums]
    n_in, n_out = len(ins), len(outs)

    def body(*refs):
        i = pl.program_id(0)
        res = fn(*[r[...] for r in refs[:n_in]])
        for r, v in zip(refs[n_in:n_in + n_out], res[:n_out]):
            r[...] = v.astype(r.dtype)
        first = (i == 0) | (i == nlat)
        for r, v in zip(refs[n_in + n_out:], res[n_out:]):
            s = jnp.sum(v, axis=0, keepdims=True)

            @pl.when(first)
            def _(r=r, s=s):
                r[...] = s

            @pl.when(jnp.logical_not(first))
            def _(r=r, s=s):
                r[...] += s

    res = pl.pallas_call(body, out_shape=out_shapes, grid=(T // tb,), in_specs=in_specs,
                         out_specs=out_specs, compiler_params=_params("arbitrary"), name=name)(*arrays)
    return res


def _rms(x):
    return lax.rsqrt(jnp.mean(x * x, axis=-1, keepdims=True) + RMS_EPS)


def prenorm_fwd(X, g, scale, shift, *, L):
    T = X.shape[0]

    def fn(x, g, sc, sh):
        return ((x * _rms(x)) * (g * (1.0 + sc)) + sh,)

    h, = _ew(fn, [(X, "row", 0, D_MODEL), (g, "bcast", 0, D_MODEL), (scale, "seg", 0, D_MODEL),
                  (shift, "seg", 0, D_MODEL)], [(D_MODEL, MM_DTYPE)], [], T=T, L=L, name="prenorm_fwd")
    return h


def prenorm_bwd(dh, X, g, scale, dres, *, L):
    T = X.shape[0]

    def fn(dh, x, g, sc, dres):
        r = _rms(x)
        xn = x * r
        dxn = dh * (g * (1.0 + sc))
        dx = r * (dxn - xn * jnp.mean(dxn * xn, axis=-1, keepdims=True))
        return dres + dx, dh, dh * xn

    return _ew(fn, [(dh, "row", 0, D_MODEL), (X, "row", 0, D_MODEL), (g, "bcast", 0, D_MODEL),
                    (scale, "seg", 0, D_MODEL), (dres, "row", 0, D_MODEL)],
               [(D_MODEL, F32)], [D_MODEL, D_MODEL], T=T, L=L, name="prenorm_bwd")


def postnorm_fwd(X, y, g, gate, *, L):
    T = X.shape[0]

    def fn(x, y, g, gate):
        return (x + gate * ((y * _rms(y)) * g),)

    out, = _ew(fn, [(X, "row", 0, D_MODEL), (y, "row", 0, D_MODEL), (g, "bcast", 0, D_MODEL),
                    (gate, "seg", 0, D_MODEL)], [(D_MODEL, F32)], [], T=T, L=L, name="postnorm_fwd")
    return out


def postnorm_bwd(dX, y, g, gate, *, L):
    T = dX.shape[0]

    def fn(dx, y, g, gate):
        r = _rms(y)
        yn = y * r
        dyn = dx * (gate * g)
        dy = r * (dyn - yn * jnp.mean(dyn * yn, axis=-1, keepdims=True))
        return dy, dx * yn

    return _ew(fn, [(dX, "row", 0, D_MODEL), (y, "row", 0, D_MODEL), (g, "bcast", 0, D_MODEL),
                    (gate, "seg", 0, D_MODEL)], [(D_MODEL, MM_DTYPE)], [D_MODEL], T=T, L=L, name="postnorm_bwd")


def loss_and_grad(X, target, *, L):
    T = X.shape[0]
    tb = ROW_TILE
    nlat = L // tb

    def body(x_ref, t_ref, dx_ref, part_ref):
        i = pl.program_id(0)

        @pl.when(i < nlat)
        def _():
            err = x_ref[...] - t_ref[...]
            dx_ref[...] = err * (1.0 / D_MODEL)
            part_ref[...] = jnp.full(part_ref.shape, 0.5 / D_MODEL * jnp.sum(err * err), F32)

        @pl.when(i >= nlat)
        def _():
            dx_ref[...] = jnp.zeros(dx_ref.shape, F32)
            part_ref[...] = jnp.zeros(part_ref.shape, F32)

    dx, part = pl.pallas_call(
        body, out_shape=[jax.ShapeDtypeStruct((T, D_MODEL), F32), jax.ShapeDtypeStruct((T // tb, 8, 128), F32)],
        grid=(T // tb,),
        in_specs=[pl.BlockSpec((tb, D_MODEL), lambda i: (i, 0)),
                  pl.BlockSpec((tb, D_MODEL), lambda i: (jnp.minimum(i, nlat - 1), 0))],
        out_specs=[pl.BlockSpec((tb, D_MODEL), lambda i: (i, 0)), pl.BlockSpec((None, 8, 128), lambda i: (i, 0, 0))],
        compiler_params=_params("parallel"), name="loss_and_grad")(X, target)
    return jnp.sum(part[:, 0, 0]), dx


Q_BLOCK = WIN_ROWS * GRID_W
BAND = 2 * WIN_ROWS * GRID_W


PAIR_TILES = 2 * WIN_ROWS
HEAD_PAIRS = N_HEADS // 2
LANES = 2 * HEAD_DIM
ROW_SHIFT = GRID_W.bit_length() - 1


def bias_pair_tiles(rpb):
    col = np.arange(GRID_W)
    col_start = np.clip(col - WIN_COLS // 2, 0, GRID_W - WIN_COLS)
    in_win = (col[None, :] >= col_start[:, None]) & (col[None, :] < col_start[:, None] + WIN_COLS)
    dcol = np.clip(col[None, :] - col[:, None] + (WIN_COLS - 1), 0, 2 * WIN_COLS - 2)
    E = np.stack([(dcol == dc) & in_win for dc in range(2 * WIN_COLS - 1)]).astype(np.float32)
    tiles = jnp.einsum("hrd,dqk->hrqk", rpb, E, precision=lax.Precision.HIGHEST)
    z = jnp.zeros((N_HEADS, 1, GRID_W, GRID_W), F32)
    return jnp.concatenate([jnp.concatenate([z, tiles], axis=1), jnp.concatenate([tiles, z], axis=1)], axis=-1)


def _band_row(i, rows):
    return jnp.clip(WIN_ROWS * i - WIN_ROWS // 2, 0, rows - 2 * WIN_ROWS)


def _band_start(i, rows):
    return pl.multiple_of(_band_row(i, rows) * GRID_W, 256)


def _window_mask(i, rows):
    r = lax.broadcasted_iota(jnp.int32, (Q_BLOCK, BAND), 0)
    k = lax.broadcasted_iota(jnp.int32, (Q_BLOCK, BAND), 1)
    qr, qc = WIN_ROWS * i + (r >> ROW_SHIFT), r & (GRID_W - 1)
    kr, kc = _band_row(i, rows) + (k >> ROW_SHIFT), k & (GRID_W - 1)
    ws = jnp.clip(qr - WIN_ROWS // 2, 0, rows - WIN_ROWS)
    cs = jnp.clip(qc - WIN_COLS // 2, 0, GRID_W - WIN_COLS)
    return (kr >= ws) & (kr < ws + WIN_ROWS) & (kc >= cs) & (kc < cs + WIN_COLS)


def _pair_index(i, rows, a, j):
    off = _band_row(i, rows) - WIN_ROWS * i
    return jnp.clip(2 * j - a + WIN_ROWS + off, 0, PAIR_TILES - 1)


def _band_bias(p_ref, hh, i, rows):
    bands = [jnp.concatenate([p_ref[hh, _pair_index(i, rows, a, j)] for j in range(WIN_ROWS)], axis=1)
             for a in range(WIN_ROWS)]
    return jnp.concatenate(bands, axis=0)


def _dot_nt(a, b):
    return lax.dot_general(a, b, (((1,), (1,)), ((), ())), preferred_element_type=F32)


def _dot_tn(a, b):
    return lax.dot_general(a, b, (((0,), (0,)), ((), ())), preferred_element_type=F32)


def _dot(a, b):
    return jnp.dot(a, b, preferred_element_type=F32)


QKV_BLOCKS = tuple(COL[n] // LANES for n in ("q", "k", "v"))
SCALE = HEAD_DIM ** -0.5


def _head(x, hh):
    return x[:, hh * HEAD_DIM:(hh + 1) * HEAD_DIM]


def _both(fn):
    res = [fn(0), fn(1)]
    return [jnp.concatenate([a, b], axis=1) for a, b in zip(*res)]


def attn_fwd(proj, ptiles, *, L):
    T = proj.shape[0]
    N = T - L
    rows, nq = L // GRID_W, L // Q_BLOCK
    qb, kb, vb = QKV_BLOCKS

    def body(q_ref, k_ref, v_ref, p_ref, o_ref, lse_ref):
        i = pl.program_id(1)
        ks = _band_start(i, rows)
        mask = _window_mask(i, rows)
        qv = q_ref[...].astype(MM_DTYPE)
        kband, vband = k_ref[pl.ds(ks, BAND), :].astype(MM_DTYPE), v_ref[pl.ds(ks, BAND), :].astype(MM_DTYPE)
        kctx, vctx = k_ref[pl.ds(L, N), :].astype(MM_DTYPE), v_ref[pl.ds(L, N), :].astype(MM_DTYPE)

        def head(hh):
            q = _head(qv, hh)
            sb = _dot_nt(q, _head(kband, hh)) * SCALE + jnp.where(mask, _band_bias(p_ref, hh, i, rows), NEG_INF)
            sc = _dot_nt(q, _head(kctx, hh)) * SCALE
            m = jnp.maximum(jnp.max(sb, axis=-1, keepdims=True), jnp.max(sc, axis=-1, keepdims=True))
            pb, pc = jnp.exp(sb - m), jnp.exp(sc - m)
            l = jnp.sum(pb, axis=-1, keepdims=True) + jnp.sum(pc, axis=-1, keepdims=True)
            o = _dot(pb.astype(MM_DTYPE), _head(vband, hh)) + _dot(pc.astype(MM_DTYPE), _head(vctx, hh))
            return o / l, jnp.broadcast_to(m + jnp.log(l), (Q_BLOCK, HEAD_DIM))

        o_ref[...], lse_ref[...] = _both(head)

    qspec = lambda b0: pl.BlockSpec((Q_BLOCK, LANES), lambda hp, i: (i, b0 + hp))
    kspec = lambda b0: pl.BlockSpec((T, LANES), lambda hp, i: (0, b0 + hp))
    return pl.pallas_call(
        body, out_shape=[jax.ShapeDtypeStruct((T, BRANCH), F32), jax.ShapeDtypeStruct((L, BRANCH), F32)],
        grid=(HEAD_PAIRS, nq),
        in_specs=[qspec(qb), kspec(kb), kspec(vb),
                  pl.BlockSpec((2, PAIR_TILES, GRID_W, LANES), lambda hp, i: (hp, 0, 0, 0))],
        out_specs=[qspec(0), qspec(0)],
        compiler_params=_params("parallel", "arbitrary"), name="attn_fwd")(proj, proj, proj, ptiles)


def attn_bwd(proj, ptiles, o, do, lse, *, L):
    T = proj.shape[0]
    N = T - L
    rows, nq = L // GRID_W, L // Q_BLOCK
    qb, kb, vb = QKV_BLOCKS

    def body(q_ref, k_ref, v_ref, p_ref, o_ref, do_ref, lse_ref, dq_ref, dk_ref, dv_ref, dp_ref):
        i = pl.program_id(1)
        ks = _band_start(i, rows)

        @pl.when(i == 0)
        def _():
            dk_ref[...] = jnp.zeros(dk_ref.shape, F32)
            dv_ref[...] = jnp.zeros(dv_ref.shape, F32)
            dp_ref[...] = jnp.zeros(dp_ref.shape, F32)

        mask = _window_mask(i, rows)
        qv = q_ref[...].astype(MM_DTYPE)
        kband, vband = k_ref[pl.ds(ks, BAND), :].astype(MM_DTYPE), v_ref[pl.ds(ks, BAND), :].astype(MM_DTYPE)
        kctx, vctx = k_ref[pl.ds(L, N), :].astype(MM_DTYPE), v_ref[pl.ds(L, N), :].astype(MM_DTYPE)
        ov, dof, lsev = o_ref[...], do_ref[...], lse_ref[...]

        def head(hh):
            q, kb_h, kc_h, vb_h, vc_h = (_head(t, hh) for t in (qv, kband, kctx, vband, vctx))
            lse = _head(lsev, hh)[:, 0:1]
            pb = jnp.exp(_dot_nt(q, kb_h) * SCALE + jnp.where(mask, _band_bias(p_ref, hh, i, rows), NEG_INF) - lse)
            pc = jnp.exp(_dot_nt(q, kc_h) * SCALE - lse)
            do_h = _head(dof, hh)
            delta = jnp.sum(do_h * _head(ov, hh), axis=-1, keepdims=True)
            dov = do_h.astype(MM_DTYPE)
            dsb = pb * (_dot_nt(dov, vb_h) - delta)
            dsc = pc * (_dot_nt(dov, vc_h) - delta)
            for a in range(WIN_ROWS):
                for j in range(WIN_ROWS):
                    dp_ref[hh, _pair_index(i, rows, a, j)] += dsb[a * GRID_W:(a + 1) * GRID_W, j * LANES:(j + 1) * LANES]
            dsb_s, dsc_s = (dsb * SCALE).astype(MM_DTYPE), (dsc * SCALE).astype(MM_DTYPE)
            dq = _dot(dsb_s, kb_h) + _dot(dsc_s, kc_h)
            return (dq, _dot_tn(dsb_s, q), _dot_tn(dsc_s, q), _dot_tn(pb.astype(MM_DTYPE), dov),
                    _dot_tn(pc.astype(MM_DTYPE), dov))

        dq, dkb, dkc, dvb, dvc = _both(head)
        dq_ref[...] = dq
        dk_ref[pl.ds(ks, BAND), :] += dkb
        dk_ref[pl.ds(L, N), :] += dkc
        dv_ref[pl.ds(ks, BAND), :] += dvb
        dv_ref[pl.ds(L, N), :] += dvc

    qspec = lambda b0: pl.BlockSpec((Q_BLOCK, LANES), lambda hp, i: (i, b0 + hp))
    kspec = lambda b0: pl.BlockSpec((T, LANES), lambda hp, i: (0, b0 + hp))
    pspec = pl.BlockSpec((2, PAIR_TILES, GRID_W, LANES), lambda hp, i: (hp, 0, 0, 0))
    return pl.pallas_call(
        body,
        out_shape=[jax.ShapeDtypeStruct((T, BRANCH), F32)] * 3 + [jax.ShapeDtypeStruct(ptiles.shape, F32)],
        grid=(HEAD_PAIRS, nq),
        in_specs=[qspec(qb), kspec(kb), kspec(vb), pspec, qspec(0), qspec(0), qspec(0)],
        out_specs=[qspec(0), kspec(0), kspec(0), pspec],
        compiler_params=_params("parallel", "arbitrary"), name="attn_bwd")(proj, proj, proj, ptiles, o, do, lse)


ANY_SPEC = pl.BlockSpec(memory_space=pl.ANY)


def cattn_fwd(proj, o, *, L):
    T = proj.shape[0]
    N = T - L
    qb, kb, vb = QKV_BLOCKS
    cspec = lambda b0: pl.BlockSpec((N, LANES), lambda hp: (L // N, b0 + hp))

    def body(q_ref, k_ref, v_ref, o_in, o_ref, lse_ref):
        qv, kv, vv = (r[...].astype(MM_DTYPE) for r in (q_ref, k_ref, v_ref))

        def head(hh):
            s = _dot_nt(_head(qv, hh), _head(kv, hh)) * SCALE
            m = jnp.max(s, axis=-1, keepdims=True)
            p = jnp.exp(s - m)
            l = jnp.sum(p, axis=-1, keepdims=True)
            return _dot(p.astype(MM_DTYPE), _head(vv, hh)) / l, jnp.broadcast_to(m + jnp.log(l), (N, HEAD_DIM))

        o_ref[...], lse_ref[...] = _both(head)

    return pl.pallas_call(
        body, out_shape=[jax.ShapeDtypeStruct(o.shape, F32), jax.ShapeDtypeStruct((N, BRANCH), F32)],
        grid=(HEAD_PAIRS,), in_specs=[cspec(qb), cspec(kb), cspec(vb), ANY_SPEC],
        out_specs=[cspec(0), pl.BlockSpec((N, LANES), lambda hp: (0, hp))], input_output_aliases={3: 0},
        compiler_params=_params("parallel"), name="cattn_fwd")(proj, proj, proj, o)


def cattn_bwd(proj, o, do, lse, dq, dk, dv, *, L):
    T = proj.shape[0]
    N = T - L
    qb, kb, vb = QKV_BLOCKS
    cspec = lambda b0: pl.BlockSpec((N, LANES), lambda hp: (L // N, b0 + hp))

    def body(q_ref, k_ref, v_ref, o_ref, do_ref, lse_ref, dq_in, dk_in, dv_in, dq_ref, dk_ref, dv_ref):
        qv, kv, vv = (r[...].astype(MM_DTYPE) for r in (q_ref, k_ref, v_ref))
        ov, dof, lsev = o_ref[...], do_ref[...], lse_ref[...]

        def head(hh):
            q, k, v = _head(qv, hh), _head(kv, hh), _head(vv, hh)
            p = jnp.exp(_dot_nt(q, k) * SCALE - _head(lsev, hh)[:, 0:1])
            do_h = _head(dof, hh)
            delta = jnp.sum(do_h * _head(ov, hh), axis=-1, keepdims=True)
            dov = do_h.astype(MM_DTYPE)
            ds = (p * (_dot_nt(dov, v) - delta) * SCALE).astype(MM_DTYPE)
            return _dot(ds, k), _dot_tn(ds, q), _dot_tn(p.astype(MM_DTYPE), dov)

        dq_c, dk_c, dv_c = _both(head)
        dq_ref[...] = dq_c
        dk_ref[...] = dk_in[...] + dk_c
        dv_ref[...] = dv_in[...] + dv_c

    return pl.pallas_call(
        body, out_shape=[jax.ShapeDtypeStruct(dq.shape, F32)] * 3, grid=(HEAD_PAIRS,),
        in_specs=[cspec(qb), cspec(kb), cspec(vb), cspec(0), cspec(0), pl.BlockSpec((N, LANES), lambda hp: (0, hp)),
                  ANY_SPEC, cspec(0), cspec(0)],
        out_specs=[cspec(0)] * 3, input_output_aliases={6: 0, 7: 1, 8: 2},
        compiler_params=_params("parallel"), name="cattn_bwd")(proj, proj, proj, o, do, lse, dq, dk, dv)


PAD = 16


def _row_ids(T):
    return lax.broadcasted_iota(jnp.int32, (T, POOL_DIM), 0)


def _same_segment(t, s, L, T):
    return (s >= 0) & (s < T) & ((t < L) == (s < L))


def _window_sum(buf_ref, x, half, *, L, T, transpose):
    buf_ref[pl.ds(PAD, T), :] = x
    t = _row_ids(T)
    acc = jnp.zeros((T, POOL_DIM), F32)
    for j in range(-8, 9):
        inside = ((j > -half) & (j <= half)) if transpose else ((j >= -half) & (j < half))
        ok = _same_segment(t, t + j, L, T) & inside
        acc = acc + jnp.where(ok, buf_ref[pl.ds(PAD + j, T), :], 0.0)
    return acc


def _window_count(half, *, L, T):
    t = _row_ids(T)
    pos = jnp.where(t < L, t, t - L)
    seg_len = jnp.where(t < L, L, T - L)
    return (jnp.minimum(pos + half, seg_len) - jnp.maximum(pos - half, 0)).astype(F32)


def _zero_pads(buf_ref, T):
    buf_ref[pl.ds(0, PAD), :] = jnp.zeros((PAD, POOL_DIM), F32)
    buf_ref[pl.ds(PAD + T, PAD), :] = jnp.zeros((PAD, POOL_DIM), F32)


def pool_fwd(proj, pool_w, pool_scale, *, L):
    T = proj.shape[0]
    cb0 = COL["pool_u"] // POOL_DIM

    def body(u_ref, w_ref, s_ref, o_ref, p_ref, buf_ref):
        half = jnp.left_shift(1, pl.program_id(0))
        _zero_pads(buf_ref, T)
        u = u_ref[...].astype(F32)
        pooled = _window_sum(buf_ref, u, half, L=L, T=T, transpose=False) / _window_count(half, L=L, T=T) - u
        pm = pooled.astype(MM_DTYPE)
        p_ref[...] = pm
        o_ref[...] = _dot(pm, w_ref[...].astype(MM_DTYPE)) * s_ref[...]

    cspec = pl.BlockSpec((T, POOL_DIM), lambda g: (0, g))
    return pl.pallas_call(
        body, out_shape=[jax.ShapeDtypeStruct((T, BRANCH), F32), jax.ShapeDtypeStruct((T, BRANCH), MM_DTYPE)],
        grid=(POOL_GROUPS,),
        in_specs=[pl.BlockSpec((T, POOL_DIM), lambda g: (0, cb0 + g)),
                  pl.BlockSpec((None, POOL_DIM, POOL_DIM), lambda g: (g, 0, 0)),
                  pl.BlockSpec((1, POOL_DIM), lambda g: (0, g))],
        out_specs=[cspec, cspec], scratch_shapes=[pltpu.VMEM((T + 2 * PAD, POOL_DIM), F32)],
        compiler_params=_params("parallel"), name="pool_fwd")(proj, pool_w, pool_scale)


def pool_bwd(do, pooled, pool_w, pool_scale, *, L):
    T = do.shape[0]

    def body(do_ref, p_ref, w_ref, s_ref, du_ref, dw_ref, ds_ref, buf_ref):
        half = jnp.left_shift(1, pl.program_id(0))
        _zero_pads(buf_ref, T)
        pm = p_ref[...]
        w = w_ref[...].astype(MM_DTYPE)
        mixed = _dot(pm, w)
        dov = do_ref[...]
        ds_ref[...] = jnp.broadcast_to(jnp.sum(dov * mixed, axis=0, keepdims=True), ds_ref.shape)
        dmixed = (dov * s_ref[...]).astype(MM_DTYPE)
        dw_ref[...] = _dot_tn(pm, dmixed)
        dpooled = _dot_nt(dmixed, w)
        scaled = dpooled / _window_count(half, L=L, T=T)
        du = _window_sum(buf_ref, scaled, half, L=L, T=T, transpose=True) - dpooled
        du_ref[...] = du.astype(du_ref.dtype)

    cspec = pl.BlockSpec((T, POOL_DIM), lambda g: (0, g))
    return pl.pallas_call(
        body, out_shape=[jax.ShapeDtypeStruct((T, BRANCH), MM_DTYPE),
                         jax.ShapeDtypeStruct((POOL_GROUPS, POOL_DIM, POOL_DIM), F32),
                         jax.ShapeDtypeStruct((8, BRANCH), F32)],
        grid=(POOL_GROUPS,),
        in_specs=[cspec, cspec, pl.BlockSpec((None, POOL_DIM, POOL_DIM), lambda g: (g, 0, 0)),
                  pl.BlockSpec((1, POOL_DIM), lambda g: (0, g))],
        out_specs=[cspec, pl.BlockSpec((None, POOL_DIM, POOL_DIM), lambda g: (g, 0, 0)),
                   pl.BlockSpec((8, POOL_DIM), lambda g: (0, g))],
        scratch_shapes=[pltpu.VMEM((T + 2 * PAD, POOL_DIM), F32)],
        compiler_params=_params("parallel"), name="pool_bwd")(do, pooled, pool_w, pool_scale)


def _shifted(buf_ref, x, j, *, L, T):
    buf_ref[pl.ds(PAD, T), :] = x
    t = _row_ids(T)
    return jnp.where(_same_segment(t, t + j, L, T), buf_ref[pl.ds(PAD + j, T), :], 0.0)


def conv_fwd(proj, conv_w, *, L):
    T = proj.shape[0]
    nb = BRANCH // POOL_DIM
    cx, cbb, cc = (COL[n] // POOL_DIM for n in ("conv_x", "conv_b", "conv_c"))

    def body(x_ref, b_ref, c_ref, w_ref, o_ref, buf_ref):
        _zero_pads(buf_ref, T)
        xc = c_ref[...].astype(F32) * x_ref[...].astype(F32)
        w = w_ref[...]
        conv = (w[0:1] * _shifted(buf_ref, xc, -1, L=L, T=T) + w[1:2] * xc
                + w[2:3] * _shifted(buf_ref, xc, 1, L=L, T=T))
        o_ref[...] = b_ref[...].astype(F32) * conv

    return pl.pallas_call(
        body, out_shape=jax.ShapeDtypeStruct((T, BRANCH), F32), grid=(nb,),
        in_specs=[pl.BlockSpec((T, POOL_DIM), lambda g: (0, cx + g)), pl.BlockSpec((T, POOL_DIM), lambda g: (0, cbb + g)),
                  pl.BlockSpec((T, POOL_DIM), lambda g: (0, cc + g)), pl.BlockSpec((8, POOL_DIM), lambda g: (0, g))],
        out_specs=pl.BlockSpec((T, POOL_DIM), lambda g: (0, g)),
        scratch_shapes=[pltpu.VMEM((T + 2 * PAD, POOL_DIM), F32)],
        compiler_params=_params("parallel"), name="conv_fwd")(proj, proj, proj, conv_w)


def conv_bwd(do, proj, conv_w, *, L):
    T = proj.shape[0]
    nb = BRANCH // POOL_DIM
    cx, cbb, cc = (COL[n] // POOL_DIM for n in ("conv_x", "conv_b", "conv_c"))

    def body(do_ref, x_ref, b_ref, c_ref, w_ref, dx_ref, db_ref, dc_ref, dw_ref, buf_ref):
        _zero_pads(buf_ref, T)
        xv, gb, gc = (r[...].astype(F32) for r in (x_ref, b_ref, c_ref))
        xc = gc * xv
        w = w_ref[...]
        xm = _shifted(buf_ref, xc, -1, L=L, T=T)
        xp = _shifted(buf_ref, xc, 1, L=L, T=T)
        conv = w[0:1] * xm + w[1:2] * xc + w[2:3] * xp
        dov = do_ref[...]
        db_ref[...] = (dov * conv).astype(db_ref.dtype)
        dconv = dov * gb
        sums = [jnp.sum(dconv * a, axis=0, keepdims=True) for a in (xm, xc, xp)]
        dw_ref[...] = jnp.concatenate(sums + [jnp.zeros((5, POOL_DIM), F32)], axis=0)
        dxc = (w[0:1] * _shifted(buf_ref, dconv, 1, L=L, T=T) + w[1:2] * dconv
               + w[2:3] * _shifted(buf_ref, dconv, -1, L=L, T=T))
        dc_ref[...] = (dxc * xv).astype(dc_ref.dtype)
        dx_ref[...] = (dxc * gc).astype(dx_ref.dtype)

    ospec = lambda off: pl.BlockSpec((T, POOL_DIM), lambda g: (0, off + g))
    return pl.pallas_call(
        body, out_shape=[jax.ShapeDtypeStruct((T, BRANCH), MM_DTYPE)] * 3 + [jax.ShapeDtypeStruct((8, BRANCH), F32)],
        grid=(nb,),
        in_specs=[ospec(0), ospec(cx), ospec(cbb), ospec(cc), pl.BlockSpec((8, POOL_DIM), lambda g: (0, g))],
        out_specs=[ospec(0), ospec(0), ospec(0), pl.BlockSpec((8, POOL_DIM), lambda g: (0, g))],
        scratch_shapes=[pltpu.VMEM((T + 2 * PAD, POOL_DIM), F32)],
        compiler_params=_params("parallel"), name="conv_bwd")(do, proj, proj, proj, conv_w)


SCAN_COLS = 1024
SCAN_ROWS = 256


def ssm_operators(a_re, a_im, log_dt, b_re, b_im, c_re, c_im):
    dt = jnp.exp(log_dt)[..., None]
    mag = jnp.exp(a_re * dt)
    abar_re, abar_im = mag * jnp.cos(a_im * dt), mag * jnp.sin(a_im * dt)
    den = a_re * a_re + a_im * a_im
    num_re, num_im = abar_re - 1.0, abar_im
    f_re = (num_re * a_re + num_im * a_im) / den
    f_im = (num_im * a_re - num_re * a_im) / den
    bbar_re = f_re[..., None] * b_re - f_im[..., None] * b_im
    bbar_im = f_re[..., None] * b_im + f_im[..., None] * b_re
    gpb = SSM_GROUPS // SSM_BLOCKS
    eye = jnp.eye(gpb, dtype=bool)[None, None, :, None, :, None]

    def blocks(t):
        _, _, a, b = t.shape
        t = t.reshape(2, SSM_BLOCKS, gpb, a, 1, b)
        return jnp.where(eye, t, 0.0).reshape(2, SSM_BLOCKS, gpb * a, gpb * b)

    in_map = lambda bbar: blocks(bbar.transpose(0, 1, 3, 2))
    out_map = lambda c: blocks(c.transpose(0, 1, 3, 2))
    abar = jnp.concatenate([abar_re.reshape(2, 1, N_STATE), abar_im.reshape(2, 1, N_STATE)], axis=-1)
    bcat = jnp.concatenate([in_map(bbar_re), in_map(bbar_im)], axis=1)
    ccat = jnp.concatenate([out_map(c_re), -out_map(c_im)], axis=1)
    return abar, bcat, ccat


SSM_BLOCKS = 4
SSM_BCH = BRANCH // SSM_BLOCKS
SSM_BST = N_STATE // SSM_BLOCKS


def _ssm_rows(T, perm):
    tm = ROW_TILE if perm else _row_tile(T)
    return tm, (perm or (lambda i: i))


def _lanes(x, n, width):
    return x[:, n * width:(n + 1) * width]


def _ssm_specs(T, perm, ucol0=None):
    tm, rows = _ssm_rows(T, perm)
    chan = pl.BlockSpec((tm, BRANCH), lambda i: (rows(i), 0 if ucol0 is None else ucol0 // BRANCH))
    state = pl.BlockSpec((tm, 2 * N_STATE), lambda i: (i, 0))
    bspec = pl.BlockSpec((2 * SSM_BLOCKS, SSM_BCH, SSM_BST), lambda i: (0, 0, 0))
    cspec = pl.BlockSpec((2 * SSM_BLOCKS, SSM_BST, SSM_BCH), lambda i: (0, 0, 0))
    return T // tm, chan, state, bspec, cspec


def ssm_in(u, bcat, *, ucol0, perm, name):
    T = u.shape[0]
    steps, chan, state, bspec, _ = _ssm_specs(T, perm, ucol0)

    def body(u_ref, b_ref, o_ref):
        uv = u_ref[...].astype(MM_DTYPE)
        for n in range(2 * SSM_BLOCKS):
            o_ref[:, n * SSM_BST:(n + 1) * SSM_BST] = _dot(_lanes(uv, n % SSM_BLOCKS, SSM_BCH), b_ref[n].astype(MM_DTYPE))

    return pl.pallas_call(body, out_shape=jax.ShapeDtypeStruct((T, 2 * N_STATE), F32), grid=(steps,),
                          in_specs=[chan, bspec], out_specs=state, compiler_params=_params("parallel"), name=name)(u, bcat)


def ssm_out(s, ccat, *, perm, name):
    T = s.shape[0]
    steps, chan, state, _, cspec = _ssm_specs(T, perm)

    def body(s_ref, c_ref, o_ref):
        sv = s_ref[...].astype(MM_DTYPE)
        o_ref[...] = jnp.concatenate(
            [_dot(_lanes(sv, j, SSM_BST), c_ref[j].astype(MM_DTYPE))
             + _dot(_lanes(sv, SSM_BLOCKS + j, SSM_BST), c_ref[SSM_BLOCKS + j].astype(MM_DTYPE))
             for j in range(SSM_BLOCKS)], axis=1)

    return pl.pallas_call(body, out_shape=jax.ShapeDtypeStruct((T, BRANCH), F32), grid=(steps,),
                          in_specs=[state, cspec], out_specs=chan, compiler_params=_params("parallel"), name=name)(s, ccat)


def ssm_out_dx(dy, ccat, *, perm, name):
    T = dy.shape[0]
    steps, chan, state, _, cspec = _ssm_specs(T, perm)

    def body(d_ref, c_ref, o_ref):
        dv = d_ref[...].astype(MM_DTYPE)
        for n in range(2 * SSM_BLOCKS):
            o_ref[:, n * SSM_BST:(n + 1) * SSM_BST] = _dot_nt(_lanes(dv, n % SSM_BLOCKS, SSM_BCH), c_ref[n].astype(MM_DTYPE))

    return pl.pallas_call(body, out_shape=jax.ShapeDtypeStruct((T, 2 * N_STATE), F32), grid=(steps,),
                          in_specs=[chan, cspec], out_specs=state, compiler_params=_params("parallel"), name=name)(dy, ccat)


def ssm_in_dx(lam, bcat, *, perm, name):
    T = lam.shape[0]
    steps, chan, state, bspec, _ = _ssm_specs(T, perm)

    def body(l_ref, b_ref, o_ref):
        lv = l_ref[...].astype(MM_DTYPE)
        o_ref[...] = jnp.concatenate(
            [_dot_nt(_lanes(lv, j, SSM_BST), b_ref[j].astype(MM_DTYPE))
             + _dot_nt(_lanes(lv, SSM_BLOCKS + j, SSM_BST), b_ref[SSM_BLOCKS + j].astype(MM_DTYPE))
             for j in range(SSM_BLOCKS)], axis=1)

    return pl.pallas_call(body, out_shape=jax.ShapeDtypeStruct((T, BRANCH), F32), grid=(steps,),
                          in_specs=[state, bspec], out_specs=chan, compiler_params=_params("parallel"), name=name)(lam, bcat)


def _ssm_dw(chan_arr, state_arr, chan_spec, state_spec, out_block, steps, chan_first, name):
    def body(c_ref, s_ref, o_ref):
        @pl.when(pl.program_id(0) == 0)
        def _():
            o_ref[...] = jnp.zeros(o_ref.shape, F32)

        cv, sv = c_ref[...].astype(MM_DTYPE), s_ref[...].astype(MM_DTYPE)
        for n in range(2 * SSM_BLOCKS):
            c, s = _lanes(cv, n % SSM_BLOCKS, SSM_BCH), _lanes(sv, n, SSM_BST)
            o_ref[n] += _dot_tn(c, s) if chan_first else _dot_tn(s, c)

    shape = (2 * SSM_BLOCKS,) + out_block
    return pl.pallas_call(body, out_shape=jax.ShapeDtypeStruct(shape, F32), grid=(steps,),
                          in_specs=[chan_spec, state_spec], out_specs=pl.BlockSpec(shape, lambda k: (0, 0, 0)),
                          compiler_params=_params("arbitrary"), name=name)(chan_arr, state_arr)


def ssm_in_dw(u, lam, *, ucol0, perm, name):
    steps, chan, state, _, _ = _ssm_specs(u.shape[0], perm, ucol0)
    return _ssm_dw(u, lam, chan, state, (SSM_BCH, SSM_BST), steps, True, name)


def ssm_out_dw(s, dy, *, perm, name):
    steps, chan, state, _, _ = _ssm_specs(s.shape[0], perm)
    return _ssm_dw(dy, s, chan, state, (SSM_BST, SSM_BCH), steps, False, name)


def _time_block(T, reverse):
    nt = T // SCAN_ROWS
    tix = (lambda i: nt - 1 - i) if reverse else (lambda i: i)
    return nt, pl.BlockSpec((SCAN_ROWS, 2 * N_STATE), lambda i: (tix(i), 0))


def ssm_scan(bu, abar, *, reverse):
    T = bu.shape[0]
    nt, tspec = _time_block(T, reverse)

    def body(b_ref, a_ref, s_ref, c_ref):
        @pl.when(pl.program_id(0) == 0)
        def _():
            c_ref[...] = jnp.zeros(c_ref.shape, F32)

        for c0 in range(0, N_STATE, SCAN_COLS):
            re, im = pl.ds(c0, SCAN_COLS), pl.ds(N_STATE + c0, SCAN_COLS)
            ar, ai = a_ref[:, re], a_ref[:, im]

            def step(n, carry, re=re, im=im, ar=ar, ai=ai):
                sr, si = carry
                t = (SCAN_ROWS - 1 - n) if reverse else n
                nr = ar * sr - ai * si + b_ref[pl.ds(t, 1), re]
                ni = ar * si + ai * sr + b_ref[pl.ds(t, 1), im]
                s_ref[pl.ds(t, 1), re] = nr
                s_ref[pl.ds(t, 1), im] = ni
                return nr, ni

            sr, si = lax.fori_loop(0, SCAN_ROWS, step, (c_ref[:, re], c_ref[:, im]))
            c_ref[:, re] = sr
            c_ref[:, im] = si

    return pl.pallas_call(
        body, out_shape=jax.ShapeDtypeStruct((T, 2 * N_STATE), F32), grid=(nt,),
        in_specs=[tspec, pl.BlockSpec((1, 2 * N_STATE), lambda i: (0, 0))], out_specs=tspec,
        scratch_shapes=[pltpu.VMEM((1, 2 * N_STATE), F32)],
        compiler_params=_params("arbitrary"), name="ssm_scan_rev" if reverse else "ssm_scan_fwd")(bu, abar)


def ssm_scan_bwd(g, s, abar, *, reverse):
    T = g.shape[0]
    nt, tspec = _time_block(T, not reverse)
    back = not reverse

    def body(g_ref, s_ref, a_ref, l_ref, da_ref, c_ref):
        @pl.when(pl.program_id(0) == 0)
        def _():
            c_ref[...] = jnp.zeros(c_ref.shape, F32)
            da_ref[...] = jnp.zeros(da_ref.shape, F32)

        for c0 in range(0, N_STATE, SCAN_COLS):
            re, im = pl.ds(c0, SCAN_COLS), pl.ds(N_STATE + c0, SCAN_COLS)
            ar, ai = a_ref[:, re], a_ref[:, im]

            def step(n, carry, re=re, im=im, ar=ar, ai=ai):
                lr, li, dr, di = carry
                t = (SCAN_ROWS - 1 - n) if back else n
                sr, si = s_ref[pl.ds(t, 1), re], s_ref[pl.ds(t, 1), im]
                dr = dr + sr * lr + si * li
                di = di + sr * li - si * lr
                nr = g_ref[pl.ds(t, 1), re] + ar * lr + ai * li
                ni = g_ref[pl.ds(t, 1), im] + ar * li - ai * lr
                l_ref[pl.ds(t, 1), re] = nr
                l_ref[pl.ds(t, 1), im] = ni
                return nr, ni, dr, di

            zero = jnp.zeros((1, SCAN_COLS), F32)
            lr, li, dr, di = lax.fori_loop(0, SCAN_ROWS, step, (c_ref[:, re], c_ref[:, im], zero, zero))
            c_ref[:, re] = lr
            c_ref[:, im] = li
            da_ref[:, re] += jnp.broadcast_to(dr, (8, SCAN_COLS))
            da_ref[:, im] += jnp.broadcast_to(di, (8, SCAN_COLS))

    return pl.pallas_call(
        body, out_shape=[jax.ShapeDtypeStruct((T, 2 * N_STATE), F32), jax.ShapeDtypeStruct((8, 2 * N_STATE), F32)],
        grid=(nt,), in_specs=[tspec, tspec, pl.BlockSpec((1, 2 * N_STATE), lambda i: (0, 0))],
        out_specs=[tspec, pl.BlockSpec((8, 2 * N_STATE), lambda i: (0, 0))],
        scratch_shapes=[pltpu.VMEM((1, 2 * N_STATE), F32)],
        compiler_params=_params("arbitrary"),
        name="ssm_scan_bwd_rev" if reverse else "ssm_scan_bwd_fwd")(g, s, abar)


def _gelu(x):
    return 0.5 * x * (1.0 + jnp.tanh(0.7978845608028654 * (x + 0.044715 * x * x * x)))


def _gelu_grad(x):
    t = jnp.tanh(0.7978845608028654 * (x + 0.044715 * x * x * x))
    return 0.5 * (1.0 + t) + 0.5 * x * (1.0 - t * t) * 0.7978845608028654 * (1.0 + 3 * 0.044715 * x * x)


def _silu(z):
    return z * _sigmoid(z)


def _silu_grad(z):
    s = _sigmoid(z)
    return s * (1.0 + z * (1.0 - s))


def ssm_fwd(proj, ops, dsk, glu_w, *, L):
    T = proj.shape[0]
    abar, bcat, ccat = ops
    nb, nlat = T // ROW_TILE, L // ROW_TILE
    to_f = lambda i: (i + nlat) % nb
    states, ys = [], []
    for d in (0, 1):
        perm = to_f if d == 0 else None
        bu = ssm_in(proj, bcat[d], ucol0=COL["ssm_u"], perm=perm, name=f"ssm_in{d}")
        s = ssm_scan(bu, abar[d], reverse=(d == 1))
        states.append(s)
        ys.append(ssm_out(s, ccat[d], perm=perm, name=f"ssm_out{d}"))

    def pre(u, yf, yr, dsk):
        y = dsk * u + yf + yr
        return y, _gelu(y)

    ypre, gy = _ew(pre, [(proj, "row", COL["ssm_u"] // BRANCH, BRANCH), (ys[0], "row", 0, BRANCH),
                         (ys[1], "row", 0, BRANCH), (dsk, "bcast", 0, BRANCH)],
                   [(BRANCH, F32), (BRANCH, MM_DTYPE)], [], T=T, L=L, name="ssm_pre")
    gg = mm_nn(gy, glu_w, out_dtype=F32, name="ssm_glu")

    def post(ga, gb):
        return (ga * _sigmoid(gb),)

    o, = _ew(post, [(gg, "row", 0, BRANCH), (gg, "row", 1, BRANCH)], [(BRANCH, F32)], [], T=T, L=L, name="ssm_post")
    return o, dict(states=states, ypre=ypre, gy=gy, gg=gg)


def ssm_bwd(do, proj, ops, dsk, glu_w, saved, *, L):
    T = proj.shape[0]
    abar, bcat, ccat = ops
    nb, nlat = T // ROW_TILE, L // ROW_TILE
    to_f = lambda i: (i + nlat) % nb
    gg, gy, ypre = saved["gg"], saved["gy"], saved["ypre"]

    def post_bwd(do, ga, gb):
        sg = _sigmoid(gb)
        return (jnp.concatenate([do * sg, do * ga * sg * (1.0 - sg)], axis=1),)

    dgg, = _ew(post_bwd, [(do, "row", 0, BRANCH), (gg, "row", 0, BRANCH), (gg, "row", 1, BRANCH)],
               [(2 * BRANCH, MM_DTYPE)], [], T=T, L=L, name="ssm_post_bwd")
    dgy = mm_nt(dgg, glu_w, out_dtype=F32, name="ssm_glu_dx")
    dglu = mm_tn(gy, dgg, out_dtype=F32, name="ssm_glu_dw")

    def pre_bwd(dgy, y, u, dsk):
        dy = dgy * _gelu_grad(y)
        return dy, dy * dsk, dy * u

    dy, du_skip, dd = _ew(pre_bwd, [(dgy, "row", 0, BRANCH), (ypre, "row", 0, BRANCH),
                                    (proj, "row", COL["ssm_u"] // BRANCH, BRANCH), (dsk, "bcast", 0, BRANCH)],
                          [(BRANCH, MM_DTYPE), (BRANCH, F32)], [BRANCH], T=T, L=L, name="ssm_pre_bwd")
    du = du_skip
    dabar, dbcat, dccat = [], [], []
    for d in (0, 1):
        perm = to_f if d == 0 else None
        s = saved["states"][d]
        g = ssm_out_dx(dy, ccat[d], perm=perm, name=f"ssm_out{d}_dx")
        lam, da = ssm_scan_bwd(g, s, abar[d], reverse=(d == 1))
        dabar.append(da[0:1])
        dccat.append(ssm_out_dw(s, dy, perm=perm, name=f"ssm_out{d}_dw"))
        du = du + ssm_in_dx(lam, bcat[d], perm=perm, name=f"ssm_in{d}_dx")
        dbcat.append(ssm_in_dw(proj, lam, ucol0=COL["ssm_u"], perm=perm, name=f"ssm_in{d}_dw"))
    d_ops = (jnp.stack(dabar), jnp.stack(dbcat), jnp.stack(dccat))
    return du, d_ops, dd[0, 0] + dd[1, 0], dglu


Z_COLS = tuple(COL[n] // BRANCH for n in ("na_z", "pool_z", "conv_z", "ssm_z"))


def gate_act(o, proj, *, L):
    T = o.shape[0]

    def fn(o, z0, z1, z2, z3):
        return (o * _silu(jnp.concatenate([z0, z1, z2, z3], axis=1).astype(F32)),)

    a, = _ew(fn, [(o, "row", 0, D_MODEL)] + [(proj, "row", c, BRANCH) for c in Z_COLS],
             [(D_MODEL, MM_DTYPE)], [], T=T, L=L, name="gate_act")
    return a


def gate_act_bwd(da, o, proj, *, L):
    T = o.shape[0]

    def fn(da, o, z0, z1, z2, z3):
        z = jnp.concatenate([z0, z1, z2, z3], axis=1).astype(F32)
        return da * _silu(z), da * o * _silu_grad(z)

    return _ew(fn, [(da, "row", 0, D_MODEL), (o, "row", 0, D_MODEL)] + [(proj, "row", c, BRANCH) for c in Z_COLS],
               [(D_MODEL, F32), (D_MODEL, MM_DTYPE)], [], T=T, L=L, name="gate_act_bwd")


MERGE_TN = 512


def merge_fwd(a, w_br, proj, b_gate):
    T = a.shape[0]
    tm, tn = _row_tile(T), MERGE_TN
    nn = D_MODEL // tn
    lb0 = COL["merge"] // tn

    def body(a_ref, w_ref, l_ref, b_ref, m_ref, br_ref, acc_ref):
        i = pl.program_id(2)
        br = _dot(a_ref[...].astype(MM_DTYPE), w_ref[...].astype(MM_DTYPE))
        br_ref[...] = br.astype(br_ref.dtype)
        term = _sigmoid(l_ref[...].astype(F32) + b_ref[...]) * br

        @pl.when(i == 0)
        def _():
            acc_ref[...] = term

        @pl.when(i > 0)
        def _():
            acc_ref[...] += term

        @pl.when(i == 3)
        def _():
            m_ref[...] = acc_ref[...].astype(m_ref.dtype)

    return pl.pallas_call(
        body, out_shape=[jax.ShapeDtypeStruct((T, D_MODEL), MM_DTYPE), jax.ShapeDtypeStruct((T, 4 * D_MODEL), MM_DTYPE)],
        grid=(T // tm, nn, 4),
        in_specs=[pl.BlockSpec((tm, BRANCH), lambda m, n, i: (m, i)),
                  pl.BlockSpec((BRANCH, tn), lambda m, n, i: (i, n)),
                  pl.BlockSpec((tm, tn), lambda m, n, i: (m, lb0 + i * nn + n)),
                  pl.BlockSpec((1, tn), lambda m, n, i: (0, i * nn + n))],
        out_specs=[pl.BlockSpec((tm, tn), lambda m, n, i: (m, n)), pl.BlockSpec((tm, tn), lambda m, n, i: (m, i * nn + n))],
        scratch_shapes=[pltpu.VMEM((tm, tn), F32)],
        compiler_params=_params("parallel", "parallel", "arbitrary"), name="merge_fwd")(a, w_br, proj, b_gate)


def merge_bwd(dmerged, br, proj, b_gate):
    T = dmerged.shape[0]
    tb = ROW_TILE
    lb0 = COL["merge"] // D_MODEL

    def body(dm_ref, br_ref, l_ref, b_ref, dbr_ref, dl_ref, db_ref):
        dm = dm_ref[...]
        gates = _sigmoid(l_ref[...] + b_ref[...])
        dbr_ref[...] = (dm * gates).astype(dbr_ref.dtype)
        dl = dm * br_ref[...] * gates * (1.0 - gates)
        dl_ref[...] = dl.astype(dl_ref.dtype)
        s = jnp.broadcast_to(jnp.sum(dl, axis=0, keepdims=True), db_ref.shape)

        @pl.when(pl.program_id(1) == 0)
        def _():
            db_ref[...] = s

        @pl.when(pl.program_id(1) > 0)
        def _():
            db_ref[...] += s

    wide = pl.BlockSpec((tb, D_MODEL), lambda b, i: (i, b))
    return pl.pallas_call(
        body, out_shape=[jax.ShapeDtypeStruct((T, 4 * D_MODEL), MM_DTYPE)] * 2 + [jax.ShapeDtypeStruct((8, 4 * D_MODEL), F32)],
        grid=(4, T // tb),
        in_specs=[pl.BlockSpec((tb, D_MODEL), lambda b, i: (i, 0)), wide,
                  pl.BlockSpec((tb, D_MODEL), lambda b, i: (i, lb0 + b)), pl.BlockSpec((1, D_MODEL), lambda b, i: (0, b))],
        out_specs=[wide, wide, pl.BlockSpec((8, D_MODEL), lambda b, i: (0, b))],
        compiler_params=_params("parallel", "arbitrary"), name="merge_bwd")(dmerged, br, proj, b_gate)


def branch_dx(dbr, w_br):
    T = dbr.shape[0]
    tm, tk = _row_tile(T), 1024
    nk = D_MODEL // tk
    return _matmul(dbr, w_br, pl.BlockSpec((tm, tk), lambda m, i, k: (m, i * nk + k)),
                   pl.BlockSpec((BRANCH, tk), lambda m, i, k: (i, k)),
                   pl.BlockSpec((tm, BRANCH), lambda m, i, k: (m, i)),
                   jax.ShapeDtypeStruct((T, D_MODEL), F32), (T // tm, 4, nk), tb=True, name="branch_dx")


def branch_dw(a, dbr):
    T = a.shape[0]
    tk, tn = T, 1024
    nn = D_MODEL // tn
    return _matmul(a, dbr, pl.BlockSpec((tk, BRANCH), lambda i, n, k: (k, i)),
                   pl.BlockSpec((tk, tn), lambda i, n, k: (k, i * nn + n)),
                   pl.BlockSpec((BRANCH, tn), lambda i, n, k: (i, n)),
                   jax.ShapeDtypeStruct((D_MODEL, D_MODEL), F32), (4, nn, T // tk), ta=True, name="branch_dw")


def proj_fwd(h, w_in, carried=None):
    T = h.shape[0]
    tm, tn = _row_tile(T), 1792
    per = W_IN_SHARD // tn
    return _matmul(h, w_in, pl.BlockSpec((tm, D_MODEL), lambda i, j, k: (i, 0)),
                   pl.BlockSpec((None, D_MODEL, tn), lambda i, j, k: (j // per, 0, j % per)),
                   pl.BlockSpec((tm, tn), lambda i, j, k: (i, j)),
                   jax.ShapeDtypeStruct((T, IN_TOTAL), MM_DTYPE), (T // tm, IN_TOTAL // tn, 1),
                   name="proj_fwd" if carried is None else "proj_fwd_gather", carried=carried)


def proj_dx(dproj, w_in, carried=None):
    T = dproj.shape[0]
    tm, tk = _row_tile(T), 1792
    per = W_IN_SHARD // tk
    return _matmul(dproj, w_in, pl.BlockSpec((tm, tk), lambda i, j, k: (i, k)),
                   pl.BlockSpec((None, D_MODEL, tk), lambda i, j, k: (k // per, 0, k % per)),
                   pl.BlockSpec((tm, D_MODEL), lambda i, j, k: (i, 0)),
                   jax.ShapeDtypeStruct((T, D_MODEL), F32), (T // tm, 1, IN_TOTAL // tk), tb=True,
                   name="proj_dx" if carried is None else "proj_dx_exchange", carried=carried)


def proj_dw(h, dproj):
    T = h.shape[0]
    tk, tm, tn = T, 512, 512
    per = W_IN_SHARD // tn
    return _matmul(h, dproj, pl.BlockSpec((tk, tm), lambda i, j, k: (k, i)),
                   pl.BlockSpec((tk, tn), lambda i, j, k: (k, j)),
                   pl.BlockSpec((None, tm, tn), lambda i, j, k: (j // per, i, j % per)),
                   jax.ShapeDtypeStruct((N_SHARDS, D_MODEL, W_IN_SHARD), F32),
                   (D_MODEL // tm, IN_TOTAL // tn, T // tk), ta=True, name="proj_dw")


def layer_fwd(X, P, *, L, carried=None):
    h = prenorm_fwd(X, P["g_pre"], P["scale"], P["shift"], L=L)
    proj, extras = proj_fwd(h, P["w_in"], carried) if carried is not None else (proj_fwd(h, P["w_in"]), None)
    o_att, lse = attn_fwd(proj, P["ptiles"], L=L)
    o_att, lse_c = cattn_fwd(proj, o_att, L=L)
    o_pool, pooled = pool_fwd(proj, P["pool_w"], P["pool_scale"], L=L)
    o_conv = conv_fwd(proj, P["conv_w"], L=L)
    o_ssm, ssm_saved = ssm_fwd(proj, P["ssm_ops"], P["ssm_d"], P["glu_w"], L=L)
    o = jnp.concatenate([o_att, o_pool, o_conv, o_ssm], axis=1)
    a = gate_act(o, proj, L=L)
    merged, br = merge_fwd(a, P["w_br"], proj, P["b_gate"])
    y = mm_nn(merged, P["w_o"], out_dtype=F32, name="out_proj")
    Xn = postnorm_fwd(X, y, P["g_post"], P["gate"], L=L)
    saved = dict(X=X, h=h, proj=proj, lse=lse, lse_c=lse_c, pooled=pooled, ssm=ssm_saved, o=o, a=a, merged=merged,
                 br=br, y=y)
    return Xn, saved, extras


def layer_bwd(dXn, P, S, bias_vjp, ops_vjp, *, L, carried=None):
    proj = S["proj"]
    dy, cs_post = postnorm_bwd(dXn, S["y"], P["g_post"], P["gate"], L=L)
    dmerged = mm_nt(dy, P["w_o"], out_dtype=F32, name="out_proj_dx")
    d_w_o = mm_tn(S["merged"], dy, out_dtype=F32, name="out_proj_dw", tm=512, tn=1024)
    dbr, dlogit, d_bgate = merge_bwd(dmerged, S["br"], proj, P["b_gate"])
    da = branch_dx(dbr, P["w_br"])
    d_w_br = branch_dw(S["a"], dbr)
    do, dz = gate_act_bwd(da, S["o"], proj, L=L)
    dq, dk, dv, dptiles = attn_bwd(proj, P["ptiles"], S["o"], do, S["lse"], L=L)
    dq, dk, dv = cattn_bwd(proj, S["o"], do, S["lse_c"], dq, dk, dv, L=L)
    d_rpb, = bias_vjp(dptiles)
    dpool_u, d_pool_w, d_pool_scale = pool_bwd(do[:, BRANCH:2 * BRANCH], S["pooled"], P["pool_w"], P["pool_scale"], L=L)
    dcx, dcb, dcc, d_conv_w = conv_bwd(do[:, 2 * BRANCH:3 * BRANCH], proj, P["conv_w"], L=L)
    dssm_u, d_ops, d_ssm_d, d_glu = ssm_bwd(do[:, 3 * BRANCH:], proj, P["ssm_ops"], P["ssm_d"], P["glu_w"], S["ssm"], L=L)
    d_ssm = ops_vjp(d_ops)
    z = lambda i: dz[:, i * BRANCH:(i + 1) * BRANCH]
    cast = lambda t: t.astype(MM_DTYPE)
    dproj = jnp.concatenate([cast(dq), cast(dk), cast(dv), z(0), dpool_u, z(1),
                             dcx, dcb, dcc, z(2), cast(dssm_u), z(3), dlogit], axis=1)
    dh, extras = proj_dx(dproj, P["w_in"], carried) if carried is not None else (proj_dx(dproj, P["w_in"]), None)
    d_w_in = proj_dw(S["h"], dproj)
    dX, cs_h, cs_hx = prenorm_bwd(dh, S["X"], P["g_pre"], P["scale"], dXn, L=L)
    g_pre, g_post = P["g_pre"], P["g_post"]
    d_shift = cs_h
    d_scale = cs_hx * g_pre
    d_gate = cs_post * g_post
    d_g_pre = jnp.sum(cs_hx * (1.0 + P["scale"]), axis=0)[0]
    d_g_post = jnp.sum(cs_post * P["gate"], axis=0)[0]
    grads = dict(w_in=d_w_in, w_br=d_w_br, w_o=d_w_o, glu_w=d_glu, conv_w=d_conv_w[0:3], pool_w=d_pool_w,
                 pool_scale=d_pool_scale[0], b_gate=d_bgate[0], na_rpb=d_rpb, ssm=d_ssm, ssm_d=d_ssm_d,
                 g_pre=d_g_pre, g_post=d_g_post,
                 mod=jnp.concatenate([d_shift, d_scale, d_gate], axis=-1)[:, 0])
    return dX, grads, extras


BIG_NAMES = ("w_in", "glu_w", "w_br", "w_o")


def gathered_weights(g):
    w_in, glu, w_br, w_o = g
    return dict(w_in=w_in, glu_w=glu.transpose(1, 0, 2).reshape(BRANCH, 2 * BRANCH),
                w_br=w_br.reshape(D_MODEL, D_MODEL), w_o=w_o.reshape(D_MODEL, D_MODEL))


def shard_major_grads(g):
    return [g["w_in"], g["glu_w"].reshape(BRANCH, N_SHARDS, -1).transpose(1, 0, 2),
            g["w_br"].reshape(N_SHARDS, BRANCH, D_MODEL), g["w_o"].reshape(N_SHARDS, BRANCH, D_MODEL)]


def local_step(x, ctx, target, mods, layers, *, shard_halves=None, reduce=False):
    L = x.shape[0]
    nl = len(layers)
    X = jnp.concatenate([x, ctx], axis=0)
    saved, Ps, vjps = [], [], []
    arrived = None
    for i, lay in enumerate(layers):
        P = dict(lay)
        if arrived is not None:
            P.update(gathered_weights(arrived))
        m = mods[i][:, None, :]
        P["shift"], P["scale"], P["gate"] = m[..., :D_MODEL], m[..., D_MODEL:2 * D_MODEL], m[..., 2 * D_MODEL:]
        P["ptiles"], bias_vjp = jax.vjp(bias_pair_tiles, lay["na_rpb"])
        P["ssm_ops"], ops_vjp = jax.vjp(ssm_operators, *lay["ssm_params"])
        ahead = shard_halves is not None and i + 1 < nl
        X, S, arrived = layer_fwd(X, P, L=L, carried=gather_plan(shard_halves[i + 1], shard_major=True) if ahead else None)
        saved.append(S)
        Ps.append(P)
        vjps.append((bias_vjp, ops_vjp))
    loss, dX = loss_and_grad(X, target, L=L)
    grads, reduced = [None] * nl, [None] * nl
    finish = lambda parts: [pair_finish(a, name=f"grad_finish_{n}") for n, a in zip(BIG_NAMES, parts)]
    pending = None
    for i in reversed(range(nl)):
        dX, grads[i], arrived = layer_bwd(dX, Ps[i], saved[i], *vjps[i], L=L,
                                          carried=exchange_plan(pending) if pending is not None else None)
        if pending is not None:
            reduced[i + 1] = finish(arrived)
        if reduce:
            pending = [pair_reduce(a, out_dtype=MM_DTYPE, name=f"grad_pair_{n}")
                       for n, a in zip(BIG_NAMES, shard_major_grads(grads[i]))]
    if reduce:
        reduced[0] = finish(chip_exchange(pending, name="grad_exchange"))
    return loss, dX[:L], grads, reduced


MESH_ID = pl.DeviceIdType.MESH
HBM_SPEC = pl.BlockSpec(memory_space=pltpu.HBM)


def _place():
    return lax.axis_index("x"), lax.axis_index("y"), lax.axis_index("c")


def _other_chips(x, y):
    return [(1 - x, y), (x, 1 - y), (1 - x, 1 - y)]


def _remote(src, dst, send_sem, recv_sem, to):
    return pltpu.make_async_remote_copy(src_ref=src, dst_ref=dst, send_sem=send_sem, recv_sem=recv_sem,
                                        device_id=to, device_id_type=MESH_ID)


DMA_CHUNK_BYTES = 1 << 20


def _row_pieces(rows, row_bytes):
    n = max(1, min(64, rows * row_bytes // DMA_CHUNK_BYTES))
    while n > 1 and (rows % n or (rows // n) % 16):
        n -= 1
    return [(r * (rows // n), rows // n) for r in range(n)]


def _row_bytes(ref):
    return math.prod(ref.shape[1:]) * jnp.dtype(ref.dtype).itemsize


def _start_in_pieces(make, src, dst):
    for r0, nr in _row_pieces(src.shape[0], _row_bytes(src)):
        make(src.at[pl.ds(r0, nr)], dst.at[pl.ds(r0, nr)]).start()


class Carried:
    def __init__(self, ins, out_shapes, sem_shapes, start, mid, finish):
        self.ins, self.out_shapes, self.sem_shapes = list(ins), list(out_shapes), list(sem_shapes)
        self.start, self.mid, self.finish = start, mid, finish


def run_carried(plan, name):
    n_in, n_out = len(plan.ins), len(plan.out_shapes)

    def body(*refs):
        parts = refs[:n_in], refs[n_in:n_in + n_out], refs[n_in + n_out:]
        plan.start(*parts)
        plan.mid(*parts)
        plan.finish(*parts)

    return pl.pallas_call(body, out_shape=plan.out_shapes, in_specs=[HBM_SPEC] * n_in, out_specs=[HBM_SPEC] * n_out,
                          scratch_shapes=plan.sem_shapes, name=name)(*plan.ins)


def gather_plan(blocks, *, shard_major=False):
    K = len(blocks)

    def tools(ins, outs, sems):
        send_sems, recv_sems, local_sems = sems
        x, y, c = _place()

        def slot(k, block):
            px, py, pc = block
            if shard_major:
                h = ins[k].shape[0]
                return outs[k].at[2 * px + py, pl.ds(pl.multiple_of(pc * h, 16), h)]
            return outs[k].at[4 * px + 2 * py + pc]

        def copy(k, j, to):
            return lambda s, d: _remote(s, d, send_sems.at[k, j], recv_sems.at[k, j], to)

        return (x, y, c), (x, y, 1 - c), _other_chips(x, y), c, slot, copy, local_sems

    def start(ins, outs, sems):
        me, sibling, chips, c, slot, copy, local_sems = tools(ins, outs, sems)
        for k in range(K):
            _start_in_pieces(lambda s, d, k=k: pltpu.make_async_copy(s, d, local_sems.at[k]), ins[k], slot(k, me))
            _start_in_pieces(copy(k, 0, sibling), ins[k], slot(k, me))
            for j, chip in enumerate(chips):
                _start_in_pieces(copy(k, 1 + j, (*chip, c)), ins[k], slot(k, me))

    def mid(ins, outs, sems):
        me, sibling, chips, c, slot, copy, _ = tools(ins, outs, sems)
        for j, chip in enumerate(chips):
            for k in range(K):
                got = slot(k, (*chip, c))
                copy(k, 1 + j, me)(got, got).wait_recv()
                _start_in_pieces(copy(k, 4 + j, sibling), got, got)

    def finish(ins, outs, sems):
        me, sibling, chips, c, slot, copy, local_sems = tools(ins, outs, sems)
        for k in range(K):
            sib = slot(k, sibling)
            copy(k, 0, me)(sib, sib).wait_recv()
            for j, chip in enumerate(chips):
                got = slot(k, (*chip, 1 - c))
                copy(k, 4 + j, me)(got, got).wait_recv()
        for k in range(K):
            own = slot(k, me)
            for j in range(4):
                copy(k, j, me)(ins[k], own).wait_send()
            for j, chip in enumerate(chips):
                got = slot(k, (*chip, c))
                copy(k, 4 + j, me)(got, got).wait_send()
            pltpu.make_async_copy(ins[k], own, local_sems.at[k]).wait()

    def out_shape(b):
        if shard_major:
            return jax.ShapeDtypeStruct((N_SHARDS, 2 * b.shape[0]) + b.shape[1:], b.dtype)
        return jax.ShapeDtypeStruct((8,) + b.shape, b.dtype)

    sems = [pltpu.SemaphoreType.DMA((K, 7)), pltpu.SemaphoreType.DMA((K, 7)), pltpu.SemaphoreType.DMA((K,))]
    return Carried(blocks, [out_shape(b) for b in blocks], sems, start, mid, finish)


def all_gather8(blocks, name, *, shard_major=False):
    return run_carried(gather_plan(blocks, shard_major=shard_major), name)


def exchange_plan(parts):
    K = len(parts)

    def copies(ins, outs, sems):
        send_sems, recv_sems, local_sems = sems
        x, y, c = _place()
        p = 2 * x + y
        makes = []
        for k in range(K):
            for j, (cx, cy) in enumerate(_other_chips(x, y)):
                make = lambda s, d, k=k, j=j, to=(cx, cy, c): _remote(s, d, send_sems.at[k, j], recv_sems.at[k, j], to)
                makes.append((k, 2 * cx + cy, make))
        return p, makes, local_sems

    def start(ins, outs, sems):
        p, makes, local_sems = copies(ins, outs, sems)
        for k in range(K):
            _start_in_pieces(lambda s, d, k=k: pltpu.make_async_copy(s, d, local_sems.at[k]), ins[k].at[p], outs[k].at[p])
        for k, q, make in makes:
            _start_in_pieces(make, ins[k].at[q], outs[k].at[p])

    def mid(ins, outs, sems):
        pass

    def finish(ins, outs, sems):
        p, makes, local_sems = copies(ins, outs, sems)
        waits = [make(ins[k].at[q], outs[k].at[q]) for k, q, make in makes]
        for cp in waits:
            cp.wait_recv()
        for cp in waits:
            cp.wait_send()
        for k in range(K):
            pltpu.make_async_copy(ins[k].at[p], outs[k].at[p], local_sems.at[k]).wait()

    sems = [pltpu.SemaphoreType.DMA((K, 3)), pltpu.SemaphoreType.DMA((K, 3)), pltpu.SemaphoreType.DMA((K,))]
    return Carried(parts, [jax.ShapeDtypeStruct(a.shape, a.dtype) for a in parts], sems, start, mid, finish)


def chip_exchange(parts, name):
    return run_carried(exchange_plan(parts), name)


PAIR_TILE_BYTES = 2 << 20


def _pair_rows(h, n):
    return _pick(h, tuple(t for t in (512, 256, 128, 64, 32, 16) if t * n * 4 <= PAIR_TILE_BYTES))


def _core_index():
    return jnp.reshape(lax.axis_index("c"), (1,)).astype(jnp.int32)


def pair_reduce(g, *, out_dtype, name):
    S, R, n = g.shape
    h = R // 2
    tr = _pair_rows(h, n)
    nt = h // tr

    def body(c_ref, keep_ref, give_ref, o_ref, recv_ref, send_sems, recv_sems):
        x, y, c = _place()
        slot = (pl.program_id(0) * nt + pl.program_id(1)) % 2
        cp = _remote(give_ref, recv_ref.at[slot], send_sems.at[slot], recv_sems.at[slot], (x, y, 1 - c))
        cp.start()
        cp.wait_recv()
        o_ref[...] = (keep_ref[...] + recv_ref[slot]).astype(o_ref.dtype)
        cp.wait_send()

    g2 = g.reshape(S * R, n)
    grid_spec = pltpu.PrefetchScalarGridSpec(
        num_scalar_prefetch=1, grid=(S, nt),
        in_specs=[pl.BlockSpec((tr, n), lambda q, i, c: ((2 * q + c[0]) * nt + i, 0)),
                  pl.BlockSpec((tr, n), lambda q, i, c: ((2 * q + 1 - c[0]) * nt + i, 0))],
        out_specs=pl.BlockSpec((tr, n), lambda q, i, c: (q * nt + i, 0)),
        scratch_shapes=[pltpu.VMEM((2, tr, n), F32), pltpu.SemaphoreType.DMA((2,)), pltpu.SemaphoreType.DMA((2,))])
    out = pl.pallas_call(body, out_shape=jax.ShapeDtypeStruct((S * h, n), out_dtype), grid_spec=grid_spec,
                         compiler_params=_params("arbitrary", "arbitrary"), name=name)(_core_index(), g2, g2)
    return out.reshape(S, h, n)


def pair_finish(parts, *, name):
    S, h, n = parts.shape
    tr = _pair_rows(h, n)
    nt = h // tr

    def body(c_ref, p_ref, o_ref, recv_ref, send_sem, recv_sem):
        x, y, c = _place()
        phase, i = pl.program_id(0), pl.program_id(1)
        rows = pl.ds(pl.multiple_of(i * tr, 16), tr)

        @pl.when(phase == 0)
        def _():
            acc = p_ref[0].astype(F32)
            for s in range(1, S):
                acc = acc + p_ref[s].astype(F32)
            o_ref[...] = acc
            cp = _remote(o_ref, recv_ref.at[rows], send_sem, recv_sem, (x, y, 1 - c))
            cp.start()
            cp.wait_send()

        @pl.when((phase == 1) & (i == 0))
        def _():
            _remote(recv_ref, recv_ref, send_sem, recv_sem, (x, y, 1 - c)).wait_recv()

        @pl.when(phase == 1)
        def _():
            o_ref[...] = recv_ref[rows]

    grid_spec = pltpu.PrefetchScalarGridSpec(
        num_scalar_prefetch=1, grid=(2, nt),
        in_specs=[pl.BlockSpec((S, tr, n), lambda ph, i, c: (0, jnp.where(ph == 0, i, nt - 1), 0))],
        out_specs=pl.BlockSpec((tr, n), lambda ph, i, c: (jnp.where(ph == 0, c[0], 1 - c[0]) * nt + i, 0)),
        scratch_shapes=[pltpu.VMEM((h, n), F32), pltpu.SemaphoreType.DMA(()), pltpu.SemaphoreType.DMA(())])
    return pl.pallas_call(body, out_shape=jax.ShapeDtypeStruct((2 * h, n), F32), grid_spec=grid_spec,
                          compiler_params=_params("arbitrary", "arbitrary"), name=name)(_core_index(), parts)


def _slab_rows(n):
    return max(8, min(256, (1 << 18) // n // 8 * 8))


def sum_slabs(a, *, name):
    S, h, n = a.shape
    tr = _pick(h, tuple(t for t in (256, 128, 64, 32, 16, 8) if t <= _slab_rows(n)))

    def body(a_ref, o_ref):
        acc = a_ref[0].astype(F32)
        for s in range(1, S):
            acc = acc + a_ref[s].astype(F32)
        o_ref[...] = acc

    return pl.pallas_call(body, out_shape=jax.ShapeDtypeStruct((h, n), F32), grid=(h // tr,),
                          in_specs=[pl.BlockSpec((S, tr, n), lambda r: (0, r, 0))],
                          out_specs=pl.BlockSpec((tr, n), lambda r: (r, 0)), compiler_params=_params("parallel"),
                          name=name)(a)


def _adam_math(w, g, m, v):
    m = ADAM_B1 * m + (1.0 - ADAM_B1) * g
    v = ADAM_B2 * v + (1.0 - ADAM_B2) * (g * g)
    m_hat = m / (1.0 - ADAM_B1 ** ADAM_STEP)
    v_hat = v / (1.0 - ADAM_B2 ** ADAM_STEP)
    delta = -ADAM_LR * (m_hat / (jnp.sqrt(v_hat) + ADAM_EPS) + ADAM_WD * w)
    return delta, m, v


def adamw(w, g, m, v, *, name):
    R, n = w.shape
    tr = _pick(R, tuple(t for t in (256, 128, 64, 32, 16, 8) if t <= _slab_rows(n)))
    spec = pl.BlockSpec((tr, n), lambda r: (r, 0))

    def body(w_ref, g_ref, m_ref, v_ref, d_ref, nm_ref, nv_ref):
        d, nm, nv = _adam_math(w_ref[...], g_ref[...], m_ref[...], v_ref[...])
        d_ref[...] = d
        nm_ref[...] = nm
        nv_ref[...] = nv

    return pl.pallas_call(body, out_shape=[jax.ShapeDtypeStruct(w.shape, F32)] * 3, grid=(R // tr,),
                          in_specs=[spec] * 4, out_specs=[spec] * 3, compiler_params=_params("parallel"),
                          name=name)(w, g, m, v)


MOD_ROWS = 16


def mod_fwd(cact_in, w_mod, b_mod):
    nl, _, cols = w_mod.shape
    tn = 512

    def body(c_ref, w_ref, b_ref, o_ref):
        o_ref[...] = _dot(_silu(c_ref[...]).astype(MM_DTYPE), w_ref[...].astype(MM_DTYPE)) + b_ref[...]

    return pl.pallas_call(
        body, out_shape=jax.ShapeDtypeStruct((nl, MOD_ROWS, cols), F32), grid=(nl, cols // tn),
        in_specs=[pl.BlockSpec((MOD_ROWS, D_MODEL), lambda i, j: (0, 0)),
                  pl.BlockSpec((None, D_MODEL, tn), lambda i, j: (i, 0, j)),
                  pl.BlockSpec((None, 1, tn), lambda i, j: (i, 0, j))],
        out_specs=pl.BlockSpec((None, MOD_ROWS, tn), lambda i, j: (i, 0, j)),
        compiler_params=_params("parallel", "parallel"), name="mod_fwd")(cact_in, w_mod, b_mod)


def mod_update(c_rows, dmod, w, m, v):
    nl, _, cols = w.shape
    tr, tn = 256, 512
    wspec = pl.BlockSpec((None, tr, tn), lambda i, r, j: (i, r, j))

    def body(c_ref, d_ref, w_ref, m_ref, v_ref, g_ref, dl_ref, nm_ref, nv_ref):
        g = _dot_tn(_silu(c_ref[...]).astype(MM_DTYPE), d_ref[...].astype(MM_DTYPE))
        g_ref[...] = g
        dl, nm, nv = _adam_math(w_ref[...], g, m_ref[...], v_ref[...])
        dl_ref[...] = dl
        nm_ref[...] = nm
        nv_ref[...] = nv

    return pl.pallas_call(
        body, out_shape=[jax.ShapeDtypeStruct(w.shape, F32)] * 4, grid=(nl, D_MODEL // tr, cols // tn),
        in_specs=[pl.BlockSpec((MOD_ROWS, tr), lambda i, r, j: (0, r)),
                  pl.BlockSpec((None, MOD_ROWS, tn), lambda i, r, j: (i, 0, j)), wspec, wspec, wspec],
        out_specs=[wspec] * 4, compiler_params=_params("parallel", "parallel", "parallel"),
        name="mod_update")(c_rows, dmod, w, m, v)


def cctx_partial(dmod_ctx, w_mod, c_ctx):
    nl, _, cols = w_mod.shape
    tk = 512
    per = cols // tk
    part = _matmul(dmod_ctx, w_mod, pl.BlockSpec((8, tk), lambda i, j, k: (0, k)),
                   pl.BlockSpec((None, D_MODEL, tk), lambda i, j, k: (k // per, 0, k % per)),
                   pl.BlockSpec((8, D_MODEL), lambda i, j, k: (0, 0)),
                   jax.ShapeDtypeStruct((8, D_MODEL), F32), (1, 1, nl * per), tb=True, name="cctx_partial")

    def body(p_ref, c_ref, o_ref):
        o_ref[...] = 0.5 * jnp.sum(p_ref[...], axis=0, keepdims=True) * _silu_grad(c_ref[...])

    return pl.pallas_call(body, out_shape=jax.ShapeDtypeStruct((1, D_MODEL), F32), name="cctx_scale")(part, c_ctx)


WEIGHT_NAMES = ("c_ctx", "w_mod", "b_mod", "g_pre", "g_post", "w_in", "b_gate", "na_rpb", "pool_w", "pool_scale",
                "conv_w", "ssm_a_re", "ssm_a_im", "ssm_log_dt", "ssm_b_re", "ssm_b_im", "ssm_c_re", "ssm_c_im",
                "ssm_d", "glu_w", "w_br", "w_o")
SSM_NAMES = ("ssm_a_re", "ssm_a_im", "ssm_log_dt", "ssm_b_re", "ssm_b_im", "ssm_c_re", "ssm_c_im")
SMALL_NAMES = ("c_ctx", "b_mod", "g_pre", "g_post", "b_gate", "na_rpb", "pool_w", "pool_scale") + SSM_NAMES + ("ssm_d",)
FLAT_COLS = 1024


def _flat(parts):
    v = jnp.concatenate([p.reshape(-1) for p in parts])
    pad = -v.shape[0] % (64 * FLAT_COLS)
    return jnp.pad(v, (0, pad)).reshape(-1, FLAT_COLS)


def _unflat(flat, shapes):
    v = flat.reshape(-1)
    out, off = [], 0
    for s in shapes:
        n = math.prod(s)
        out.append(v[off:off + n].reshape(s))
        off += n
    return out


def kernel(x, c, ctx, c_ctx, w_mod, b_mod, g_pre, g_post, w_in, b_gate, na_rpb, pool_w, pool_scale, conv_w, ssm_a_re, ssm_a_im, ssm_log_dt, ssm_b_re, ssm_b_im, ssm_c_re, ssm_c_im, ssm_d, glu_w, w_br, w_o, loss_target, m_c_ctx, m_w_mod, m_b_mod, m_g_pre, m_g_post, m_w_in, m_b_gate, m_na_rpb, m_pool_w, m_pool_scale, m_conv_w, m_ssm_a_re, m_ssm_a_im, m_ssm_log_dt, m_ssm_b_re, m_ssm_b_im, m_ssm_c_re, m_ssm_c_im, m_ssm_d, m_glu_w, m_w_br, m_w_o, v_c_ctx, v_w_mod, v_b_mod, v_g_pre, v_g_post, v_w_in, v_b_gate, v_na_rpb, v_pool_w, v_pool_scale, v_conv_w, v_ssm_a_re, v_ssm_a_im, v_ssm_log_dt, v_ssm_b_re, v_ssm_b_im, v_ssm_c_re, v_ssm_c_im, v_ssm_d, v_glu_w, v_w_br, v_w_o):
    W = dict(c_ctx=c_ctx, w_mod=w_mod, b_mod=b_mod, g_pre=g_pre, g_post=g_post, w_in=w_in, b_gate=b_gate,
             na_rpb=na_rpb, pool_w=pool_w, pool_scale=pool_scale, conv_w=conv_w, ssm_a_re=ssm_a_re, ssm_a_im=ssm_a_im,
             ssm_log_dt=ssm_log_dt, ssm_b_re=ssm_b_re, ssm_b_im=ssm_b_im, ssm_c_re=ssm_c_re, ssm_c_im=ssm_c_im,
             ssm_d=ssm_d, glu_w=glu_w, w_br=w_br, w_o=w_o)
    M = dict(c_ctx=m_c_ctx, w_mod=m_w_mod, b_mod=m_b_mod, g_pre=m_g_pre, g_post=m_g_post, w_in=m_w_in, b_gate=m_b_gate,
             na_rpb=m_na_rpb, pool_w=m_pool_w, pool_scale=m_pool_scale, conv_w=m_conv_w, ssm_a_re=m_ssm_a_re,
             ssm_a_im=m_ssm_a_im, ssm_log_dt=m_ssm_log_dt, ssm_b_re=m_ssm_b_re, ssm_b_im=m_ssm_b_im,
             ssm_c_re=m_ssm_c_re, ssm_c_im=m_ssm_c_im, ssm_d=m_ssm_d, glu_w=m_glu_w, w_br=m_w_br, w_o=m_w_o)
    V = dict(c_ctx=v_c_ctx, w_mod=v_w_mod, b_mod=v_b_mod, g_pre=v_g_pre, g_post=v_g_post, w_in=v_w_in, b_gate=v_b_gate,
             na_rpb=v_na_rpb, pool_w=v_pool_w, pool_scale=v_pool_scale, conv_w=v_conv_w, ssm_a_re=v_ssm_a_re,
             ssm_a_im=v_ssm_a_im, ssm_log_dt=v_ssm_log_dt, ssm_b_re=v_ssm_b_re, ssm_b_im=v_ssm_b_im,
             ssm_c_re=v_ssm_c_re, ssm_c_im=v_ssm_c_im, ssm_d=v_ssm_d, glu_w=v_glu_w, w_br=v_w_br, w_o=v_w_o)
    nl = w_in.shape[0]
    xi, yi, ci = _place()
    chip = 2 * xi + yi
    example = 4 * xi + 2 * yi + ci
    mod_cols = w_mod.shape[2]

    def my_half(a):
        half = a.shape[0] // 2
        return lax.dynamic_slice_in_dim(a, ci * half, half, axis=0).astype(MM_DTYPE)

    halves = [[my_half(W[n][i]) for n in BIG_NAMES] for i in range(nl)]
    first = gathered_weights(all_gather8(halves[0], name="gather_weights", shard_major=True))

    c8 = jnp.pad(c, ((0, 7), (0, 0)))
    c_all, = all_gather8([c8], name="gather_c")
    c_rows = jnp.concatenate([c_all[:, 0], jnp.broadcast_to(c_ctx[None], (8, D_MODEL))], axis=0)
    b_cols = lax.dynamic_slice_in_dim(b_mod, chip * mod_cols, mod_cols, axis=1)[:, None]
    mod_part = mod_fwd(c_rows, w_mod, b_cols)
    conv_part = jnp.pad(conv_w.reshape(nl * 3, -1), ((0, 16 - nl * 3), (0, 0)))
    parts, conv_all = all_gather8([mod_part.reshape(nl * MOD_ROWS, mod_cols), conv_part], name="gather_mod")
    mod_full = jnp.concatenate([parts[2 * p].reshape(nl, MOD_ROWS, mod_cols) for p in range(N_SHARDS)], axis=-1)
    conv_full = jnp.concatenate([conv_all[2 * p][:nl * 3].reshape(nl, 3, -1) for p in range(N_SHARDS)], axis=-1)
    own = lax.dynamic_index_in_dim(mod_full, example, axis=1, keepdims=False)
    mods = [jnp.stack([own[i], mod_full[i, 8]]) for i in range(nl)]

    layers = []
    for i in range(nl):
        layers.append(dict(
            conv_w=jnp.pad(conv_full[i], ((0, 5), (0, 0))), pool_w=pool_w[i], pool_scale=pool_scale[i][None],
            b_gate=b_gate[i][None], g_pre=g_pre[i][None], g_post=g_post[i][None], na_rpb=na_rpb[i],
            ssm_d=ssm_d[i][None], ssm_params=tuple(W[n][i] for n in SSM_NAMES)))
    layers[0].update(first)

    loss_local, grad_x, grads, reduced = local_step(x[0], ctx[0], loss_target[0], mods, layers,
                                                    shard_halves=halves, reduce=True)
    loss = lax.psum(loss_local, ("x", "y", "c"))

    dmod_local = jnp.stack([g["mod"] for g in grads])
    dmod_all, = all_gather8([jnp.pad(dmod_local.reshape(nl * 2, -1), ((0, 8 - nl * 2), (0, 0)))], name="gather_dmod")
    dmod_all = dmod_all[:, :nl * 2].reshape(8, nl, 2, 3 * D_MODEL)
    dmod_rows = jnp.concatenate([dmod_all[:, :, 0], dmod_all[:, :, 1]], axis=0).transpose(1, 0, 2)
    dmod_cols = lax.dynamic_slice_in_dim(dmod_rows, chip * mod_cols, mod_cols, axis=2)
    g_w_mod, d_w_mod, nm_w_mod, nv_w_mod = mod_update(c_rows, dmod_cols, w_mod, m_w_mod, v_w_mod)
    dctx_cols = dmod_cols[:, 8:].transpose(1, 0, 2).reshape(8, nl * mod_cols)
    g_cctx_part = cctx_partial(dctx_cols, w_mod, c_ctx[None])[0]

    def small_grad(n):
        if n == "c_ctx":
            return g_cctx_part
        if n == "b_mod":
            return jnp.stack([g["mod"][0] + g["mod"][1] for g in grads])
        if n in SSM_NAMES:
            return jnp.stack([g["ssm"][SSM_NAMES.index(n)] for g in grads])
        return jnp.stack([g[n] for g in grads])

    conv_grad_full = jnp.stack([g["conv_w"] for g in grads])
    flat_g = _flat([small_grad(n) for n in SMALL_NAMES] + [conv_grad_full])

    G, DL, NM, NV = {}, {}, {}, {}
    for k, n in enumerate(BIG_NAMES):
        g = jnp.stack([reduced[i][k] for i in range(nl)])
        rows = g.shape[0] * g.shape[1]
        d, nm, nv = adamw(W[n].reshape(rows, -1), g.reshape(rows, -1), M[n].reshape(rows, -1), V[n].reshape(rows, -1),
                          name=f"adamw_{n}")
        G[n], DL[n], NM[n], NV[n] = g, d.reshape(g.shape), nm.reshape(g.shape), nv.reshape(g.shape)

    flat_all, = all_gather8([flat_g], name="gather_small_grads")
    flat_sum = sum_slabs(flat_all, name="sum_small_grads")
    small_shapes = [W[n].shape for n in SMALL_NAMES]
    small_g = _unflat(flat_sum, small_shapes + [conv_grad_full.shape])
    conv_g = lax.dynamic_slice_in_dim(small_g[-1], chip * conv_w.shape[2], conv_w.shape[2], axis=2)
    adam_names = SMALL_NAMES + ("conv_w",)
    adam_shapes = small_shapes + [conv_w.shape]
    g_list = small_g[:-1] + [conv_g]
    upd = adamw(_flat([W[n] for n in adam_names]), _flat(g_list), _flat([M[n] for n in adam_names]),
                _flat([V[n] for n in adam_names]), name="adamw_small")
    G.update(zip(adam_names, g_list))
    for group, u in zip((DL, NM, NV), upd):
        group.update(zip(adam_names, _unflat(u, adam_shapes)))
    G["w_mod"], DL["w_mod"], NM["w_mod"], NV["w_mod"] = g_w_mod, d_w_mod, nm_w_mod, nv_w_mod

    out = [loss, grad_x[None]]
    for group in (G, DL, NM, NV):
        out += [group[n].reshape(W[n].shape) for n in WEIGHT_NAMES]
    return tuple(out)
```

```python
import functools
import math

import numpy as np
import jax
import jax.numpy as jnp
from jax import lax
from jax.experimental import pallas as pl
from jax.experimental.pallas import tpu as pltpu

F32 = jnp.float32
BF16 = jnp.bfloat16
MM_DTYPE = jnp.bfloat16

D_MODEL = 2048
BRANCH = 512
N_HEADS = 8
HEAD_DIM = 64
GRID_W = 64
WIN_ROWS = 8
WIN_COLS = 16
POOL_GROUPS = 4
POOL_DIM = 128
SSM_GROUPS = 32
SSM_GDIM = 16
SSM_STATE = 64
N_STATE = SSM_GROUPS * SSM_STATE
IN_TOTAL = 14336
RMS_EPS = 1e-6
NEG_INF = -1e30
COL = dict(q=0, k=512, v=1024, na_z=1536, pool_u=2048, pool_z=2560, conv_x=3072, conv_b=3584,
           conv_c=4096, conv_z=4608, ssm_u=5120, ssm_z=5632, merge=6144)
N_SHARDS = 4
W_IN_SHARD = IN_TOTAL // N_SHARDS
VMEM_LIMIT_BYTES = 48 * 1024 * 1024
ROW_TILE = 256

ADAM_LR = 0.001
ADAM_B1 = 0.9
ADAM_B2 = 0.999
ADAM_EPS = 1e-08
ADAM_WD = 0.01
ADAM_STEP = 10


def _params(*sem):
    return pltpu.CompilerParams(dimension_semantics=sem, vmem_limit_bytes=VMEM_LIMIT_BYTES)


def _sigmoid(x):
    return 1.0 / (1.0 + jnp.exp(-x))


def _matmul(a, b, a_spec, b_spec, o_spec, out_shape, grid, *, ta=False, tb=False, name, carried=None):
    nk = grid[-1]
    kaxis = len(grid) - 1
    dims = (((0,) if ta else (1,), (1,) if tb else (0,)), ((), ()))
    n_acc = 0 if nk == 1 else 1

    def compute(a_ref, b_ref, o_ref, acc):
        p = lax.dot_general(a_ref[...].astype(MM_DTYPE), b_ref[...].astype(MM_DTYPE), dims,
                            preferred_element_type=F32)
        if nk == 1:
            o_ref[...] = p.astype(o_ref.dtype)
            return
        acc_ref, = acc
        k = pl.program_id(kaxis)

        @pl.when(k == 0)
        def _():
            acc_ref[...] = p

        @pl.when(k > 0)
        def _():
            acc_ref[...] += p

        @pl.when(k == nk - 1)
        def _():
            o_ref[...] = acc_ref[...].astype(o_ref.dtype)

    oblock = tuple(s for s in o_spec.block_shape if s is not None)
    scratch = [] if nk == 1 else [pltpu.VMEM(oblock, F32)]
    if carried is None:
        def body(a_ref, b_ref, o_ref, *acc):
            compute(a_ref, b_ref, o_ref, acc)

        sem = ("parallel",) * (len(grid) - 1) + ("arbitrary",)
        return pl.pallas_call(body, out_shape=out_shape, grid=grid, in_specs=[a_spec, b_spec],
                              out_specs=o_spec, scratch_shapes=scratch, compiler_params=_params(*sem),
                              name=name)(a, b)

    n_in, n_out = len(carried.ins), len(carried.out_shapes)
    steps = math.prod(grid)

    def body(a_ref, b_ref, *rest):
        c_ins, o_ref, c_outs = rest[:n_in], rest[n_in], rest[n_in + 1:n_in + 1 + n_out]
        acc, sems = rest[n_in + 1 + n_out:][:n_acc], rest[n_in + 1 + n_out + n_acc:]
        step = pl.program_id(0)
        for ax in range(1, len(grid)):
            step = step * grid[ax] + pl.program_id(ax)

        @pl.when(step == 0)
        def _():
            carried.start(c_ins, c_outs, sems)

        compute(a_ref, b_ref, o_ref, acc)

        @pl.when(step == steps // 2)
        def _():
            carried.mid(c_ins, c_outs, sems)

        @pl.when(step == steps - 1)
        def _():
            carried.finish(c_ins, c_outs, sems)

    res = pl.pallas_call(body, out_shape=[out_shape] + list(carried.out_shapes), grid=grid,
                         in_specs=[a_spec, b_spec] + [HBM_SPEC] * n_in, out_specs=[o_spec] + [HBM_SPEC] * n_out,
                         scratch_shapes=scratch + list(carried.sem_shapes),
                         compiler_params=_params(*(("arbitrary",) * len(grid))), name=name)(a, b, *carried.ins)
    return res[0], res[1:]


def _pick(n, cands):
    for c in cands:
        if n % c == 0:
            return c
    raise ValueError(f"no tile for {n}")


def _row_tile(T):
    return _pick(T, (544, 512, 256, 128))


def mm_nn(a, b, *, out_dtype, name, tn=512, a_rows=None, o_rows=None, a_cols=None):
    M = a.shape[0]
    c0, K = a_cols or (0, a.shape[1])
    N = b.shape[1]
    tm = ROW_TILE if (a_rows or o_rows) else _row_tile(M)
    tn = min(tn, N)
    tk = K if K <= 2048 else _pick(K, (2048, 1024, 512))
    kb0 = c0 // tk
    ar = a_rows or (lambda i: i)
    orr = o_rows or (lambda i: i)
    return _matmul(a, b, pl.BlockSpec((tm, tk), lambda i, j, k: (ar(i), kb0 + k)),
                   pl.BlockSpec((tk, tn), lambda i, j, k: (k, j)),
                   pl.BlockSpec((tm, tn), lambda i, j, k: (orr(i), j)),
                   jax.ShapeDtypeStruct((M, N), out_dtype), (M // tm, N // tn, K // tk), name=name)


def mm_nt(a, b, *, out_dtype, name, a_rows=None, o_rows=None):
    M, K = a.shape
    N = b.shape[0]
    tm = ROW_TILE if (a_rows or o_rows) else _row_tile(M)
    tn = min(N, 2048)
    tk = K if K <= 1024 else _pick(K, (1024, 512))
    ar = a_rows or (lambda i: i)
    orr = o_rows or (lambda i: i)
    return _matmul(a, b, pl.BlockSpec((tm, tk), lambda i, j, k: (ar(i), k)),
                   pl.BlockSpec((tn, tk), lambda i, j, k: (j, k)),
                   pl.BlockSpec((tm, tn), lambda i, j, k: (orr(i), j)),
                   jax.ShapeDtypeStruct((M, N), out_dtype), (M // tm, N // tn, K // tk), tb=True, name=name)


def mm_tn(a, b, *, out_dtype, name, a_rows=None, b_rows=None, tm=512, tn=1024, a_cols=None):
    K = a.shape[0]
    c0, M = a_cols or (0, a.shape[1])
    N = b.shape[1]
    tk = ROW_TILE if (a_rows or b_rows) else K
    tm = min(tm, M)
    tn = min(tn, N)
    mb0 = c0 // tm
    ar = a_rows or (lambda k: k)
    br = b_rows or (lambda k: k)
    return _matmul(a, b, pl.BlockSpec((tk, tm), lambda i, j, k: (ar(k), mb0 + i)),
                   pl.BlockSpec((tk, tn), lambda i, j, k: (br(k), j)),
                   pl.BlockSpec((tm, tn), lambda i, j, k: (i, j)),
                   jax.ShapeDtypeStruct((M, N), out_dtype), (M // tm, N // tn, K // tk), ta=True, name=name)


def _ew(fn, ins, outs, colsums, *, T, L, name):
    tb = ROW_TILE
    nlat = L // tb
    seg = lambda i: jnp.where(i >= nlat, 1, 0)
    in_specs, arrays = [], []
    for arr, kind, cb, width in ins:
        arrays.append(arr)
        if kind == "row":
            in_specs.append(pl.BlockSpec((tb, width), lambda i, cb=cb: (i, cb)))
        elif kind == "bcast":
            in_specs.append(pl.BlockSpec((1, width), lambda i, cb=cb: (0, cb)))
        else:
            in_specs.append(pl.BlockSpec((None, 1, width), lambda i, cb=cb: (seg(i), 0, cb)))
    out_specs = [pl.BlockSpec((tb, w), lambda i: (i, 0)) for w, _ in outs]
    out_shapes = [jax.ShapeDtypeStruct((T, w), dt) for w, dt in outs]
    out_specs += [pl.BlockSpec((None, 1, w), lambda i: (seg(i), 0, 0)) for w in colsums]
    out_shapes += [jax.ShapeDtypeStruct((2, 1, w), F32) for w in colsums]
    n_in, n_out = len(ins), len(outs)

    def body(*refs):
        i = pl.program_id(0)
        res = fn(*[r[...] for r in refs[:n_in]])
        for r, v in zip(refs[n_in:n_in + n_out], res[:n_out]):
            r[...] = v.astype(r.dtype)
        first = (i == 0) | (i == nlat)
        for r, v in zip(refs[n_in + n_out:], res[n_out:]):
            s = jnp.sum(v, axis=0, keepdims=True)

            @pl.when(first)
            def _(r=r, s=s):
                r[...] = s

            @pl.when(jnp.logical_not(first))
            def _(r=r, s=s):
                r[...] += s

    res = pl.pallas_call(body, out_shape=out_shapes, grid=(T // tb,), in_specs=in_specs,
                         out_specs=out_specs, compiler_params=_params("arbitrary"), name=name)(*arrays)
    return res


def _rms(x):
    return lax.rsqrt(jnp.mean(x * x, axis=-1, keepdims=True) + RMS_EPS)


def prenorm_fwd(X, g, scale, shift, *, L):
    T = X.shape[0]

    def fn(x, g, sc, sh):
        return ((x * _rms(x)) * (g * (1.0 + sc)) + sh,)

    h, = _ew(fn, [(X, "row", 0, D_MODEL), (g, "bcast", 0, D_MODEL), (scale, "seg", 0, D_MODEL),
                  (shift, "seg", 0, D_MODEL)], [(D_MODEL, MM_DTYPE)], [], T=T, L=L, name="prenorm_fwd")
    return h


def prenorm_bwd(dh, X, g, scale, dres, *, L):
    T = X.shape[0]

    def fn(dh, x, g, sc, dres):
        r = _rms(x)
        xn = x * r
        dxn = dh * (g * (1.0 + sc))
        dx = r * (dxn - xn * jnp.mean(dxn * xn, axis=-1, keepdims=True))
        return dres + dx, dh, dh * xn

    return _ew(fn, [(dh, "row", 0, D_MODEL), (X, "row", 0, D_MODEL), (g, "bcast", 0, D_MODEL),
                    (scale, "seg", 0, D_MODEL), (dres, "row", 0, D_MODEL)],
               [(D_MODEL, F32)], [D_MODEL, D_MODEL], T=T, L=L, name="prenorm_bwd")


def postnorm_fwd(X, y, g, gate, *, L):
    T = X.shape[0]

    def fn(x, y, g, gate):
        return (x + gate * ((y * _rms(y)) * g),)

    out, = _ew(fn, [(X, "row", 0, D_MODEL), (y, "row", 0, D_MODEL), (g, "bcast", 0, D_MODEL),
                    (gate, "seg", 0, D_MODEL)], [(D_MODEL, F32)], [], T=T, L=L, name="postnorm_fwd")
    return out


def postnorm_bwd(dX, y, g, gate, *, L):
    T = dX.shape[0]

    def fn(dx, y, g, gate):
        r = _rms(y)
        yn = y * r
        dyn = dx * (gate * g)
        dy = r * (dyn - yn * jnp.mean(dyn * yn, axis=-1, keepdims=True))
        return dy, dx * yn

    return _ew(fn, [(dX, "row", 0, D_MODEL), (y, "row", 0, D_MODEL), (g, "bcast", 0, D_MODEL),
                    (gate, "seg", 0, D_MODEL)], [(D_MODEL, MM_DTYPE)], [D_MODEL], T=T, L=L, name="postnorm_bwd")


def loss_and_grad(X, target, *, L):
    T = X.shape[0]
    tb = ROW_TILE
    nlat = L // tb

    def body(x_ref, t_ref, dx_ref, part_ref):
        i = pl.program_id(0)

        @pl.when(i < nlat)
        def _():
            err = x_ref[...] - t_ref[...]
            dx_ref[...] = err * (1.0 / D_MODEL)
            part_ref[...] = jnp.full(part_ref.shape, 0.5 / D_MODEL * jnp.sum(err * err), F32)

        @pl.when(i >= nlat)
        def _():
            dx_ref[...] = jnp.zeros(dx_ref.shape, F32)
            part_ref[...] = jnp.zeros(part_ref.shape, F32)

    dx, part = pl.pallas_call(
        body, out_shape=[jax.ShapeDtypeStruct((T, D_MODEL), F32), jax.ShapeDtypeStruct((T // tb, 8, 128), F32)],
        grid=(T // tb,),
        in_specs=[pl.BlockSpec((tb, D_MODEL), lambda i: (i, 0)),
                  pl.BlockSpec((tb, D_MODEL), lambda i: (jnp.minimum(i, nlat - 1), 0))],
        out_specs=[pl.BlockSpec((tb, D_MODEL), lambda i: (i, 0)), pl.BlockSpec((None, 8, 128), lambda i: (i, 0, 0))],
        compiler_params=_params("parallel"), name="loss_and_grad")(X, target)
    return jnp.sum(part[:, 0, 0]), dx


Q_BLOCK = WIN_ROWS * GRID_W
BAND = 2 * WIN_ROWS * GRID_W


PAIR_TILES = 2 * WIN_ROWS
HEAD_PAIRS = N_HEADS // 2
LANES = 2 * HEAD_DIM
ROW_SHIFT = GRID_W.bit_length() - 1


def bias_pair_tiles(rpb):
    col = np.arange(GRID_W)
    col_start = np.clip(col - WIN_COLS // 2, 0, GRID_W - WIN_COLS)
    in_win = (col[None, :] >= col_start[:, None]) & (col[None, :] < col_start[:, None] + WIN_COLS)
    dcol = np.clip(col[None, :] - col[:, None] + (WIN_COLS - 1), 0, 2 * WIN_COLS - 2)
    E = np.stack([(dcol == dc) & in_win for dc in range(2 * WIN_COLS - 1)]).astype(np.float32)
    tiles = jnp.einsum("...rd,dqk->...rqk", rpb, E, precision=lax.Precision.HIGHEST)
    z = jnp.zeros(tiles.shape[:-3] + (1, GRID_W, GRID_W), F32)
    return jnp.concatenate([jnp.concatenate([z, tiles], axis=-3), jnp.concatenate([tiles, z], axis=-3)], axis=-1)


def _band_row(i, rows):
    return jnp.clip(WIN_ROWS * i - WIN_ROWS // 2, 0, rows - 2 * WIN_ROWS)


def _band_start(i, rows):
    return pl.multiple_of(_band_row(i, rows) * GRID_W, 256)


def _window_mask(i, rows):
    r = lax.broadcasted_iota(jnp.int32, (Q_BLOCK, BAND), 0)
    k = lax.broadcasted_iota(jnp.int32, (Q_BLOCK, BAND), 1)
    qr, qc = WIN_ROWS * i + (r >> ROW_SHIFT), r & (GRID_W - 1)
    kr, kc = _band_row(i, rows) + (k >> ROW_SHIFT), k & (GRID_W - 1)
    ws = jnp.clip(qr - WIN_ROWS // 2, 0, rows - WIN_ROWS)
    cs = jnp.clip(qc - WIN_COLS // 2, 0, GRID_W - WIN_COLS)
    return (kr >= ws) & (kr < ws + WIN_ROWS) & (kc >= cs) & (kc < cs + WIN_COLS)


def _pair_index(i, rows, a, j):
    off = _band_row(i, rows) - WIN_ROWS * i
    return jnp.clip(2 * j - a + WIN_ROWS + off, 0, PAIR_TILES - 1)


def _band_bias(p_ref, hh, i, rows):
    bands = [jnp.concatenate([p_ref[hh, _pair_index(i, rows, a, j)] for j in range(WIN_ROWS)], axis=1)
             for a in range(WIN_ROWS)]
    return jnp.concatenate(bands, axis=0)


def _dot_nt(a, b):
    return lax.dot_general(a, b, (((1,), (1,)), ((), ())), preferred_element_type=F32)


def _dot_tn(a, b):
    return lax.dot_general(a, b, (((0,), (0,)), ((), ())), preferred_element_type=F32)


def _dot(a, b):
    return jnp.dot(a, b, preferred_element_type=F32)


QKV_BLOCKS = tuple(COL[n] // LANES for n in ("q", "k", "v"))
SCALE = HEAD_DIM ** -0.5


def _head(x, hh):
    return x[:, hh * HEAD_DIM:(hh + 1) * HEAD_DIM]


def _both(fn):
    res = [fn(0), fn(1)]
    return [jnp.concatenate([a, b], axis=1) for a, b in zip(*res)]


def attn_fwd(proj, ptiles, *, L):
    T = proj.shape[0]
    N = T - L
    rows, nq = L // GRID_W, L // Q_BLOCK
    qb, kb, vb = QKV_BLOCKS

    def body(q_ref, k_ref, v_ref, p_ref, o_ref, lse_ref):
        i = pl.program_id(1)
        ks = _band_start(i, rows)
        mask = _window_mask(i, rows)
        qv = q_ref[...].astype(MM_DTYPE)
        kband, vband = k_ref[pl.ds(ks, BAND), :].astype(MM_DTYPE), v_ref[pl.ds(ks, BAND), :].astype(MM_DTYPE)
        kctx, vctx = k_ref[pl.ds(L, N), :].astype(MM_DTYPE), v_ref[pl.ds(L, N), :].astype(MM_DTYPE)

        def head(hh):
            q = _head(qv, hh)
            sb = _dot_nt(q, _head(kband, hh)) * SCALE + jnp.where(mask, _band_bias(p_ref, hh, i, rows), NEG_INF)
            sc = _dot_nt(q, _head(kctx, hh)) * SCALE
            m = jnp.maximum(jnp.max(sb, axis=-1, keepdims=True), jnp.max(sc, axis=-1, keepdims=True))
            pb, pc = jnp.exp(sb - m), jnp.exp(sc - m)
            l = jnp.sum(pb, axis=-1, keepdims=True) + jnp.sum(pc, axis=-1, keepdims=True)
            o = _dot(pb.astype(MM_DTYPE), _head(vband, hh)) + _dot(pc.astype(MM_DTYPE), _head(vctx, hh))
            return o / l, jnp.broadcast_to(m + jnp.log(l), (Q_BLOCK, HEAD_DIM))

        o_ref[...], lse_ref[...] = _both(head)

    qspec = lambda b0: pl.BlockSpec((Q_BLOCK, LANES), lambda hp, i: (i, b0 + hp))
    kspec = lambda b0: pl.BlockSpec((T, LANES), lambda hp, i: (0, b0 + hp))
    return pl.pallas_call(
        body, out_shape=[jax.ShapeDtypeStruct((T, BRANCH), F32), jax.ShapeDtypeStruct((L, BRANCH), F32)],
        grid=(HEAD_PAIRS, nq),
        in_specs=[qspec(qb), kspec(kb), kspec(vb),
                  pl.BlockSpec((2, PAIR_TILES, GRID_W, LANES), lambda hp, i: (hp, 0, 0, 0))],
        out_specs=[qspec(0), qspec(0)],
        compiler_params=_params("parallel", "arbitrary"), name="attn_fwd")(proj, proj, proj, ptiles)


def attn_bwd(proj, ptiles, o, do, lse, *, L):
    T = proj.shape[0]
    N = T - L
    rows, nq = L // GRID_W, L // Q_BLOCK
    qb, kb, vb = QKV_BLOCKS

    def body(q_ref, k_ref, v_ref, p_ref, o_ref, do_ref, lse_ref, dq_ref, dk_ref, dv_ref, dp_ref):
        i = pl.program_id(1)
        ks = _band_start(i, rows)

        @pl.when(i == 0)
        def _():
            dk_ref[...] = jnp.zeros(dk_ref.shape, F32)
            dv_ref[...] = jnp.zeros(dv_ref.shape, F32)
            dp_ref[...] = jnp.zeros(dp_ref.shape, F32)

        mask = _window_mask(i, rows)
        qv = q_ref[...].astype(MM_DTYPE)
        kband, vband = k_ref[pl.ds(ks, BAND), :].astype(MM_DTYPE), v_ref[pl.ds(ks, BAND), :].astype(MM_DTYPE)
        kctx, vctx = k_ref[pl.ds(L, N), :].astype(MM_DTYPE), v_ref[pl.ds(L, N), :].astype(MM_DTYPE)
        ov, dof, lsev = o_ref[...], do_ref[...], lse_ref[...]

        def head(hh):
            q, kb_h, kc_h, vb_h, vc_h = (_head(t, hh) for t in (qv, kband, kctx, vband, vctx))
            lse = _head(lsev, hh)[:, 0:1]
            pb = jnp.exp(_dot_nt(q, kb_h) * SCALE + jnp.where(mask, _band_bias(p_ref, hh, i, rows), NEG_INF) - lse)
            pc = jnp.exp(_dot_nt(q, kc_h) * SCALE - lse)
            do_h = _head(dof, hh)
            delta = jnp.sum(do_h * _head(ov, hh), axis=-1, keepdims=True)
            dov = do_h.astype(MM_DTYPE)
            dsb = pb * (_dot_nt(dov, vb_h) - delta)
            dsc = pc * (_dot_nt(dov, vc_h) - delta)
            for a in range(WIN_ROWS):
                for j in range(WIN_ROWS):
                    dp_ref[hh, _pair_index(i, rows, a, j)] += dsb[a * GRID_W:(a + 1) * GRID_W, j * LANES:(j + 1) * LANES]
            dsb_s, dsc_s = (dsb * SCALE).astype(MM_DTYPE), (dsc * SCALE).astype(MM_DTYPE)
            dq = _dot(dsb_s, kb_h) + _dot(dsc_s, kc_h)
            return (dq, _dot_tn(dsb_s, q), _dot_tn(dsc_s, q), _dot_tn(pb.astype(MM_DTYPE), dov),
                    _dot_tn(pc.astype(MM_DTYPE), dov))

        dq, dkb, dkc, dvb, dvc = _both(head)
        dq_ref[...] = dq
        dk_ref[pl.ds(ks, BAND), :] += dkb
        dk_ref[pl.ds(L, N), :] += dkc
        dv_ref[pl.ds(ks, BAND), :] += dvb
        dv_ref[pl.ds(L, N), :] += dvc

    qspec = lambda b0: pl.BlockSpec((Q_BLOCK, LANES), lambda hp, i: (i, b0 + hp))
    kspec = lambda b0: pl.BlockSpec((T, LANES), lambda hp, i: (0, b0 + hp))
    pspec = pl.BlockSpec((2, PAIR_TILES, GRID_W, LANES), lambda hp, i: (hp, 0, 0, 0))
    return pl.pallas_call(
        body,
        out_shape=[jax.ShapeDtypeStruct((T, BRANCH), F32)] * 3 + [jax.ShapeDtypeStruct(ptiles.shape, F32)],
        grid=(HEAD_PAIRS, nq),
        in_specs=[qspec(qb), kspec(kb), kspec(vb), pspec, qspec(0), qspec(0), qspec(0)],
        out_specs=[qspec(0), kspec(0), kspec(0), pspec],
        compiler_params=_params("parallel", "arbitrary"), name="attn_bwd")(proj, proj, proj, ptiles, o, do, lse)


ANY_SPEC = pl.BlockSpec(memory_space=pl.ANY)


def cattn_fwd(proj, o, *, L):
    T = proj.shape[0]
    N = T - L
    qb, kb, vb = QKV_BLOCKS
    cspec = lambda b0: pl.BlockSpec((N, LANES), lambda hp: (L // N, b0 + hp))

    def body(q_ref, k_ref, v_ref, o_in, o_ref, lse_ref):
        qv, kv, vv = (r[...].astype(MM_DTYPE) for r in (q_ref, k_ref, v_ref))

        def head(hh):
            s = _dot_nt(_head(qv, hh), _head(kv, hh)) * SCALE
            m = jnp.max(s, axis=-1, keepdims=True)
            p = jnp.exp(s - m)
            l = jnp.sum(p, axis=-1, keepdims=True)
            return _dot(p.astype(MM_DTYPE), _head(vv, hh)) / l, jnp.broadcast_to(m + jnp.log(l), (N, HEAD_DIM))

        o_ref[...], lse_ref[...] = _both(head)

    return pl.pallas_call(
        body, out_shape=[jax.ShapeDtypeStruct(o.shape, F32), jax.ShapeDtypeStruct((N, BRANCH), F32)],
        grid=(HEAD_PAIRS,), in_specs=[cspec(qb), cspec(kb), cspec(vb), ANY_SPEC],
        out_specs=[cspec(0), pl.BlockSpec((N, LANES), lambda hp: (0, hp))], input_output_aliases={3: 0},
        compiler_params=_params("parallel"), name="cattn_fwd")(proj, proj, proj, o)


def cattn_bwd(proj, o, do, lse, dq, dk, dv, *, L):
    T = proj.shape[0]
    N = T - L
    qb, kb, vb = QKV_BLOCKS
    cspec = lambda b0: pl.BlockSpec((N, LANES), lambda hp: (L // N, b0 + hp))

    def body(q_ref, k_ref, v_ref, o_ref, do_ref, lse_ref, dq_in, dk_in, dv_in, dq_ref, dk_ref, dv_ref):
        qv, kv, vv = (r[...].astype(MM_DTYPE) for r in (q_ref, k_ref, v_ref))
        ov, dof, lsev = o_ref[...], do_ref[...], lse_ref[...]

        def head(hh):
            q, k, v = _head(qv, hh), _head(kv, hh), _head(vv, hh)
            p = jnp.exp(_dot_nt(q, k) * SCALE - _head(lsev, hh)[:, 0:1])
            do_h = _head(dof, hh)
            delta = jnp.sum(do_h * _head(ov, hh), axis=-1, keepdims=True)
            dov = do_h.astype(MM_DTYPE)
            ds = (p * (_dot_nt(dov, v) - delta) * SCALE).astype(MM_DTYPE)
            return _dot(ds, k), _dot_tn(ds, q), _dot_tn(p.astype(MM_DTYPE), dov)

        dq_c, dk_c, dv_c = _both(head)
        dq_ref[...] = dq_c
        dk_ref[...] = dk_in[...] + dk_c
        dv_ref[...] = dv_in[...] + dv_c

    return pl.pallas_call(
        body, out_shape=[jax.ShapeDtypeStruct(dq.shape, F32)] * 3, grid=(HEAD_PAIRS,),
        in_specs=[cspec(qb), cspec(kb), cspec(vb), cspec(0), cspec(0), pl.BlockSpec((N, LANES), lambda hp: (0, hp)),
                  ANY_SPEC, cspec(0), cspec(0)],
        out_specs=[cspec(0)] * 3, input_output_aliases={6: 0, 7: 1, 8: 2},
        compiler_params=_params("parallel"), name="cattn_bwd")(proj, proj, proj, o, do, lse, dq, dk, dv)


PAD = 16


def _row_ids(T):
    return lax.broadcasted_iota(jnp.int32, (T, POOL_DIM), 0)


def _same_segment(t, s, L, T):
    return (s >= 0) & (s < T) & ((t < L) == (s < L))


def _window_sum(buf_ref, x, half, *, L, T, transpose):
    buf_ref[pl.ds(PAD, T), :] = x
    t = _row_ids(T)
    acc = jnp.zeros((T, POOL_DIM), F32)
    for j in range(-8, 9):
        inside = ((j > -half) & (j <= half)) if transpose else ((j >= -half) & (j < half))
        ok = _same_segment(t, t + j, L, T) & inside
        acc = acc + jnp.where(ok, buf_ref[pl.ds(PAD + j, T), :], 0.0)
    return acc


def _window_count(half, *, L, T):
    t = _row_ids(T)
    pos = jnp.where(t < L, t, t - L)
    seg_len = jnp.where(t < L, L, T - L)
    return (jnp.minimum(pos + half, seg_len) - jnp.maximum(pos - half, 0)).astype(F32)


def _zero_pads(buf_ref, T):
    buf_ref[pl.ds(0, PAD), :] = jnp.zeros((PAD, POOL_DIM), F32)
    buf_ref[pl.ds(PAD + T, PAD), :] = jnp.zeros((PAD, POOL_DIM), F32)


def pool_fwd(proj, pool_w, pool_scale, *, L):
    T = proj.shape[0]
    cb0 = COL["pool_u"] // POOL_DIM

    def body(u_ref, w_ref, s_ref, o_ref, p_ref, buf_ref):
        half = jnp.left_shift(1, pl.program_id(0))
        _zero_pads(buf_ref, T)
        u = u_ref[...].astype(F32)
        pooled = _window_sum(buf_ref, u, half, L=L, T=T, transpose=False) / _window_count(half, L=L, T=T) - u
        pm = pooled.astype(MM_DTYPE)
        p_ref[...] = pm
        o_ref[...] = _dot(pm, w_ref[...].astype(MM_DTYPE)) * s_ref[...]

    cspec = pl.BlockSpec((T, POOL_DIM), lambda g: (0, g))
    return pl.pallas_call(
        body, out_shape=[jax.ShapeDtypeStruct((T, BRANCH), F32), jax.ShapeDtypeStruct((T, BRANCH), MM_DTYPE)],
        grid=(POOL_GROUPS,),
        in_specs=[pl.BlockSpec((T, POOL_DIM), lambda g: (0, cb0 + g)),
                  pl.BlockSpec((None, POOL_DIM, POOL_DIM), lambda g: (g, 0, 0)),
                  pl.BlockSpec((1, POOL_DIM), lambda g: (0, g))],
        out_specs=[cspec, cspec], scratch_shapes=[pltpu.VMEM((T + 2 * PAD, POOL_DIM), F32)],
        compiler_params=_params("parallel"), name="pool_fwd")(proj, pool_w, pool_scale)


def pool_bwd(do, pooled, pool_w, pool_scale, *, L):
    T = do.shape[0]

    def body(do_ref, p_ref, w_ref, s_ref, du_ref, dw_ref, ds_ref, buf_ref):
        half = jnp.left_shift(1, pl.program_id(0))
        _zero_pads(buf_ref, T)
        pm = p_ref[...]
        w = w_ref[...].astype(MM_DTYPE)
        mixed = _dot(pm, w)
        dov = do_ref[...]
        ds_ref[...] = jnp.broadcast_to(jnp.sum(dov * mixed, axis=0, keepdims=True), ds_ref.shape)
        dmixed = (dov * s_ref[...]).astype(MM_DTYPE)
        dw_ref[...] = _dot_tn(pm, dmixed)
        dpooled = _dot_nt(dmixed, w)
        scaled = dpooled / _window_count(half, L=L, T=T)
        du = _window_sum(buf_ref, scaled, half, L=L, T=T, transpose=True) - dpooled
        du_ref[...] = du.astype(du_ref.dtype)

    cspec = pl.BlockSpec((T, POOL_DIM), lambda g: (0, g))
    return pl.pallas_call(
        body, out_shape=[jax.ShapeDtypeStruct((T, BRANCH), MM_DTYPE),
                         jax.ShapeDtypeStruct((POOL_GROUPS, POOL_DIM, POOL_DIM), F32),
                         jax.ShapeDtypeStruct((8, BRANCH), F32)],
        grid=(POOL_GROUPS,),
        in_specs=[cspec, cspec, pl.BlockSpec((None, POOL_DIM, POOL_DIM), lambda g: (g, 0, 0)),
                  pl.BlockSpec((1, POOL_DIM), lambda g: (0, g))],
        out_specs=[cspec, pl.BlockSpec((None, POOL_DIM, POOL_DIM), lambda g: (g, 0, 0)),
                   pl.BlockSpec((8, POOL_DIM), lambda g: (0, g))],
        scratch_shapes=[pltpu.VMEM((T + 2 * PAD, POOL_DIM), F32)],
        compiler_params=_params("parallel"), name="pool_bwd")(do, pooled, pool_w, pool_scale)


def _shifted(buf_ref, x, j, *, L, T):
    buf_ref[pl.ds(PAD, T), :] = x
    t = _row_ids(T)
    return jnp.where(_same_segment(t, t + j, L, T), buf_ref[pl.ds(PAD + j, T), :], 0.0)


def conv_fwd(proj, conv_w, *, L):
    T = proj.shape[0]
    nb = BRANCH // POOL_DIM
    cx, cbb, cc = (COL[n] // POOL_DIM for n in ("conv_x", "conv_b", "conv_c"))

    def body(x_ref, b_ref, c_ref, w_ref, o_ref, buf_ref):
        _zero_pads(buf_ref, T)
        xc = c_ref[...].astype(F32) * x_ref[...].astype(F32)
        w = w_ref[...]
        conv = (w[0:1] * _shifted(buf_ref, xc, -1, L=L, T=T) + w[1:2] * xc
                + w[2:3] * _shifted(buf_ref, xc, 1, L=L, T=T))
        o_ref[...] = b_ref[...].astype(F32) * conv

    return pl.pallas_call(
        body, out_shape=jax.ShapeDtypeStruct((T, BRANCH), F32), grid=(nb,),
        in_specs=[pl.BlockSpec((T, POOL_DIM), lambda g: (0, cx + g)), pl.BlockSpec((T, POOL_DIM), lambda g: (0, cbb + g)),
                  pl.BlockSpec((T, POOL_DIM), lambda g: (0, cc + g)), pl.BlockSpec((8, POOL_DIM), lambda g: (0, g))],
        out_specs=pl.BlockSpec((T, POOL_DIM), lambda g: (0, g)),
        scratch_shapes=[pltpu.VMEM((T + 2 * PAD, POOL_DIM), F32)],
        compiler_params=_params("parallel"), name="conv_fwd")(proj, proj, proj, conv_w)


def conv_bwd(do, proj, conv_w, *, L):
    T = proj.shape[0]
    nb = BRANCH // POOL_DIM
    cx, cbb, cc = (COL[n] // POOL_DIM for n in ("conv_x", "conv_b", "conv_c"))

    def body(do_ref, x_ref, b_ref, c_ref, w_ref, dx_ref, db_ref, dc_ref, dw_ref, buf_ref):
        _zero_pads(buf_ref, T)
        xv, gb, gc = (r[...].astype(F32) for r in (x_ref, b_ref, c_ref))
        xc = gc * xv
        w = w_ref[...]
        xm = _shifted(buf_ref, xc, -1, L=L, T=T)
        xp = _shifted(buf_ref, xc, 1, L=L, T=T)
        conv = w[0:1] * xm + w[1:2] * xc + w[2:3] * xp
        dov = do_ref[...]
        db_ref[...] = (dov * conv).astype(db_ref.dtype)
        dconv = dov * gb
        sums = [jnp.sum(dconv * a, axis=0, keepdims=True) for a in (xm, xc, xp)]
        dw_ref[...] = jnp.concatenate(sums + [jnp.zeros((5, POOL_DIM), F32)], axis=0)
        dxc = (w[0:1] * _shifted(buf_ref, dconv, 1, L=L, T=T) + w[1:2] * dconv
               + w[2:3] * _shifted(buf_ref, dconv, -1, L=L, T=T))
        dc_ref[...] = (dxc * xv).astype(dc_ref.dtype)
        dx_ref[...] = (dxc * gc).astype(dx_ref.dtype)

    ospec = lambda off: pl.BlockSpec((T, POOL_DIM), lambda g: (0, off + g))
    return pl.pallas_call(
        body, out_shape=[jax.ShapeDtypeStruct((T, BRANCH), MM_DTYPE)] * 3 + [jax.ShapeDtypeStruct((8, BRANCH), F32)],
        grid=(nb,),
        in_specs=[ospec(0), ospec(cx), ospec(cbb), ospec(cc), pl.BlockSpec((8, POOL_DIM), lambda g: (0, g))],
        out_specs=[ospec(0), ospec(0), ospec(0), pl.BlockSpec((8, POOL_DIM), lambda g: (0, g))],
        scratch_shapes=[pltpu.VMEM((T + 2 * PAD, POOL_DIM), F32)],
        compiler_params=_params("parallel"), name="conv_bwd")(do, proj, proj, proj, conv_w)


SCAN_COLS = 1024
SCAN_ROWS = 256


def ssm_operators(a_re, a_im, log_dt, b_re, b_im, c_re, c_im):
    n = a_re.shape[0]
    dt = jnp.exp(log_dt)[..., None]
    mag = jnp.exp(a_re * dt)
    abar_re, abar_im = mag * jnp.cos(a_im * dt), mag * jnp.sin(a_im * dt)
    den = a_re * a_re + a_im * a_im
    num_re, num_im = abar_re - 1.0, abar_im
    f_re = (num_re * a_re + num_im * a_im) / den
    f_im = (num_im * a_re - num_re * a_im) / den
    bbar_re = f_re[..., None] * b_re - f_im[..., None] * b_im
    bbar_im = f_re[..., None] * b_im + f_im[..., None] * b_re
    gpb = SSM_GROUPS // SSM_BLOCKS
    eye = jnp.eye(gpb, dtype=bool)[None, None, :, None, :, None]

    def blocks(t):
        _, _, a, b = t.shape
        t = t.reshape(n, SSM_BLOCKS, gpb, a, 1, b)
        return jnp.where(eye, t, 0.0).reshape(n, SSM_BLOCKS, gpb * a, gpb * b)

    in_map = lambda bbar: blocks(bbar.transpose(0, 1, 3, 2))
    out_map = lambda c: blocks(c.transpose(0, 1, 3, 2))
    abar = jnp.concatenate([abar_re.reshape(n, 1, N_STATE), abar_im.reshape(n, 1, N_STATE)], axis=-1)
    bcat = jnp.concatenate([in_map(bbar_re), in_map(bbar_im)], axis=1)
    ccat = jnp.concatenate([out_map(c_re), -out_map(c_im)], axis=1)
    return abar, bcat, ccat


SSM_BLOCKS = 4
SSM_BCH = BRANCH // SSM_BLOCKS
SSM_BST = N_STATE // SSM_BLOCKS


def _ssm_rows(T, perm):
    tm = ROW_TILE if perm else _row_tile(T)
    return tm, (perm or (lambda i: i))


def _lanes(x, n, width):
    return x[:, n * width:(n + 1) * width]


def _ssm_specs(T, perm, ucol0=None):
    tm, rows = _ssm_rows(T, perm)
    chan = pl.BlockSpec((tm, BRANCH), lambda i: (rows(i), 0 if ucol0 is None else ucol0 // BRANCH))
    state = pl.BlockSpec((tm, 2 * N_STATE), lambda i: (i, 0))
    bspec = pl.BlockSpec((2 * SSM_BLOCKS, SSM_BCH, SSM_BST), lambda i: (0, 0, 0))
    cspec = pl.BlockSpec((2 * SSM_BLOCKS, SSM_BST, SSM_BCH), lambda i: (0, 0, 0))
    return T // tm, chan, state, bspec, cspec


def ssm_in(u, bcat, *, ucol0, perm, name):
    T = u.shape[0]
    steps, chan, state, bspec, _ = _ssm_specs(T, perm, ucol0)

    def body(u_ref, b_ref, o_ref):
        uv = u_ref[...].astype(MM_DTYPE)
        for n in range(2 * SSM_BLOCKS):
            o_ref[:, n * SSM_BST:(n + 1) * SSM_BST] = _dot(_lanes(uv, n % SSM_BLOCKS, SSM_BCH), b_ref[n].astype(MM_DTYPE))

    return pl.pallas_call(body, out_shape=jax.ShapeDtypeStruct((T, 2 * N_STATE), F32), grid=(steps,),
                          in_specs=[chan, bspec], out_specs=state, compiler_params=_params("parallel"), name=name)(u, bcat)


def ssm_out(s, ccat, *, perm, name):
    T = s.shape[0]
    steps, chan, state, _, cspec = _ssm_specs(T, perm)

    def body(s_ref, c_ref, o_ref):
        sv = s_ref[...].astype(MM_DTYPE)
        o_ref[...] = jnp.concatenate(
            [_dot(_lanes(sv, j, SSM_BST), c_ref[j].astype(MM_DTYPE))
             + _dot(_lanes(sv, SSM_BLOCKS + j, SSM_BST), c_ref[SSM_BLOCKS + j].astype(MM_DTYPE))
             for j in range(SSM_BLOCKS)], axis=1)

    return pl.pallas_call(body, out_shape=jax.ShapeDtypeStruct((T, BRANCH), F32), grid=(steps,),
                          in_specs=[state, cspec], out_specs=chan, compiler_params=_params("parallel"), name=name)(s, ccat)


def ssm_out_dx(dy, ccat, *, perm, name):
    T = dy.shape[0]
    steps, chan, state, _, cspec = _ssm_specs(T, perm)

    def body(d_ref, c_ref, o_ref):
        dv = d_ref[...].astype(MM_DTYPE)
        for n in range(2 * SSM_BLOCKS):
            o_ref[:, n * SSM_BST:(n + 1) * SSM_BST] = _dot_nt(_lanes(dv, n % SSM_BLOCKS, SSM_BCH), c_ref[n].astype(MM_DTYPE))

    return pl.pallas_call(body, out_shape=jax.ShapeDtypeStruct((T, 2 * N_STATE), F32), grid=(steps,),
                          in_specs=[chan, cspec], out_specs=state, compiler_params=_params("parallel"), name=name)(dy, ccat)


def ssm_in_dx(lam, bcat, *, perm, name):
    T = lam.shape[0]
    steps, chan, state, bspec, _ = _ssm_specs(T, perm)

    def body(l_ref, b_ref, o_ref):
        lv = l_ref[...].astype(MM_DTYPE)
        o_ref[...] = jnp.concatenate(
            [_dot_nt(_lanes(lv, j, SSM_BST), b_ref[j].astype(MM_DTYPE))
             + _dot_nt(_lanes(lv, SSM_BLOCKS + j, SSM_BST), b_ref[SSM_BLOCKS + j].astype(MM_DTYPE))
             for j in range(SSM_BLOCKS)], axis=1)

    return pl.pallas_call(body, out_shape=jax.ShapeDtypeStruct((T, BRANCH), F32), grid=(steps,),
                          in_specs=[state, bspec], out_specs=chan, compiler_params=_params("parallel"), name=name)(lam, bcat)


def _ssm_dw(chan_arr, state_arr, chan_spec, state_spec, out_block, steps, chan_first, name):
    def body(c_ref, s_ref, o_ref):
        @pl.when(pl.program_id(0) == 0)
        def _():
            o_ref[...] = jnp.zeros(o_ref.shape, F32)

        cv, sv = c_ref[...].astype(MM_DTYPE), s_ref[...].astype(MM_DTYPE)
        for n in range(2 * SSM_BLOCKS):
            c, s = _lanes(cv, n % SSM_BLOCKS, SSM_BCH), _lanes(sv, n, SSM_BST)
            o_ref[n] += _dot_tn(c, s) if chan_first else _dot_tn(s, c)

    shape = (2 * SSM_BLOCKS,) + out_block
    return pl.pallas_call(body, out_shape=jax.ShapeDtypeStruct(shape, F32), grid=(steps,),
                          in_specs=[chan_spec, state_spec], out_specs=pl.BlockSpec(shape, lambda k: (0, 0, 0)),
                          compiler_params=_params("arbitrary"), name=name)(chan_arr, state_arr)


def ssm_in_dw(u, lam, *, ucol0, perm, name):
    steps, chan, state, _, _ = _ssm_specs(u.shape[0], perm, ucol0)
    return _ssm_dw(u, lam, chan, state, (SSM_BCH, SSM_BST), steps, True, name)


def ssm_out_dw(s, dy, *, perm, name):
    steps, chan, state, _, _ = _ssm_specs(s.shape[0], perm)
    return _ssm_dw(dy, s, chan, state, (SSM_BST, SSM_BCH), steps, False, name)


def _time_block(T, reverse):
    nt = T // SCAN_ROWS
    tix = (lambda i: nt - 1 - i) if reverse else (lambda i: i)
    return nt, pl.BlockSpec((SCAN_ROWS, 2 * N_STATE), lambda i: (tix(i), 0))


def ssm_scan(bu, abar, *, reverse):
    T = bu.shape[0]
    nt, tspec = _time_block(T, reverse)

    def body(b_ref, a_ref, s_ref, c_ref):
        @pl.when(pl.program_id(0) == 0)
        def _():
            c_ref[...] = jnp.zeros(c_ref.shape, F32)

        for c0 in range(0, N_STATE, SCAN_COLS):
            re, im = pl.ds(c0, SCAN_COLS), pl.ds(N_STATE + c0, SCAN_COLS)
            ar, ai = a_ref[:, re], a_ref[:, im]

            def step(n, carry, re=re, im=im, ar=ar, ai=ai):
                sr, si = carry
                t = (SCAN_ROWS - 1 - n) if reverse else n
                nr = ar * sr - ai * si + b_ref[pl.ds(t, 1), re]
                ni = ar * si + ai * sr + b_ref[pl.ds(t, 1), im]
                s_ref[pl.ds(t, 1), re] = nr
                s_ref[pl.ds(t, 1), im] = ni
                return nr, ni

            sr, si = lax.fori_loop(0, SCAN_ROWS, step, (c_ref[:, re], c_ref[:, im]))
            c_ref[:, re] = sr
            c_ref[:, im] = si

    return pl.pallas_call(
        body, out_shape=jax.ShapeDtypeStruct((T, 2 * N_STATE), F32), grid=(nt,),
        in_specs=[tspec, pl.BlockSpec((1, 2 * N_STATE), lambda i: (0, 0))], out_specs=tspec,
        scratch_shapes=[pltpu.VMEM((1, 2 * N_STATE), F32)],
        compiler_params=_params("arbitrary"), name="ssm_scan_rev" if reverse else "ssm_scan_fwd")(bu, abar)


def ssm_scan_bwd(g, s, abar, *, reverse):
    T = g.shape[0]
    nt, tspec = _time_block(T, not reverse)
    back = not reverse

    def body(g_ref, s_ref, a_ref, l_ref, da_ref, c_ref):
        @pl.when(pl.program_id(0) == 0)
        def _():
            c_ref[...] = jnp.zeros(c_ref.shape, F32)
            da_ref[...] = jnp.zeros(da_ref.shape, F32)

        for c0 in range(0, N_STATE, SCAN_COLS):
            re, im = pl.ds(c0, SCAN_COLS), pl.ds(N_STATE + c0, SCAN_COLS)
            ar, ai = a_ref[:, re], a_ref[:, im]

            def step(n, carry, re=re, im=im, ar=ar, ai=ai):
                lr, li, dr, di = carry
                t = (SCAN_ROWS - 1 - n) if back else n
                sr, si = s_ref[pl.ds(t, 1), re], s_ref[pl.ds(t, 1), im]
                dr = dr + sr * lr + si * li
                di = di + sr * li - si * lr
                nr = g_ref[pl.ds(t, 1), re] + ar * lr + ai * li
                ni = g_ref[pl.ds(t, 1), im] + ar * li - ai * lr
                l_ref[pl.ds(t, 1), re] = nr
                l_ref[pl.ds(t, 1), im] = ni
                return nr, ni, dr, di

            zero = jnp.zeros((1, SCAN_COLS), F32)
            lr, li, dr, di = lax.fori_loop(0, SCAN_ROWS, step, (c_ref[:, re], c_ref[:, im], zero, zero))
            c_ref[:, re] = lr
            c_ref[:, im] = li
            da_ref[:, re] += jnp.broadcast_to(dr, (8, SCAN_COLS))
            da_ref[:, im] += jnp.broadcast_to(di, (8, SCAN_COLS))

    return pl.pallas_call(
        body, out_shape=[jax.ShapeDtypeStruct((T, 2 * N_STATE), F32), jax.ShapeDtypeStruct((8, 2 * N_STATE), F32)],
        grid=(nt,), in_specs=[tspec, tspec, pl.BlockSpec((1, 2 * N_STATE), lambda i: (0, 0))],
        out_specs=[tspec, pl.BlockSpec((8, 2 * N_STATE), lambda i: (0, 0))],
        scratch_shapes=[pltpu.VMEM((1, 2 * N_STATE), F32)],
        compiler_params=_params("arbitrary"),
        name="ssm_scan_bwd_rev" if reverse else "ssm_scan_bwd_fwd")(g, s, abar)


def _gelu(x):
    return 0.5 * x * (1.0 + jnp.tanh(0.7978845608028654 * (x + 0.044715 * x * x * x)))


def _gelu_grad(x):
    t = jnp.tanh(0.7978845608028654 * (x + 0.044715 * x * x * x))
    return 0.5 * (1.0 + t) + 0.5 * x * (1.0 - t * t) * 0.7978845608028654 * (1.0 + 3 * 0.044715 * x * x)


def _silu(z):
    return z * _sigmoid(z)


def _silu_grad(z):
    s = _sigmoid(z)
    return s * (1.0 + z * (1.0 - s))


def ssm_fwd(proj, ops, dsk, glu_w, *, L):
    T = proj.shape[0]
    abar, bcat, ccat = ops
    nb, nlat = T // ROW_TILE, L // ROW_TILE
    to_f = lambda i: (i + nlat) % nb
    states, ys = [], []
    for d in (0, 1):
        perm = to_f if d == 0 else None
        bu = ssm_in(proj, bcat[d], ucol0=COL["ssm_u"], perm=perm, name=f"ssm_in{d}")
        s = ssm_scan(bu, abar[d], reverse=(d == 1))
        states.append(s)
        ys.append(ssm_out(s, ccat[d], perm=perm, name=f"ssm_out{d}"))

    def pre(u, yf, yr, dsk):
        y = dsk * u + yf + yr
        return y, _gelu(y)

    ypre, gy = _ew(pre, [(proj, "row", COL["ssm_u"] // BRANCH, BRANCH), (ys[0], "row", 0, BRANCH),
                         (ys[1], "row", 0, BRANCH), (dsk, "bcast", 0, BRANCH)],
                   [(BRANCH, F32), (BRANCH, MM_DTYPE)], [], T=T, L=L, name="ssm_pre")
    gg = mm_nn(gy, glu_w, out_dtype=F32, name="ssm_glu")

    def post(ga, gb):
        return (ga * _sigmoid(gb),)

    o, = _ew(post, [(gg, "row", 0, BRANCH), (gg, "row", 1, BRANCH)], [(BRANCH, F32)], [], T=T, L=L, name="ssm_post")
    return o, dict(states=states, ypre=ypre, gy=gy, gg=gg)


def ssm_bwd(do, proj, ops, dsk, glu_w, saved, *, L):
    T = proj.shape[0]
    abar, bcat, ccat = ops
    nb, nlat = T // ROW_TILE, L // ROW_TILE
    to_f = lambda i: (i + nlat) % nb
    gg, gy, ypre = saved["gg"], saved["gy"], saved["ypre"]

    def post_bwd(do, ga, gb):
        sg = _sigmoid(gb)
        return (jnp.concatenate([do * sg, do * ga * sg * (1.0 - sg)], axis=1),)

    dgg, = _ew(post_bwd, [(do, "row", 0, BRANCH), (gg, "row", 0, BRANCH), (gg, "row", 1, BRANCH)],
               [(2 * BRANCH, MM_DTYPE)], [], T=T, L=L, name="ssm_post_bwd")
    dgy = mm_nt(dgg, glu_w, out_dtype=F32, name="ssm_glu_dx")
    dglu = mm_tn(gy, dgg, out_dtype=F32, name="ssm_glu_dw")

    def pre_bwd(dgy, y, u, dsk):
        dy = dgy * _gelu_grad(y)
        return dy, dy * dsk, dy * u

    dy, du_skip, dd = _ew(pre_bwd, [(dgy, "row", 0, BRANCH), (ypre, "row", 0, BRANCH),
                                    (proj, "row", COL["ssm_u"] // BRANCH, BRANCH), (dsk, "bcast", 0, BRANCH)],
                          [(BRANCH, MM_DTYPE), (BRANCH, F32)], [BRANCH], T=T, L=L, name="ssm_pre_bwd")
    du = du_skip
    dabar, dbcat, dccat = [], [], []
    for d in (0, 1):
        perm = to_f if d == 0 else None
        s = saved["states"][d]
        g = ssm_out_dx(dy, ccat[d], perm=perm, name=f"ssm_out{d}_dx")
        lam, da = ssm_scan_bwd(g, s, abar[d], reverse=(d == 1))
        dabar.append(da[0:1])
        dccat.append(ssm_out_dw(s, dy, perm=perm, name=f"ssm_out{d}_dw"))
        du = du + ssm_in_dx(lam, bcat[d], perm=perm, name=f"ssm_in{d}_dx")
        dbcat.append(ssm_in_dw(proj, lam, ucol0=COL["ssm_u"], perm=perm, name=f"ssm_in{d}_dw"))
    d_ops = (jnp.stack(dabar), jnp.stack(dbcat), jnp.stack(dccat))
    return du, d_ops, dd[0, 0] + dd[1, 0], dglu


Z_COLS = tuple(COL[n] // BRANCH for n in ("na_z", "pool_z", "conv_z", "ssm_z"))


def gate_act(o, proj, *, L):
    T = o.shape[0]

    def fn(o, z0, z1, z2, z3):
        return (o * _silu(jnp.concatenate([z0, z1, z2, z3], axis=1).astype(F32)),)

    a, = _ew(fn, [(o, "row", 0, D_MODEL)] + [(proj, "row", c, BRANCH) for c in Z_COLS],
             [(D_MODEL, MM_DTYPE)], [], T=T, L=L, name="gate_act")
    return a


def gate_act_bwd(da, o, proj, *, L):
    T = o.shape[0]

    def fn(da, o, z0, z1, z2, z3):
        z = jnp.concatenate([z0, z1, z2, z3], axis=1).astype(F32)
        return da * _silu(z), da * o * _silu_grad(z)

    return _ew(fn, [(da, "row", 0, D_MODEL), (o, "row", 0, D_MODEL)] + [(proj, "row", c, BRANCH) for c in Z_COLS],
               [(D_MODEL, F32), (D_MODEL, MM_DTYPE)], [], T=T, L=L, name="gate_act_bwd")


MERGE_TN = 512


def merge_fwd(a, w_br, proj, b_gate):
    T = a.shape[0]
    tm, tn = _row_tile(T), MERGE_TN
    nn = D_MODEL // tn
    lb0 = COL["merge"] // tn

    def body(a_ref, w_ref, l_ref, b_ref, m_ref, br_ref, acc_ref):
        i = pl.program_id(2)
        br = _dot(a_ref[...].astype(MM_DTYPE), w_ref[...].astype(MM_DTYPE))
        br_ref[...] = br.astype(br_ref.dtype)
        term = _sigmoid(l_ref[...].astype(F32) + b_ref[...]) * br

        @pl.when(i == 0)
        def _():
            acc_ref[...] = term

        @pl.when(i > 0)
        def _():
            acc_ref[...] += term

        @pl.when(i == 3)
        def _():
            m_ref[...] = acc_ref[...].astype(m_ref.dtype)

    return pl.pallas_call(
        body, out_shape=[jax.ShapeDtypeStruct((T, D_MODEL), MM_DTYPE), jax.ShapeDtypeStruct((T, 4 * D_MODEL), MM_DTYPE)],
        grid=(T // tm, nn, 4),
        in_specs=[pl.BlockSpec((tm, BRANCH), lambda m, n, i: (m, i)),
                  pl.BlockSpec((BRANCH, tn), lambda m, n, i: (i, n)),
                  pl.BlockSpec((tm, tn), lambda m, n, i: (m, lb0 + i * nn + n)),
                  pl.BlockSpec((1, tn), lambda m, n, i: (0, i * nn + n))],
        out_specs=[pl.BlockSpec((tm, tn), lambda m, n, i: (m, n)), pl.BlockSpec((tm, tn), lambda m, n, i: (m, i * nn + n))],
        scratch_shapes=[pltpu.VMEM((tm, tn), F32)],
        compiler_params=_params("parallel", "parallel", "arbitrary"), name="merge_fwd")(a, w_br, proj, b_gate)


def merge_bwd(dmerged, br, proj, b_gate):
    T = dmerged.shape[0]
    tb = ROW_TILE
    lb0 = COL["merge"] // D_MODEL

    def body(dm_ref, br_ref, l_ref, b_ref, dbr_ref, dl_ref, db_ref):
        dm = dm_ref[...]
        gates = _sigmoid(l_ref[...] + b_ref[...])
        dbr_ref[...] = (dm * gates).astype(dbr_ref.dtype)
        dl = dm * br_ref[...] * gates * (1.0 - gates)
        dl_ref[...] = dl.astype(dl_ref.dtype)
        s = jnp.broadcast_to(jnp.sum(dl, axis=0, keepdims=True), db_ref.shape)

        @pl.when(pl.program_id(1) == 0)
        def _():
            db_ref[...] = s

        @pl.when(pl.program_id(1) > 0)
        def _():
            db_ref[...] += s

    wide = pl.BlockSpec((tb, D_MODEL), lambda b, i: (i, b))
    return pl.pallas_call(
        body, out_shape=[jax.ShapeDtypeStruct((T, 4 * D_MODEL), MM_DTYPE)] * 2 + [jax.ShapeDtypeStruct((8, 4 * D_MODEL), F32)],
        grid=(4, T // tb),
        in_specs=[pl.BlockSpec((tb, D_MODEL), lambda b, i: (i, 0)), wide,
                  pl.BlockSpec((tb, D_MODEL), lambda b, i: (i, lb0 + b)), pl.BlockSpec((1, D_MODEL), lambda b, i: (0, b))],
        out_specs=[wide, wide, pl.BlockSpec((8, D_MODEL), lambda b, i: (0, b))],
        compiler_params=_params("parallel", "arbitrary"), name="merge_bwd")(dmerged, br, proj, b_gate)


def branch_dx(dbr, w_br):
    T = dbr.shape[0]
    tm, tk = _row_tile(T), 1024
    nk = D_MODEL // tk
    return _matmul(dbr, w_br, pl.BlockSpec((tm, tk), lambda m, i, k: (m, i * nk + k)),
                   pl.BlockSpec((BRANCH, tk), lambda m, i, k: (i, k)),
                   pl.BlockSpec((tm, BRANCH), lambda m, i, k: (m, i)),
                   jax.ShapeDtypeStruct((T, D_MODEL), F32), (T // tm, 4, nk), tb=True, name="branch_dx")


def branch_dw(a, dbr):
    T = a.shape[0]
    tk, tn = T, 1024
    nn = D_MODEL // tn
    return _matmul(a, dbr, pl.BlockSpec((tk, BRANCH), lambda i, n, k: (k, i)),
                   pl.BlockSpec((tk, tn), lambda i, n, k: (k, i * nn + n)),
                   pl.BlockSpec((BRANCH, tn), lambda i, n, k: (i, n)),
                   jax.ShapeDtypeStruct((D_MODEL, D_MODEL), F32), (4, nn, T // tk), ta=True, name="branch_dw")


def proj_fwd(h, w_in, carried=None):
    T = h.shape[0]
    tm, tn = _row_tile(T), 1792
    per = W_IN_SHARD // tn
    return _matmul(h, w_in, pl.BlockSpec((tm, D_MODEL), lambda i, j, k: (i, 0)),
                   pl.BlockSpec((None, D_MODEL, tn), lambda i, j, k: (j // per, 0, j % per)),
                   pl.BlockSpec((tm, tn), lambda i, j, k: (i, j)),
                   jax.ShapeDtypeStruct((T, IN_TOTAL), MM_DTYPE), (T // tm, IN_TOTAL // tn, 1),
                   name="proj_fwd" if carried is None else "proj_fwd_gather", carried=carried)


def proj_dx(dproj, w_in, carried=None):
    T = dproj.shape[0]
    tm, tk = _row_tile(T), 1792
    per = W_IN_SHARD // tk
    return _matmul(dproj, w_in, pl.BlockSpec((tm, tk), lambda i, j, k: (i, k)),
                   pl.BlockSpec((None, D_MODEL, tk), lambda i, j, k: (k // per, 0, k % per)),
                   pl.BlockSpec((tm, D_MODEL), lambda i, j, k: (i, 0)),
                   jax.ShapeDtypeStruct((T, D_MODEL), F32), (T // tm, 1, IN_TOTAL // tk), tb=True,
                   name="proj_dx" if carried is None else "proj_dx_exchange", carried=carried)


def proj_dw(h, dproj):
    T = h.shape[0]
    tk, tm, tn = T, 512, 512
    per = W_IN_SHARD // tn
    return _matmul(h, dproj, pl.BlockSpec((tk, tm), lambda i, j, k: (k, i)),
                   pl.BlockSpec((tk, tn), lambda i, j, k: (k, j)),
                   pl.BlockSpec((None, tm, tn), lambda i, j, k: (j // per, i, j % per)),
                   jax.ShapeDtypeStruct((N_SHARDS, D_MODEL, W_IN_SHARD), F32),
                   (D_MODEL // tm, IN_TOTAL // tn, T // tk), ta=True, name="proj_dw")


def layer_fwd(X, P, *, L, carried=None):
    h = prenorm_fwd(X, P["g_pre"], P["scale"], P["shift"], L=L)
    proj, extras = proj_fwd(h, P["w_in"], carried) if carried is not None else (proj_fwd(h, P["w_in"]), None)
    o_att, lse = attn_fwd(proj, P["ptiles"], L=L)
    o_att, lse_c = cattn_fwd(proj, o_att, L=L)
    o_pool, pooled = pool_fwd(proj, P["pool_w"], P["pool_scale"], L=L)
    o_conv = conv_fwd(proj, P["conv_w"], L=L)
    o_ssm, ssm_saved = ssm_fwd(proj, P["ssm_ops"], P["ssm_d"], P["glu_w"], L=L)
    o = jnp.concatenate([o_att, o_pool, o_conv, o_ssm], axis=1)
    a = gate_act(o, proj, L=L)
    merged, br = merge_fwd(a, P["w_br"], proj, P["b_gate"])
    y = mm_nn(merged, P["w_o"], out_dtype=F32, name="out_proj")
    Xn = postnorm_fwd(X, y, P["g_post"], P["gate"], L=L)
    saved = dict(X=X, h=h, proj=proj, lse=lse, lse_c=lse_c, pooled=pooled, ssm=ssm_saved, o=o, a=a, merged=merged,
                 br=br, y=y)
    return Xn, saved, extras


def layer_bwd(dXn, P, S, *, L, carried=None):
    proj = S["proj"]
    dy, cs_post = postnorm_bwd(dXn, S["y"], P["g_post"], P["gate"], L=L)
    dmerged = mm_nt(dy, P["w_o"], out_dtype=F32, name="out_proj_dx")
    d_w_o = mm_tn(S["merged"], dy, out_dtype=F32, name="out_proj_dw", tm=512, tn=1024)
    dbr, dlogit, d_bgate = merge_bwd(dmerged, S["br"], proj, P["b_gate"])
    da = branch_dx(dbr, P["w_br"])
    d_w_br = branch_dw(S["a"], dbr)
    do, dz = gate_act_bwd(da, S["o"], proj, L=L)
    dq, dk, dv, dptiles = attn_bwd(proj, P["ptiles"], S["o"], do, S["lse"], L=L)
    dq, dk, dv = cattn_bwd(proj, S["o"], do, S["lse_c"], dq, dk, dv, L=L)
    dpool_u, d_pool_w, d_pool_scale = pool_bwd(do[:, BRANCH:2 * BRANCH], S["pooled"], P["pool_w"], P["pool_scale"], L=L)
    dcx, dcb, dcc, d_conv_w = conv_bwd(do[:, 2 * BRANCH:3 * BRANCH], proj, P["conv_w"], L=L)
    dssm_u, d_ops, d_ssm_d, d_glu = ssm_bwd(do[:, 3 * BRANCH:], proj, P["ssm_ops"], P["ssm_d"], P["glu_w"], S["ssm"], L=L)
    z = lambda i: dz[:, i * BRANCH:(i + 1) * BRANCH]
    cast = lambda t: t.astype(MM_DTYPE)
    dproj = jnp.concatenate([cast(dq), cast(dk), cast(dv), z(0), dpool_u, z(1),
                             dcx, dcb, dcc, z(2), cast(dssm_u), z(3), dlogit], axis=1)
    dh, extras = proj_dx(dproj, P["w_in"], carried) if carried is not None else (proj_dx(dproj, P["w_in"]), None)
    d_w_in = proj_dw(S["h"], dproj)
    dX, cs_h, cs_hx = prenorm_bwd(dh, S["X"], P["g_pre"], P["scale"], dXn, L=L)
    g_pre, g_post = P["g_pre"], P["g_post"]
    d_shift = cs_h
    d_scale = cs_hx * g_pre
    d_gate = cs_post * g_post
    d_g_pre = jnp.sum(cs_hx * (1.0 + P["scale"]), axis=0)[0]
    d_g_post = jnp.sum(cs_post * P["gate"], axis=0)[0]
    grads = dict(w_in=d_w_in, w_br=d_w_br, w_o=d_w_o, glu_w=d_glu, conv_w=d_conv_w[0:3], pool_w=d_pool_w,
                 pool_scale=d_pool_scale[0], b_gate=d_bgate[0], d_ptiles=dptiles, d_ops=d_ops, ssm_d=d_ssm_d,
                 g_pre=d_g_pre, g_post=d_g_post,
                 mod=jnp.concatenate([d_shift, d_scale, d_gate], axis=-1)[:, 0])
    return dX, grads, extras


BIG_NAMES = ("w_in", "glu_w", "w_br", "w_o")


def gathered_weights(g):
    w_in, glu, w_br, w_o = g
    return dict(w_in=w_in, glu_w=glu.transpose(1, 0, 2).reshape(BRANCH, 2 * BRANCH),
                w_br=w_br.reshape(D_MODEL, D_MODEL), w_o=w_o.reshape(D_MODEL, D_MODEL))


def shard_major_grads(g):
    return [g["w_in"], g["glu_w"].reshape(BRANCH, N_SHARDS, -1).transpose(1, 0, 2),
            g["w_br"].reshape(N_SHARDS, BRANCH, D_MODEL), g["w_o"].reshape(N_SHARDS, BRANCH, D_MODEL)]


def local_step(x, ctx, target, mods, layers, *, shard_halves=None, reduce=False):
    L = x.shape[0]
    nl = len(layers)
    X = jnp.concatenate([x, ctx], axis=0)
    ssm_all = [jnp.stack([lay["ssm_params"][j] for lay in layers]) for j in range(len(layers[0]["ssm_params"]))]
    ops_all, ops_vjp = jax.vjp(ssm_operators, *[t.reshape((2 * nl,) + t.shape[2:]) for t in ssm_all])
    ptiles_all, bias_vjp = jax.vjp(bias_pair_tiles, jnp.stack([lay["na_rpb"] for lay in layers]))
    saved, Ps = [], []
    arrived = None
    for i, lay in enumerate(layers):
        P = dict(lay)
        if arrived is not None:
            P.update(gathered_weights(arrived))
        m = mods[i][:, None, :]
        P["shift"], P["scale"], P["gate"] = m[..., :D_MODEL], m[..., D_MODEL:2 * D_MODEL], m[..., 2 * D_MODEL:]
        P["ptiles"] = ptiles_all[i]
        P["ssm_ops"] = tuple(t[2 * i:2 * i + 2] for t in ops_all)
        ahead = shard_halves is not None and i + 1 < nl
        X, S, arrived = layer_fwd(X, P, L=L, carried=gather_plan(shard_halves[i + 1], shard_major=True) if ahead else None)
        saved.append(S)
        Ps.append(P)
    loss, dX = loss_and_grad(X, target, L=L)
    grads, reduced = [None] * nl, [None] * nl
    finish = lambda parts: [pair_finish(a, name=f"grad_finish_{n}") for n, a in zip(BIG_NAMES, parts)]
    pending = None
    for i in reversed(range(nl)):
        dX, grads[i], arrived = layer_bwd(dX, Ps[i], saved[i], L=L,
                                          carried=exchange_plan(pending) if pending is not None else None)
        if pending is not None:
            reduced[i + 1] = finish(arrived)
        if reduce:
            pending = [pair_reduce(a, out_dtype=MM_DTYPE, name=f"grad_pair_{n}")
                       for n, a in zip(BIG_NAMES, shard_major_grads(grads[i]))]
    if reduce:
        reduced[0] = finish(chip_exchange(pending, name="grad_exchange"))
    d_rpb, = bias_vjp(jnp.stack([g.pop("d_ptiles") for g in grads]))
    d_ops = [g.pop("d_ops") for g in grads]
    d_ssm = ops_vjp(tuple(jnp.concatenate([d[j] for d in d_ops]) for j in range(3)))
    for i, g in enumerate(grads):
        g["na_rpb"] = d_rpb[i]
        g["ssm"] = tuple(t.reshape((nl, 2) + t.shape[1:])[i] for t in d_ssm)
    return loss, dX[:L], grads, reduced


MESH_ID = pl.DeviceIdType.MESH
HBM_SPEC = pl.BlockSpec(memory_space=pltpu.HBM)


def _place():
    return lax.axis_index("x"), lax.axis_index("y"), lax.axis_index("c")


def _other_chips(x, y):
    return [(1 - x, y), (x, 1 - y), (1 - x, 1 - y)]


def _remote(src, dst, send_sem, recv_sem, to):
    return pltpu.make_async_remote_copy(src_ref=src, dst_ref=dst, send_sem=send_sem, recv_sem=recv_sem,
                                        device_id=to, device_id_type=MESH_ID)


DMA_CHUNK_BYTES = 1 << 20


def _row_pieces(rows, row_bytes):
    n = max(1, min(64, rows * row_bytes // DMA_CHUNK_BYTES))
    while n > 1 and (rows % n or (rows // n) % 16):
        n -= 1
    return [(r * (rows // n), rows // n) for r in range(n)]


def _row_bytes(ref):
    return math.prod(ref.shape[1:]) * jnp.dtype(ref.dtype).itemsize


def _start_in_pieces(make, src, dst):
    for r0, nr in _row_pieces(src.shape[0], _row_bytes(src)):
        make(src.at[pl.ds(r0, nr)], dst.at[pl.ds(r0, nr)]).start()


class Carried:
    def __init__(self, ins, out_shapes, sem_shapes, start, mid, finish):
        self.ins, self.out_shapes, self.sem_shapes = list(ins), list(out_shapes), list(sem_shapes)
        self.start, self.mid, self.finish = start, mid, finish


def run_carried(plan, name):
    n_in, n_out = len(plan.ins), len(plan.out_shapes)

    def body(*refs):
        parts = refs[:n_in], refs[n_in:n_in + n_out], refs[n_in + n_out:]
        plan.start(*parts)
        plan.mid(*parts)
        plan.finish(*parts)

    return pl.pallas_call(body, out_shape=plan.out_shapes, in_specs=[HBM_SPEC] * n_in, out_specs=[HBM_SPEC] * n_out,
                          scratch_shapes=plan.sem_shapes, name=name)(*plan.ins)


def gather_plan(blocks, *, shard_major=False):
    K = len(blocks)

    def tools(ins, outs, sems):
        send_sems, recv_sems, local_sems = sems
        x, y, c = _place()

        def slot(k, block):
            px, py, pc = block
            if shard_major:
                h = ins[k].shape[0]
                return outs[k].at[2 * px + py, pl.ds(pl.multiple_of(pc * h, 16), h)]
            return outs[k].at[4 * px + 2 * py + pc]

        def copy(k, j, to):
            return lambda s, d: _remote(s, d, send_sems.at[k, j], recv_sems.at[k, j], to)

        return (x, y, c), (x, y, 1 - c), _other_chips(x, y), c, slot, copy, local_sems

    def start(ins, outs, sems):
        me, sibling, chips, c, slot, copy, local_sems = tools(ins, outs, sems)
        for k in range(K):
            _start_in_pieces(lambda s, d, k=k: pltpu.make_async_copy(s, d, local_sems.at[k]), ins[k], slot(k, me))
            _start_in_pieces(copy(k, 0, sibling), ins[k], slot(k, me))
            for j, chip in enumerate(chips):
                _start_in_pieces(copy(k, 1 + j, (*chip, c)), ins[k], slot(k, me))

    def mid(ins, outs, sems):
        me, sibling, chips, c, slot, copy, _ = tools(ins, outs, sems)
        for j, chip in enumerate(chips):
            for k in range(K):
                got = slot(k, (*chip, c))
                copy(k, 1 + j, me)(got, got).wait_recv()
                _start_in_pieces(copy(k, 4 + j, sibling), got, got)

    def finish(ins, outs, sems):
        me, sibling, chips, c, slot, copy, local_sems = tools(ins, outs, sems)
        for k in range(K):
            sib = slot(k, sibling)
            copy(k, 0, me)(sib, sib).wait_recv()
            for j, chip in enumerate(chips):
                got = slot(k, (*chip, 1 - c))
                copy(k, 4 + j, me)(got, got).wait_recv()
        for k in range(K):
            own = slot(k, me)
            for j in range(4):
                copy(k, j, me)(ins[k], own).wait_send()
            for j, chip in enumerate(chips):
                got = slot(k, (*chip, c))
                copy(k, 4 + j, me)(got, got).wait_send()
            pltpu.make_async_copy(ins[k], own, local_sems.at[k]).wait()

    def out_shape(b):
        if shard_major:
            return jax.ShapeDtypeStruct((N_SHARDS, 2 * b.shape[0]) + b.shape[1:], b.dtype)
        return jax.ShapeDtypeStruct((8,) + b.shape, b.dtype)

    sems = [pltpu.SemaphoreType.DMA((K, 7)), pltpu.SemaphoreType.DMA((K, 7)), pltpu.SemaphoreType.DMA((K,))]
    return Carried(blocks, [out_shape(b) for b in blocks], sems, start, mid, finish)


def all_gather8(blocks, name, *, shard_major=False):
    return run_carried(gather_plan(blocks, shard_major=shard_major), name)


def exchange_plan(parts):
    K = len(parts)

    def copies(ins, outs, sems):
        send_sems, recv_sems, local_sems = sems
        x, y, c = _place()
        p = 2 * x + y
        makes = []
        for k in range(K):
            for j, (cx, cy) in enumerate(_other_chips(x, y)):
                make = lambda s, d, k=k, j=j, to=(cx, cy, c): _remote(s, d, send_sems.at[k, j], recv_sems.at[k, j], to)
                makes.append((k, 2 * cx + cy, make))
        return p, makes, local_sems

    def start(ins, outs, sems):
        p, makes, local_sems = copies(ins, outs, sems)
        for k in range(K):
            _start_in_pieces(lambda s, d, k=k: pltpu.make_async_copy(s, d, local_sems.at[k]), ins[k].at[p], outs[k].at[p])
        for k, q, make in makes:
            _start_in_pieces(make, ins[k].at[q], outs[k].at[p])

    def mid(ins, outs, sems):
        pass

    def finish(ins, outs, sems):
        p, makes, local_sems = copies(ins, outs, sems)
        waits = [make(ins[k].at[q], outs[k].at[q]) for k, q, make in makes]
        for cp in waits:
            cp.wait_recv()
        for cp in waits:
            cp.wait_send()
        for k in range(K):
            pltpu.make_async_copy(ins[k].at[p], outs[k].at[p], local_sems.at[k]).wait()

    sems = [pltpu.SemaphoreType.DMA((K, 3)), pltpu.SemaphoreType.DMA((K, 3)), pltpu.SemaphoreType.DMA((K,))]
    return Carried(parts, [jax.ShapeDtypeStruct(a.shape, a.dtype) for a in parts], sems, start, mid, finish)


def chip_exchange(parts, name):
    return run_carried(exchange_plan(parts), name)


PAIR_TILE_BYTES = 2 << 20


def _pair_rows(h, n):
    return _pick(h, tuple(t for t in (512, 256, 128, 64, 32, 16) if t * n * 4 <= PAIR_TILE_BYTES))


def _core_index():
    return jnp.reshape(lax.axis_index("c"), (1,)).astype(jnp.int32)


def pair_reduce(g, *, out_dtype, name):
    S, R, n = g.shape
    h = R // 2
    tr = _pair_rows(h, n)
    nt = h // tr

    def body(c_ref, keep_ref, give_ref, o_ref, recv_ref, send_sems, recv_sems):
        x, y, c = _place()
        slot = (pl.program_id(0) * nt + pl.program_id(1)) % 2
        cp = _remote(give_ref, recv_ref.at[slot], send_sems.at[slot], recv_sems.at[slot], (x, y, 1 - c))
        cp.start()
        cp.wait_recv()
        o_ref[...] = (keep_ref[...] + recv_ref[slot]).astype(o_ref.dtype)
        cp.wait_send()

    g2 = g.reshape(S * R, n)
    grid_spec = pltpu.PrefetchScalarGridSpec(
        num_scalar_prefetch=1, grid=(S, nt),
        in_specs=[pl.BlockSpec((tr, n), lambda q, i, c: ((2 * q + c[0]) * nt + i, 0)),
                  pl.BlockSpec((tr, n), lambda q, i, c: ((2 * q + 1 - c[0]) * nt + i, 0))],
        out_specs=pl.BlockSpec((tr, n), lambda q, i, c: (q * nt + i, 0)),
        scratch_shapes=[pltpu.VMEM((2, tr, n), F32), pltpu.SemaphoreType.DMA((2,)), pltpu.SemaphoreType.DMA((2,))])
    out = pl.pallas_call(body, out_shape=jax.ShapeDtypeStruct((S * h, n), out_dtype), grid_spec=grid_spec,
                         compiler_params=_params("arbitrary", "arbitrary"), name=name)(_core_index(), g2, g2)
    return out.reshape(S, h, n)


def pair_finish(parts, *, name):
    S, h, n = parts.shape
    tr = _pair_rows(h, n)
    nt = h // tr

    def body(c_ref, p_ref, o_ref, recv_ref, send_sem, recv_sem):
        x, y, c = _place()
        phase, i = pl.program_id(0), pl.program_id(1)
        rows = pl.ds(pl.multiple_of(i * tr, 16), tr)

        @pl.when(phase == 0)
        def _():
            acc = p_ref[0].astype(F32)
            for s in range(1, S):
                acc = acc + p_ref[s].astype(F32)
            o_ref[...] = acc
            cp = _remote(o_ref, recv_ref.at[rows], send_sem, recv_sem, (x, y, 1 - c))
            cp.start()
            cp.wait_send()

        @pl.when((phase == 1) & (i == 0))
        def _():
            _remote(recv_ref, recv_ref, send_sem, recv_sem, (x, y, 1 - c)).wait_recv()

        @pl.when(phase == 1)
        def _():
            o_ref[...] = recv_ref[rows]

    grid_spec = pltpu.PrefetchScalarGridSpec(
        num_scalar_prefetch=1, grid=(2, nt),
        in_specs=[pl.BlockSpec((S, tr, n), lambda ph, i, c: (0, jnp.where(ph == 0, i, nt - 1), 0))],
        out_specs=pl.BlockSpec((tr, n), lambda ph, i, c: (jnp.where(ph == 0, c[0], 1 - c[0]) * nt + i, 0)),
        scratch_shapes=[pltpu.VMEM((h, n), F32), pltpu.SemaphoreType.DMA(()), pltpu.SemaphoreType.DMA(())])
    return pl.pallas_call(body, out_shape=jax.ShapeDtypeStruct((2 * h, n), F32), grid_spec=grid_spec,
                          compiler_params=_params("arbitrary", "arbitrary"), name=name)(_core_index(), parts)


def _slab_rows(n):
    return max(8, min(256, (1 << 18) // n // 8 * 8))


def sum_slabs(a, *, name):
    S, h, n = a.shape
    tr = _pick(h, tuple(t for t in (256, 128, 64, 32, 16, 8) if t <= _slab_rows(n)))

    def body(a_ref, o_ref):
        acc = a_ref[0].astype(F32)
        for s in range(1, S):
            acc = acc + a_ref[s].astype(F32)
        o_ref[...] = acc

    return pl.pallas_call(body, out_shape=jax.ShapeDtypeStruct((h, n), F32), grid=(h // tr,),
                          in_specs=[pl.BlockSpec((S, tr, n), lambda r: (0, r, 0))],
                          out_specs=pl.BlockSpec((tr, n), lambda r: (r, 0)), compiler_params=_params("parallel"),
                          name=name)(a)


def _adam_math(w, g, m, v):
    m = ADAM_B1 * m + (1.0 - ADAM_B1) * g
    v = ADAM_B2 * v + (1.0 - ADAM_B2) * (g * g)
    m_hat = m / (1.0 - ADAM_B1 ** ADAM_STEP)
    v_hat = v / (1.0 - ADAM_B2 ** ADAM_STEP)
    delta = -ADAM_LR * (m_hat / (jnp.sqrt(v_hat) + ADAM_EPS) + ADAM_WD * w)
    return delta, m, v


def adamw(w, g, m, v, *, name):
    R, n = w.shape
    tr = _pick(R, tuple(t for t in (256, 128, 64, 32, 16, 8) if t <= _slab_rows(n)))
    spec = pl.BlockSpec((tr, n), lambda r: (r, 0))

    def body(w_ref, g_ref, m_ref, v_ref, d_ref, nm_ref, nv_ref):
        d, nm, nv = _adam_math(w_ref[...], g_ref[...], m_ref[...], v_ref[...])
        d_ref[...] = d
        nm_ref[...] = nm
        nv_ref[...] = nv

    return pl.pallas_call(body, out_shape=[jax.ShapeDtypeStruct(w.shape, F32)] * 3, grid=(R // tr,),
                          in_specs=[spec] * 4, out_specs=[spec] * 3, compiler_params=_params("parallel"),
                          name=name)(w, g, m, v)


MOD_ROWS = 16


def mod_fwd(cact_in, w_mod, b_mod):
    nl, _, cols = w_mod.shape
    tn = 512

    def body(c_ref, w_ref, b_ref, o_ref):
        o_ref[...] = _dot(_silu(c_ref[...]).astype(MM_DTYPE), w_ref[...].astype(MM_DTYPE)) + b_ref[...]

    return pl.pallas_call(
        body, out_shape=jax.ShapeDtypeStruct((nl, MOD_ROWS, cols), F32), grid=(nl, cols // tn),
        in_specs=[pl.BlockSpec((MOD_ROWS, D_MODEL), lambda i, j: (0, 0)),
                  pl.BlockSpec((None, D_MODEL, tn), lambda i, j: (i, 0, j)),
                  pl.BlockSpec((None, 1, tn), lambda i, j: (i, 0, j))],
        out_specs=pl.BlockSpec((None, MOD_ROWS, tn), lambda i, j: (i, 0, j)),
        compiler_params=_params("parallel", "parallel"), name="mod_fwd")(cact_in, w_mod, b_mod)


def mod_update(c_rows, dmod, w, m, v):
    nl, _, cols = w.shape
    tr, tn = 256, 512
    wspec = pl.BlockSpec((None, tr, tn), lambda i, r, j: (i, r, j))

    def body(c_ref, d_ref, w_ref, m_ref, v_ref, g_ref, dl_ref, nm_ref, nv_ref):
        g = _dot_tn(_silu(c_ref[...]).astype(MM_DTYPE), d_ref[...].astype(MM_DTYPE))
        g_ref[...] = g
        dl, nm, nv = _adam_math(w_ref[...], g, m_ref[...], v_ref[...])
        dl_ref[...] = dl
        nm_ref[...] = nm
        nv_ref[...] = nv

    return pl.pallas_call(
        body, out_shape=[jax.ShapeDtypeStruct(w.shape, F32)] * 4, grid=(nl, D_MODEL // tr, cols // tn),
        in_specs=[pl.BlockSpec((MOD_ROWS, tr), lambda i, r, j: (0, r)),
                  pl.BlockSpec((None, MOD_ROWS, tn), lambda i, r, j: (i, 0, j)), wspec, wspec, wspec],
        out_specs=[wspec] * 4, compiler_params=_params("parallel", "parallel", "parallel"),
        name="mod_update")(c_rows, dmod, w, m, v)


def cctx_partial(dmod_ctx, w_mod, c_ctx):
    nl, _, cols = w_mod.shape
    tk = 512
    per = cols // tk
    part = _matmul(dmod_ctx, w_mod, pl.BlockSpec((8, tk), lambda i, j, k: (0, k)),
                   pl.BlockSpec((None, D_MODEL, tk), lambda i, j, k: (k // per, 0, k % per)),
                   pl.BlockSpec((8, D_MODEL), lambda i, j, k: (0, 0)),
                   jax.ShapeDtypeStruct((8, D_MODEL), F32), (1, 1, nl * per), tb=True, name="cctx_partial")

    def body(p_ref, c_ref, o_ref):
        o_ref[...] = 0.5 * jnp.sum(p_ref[...], axis=0, keepdims=True) * _silu_grad(c_ref[...])

    return pl.pallas_call(body, out_shape=jax.ShapeDtypeStruct((1, D_MODEL), F32), name="cctx_scale")(part, c_ctx)


WEIGHT_NAMES = ("c_ctx", "w_mod", "b_mod", "g_pre", "g_post", "w_in", "b_gate", "na_rpb", "pool_w", "pool_scale",
                "conv_w", "ssm_a_re", "ssm_a_im", "ssm_log_dt", "ssm_b_re", "ssm_b_im", "ssm_c_re", "ssm_c_im",
                "ssm_d", "glu_w", "w_br", "w_o")
SSM_NAMES = ("ssm_a_re", "ssm_a_im", "ssm_log_dt", "ssm_b_re", "ssm_b_im", "ssm_c_re", "ssm_c_im")
SMALL_NAMES = ("c_ctx", "b_mod", "g_pre", "g_post", "b_gate", "na_rpb", "pool_w", "pool_scale") + SSM_NAMES + ("ssm_d",)
FLAT_COLS = 1024


def _flat(parts):
    v = jnp.concatenate([p.reshape(-1) for p in parts])
    pad = -v.shape[0] % (64 * FLAT_COLS)
    return jnp.pad(v, (0, pad)).reshape(-1, FLAT_COLS)


def _unflat(flat, shapes):
    v = flat.reshape(-1)
    out, off = [], 0
    for s in shapes:
        n = math.prod(s)
        out.append(v[off:off + n].reshape(s))
        off += n
    return out


def kernel(x, c, ctx, c_ctx, w_mod, b_mod, g_pre, g_post, w_in, b_gate, na_rpb, pool_w, pool_scale, conv_w, ssm_a_re, ssm_a_im, ssm_log_dt, ssm_b_re, ssm_b_im, ssm_c_re, ssm_c_im, ssm_d, glu_w, w_br, w_o, loss_target, m_c_ctx, m_w_mod, m_b_mod, m_g_pre, m_g_post, m_w_in, m_b_gate, m_na_rpb, m_pool_w, m_pool_scale, m_conv_w, m_ssm_a_re, m_ssm_a_im, m_ssm_log_dt, m_ssm_b_re, m_ssm_b_im, m_ssm_c_re, m_ssm_c_im, m_ssm_d, m_glu_w, m_w_br, m_w_o, v_c_ctx, v_w_mod, v_b_mod, v_g_pre, v_g_post, v_w_in, v_b_gate, v_na_rpb, v_pool_w, v_pool_scale, v_conv_w, v_ssm_a_re, v_ssm_a_im, v_ssm_log_dt, v_ssm_b_re, v_ssm_b_im, v_ssm_c_re, v_ssm_c_im, v_ssm_d, v_glu_w, v_w_br, v_w_o):
    W = dict(c_ctx=c_ctx, w_mod=w_mod, b_mod=b_mod, g_pre=g_pre, g_post=g_post, w_in=w_in, b_gate=b_gate,
             na_rpb=na_rpb, pool_w=pool_w, pool_scale=pool_scale, conv_w=conv_w, ssm_a_re=ssm_a_re, ssm_a_im=ssm_a_im,
             ssm_log_dt=ssm_log_dt, ssm_b_re=ssm_b_re, ssm_b_im=ssm_b_im, ssm_c_re=ssm_c_re, ssm_c_im=ssm_c_im,
             ssm_d=ssm_d, glu_w=glu_w, w_br=w_br, w_o=w_o)
    M = dict(c_ctx=m_c_ctx, w_mod=m_w_mod, b_mod=m_b_mod, g_pre=m_g_pre, g_post=m_g_post, w_in=m_w_in, b_gate=m_b_gate,
             na_rpb=m_na_rpb, pool_w=m_pool_w, pool_scale=m_pool_scale, conv_w=m_conv_w, ssm_a_re=m_ssm_a_re,
             ssm_a_im=m_ssm_a_im, ssm_log_dt=m_ssm_log_dt, ssm_b_re=m_ssm_b_re, ssm_b_im=m_ssm_b_im,
             ssm_c_re=m_ssm_c_re, ssm_c_im=m_ssm_c_im, ssm_d=m_ssm_d, glu_w=m_glu_w, w_br=m_w_br, w_o=m_w_o)
    V = dict(c_ctx=v_c_ctx, w_mod=v_w_mod, b_mod=v_b_mod, g_pre=v_g_pre, g_post=v_g_post, w_in=v_w_in, b_gate=v_b_gate,
             na_rpb=v_na_rpb, pool_w=v_pool_w, pool_scale=v_pool_scale, conv_w=v_conv_w, ssm_a_re=v_ssm_a_re,
             ssm_a_im=v_ssm_a_im, ssm_log_dt=v_ssm_log_dt, ssm_b_re=v_ssm_b_re, ssm_b_im=v_ssm_b_im,
             ssm_c_re=v_ssm_c_re, ssm_c_im=v_ssm_c_im, ssm_d=v_ssm_d, glu_w=v_glu_w, w_br=v_w_br, w_o=v_w_o)
    nl = w_in.shape[0]
    xi, yi, ci = _place()
    chip = 2 * xi + yi
    example = 4 * xi + 2 * yi + ci
    mod_cols = w_mod.shape[2]

    def my_half(a):
        half = a.shape[0] // 2
        return lax.dynamic_slice_in_dim(a, ci * half, half, axis=0).astype(MM_DTYPE)

    halves = [[my_half(W[n][i]) for n in BIG_NAMES] for i in range(nl)]
    first = gathered_weights(all_gather8(halves[0], name="gather_weights", shard_major=True))

    c8 = jnp.pad(c, ((0, 7), (0, 0)))
    c_all, = all_gather8([c8], name="gather_c")
    c_rows = jnp.concatenate([c_all[:, 0], jnp.broadcast_to(c_ctx[None], (8, D_MODEL))], axis=0)
    b_cols = lax.dynamic_slice_in_dim(b_mod, chip * mod_cols, mod_cols, axis=1)[:, None]
    mod_part = mod_fwd(c_rows, w_mod, b_cols)
    conv_part = jnp.pad(conv_w.reshape(nl * 3, -1), ((0, 16 - nl * 3), (0, 0)))
    parts, conv_all = all_gather8([mod_part.reshape(nl * MOD_ROWS, mod_cols), conv_part], name="gather_mod")
    mod_full = jnp.concatenate([parts[2 * p].reshape(nl, MOD_ROWS, mod_cols) for p in range(N_SHARDS)], axis=-1)
    conv_full = jnp.concatenate([conv_all[2 * p][:nl * 3].reshape(nl, 3, -1) for p in range(N_SHARDS)], axis=-1)
    own = lax.dynamic_index_in_dim(mod_full, example, axis=1, keepdims=False)
    mods = [jnp.stack([own[i], mod_full[i, 8]]) for i in range(nl)]

    layers = []
    for i in range(nl):
        layers.append(dict(
            conv_w=jnp.pad(conv_full[i], ((0, 5), (0, 0))), pool_w=pool_w[i], pool_scale=pool_scale[i][None],
            b_gate=b_gate[i][None], g_pre=g_pre[i][None], g_post=g_post[i][None], na_rpb=na_rpb[i],
            ssm_d=ssm_d[i][None], ssm_params=tuple(W[n][i] for n in SSM_NAMES)))
    layers[0].update(first)

    loss_local, grad_x, grads, reduced = local_step(x[0], ctx[0], loss_target[0], mods, layers,
                                                    shard_halves=halves, reduce=True)
    loss = lax.psum(loss_local, ("x", "y", "c"))

    dmod_local = jnp.stack([g["mod"] for g in grads])
    dmod_all, = all_gather8([jnp.pad(dmod_local.reshape(nl * 2, -1), ((0, 8 - nl * 2), (0, 0)))], name="gather_dmod")
    dmod_all = dmod_all[:, :nl * 2].reshape(8, nl, 2, 3 * D_MODEL)
    dmod_rows = jnp.concatenate([dmod_all[:, :, 0], dmod_all[:, :, 1]], axis=0).transpose(1, 0, 2)
    dmod_cols = lax.dynamic_slice_in_dim(dmod_rows, chip * mod_cols, mod_cols, axis=2)
    g_w_mod, d_w_mod, nm_w_mod, nv_w_mod = mod_update(c_rows, dmod_cols, w_mod, m_w_mod, v_w_mod)
    dctx_cols = dmod_cols[:, 8:].transpose(1, 0, 2).reshape(8, nl * mod_cols)
    g_cctx_part = cctx_partial(dctx_cols, w_mod, c_ctx[None])[0]

    def small_grad(n):
        if n == "c_ctx":
            return g_cctx_part
        if n == "b_mod":
            return jnp.stack([g["mod"][0] + g["mod"][1] for g in grads])
        if n in SSM_NAMES:
            return jnp.stack([g["ssm"][SSM_NAMES.index(n)] for g in grads])
        return jnp.stack([g[n] for g in grads])

    conv_grad_full = jnp.stack([g["conv_w"] for g in grads])
    flat_g = _flat([small_grad(n) for n in SMALL_NAMES] + [conv_grad_full])

    G, DL, NM, NV = {}, {}, {}, {}
    for k, n in enumerate(BIG_NAMES):
        g = jnp.stack([reduced[i][k] for i in range(nl)])
        rows = g.shape[0] * g.shape[1]
        d, nm, nv = adamw(W[n].reshape(rows, -1), g.reshape(rows, -1), M[n].reshape(rows, -1), V[n].reshape(rows, -1),
                          name=f"adamw_{n}")
        G[n], DL[n], NM[n], NV[n] = g, d.reshape(g.shape), nm.reshape(g.shape), nv.reshape(g.shape)

    flat_all, = all_gather8([flat_g], name="gather_small_grads")
    flat_sum = sum_slabs(flat_all, name="sum_small_grads")
    small_shapes = [W[n].shape for n in SMALL_NAMES]
    small_g = _unflat(flat_sum, small_shapes + [conv_grad_full.shape])
    conv_g = lax.dynamic_slice_in_dim(small_g[-1], chip * conv_w.shape[2], conv_w.shape[2], axis=2)
    adam_names = SMALL_NAMES + ("conv_w",)
    adam_shapes = small_shapes + [conv_w.shape]
    g_list = small_g[:-1] + [conv_g]
    upd = adamw(_flat([W[n] for n in adam_names]), _flat(g_list), _flat([M[n] for n in adam_names]),
                _flat([V[n] for n in adam_names]), name="adamw_small")
    G.update(zip(adam_names, g_list))
    for group, u in zip((DL, NM, NV), upd):
        group.update(zip(adam_names, _unflat(u, adam_shapes)))
    G["w_mod"], DL["w_mod"], NM["w_mod"], NV["w_mod"] = g_w_mod, d_w_mod, nm_w_mod, nv_w_mod

    out = [loss, grad_x[None]]
    for group in (G, DL, NM, NV):
        out += [group[n].reshape(W[n].shape) for n in WEIGHT_NAMES]
    return tuple(out)
```

```python
import functools
import math

import numpy as np
import jax
import jax.numpy as jnp
from jax import lax
from jax.experimental import pallas as pl
from jax.experimental.pallas import tpu as pltpu

F32 = jnp.float32
BF16 = jnp.bfloat16
MM_DTYPE = jnp.bfloat16

D_MODEL = 2048
BRANCH = 512
N_HEADS = 8
HEAD_DIM = 64
GRID_W = 64
WIN_ROWS = 8
WIN_COLS = 16
POOL_GROUPS = 4
POOL_DIM = 128
SSM_GROUPS = 32
SSM_GDIM = 16
SSM_STATE = 64
N_STATE = SSM_GROUPS * SSM_STATE
IN_TOTAL = 14336
RMS_EPS = 1e-6
NEG_INF = -1e30
COL = dict(q=0, k=512, v=1024, na_z=1536, pool_u=2048, pool_z=2560, conv_x=3072, conv_b=3584,
           conv_c=4096, conv_z=4608, ssm_u=5120, ssm_z=5632, merge=6144)
N_SHARDS = 4
W_IN_SHARD = IN_TOTAL // N_SHARDS
VMEM_LIMIT_BYTES = 48 * 1024 * 1024
ROW_TILE = 256

ADAM_LR = 0.001
ADAM_B1 = 0.9
ADAM_B2 = 0.999
ADAM_EPS = 1e-08
ADAM_WD = 0.01
ADAM_STEP = 10


def _params(*sem):
    return pltpu.CompilerParams(dimension_semantics=sem, vmem_limit_bytes=VMEM_LIMIT_BYTES)


def _sigmoid(x):
    return 1.0 / (1.0 + jnp.exp(-x))


def _matmul(a, b, a_spec, b_spec, o_spec, out_shape, grid, *, ta=False, tb=False, name, carried=None):
    nk = grid[-1]
    kaxis = len(grid) - 1
    dims = (((0,) if ta else (1,), (1,) if tb else (0,)), ((), ()))
    n_acc = 0 if nk == 1 else 1

    def compute(a_ref, b_ref, o_ref, acc):
        p = lax.dot_general(a_ref[...].astype(MM_DTYPE), b_ref[...].astype(MM_DTYPE), dims,
                            preferred_element_type=F32)
        if nk == 1:
            o_ref[...] = p.astype(o_ref.dtype)
            return
        acc_ref, = acc
        k = pl.program_id(kaxis)

        @pl.when(k == 0)
        def _():
            acc_ref[...] = p

        @pl.when(k > 0)
        def _():
            acc_ref[...] += p

        @pl.when(k == nk - 1)
        def _():
            o_ref[...] = acc_ref[...].astype(o_ref.dtype)

    oblock = tuple(s for s in o_spec.block_shape if s is not None)
    scratch = [] if nk == 1 else [pltpu.VMEM(oblock, F32)]
    if carried is None:
        def body(a_ref, b_ref, o_ref, *acc):
            compute(a_ref, b_ref, o_ref, acc)

        sem = ("parallel",) * (len(grid) - 1) + ("arbitrary",)
        return pl.pallas_call(body, out_shape=out_shape, grid=grid, in_specs=[a_spec, b_spec],
                              out_specs=o_spec, scratch_shapes=scratch, compiler_params=_params(*sem),
                              name=name)(a, b)

    n_in, n_out = len(carried.ins), len(carried.out_shapes)
    steps = math.prod(grid)

    def body(a_ref, b_ref, *rest):
        c_ins, o_ref, c_outs = rest[:n_in], rest[n_in], rest[n_in + 1:n_in + 1 + n_out]
        acc, sems = rest[n_in + 1 + n_out:][:n_acc], rest[n_in + 1 + n_out + n_acc:]
        step = pl.program_id(0)
        for ax in range(1, len(grid)):
            step = step * grid[ax] + pl.program_id(ax)

        @pl.when(step == 0)
        def _():
            carried.start(c_ins, c_outs, sems)

        compute(a_ref, b_ref, o_ref, acc)

        @pl.when(step == steps - max(2, steps // 16))
        def _():
            carried.mid(c_ins, c_outs, sems)

        @pl.when(step == steps - 1)
        def _():
            carried.finish(c_ins, c_outs, sems)

    res = pl.pallas_call(body, out_shape=[out_shape] + list(carried.out_shapes), grid=grid,
                         in_specs=[a_spec, b_spec] + [HBM_SPEC] * n_in, out_specs=[o_spec] + [HBM_SPEC] * n_out,
                         scratch_shapes=scratch + list(carried.sem_shapes),
                         compiler_params=_params(*(("arbitrary",) * len(grid))), name=name)(a, b, *carried.ins)
    return res[0], res[1:]


def _pick(n, cands):
    for c in cands:
        if n % c == 0:
            return c
    raise ValueError(f"no tile for {n}")


def _row_tile(T):
    return _pick(T, (544, 512, 256, 128))


def mm_nn(a, b, *, out_dtype, name, tn=512, a_rows=None, o_rows=None, a_cols=None):
    M = a.shape[0]
    c0, K = a_cols or (0, a.shape[1])
    N = b.shape[1]
    tm = ROW_TILE if (a_rows or o_rows) else _row_tile(M)
    tn = min(tn, N)
    tk = K if K <= 2048 else _pick(K, (2048, 1024, 512))
    kb0 = c0 // tk
    ar = a_rows or (lambda i: i)
    orr = o_rows or (lambda i: i)
    return _matmul(a, b, pl.BlockSpec((tm, tk), lambda i, j, k: (ar(i), kb0 + k)),
                   pl.BlockSpec((tk, tn), lambda i, j, k: (k, j)),
                   pl.BlockSpec((tm, tn), lambda i, j, k: (orr(i), j)),
                   jax.ShapeDtypeStruct((M, N), out_dtype), (M // tm, N // tn, K // tk), name=name)


def mm_nt(a, b, *, out_dtype, name, a_rows=None, o_rows=None):
    M, K = a.shape
    N = b.shape[0]
    tm = ROW_TILE if (a_rows or o_rows) else _row_tile(M)
    tn = min(N, 2048)
    tk = K if K <= 1024 else _pick(K, (1024, 512))
    ar = a_rows or (lambda i: i)
    orr = o_rows or (lambda i: i)
    return _matmul(a, b, pl.BlockSpec((tm, tk), lambda i, j, k: (ar(i), k)),
                   pl.BlockSpec((tn, tk), lambda i, j, k: (j, k)),
                   pl.BlockSpec((tm, tn), lambda i, j, k: (orr(i), j)),
                   jax.ShapeDtypeStruct((M, N), out_dtype), (M // tm, N // tn, K // tk), tb=True, name=name)


def mm_tn(a, b, *, out_dtype, name, a_rows=None, b_rows=None, tm=512, tn=1024, a_cols=None):
    K = a.shape[0]
    c0, M = a_cols or (0, a.shape[1])
    N = b.shape[1]
    tk = ROW_TILE if (a_rows or b_rows) else K
    tm = min(tm, M)
    tn = min(tn, N)
    mb0 = c0 // tm
    ar = a_rows or (lambda k: k)
    br = b_rows or (lambda k: k)
    return _matmul(a, b, pl.BlockSpec((tk, tm), lambda i, j, k: (ar(k), mb0 + i)),
                   pl.BlockSpec((tk, tn), lambda i, j, k: (br(k), j)),
                   pl.BlockSpec((tm, tn), lambda i, j, k: (i, j)),
                   jax.ShapeDtypeStruct((M, N), out_dtype), (M // tm, N // tn, K // tk), ta=True, name=name)


def _ew(fn, ins, outs, colsums, *, T, L, name):
    tb = ROW_TILE
    nlat = L // tb
    seg = lambda i: jnp.where(i >= nlat, 1, 0)
    in_specs, arrays = [], []
    for arr, kind, cb, width in ins:
        arrays.append(arr)
        if kind == "row":
            in_specs.append(pl.BlockSpec((tb, width), lambda i, cb=cb: (i, cb)))
        elif kind == "bcast":
            in_specs.append(pl.BlockSpec((1, width), lambda i, cb=cb: (0, cb)))
        else:
            in_specs.append(pl.BlockSpec((None, 1, width), lambda i, cb=cb: (seg(i), 0, cb)))
    out_specs = [pl.BlockSpec((tb, w), lambda i: (i, 0)) for w, _ in outs]
    out_shapes = [jax.ShapeDtypeStruct((T, w), dt) for w, dt in outs]
    out_specs += [pl.BlockSpec((None, 1, w), lambda i: (seg(i), 0, 0)) for w in colsums]
    out_shapes += [jax.ShapeDtypeStruct((2, 1, w), F32) for w in colsums]
    n_in, n_out = len(ins), len(outs)

    def body(*refs):
        i = pl.program_id(0)
        res = fn(*[r[...] for r in refs[:n_in]])
        for r, v in zip(refs[n_in:n_in + n_out], res[:n_out]):
            r[...] = v.astype(r.dtype)
        first = (i == 0) | (i == nlat)
        for r, v in zip(refs[n_in + n_out:], res[n_out:]):
            s = jnp.sum(v, axis=0, keepdims=True)

            @pl.when(first)
            def _(r=r, s=s):
                r[...] = s

            @pl.when(jnp.logical_not(first))
            def _(r=r, s=s):
                r[...] += s

    res = pl.pallas_call(body, out_shape=out_shapes, grid=(T // tb,), in_specs=in_specs,
                         out_specs=out_specs, compiler_params=_params("arbitrary"), name=name)(*arrays)
    return res


def _rms(x):
    return lax.rsqrt(jnp.mean(x * x, axis=-1, keepdims=True) + RMS_EPS)


def prenorm_fwd(X, g, scale, shift, *, L):
    T = X.shape[0]

    def fn(x, g, sc, sh):
        return ((x * _rms(x)) * (g * (1.0 + sc)) + sh,)

    h, = _ew(fn, [(X, "row", 0, D_MODEL), (g, "bcast", 0, D_MODEL), (scale, "seg", 0, D_MODEL),
                  (shift, "seg", 0, D_MODEL)], [(D_MODEL, MM_DTYPE)], [], T=T, L=L, name="prenorm_fwd")
    return h


def prenorm_bwd(dh, X, g, scale, dres, *, L):
    T = X.shape[0]

    def fn(dh, x, g, sc, dres):
        r = _rms(x)
        xn = x * r
        dxn = dh * (g * (1.0 + sc))
        dx = r * (dxn - xn * jnp.mean(dxn * xn, axis=-1, keepdims=True))
        return dres + dx, dh, dh * xn

    return _ew(fn, [(dh, "row", 0, D_MODEL), (X, "row", 0, D_MODEL), (g, "bcast", 0, D_MODEL),
                    (scale, "seg", 0, D_MODEL), (dres, "row", 0, D_MODEL)],
               [(D_MODEL, F32)], [D_MODEL, D_MODEL], T=T, L=L, name="prenorm_bwd")


def postnorm_fwd(X, y, g, gate, *, L):
    T = X.shape[0]

    def fn(x, y, g, gate):
        return (x + gate * ((y * _rms(y)) * g),)

    out, = _ew(fn, [(X, "row", 0, D_MODEL), (y, "row", 0, D_MODEL), (g, "bcast", 0, D_MODEL),
                    (gate, "seg", 0, D_MODEL)], [(D_MODEL, F32)], [], T=T, L=L, name="postnorm_fwd")
    return out


def postnorm_bwd(dX, y, g, gate, *, L):
    T = dX.shape[0]

    def fn(dx, y, g, gate):
        r = _rms(y)
        yn = y * r
        dyn = dx * (gate * g)
        dy = r * (dyn - yn * jnp.mean(dyn * yn, axis=-1, keepdims=True))
        return dy, dx * yn

    return _ew(fn, [(dX, "row", 0, D_MODEL), (y, "row", 0, D_MODEL), (g, "bcast", 0, D_MODEL),
                    (gate, "seg", 0, D_MODEL)], [(D_MODEL, MM_DTYPE)], [D_MODEL], T=T, L=L, name="postnorm_bwd")


def loss_and_grad(X, target, *, L):
    T = X.shape[0]
    tb = ROW_TILE
    nlat = L // tb

    def body(x_ref, t_ref, dx_ref, part_ref):
        i = pl.program_id(0)

        @pl.when(i < nlat)
        def _():
            err = x_ref[...] - t_ref[...]
            dx_ref[...] = err * (1.0 / D_MODEL)
            part_ref[...] = jnp.full(part_ref.shape, 0.5 / D_MODEL * jnp.sum(err * err), F32)

        @pl.when(i >= nlat)
        def _():
            dx_ref[...] = jnp.zeros(dx_ref.shape, F32)
            part_ref[...] = jnp.zeros(part_ref.shape, F32)

    dx, part = pl.pallas_call(
        body, out_shape=[jax.ShapeDtypeStruct((T, D_MODEL), F32), jax.ShapeDtypeStruct((T // tb, 8, 128), F32)],
        grid=(T // tb,),
        in_specs=[pl.BlockSpec((tb, D_MODEL), lambda i: (i, 0)),
                  pl.BlockSpec((tb, D_MODEL), lambda i: (jnp.minimum(i, nlat - 1), 0))],
        out_specs=[pl.BlockSpec((tb, D_MODEL), lambda i: (i, 0)), pl.BlockSpec((None, 8, 128), lambda i: (i, 0, 0))],
        compiler_params=_params("parallel"), name="loss_and_grad")(X, target)
    return jnp.sum(part[:, 0, 0]), dx


Q_BLOCK = WIN_ROWS * GRID_W
BAND = 2 * WIN_ROWS * GRID_W


PAIR_TILES = 2 * WIN_ROWS
HEAD_PAIRS = N_HEADS // 2
LANES = 2 * HEAD_DIM
ROW_SHIFT = GRID_W.bit_length() - 1


def bias_pair_tiles(rpb):
    col = np.arange(GRID_W)
    col_start = np.clip(col - WIN_COLS // 2, 0, GRID_W - WIN_COLS)
    in_win = (col[None, :] >= col_start[:, None]) & (col[None, :] < col_start[:, None] + WIN_COLS)
    dcol = np.clip(col[None, :] - col[:, None] + (WIN_COLS - 1), 0, 2 * WIN_COLS - 2)
    E = np.stack([(dcol == dc) & in_win for dc in range(2 * WIN_COLS - 1)]).astype(np.float32)
    tiles = jnp.einsum("...rd,dqk->...rqk", rpb, E, precision=lax.Precision.HIGHEST)
    z = jnp.zeros(tiles.shape[:-3] + (1, GRID_W, GRID_W), F32)
    return jnp.concatenate([jnp.concatenate([z, tiles], axis=-3), jnp.concatenate([tiles, z], axis=-3)], axis=-1)


def _band_row(i, rows):
    return jnp.clip(WIN_ROWS * i - WIN_ROWS // 2, 0, rows - 2 * WIN_ROWS)


def _band_start(i, rows):
    return pl.multiple_of(_band_row(i, rows) * GRID_W, 256)


def _window_mask(i, rows):
    r = lax.broadcasted_iota(jnp.int32, (Q_BLOCK, BAND), 0)
    k = lax.broadcasted_iota(jnp.int32, (Q_BLOCK, BAND), 1)
    qr, qc = WIN_ROWS * i + (r >> ROW_SHIFT), r & (GRID_W - 1)
    kr, kc = _band_row(i, rows) + (k >> ROW_SHIFT), k & (GRID_W - 1)
    ws = jnp.clip(qr - WIN_ROWS // 2, 0, rows - WIN_ROWS)
    cs = jnp.clip(qc - WIN_COLS // 2, 0, GRID_W - WIN_COLS)
    return (kr >= ws) & (kr < ws + WIN_ROWS) & (kc >= cs) & (kc < cs + WIN_COLS)


def _pair_index(i, rows, a, j):
    off = _band_row(i, rows) - WIN_ROWS * i
    return jnp.clip(2 * j - a + WIN_ROWS + off, 0, PAIR_TILES - 1)


def _band_bias(p_ref, hh, i, rows):
    bands = [jnp.concatenate([p_ref[hh, _pair_index(i, rows, a, j)] for j in range(WIN_ROWS)], axis=1)
             for a in range(WIN_ROWS)]
    return jnp.concatenate(bands, axis=0)


def _dot_nt(a, b):
    return lax.dot_general(a, b, (((1,), (1,)), ((), ())), preferred_element_type=F32)


def _dot_tn(a, b):
    return lax.dot_general(a, b, (((0,), (0,)), ((), ())), preferred_element_type=F32)


def _dot(a, b):
    return jnp.dot(a, b, preferred_element_type=F32)


QKV_BLOCKS = tuple(COL[n] // LANES for n in ("q", "k", "v"))
SCALE = HEAD_DIM ** -0.5


def _head(x, hh):
    return x[:, hh * HEAD_DIM:(hh + 1) * HEAD_DIM]


def _both(fn):
    res = [fn(0), fn(1)]
    return [jnp.concatenate([a, b], axis=1) for a, b in zip(*res)]


def attn_fwd(proj, ptiles, *, L):
    T = proj.shape[0]
    N = T - L
    rows, nq = L // GRID_W, L // Q_BLOCK
    qb, kb, vb = QKV_BLOCKS

    def body(q_ref, k_ref, v_ref, p_ref, o_ref, lse_ref):
        i = pl.program_id(1)
        ks = _band_start(i, rows)
        mask = _window_mask(i, rows)
        qv = q_ref[...].astype(MM_DTYPE)
        kband, vband = k_ref[pl.ds(ks, BAND), :].astype(MM_DTYPE), v_ref[pl.ds(ks, BAND), :].astype(MM_DTYPE)
        kctx, vctx = k_ref[pl.ds(L, N), :].astype(MM_DTYPE), v_ref[pl.ds(L, N), :].astype(MM_DTYPE)

        def head(hh):
            q = _head(qv, hh)
            sb = _dot_nt(q, _head(kband, hh)) * SCALE + jnp.where(mask, _band_bias(p_ref, hh, i, rows), NEG_INF)
            sc = _dot_nt(q, _head(kctx, hh)) * SCALE
            m = jnp.maximum(jnp.max(sb, axis=-1, keepdims=True), jnp.max(sc, axis=-1, keepdims=True))
            pb, pc = jnp.exp(sb - m), jnp.exp(sc - m)
            l = jnp.sum(pb, axis=-1, keepdims=True) + jnp.sum(pc, axis=-1, keepdims=True)
            o = _dot(pb.astype(MM_DTYPE), _head(vband, hh)) + _dot(pc.astype(MM_DTYPE), _head(vctx, hh))
            return o / l, jnp.broadcast_to(m + jnp.log(l), (Q_BLOCK, HEAD_DIM))

        o_ref[...], lse_ref[...] = _both(head)

    qspec = lambda b0: pl.BlockSpec((Q_BLOCK, LANES), lambda hp, i: (i, b0 + hp))
    kspec = lambda b0: pl.BlockSpec((T, LANES), lambda hp, i: (0, b0 + hp))
    return pl.pallas_call(
        body, out_shape=[jax.ShapeDtypeStruct((T, BRANCH), F32), jax.ShapeDtypeStruct((L, BRANCH), F32)],
        grid=(HEAD_PAIRS, nq),
        in_specs=[qspec(qb), kspec(kb), kspec(vb),
                  pl.BlockSpec((2, PAIR_TILES, GRID_W, LANES), lambda hp, i: (hp, 0, 0, 0))],
        out_specs=[qspec(0), qspec(0)],
        compiler_params=_params("parallel", "arbitrary"), name="attn_fwd")(proj, proj, proj, ptiles)


def attn_bwd(proj, ptiles, o, do, lse, *, L):
    T = proj.shape[0]
    N = T - L
    rows, nq = L // GRID_W, L // Q_BLOCK
    qb, kb, vb = QKV_BLOCKS

    def body(q_ref, k_ref, v_ref, p_ref, o_ref, do_ref, lse_ref, dq_ref, dk_ref, dv_ref, dp_ref):
        i = pl.program_id(1)
        ks = _band_start(i, rows)

        @pl.when(i == 0)
        def _():
            dk_ref[...] = jnp.zeros(dk_ref.shape, F32)
            dv_ref[...] = jnp.zeros(dv_ref.shape, F32)
            dp_ref[...] = jnp.zeros(dp_ref.shape, F32)

        mask = _window_mask(i, rows)
        qv = q_ref[...].astype(MM_DTYPE)
        kband, vband = k_ref[pl.ds(ks, BAND), :].astype(MM_DTYPE), v_ref[pl.ds(ks, BAND), :].astype(MM_DTYPE)
        kctx, vctx = k_ref[pl.ds(L, N), :].astype(MM_DTYPE), v_ref[pl.ds(L, N), :].astype(MM_DTYPE)
        ov, dof, lsev = o_ref[...], do_ref[...], lse_ref[...]

        def head(hh):
            q, kb_h, kc_h, vb_h, vc_h = (_head(t, hh) for t in (qv, kband, kctx, vband, vctx))
            lse = _head(lsev, hh)[:, 0:1]
            pb = jnp.exp(_dot_nt(q, kb_h) * SCALE + jnp.where(mask, _band_bias(p_ref, hh, i, rows), NEG_INF) - lse)
            pc = jnp.exp(_dot_nt(q, kc_h) * SCALE - lse)
            do_h = _head(dof, hh)
            delta = jnp.sum(do_h * _head(ov, hh), axis=-1, keepdims=True)
            dov = do_h.astype(MM_DTYPE)
            dsb = pb * (_dot_nt(dov, vb_h) - delta)
            dsc = pc * (_dot_nt(dov, vc_h) - delta)
            for a in range(WIN_ROWS):
                for j in range(WIN_ROWS):
                    dp_ref[hh, _pair_index(i, rows, a, j)] += dsb[a * GRID_W:(a + 1) * GRID_W, j * LANES:(j + 1) * LANES]
            dsb_s, dsc_s = (dsb * SCALE).astype(MM_DTYPE), (dsc * SCALE).astype(MM_DTYPE)
            dq = _dot(dsb_s, kb_h) + _dot(dsc_s, kc_h)
            return (dq, _dot_tn(dsb_s, q), _dot_tn(dsc_s, q), _dot_tn(pb.astype(MM_DTYPE), dov),
                    _dot_tn(pc.astype(MM_DTYPE), dov))

        dq, dkb, dkc, dvb, dvc = _both(head)
        dq_ref[...] = dq
        dk_ref[pl.ds(ks, BAND), :] += dkb
        dk_ref[pl.ds(L, N), :] += dkc
        dv_ref[pl.ds(ks, BAND), :] += dvb
        dv_ref[pl.ds(L, N), :] += dvc

    qspec = lambda b0: pl.BlockSpec((Q_BLOCK, LANES), lambda hp, i: (i, b0 + hp))
    kspec = lambda b0: pl.BlockSpec((T, LANES), lambda hp, i: (0, b0 + hp))
    pspec = pl.BlockSpec((2, PAIR_TILES, GRID_W, LANES), lambda hp, i: (hp, 0, 0, 0))
    return pl.pallas_call(
        body,
        out_shape=[jax.ShapeDtypeStruct((T, BRANCH), F32)] * 3 + [jax.ShapeDtypeStruct(ptiles.shape, F32)],
        grid=(HEAD_PAIRS, nq),
        in_specs=[qspec(qb), kspec(kb), kspec(vb), pspec, qspec(0), qspec(0), qspec(0)],
        out_specs=[qspec(0), kspec(0), kspec(0), pspec],
        compiler_params=_params("parallel", "arbitrary"), name="attn_bwd")(proj, proj, proj, ptiles, o, do, lse)


ANY_SPEC = pl.BlockSpec(memory_space=pl.ANY)


def cattn_fwd(proj, o, *, L):
    T = proj.shape[0]
    N = T - L
    qb, kb, vb = QKV_BLOCKS
    cspec = lambda b0: pl.BlockSpec((N, LANES), lambda hp: (L // N, b0 + hp))

    def body(q_ref, k_ref, v_ref, o_in, o_ref, lse_ref):
        qv, kv, vv = (r[...].astype(MM_DTYPE) for r in (q_ref, k_ref, v_ref))

        def head(hh):
            s = _dot_nt(_head(qv, hh), _head(kv, hh)) * SCALE
            m = jnp.max(s, axis=-1, keepdims=True)
            p = jnp.exp(s - m)
            l = jnp.sum(p, axis=-1, keepdims=True)
            return _dot(p.astype(MM_DTYPE), _head(vv, hh)) / l, jnp.broadcast_to(m + jnp.log(l), (N, HEAD_DIM))

        o_ref[...], lse_ref[...] = _both(head)

    return pl.pallas_call(
        body, out_shape=[jax.ShapeDtypeStruct(o.shape, F32), jax.ShapeDtypeStruct((N, BRANCH), F32)],
        grid=(HEAD_PAIRS,), in_specs=[cspec(qb), cspec(kb), cspec(vb), ANY_SPEC],
        out_specs=[cspec(0), pl.BlockSpec((N, LANES), lambda hp: (0, hp))], input_output_aliases={3: 0},
        compiler_params=_params("parallel"), name="cattn_fwd")(proj, proj, proj, o)


def cattn_bwd(proj, o, do, lse, dq, dk, dv, *, L):
    T = proj.shape[0]
    N = T - L
    qb, kb, vb = QKV_BLOCKS
    cspec = lambda b0: pl.BlockSpec((N, LANES), lambda hp: (L // N, b0 + hp))

    def body(q_ref, k_ref, v_ref, o_ref, do_ref, lse_ref, dq_in, dk_in, dv_in, dq_ref, dk_ref, dv_ref):
        qv, kv, vv = (r[...].astype(MM_DTYPE) for r in (q_ref, k_ref, v_ref))
        ov, dof, lsev = o_ref[...], do_ref[...], lse_ref[...]

        def head(hh):
            q, k, v = _head(qv, hh), _head(kv, hh), _head(vv, hh)
            p = jnp.exp(_dot_nt(q, k) * SCALE - _head(lsev, hh)[:, 0:1])
            do_h = _head(dof, hh)
            delta = jnp.sum(do_h * _head(ov, hh), axis=-1, keepdims=True)
            dov = do_h.astype(MM_DTYPE)
            ds = (p * (_dot_nt(dov, v) - delta) * SCALE).astype(MM_DTYPE)
            return _dot(ds, k), _dot_tn(ds, q), _dot_tn(p.astype(MM_DTYPE), dov)

        dq_c, dk_c, dv_c = _both(head)
        dq_ref[...] = dq_c
        dk_ref[...] = dk_in[...] + dk_c
        dv_ref[...] = dv_in[...] + dv_c

    return pl.pallas_call(
        body, out_shape=[jax.ShapeDtypeStruct(dq.shape, F32)] * 3, grid=(HEAD_PAIRS,),
        in_specs=[cspec(qb), cspec(kb), cspec(vb), cspec(0), cspec(0), pl.BlockSpec((N, LANES), lambda hp: (0, hp)),
                  ANY_SPEC, cspec(0), cspec(0)],
        out_specs=[cspec(0)] * 3, input_output_aliases={6: 0, 7: 1, 8: 2},
        compiler_params=_params("parallel"), name="cattn_bwd")(proj, proj, proj, o, do, lse, dq, dk, dv)


PAD = 16


def _row_ids(T):
    return lax.broadcasted_iota(jnp.int32, (T, POOL_DIM), 0)


def _same_segment(t, s, L, T):
    return (s >= 0) & (s < T) & ((t < L) == (s < L))


def _window_sum(buf_ref, x, half, *, L, T, transpose):
    buf_ref[pl.ds(PAD, T), :] = x
    t = _row_ids(T)
    acc = jnp.zeros((T, POOL_DIM), F32)
    for j in range(-8, 9):
        inside = ((j > -half) & (j <= half)) if transpose else ((j >= -half) & (j < half))
        ok = _same_segment(t, t + j, L, T) & inside
        acc = acc + jnp.where(ok, buf_ref[pl.ds(PAD + j, T), :], 0.0)
    return acc


def _window_count(half, *, L, T):
    t = _row_ids(T)
    pos = jnp.where(t < L, t, t - L)
    seg_len = jnp.where(t < L, L, T - L)
    return (jnp.minimum(pos + half, seg_len) - jnp.maximum(pos - half, 0)).astype(F32)


def _zero_pads(buf_ref, T):
    buf_ref[pl.ds(0, PAD), :] = jnp.zeros((PAD, POOL_DIM), F32)
    buf_ref[pl.ds(PAD + T, PAD), :] = jnp.zeros((PAD, POOL_DIM), F32)


def pool_fwd(proj, pool_w, pool_scale, *, L):
    T = proj.shape[0]
    cb0 = COL["pool_u"] // POOL_DIM

    def body(u_ref, w_ref, s_ref, o_ref, p_ref, buf_ref):
        half = jnp.left_shift(1, pl.program_id(0))
        _zero_pads(buf_ref, T)
        u = u_ref[...].astype(F32)
        pooled = _window_sum(buf_ref, u, half, L=L, T=T, transpose=False) / _window_count(half, L=L, T=T) - u
        pm = pooled.astype(MM_DTYPE)
        p_ref[...] = pm
        o_ref[...] = _dot(pm, w_ref[...].astype(MM_DTYPE)) * s_ref[...]

    cspec = pl.BlockSpec((T, POOL_DIM), lambda g: (0, g))
    return pl.pallas_call(
        body, out_shape=[jax.ShapeDtypeStruct((T, BRANCH), F32), jax.ShapeDtypeStruct((T, BRANCH), MM_DTYPE)],
        grid=(POOL_GROUPS,),
        in_specs=[pl.BlockSpec((T, POOL_DIM), lambda g: (0, cb0 + g)),
                  pl.BlockSpec((None, POOL_DIM, POOL_DIM), lambda g: (g, 0, 0)),
                  pl.BlockSpec((1, POOL_DIM), lambda g: (0, g))],
        out_specs=[cspec, cspec], scratch_shapes=[pltpu.VMEM((T + 2 * PAD, POOL_DIM), F32)],
        compiler_params=_params("parallel"), name="pool_fwd")(proj, pool_w, pool_scale)


def pool_bwd(do, pooled, pool_w, pool_scale, *, L):
    T = do.shape[0]

    def body(do_ref, p_ref, w_ref, s_ref, du_ref, dw_ref, ds_ref, buf_ref):
        half = jnp.left_shift(1, pl.program_id(0))
        _zero_pads(buf_ref, T)
        pm = p_ref[...]
        w = w_ref[...].astype(MM_DTYPE)
        mixed = _dot(pm, w)
        dov = do_ref[...]
        ds_ref[...] = jnp.broadcast_to(jnp.sum(dov * mixed, axis=0, keepdims=True), ds_ref.shape)
        dmixed = (dov * s_ref[...]).astype(MM_DTYPE)
        dw_ref[...] = _dot_tn(pm, dmixed)
        dpooled = _dot_nt(dmixed, w)
        scaled = dpooled / _window_count(half, L=L, T=T)
        du = _window_sum(buf_ref, scaled, half, L=L, T=T, transpose=True) - dpooled
        du_ref[...] = du.astype(du_ref.dtype)

    cspec = pl.BlockSpec((T, POOL_DIM), lambda g: (0, g))
    return pl.pallas_call(
        body, out_shape=[jax.ShapeDtypeStruct((T, BRANCH), MM_DTYPE),
                         jax.ShapeDtypeStruct((POOL_GROUPS, POOL_DIM, POOL_DIM), F32),
                         jax.ShapeDtypeStruct((8, BRANCH), F32)],
        grid=(POOL_GROUPS,),
        in_specs=[cspec, cspec, pl.BlockSpec((None, POOL_DIM, POOL_DIM), lambda g: (g, 0, 0)),
                  pl.BlockSpec((1, POOL_DIM), lambda g: (0, g))],
        out_specs=[cspec, pl.BlockSpec((None, POOL_DIM, POOL_DIM), lambda g: (g, 0, 0)),
                   pl.BlockSpec((8, POOL_DIM), lambda g: (0, g))],
        scratch_shapes=[pltpu.VMEM((T + 2 * PAD, POOL_DIM), F32)],
        compiler_params=_params("parallel"), name="pool_bwd")(do, pooled, pool_w, pool_scale)


def _shifted(buf_ref, x, j, *, L, T):
    buf_ref[pl.ds(PAD, T), :] = x
    t = _row_ids(T)
    return jnp.where(_same_segment(t, t + j, L, T), buf_ref[pl.ds(PAD + j, T), :], 0.0)


def conv_fwd(proj, conv_w, *, L):
    T = proj.shape[0]
    nb = BRANCH // POOL_DIM
    cx, cbb, cc = (COL[n] // POOL_DIM for n in ("conv_x", "conv_b", "conv_c"))

    def body(x_ref, b_ref, c_ref, w_ref, o_ref, buf_ref):
        _zero_pads(buf_ref, T)
        xc = c_ref[...].astype(F32) * x_ref[...].astype(F32)
        w = w_ref[...]
        conv = (w[0:1] * _shifted(buf_ref, xc, -1, L=L, T=T) + w[1:2] * xc
                + w[2:3] * _shifted(buf_ref, xc, 1, L=L, T=T))
        o_ref[...] = b_ref[...].astype(F32) * conv

    return pl.pallas_call(
        body, out_shape=jax.ShapeDtypeStruct((T, BRANCH), F32), grid=(nb,),
        in_specs=[pl.BlockSpec((T, POOL_DIM), lambda g: (0, cx + g)), pl.BlockSpec((T, POOL_DIM), lambda g: (0, cbb + g)),
                  pl.BlockSpec((T, POOL_DIM), lambda g: (0, cc + g)), pl.BlockSpec((8, POOL_DIM), lambda g: (0, g))],
        out_specs=pl.BlockSpec((T, POOL_DIM), lambda g: (0, g)),
        scratch_shapes=[pltpu.VMEM((T + 2 * PAD, POOL_DIM), F32)],
        compiler_params=_params("parallel"), name="conv_fwd")(proj, proj, proj, conv_w)


def conv_bwd(do, proj, conv_w, *, L):
    T = proj.shape[0]
    nb = BRANCH // POOL_DIM
    cx, cbb, cc = (COL[n] // POOL_DIM for n in ("conv_x", "conv_b", "conv_c"))

    def body(do_ref, x_ref, b_ref, c_ref, w_ref, dx_ref, db_ref, dc_ref, dw_ref, buf_ref):
        _zero_pads(buf_ref, T)
        xv, gb, gc = (r[...].astype(F32) for r in (x_ref, b_ref, c_ref))
        xc = gc * xv
        w = w_ref[...]
        xm = _shifted(buf_ref, xc, -1, L=L, T=T)
        xp = _shifted(buf_ref, xc, 1, L=L, T=T)
        conv = w[0:1] * xm + w[1:2] * xc + w[2:3] * xp
        dov = do_ref[...]
        db_ref[...] = (dov * conv).astype(db_ref.dtype)
        dconv = dov * gb
        sums = [jnp.sum(dconv * a, axis=0, keepdims=True) for a in (xm, xc, xp)]
        dw_ref[...] = jnp.concatenate(sums + [jnp.zeros((5, POOL_DIM), F32)], axis=0)
        dxc = (w[0:1] * _shifted(buf_ref, dconv, 1, L=L, T=T) + w[1:2] * dconv
               + w[2:3] * _shifted(buf_ref, dconv, -1, L=L, T=T))
        dc_ref[...] = (dxc * xv).astype(dc_ref.dtype)
        dx_ref[...] = (dxc * gc).astype(dx_ref.dtype)

    ospec = lambda off: pl.BlockSpec((T, POOL_DIM), lambda g: (0, off + g))
    return pl.pallas_call(
        body, out_shape=[jax.ShapeDtypeStruct((T, BRANCH), MM_DTYPE)] * 3 + [jax.ShapeDtypeStruct((8, BRANCH), F32)],
        grid=(nb,),
        in_specs=[ospec(0), ospec(cx), ospec(cbb), ospec(cc), pl.BlockSpec((8, POOL_DIM), lambda g: (0, g))],
        out_specs=[ospec(0), ospec(0), ospec(0), pl.BlockSpec((8, POOL_DIM), lambda g: (0, g))],
        scratch_shapes=[pltpu.VMEM((T + 2 * PAD, POOL_DIM), F32)],
        compiler_params=_params("parallel"), name="conv_bwd")(do, proj, proj, proj, conv_w)


SCAN_COLS = 1024
SCAN_ROWS = 256


def ssm_operators(a_re, a_im, log_dt, b_re, b_im, c_re, c_im):
    n = a_re.shape[0]
    dt = jnp.exp(log_dt)[..., None]
    mag = jnp.exp(a_re * dt)
    abar_re, abar_im = mag * jnp.cos(a_im * dt), mag * jnp.sin(a_im * dt)
    den = a_re * a_re + a_im * a_im
    num_re, num_im = abar_re - 1.0, abar_im
    f_re = (num_re * a_re + num_im * a_im) / den
    f_im = (num_im * a_re - num_re * a_im) / den
    bbar_re = f_re[..., None] * b_re - f_im[..., None] * b_im
    bbar_im = f_re[..., None] * b_im + f_im[..., None] * b_re
    gpb = SSM_GROUPS // SSM_BLOCKS
    eye = jnp.eye(gpb, dtype=bool)[None, None, :, None, :, None]

    def blocks(t):
        _, _, a, b = t.shape
        t = t.reshape(n, SSM_BLOCKS, gpb, a, 1, b)
        return jnp.where(eye, t, 0.0).reshape(n, SSM_BLOCKS, gpb * a, gpb * b)

    in_map = lambda bbar: blocks(bbar.transpose(0, 1, 3, 2))
    out_map = lambda c: blocks(c.transpose(0, 1, 3, 2))
    abar = jnp.concatenate([abar_re.reshape(n, 1, N_STATE), abar_im.reshape(n, 1, N_STATE)], axis=-1)
    bcat = jnp.concatenate([in_map(bbar_re), in_map(bbar_im)], axis=1)
    ccat = jnp.concatenate([out_map(c_re), -out_map(c_im)], axis=1)
    return abar, bcat, ccat


SSM_BLOCKS = 4
SSM_BCH = BRANCH // SSM_BLOCKS
SSM_BST = N_STATE // SSM_BLOCKS


def _ssm_rows(T, perm):
    tm = ROW_TILE if perm else _row_tile(T)
    return tm, (perm or (lambda i: i))


def _lanes(x, n, width):
    return x[:, n * width:(n + 1) * width]


def _ssm_specs(T, perm, ucol0=None):
    tm, rows = _ssm_rows(T, perm)
    chan = pl.BlockSpec((tm, BRANCH), lambda i: (rows(i), 0 if ucol0 is None else ucol0 // BRANCH))
    state = pl.BlockSpec((tm, 2 * N_STATE), lambda i: (i, 0))
    bspec = pl.BlockSpec((2 * SSM_BLOCKS, SSM_BCH, SSM_BST), lambda i: (0, 0, 0))
    cspec = pl.BlockSpec((2 * SSM_BLOCKS, SSM_BST, SSM_BCH), lambda i: (0, 0, 0))
    return T // tm, chan, state, bspec, cspec


def ssm_in(u, bcat, *, ucol0, perm, name):
    T = u.shape[0]
    steps, chan, state, bspec, _ = _ssm_specs(T, perm, ucol0)

    def body(u_ref, b_ref, o_ref):
        uv = u_ref[...].astype(MM_DTYPE)
        for n in range(2 * SSM_BLOCKS):
            o_ref[:, n * SSM_BST:(n + 1) * SSM_BST] = _dot(_lanes(uv, n % SSM_BLOCKS, SSM_BCH), b_ref[n].astype(MM_DTYPE))

    return pl.pallas_call(body, out_shape=jax.ShapeDtypeStruct((T, 2 * N_STATE), F32), grid=(steps,),
                          in_specs=[chan, bspec], out_specs=state, compiler_params=_params("parallel"), name=name)(u, bcat)


def ssm_out(s, ccat, *, perm, name):
    T = s.shape[0]
    steps, chan, state, _, cspec = _ssm_specs(T, perm)

    def body(s_ref, c_ref, o_ref):
        sv = s_ref[...].astype(MM_DTYPE)
        o_ref[...] = jnp.concatenate(
            [_dot(_lanes(sv, j, SSM_BST), c_ref[j].astype(MM_DTYPE))
             + _dot(_lanes(sv, SSM_BLOCKS + j, SSM_BST), c_ref[SSM_BLOCKS + j].astype(MM_DTYPE))
             for j in range(SSM_BLOCKS)], axis=1)

    return pl.pallas_call(body, out_shape=jax.ShapeDtypeStruct((T, BRANCH), F32), grid=(steps,),
                          in_specs=[state, cspec], out_specs=chan, compiler_params=_params("parallel"), name=name)(s, ccat)


def ssm_out_dx(dy, ccat, *, perm, name):
    T = dy.shape[0]
    steps, chan, state, _, cspec = _ssm_specs(T, perm)

    def body(d_ref, c_ref, o_ref):
        dv = d_ref[...].astype(MM_DTYPE)
        for n in range(2 * SSM_BLOCKS):
            o_ref[:, n * SSM_BST:(n + 1) * SSM_BST] = _dot_nt(_lanes(dv, n % SSM_BLOCKS, SSM_BCH), c_ref[n].astype(MM_DTYPE))

    return pl.pallas_call(body, out_shape=jax.ShapeDtypeStruct((T, 2 * N_STATE), F32), grid=(steps,),
                          in_specs=[chan, cspec], out_specs=state, compiler_params=_params("parallel"), name=name)(dy, ccat)


def ssm_in_dx(lam, bcat, *, perm, name):
    T = lam.shape[0]
    steps, chan, state, bspec, _ = _ssm_specs(T, perm)

    def body(l_ref, b_ref, o_ref):
        lv = l_ref[...].astype(MM_DTYPE)
        o_ref[...] = jnp.concatenate(
            [_dot_nt(_lanes(lv, j, SSM_BST), b_ref[j].astype(MM_DTYPE))
             + _dot_nt(_lanes(lv, SSM_BLOCKS + j, SSM_BST), b_ref[SSM_BLOCKS + j].astype(MM_DTYPE))
             for j in range(SSM_BLOCKS)], axis=1)

    return pl.pallas_call(body, out_shape=jax.ShapeDtypeStruct((T, BRANCH), F32), grid=(steps,),
                          in_specs=[state, bspec], out_specs=chan, compiler_params=_params("parallel"), name=name)(lam, bcat)


def _ssm_dw(chan_arr, state_arr, chan_spec, state_spec, out_block, steps, chan_first, name):
    def body(c_ref, s_ref, o_ref):
        @pl.when(pl.program_id(0) == 0)
        def _():
            o_ref[...] = jnp.zeros(o_ref.shape, F32)

        cv, sv = c_ref[...].astype(MM_DTYPE), s_ref[...].astype(MM_DTYPE)
        for n in range(2 * SSM_BLOCKS):
            c, s = _lanes(cv, n % SSM_BLOCKS, SSM_BCH), _lanes(sv, n, SSM_BST)
            o_ref[n] += _dot_tn(c, s) if chan_first else _dot_tn(s, c)

    shape = (2 * SSM_BLOCKS,) + out_block
    return pl.pallas_call(body, out_shape=jax.ShapeDtypeStruct(shape, F32), grid=(steps,),
                          in_specs=[chan_spec, state_spec], out_specs=pl.BlockSpec(shape, lambda k: (0, 0, 0)),
                          compiler_params=_params("arbitrary"), name=name)(chan_arr, state_arr)


def ssm_in_dw(u, lam, *, ucol0, perm, name):
    steps, chan, state, _, _ = _ssm_specs(u.shape[0], perm, ucol0)
    return _ssm_dw(u, lam, chan, state, (SSM_BCH, SSM_BST), steps, True, name)


def ssm_out_dw(s, dy, *, perm, name):
    steps, chan, state, _, _ = _ssm_specs(s.shape[0], perm)
    return _ssm_dw(dy, s, chan, state, (SSM_BST, SSM_BCH), steps, False, name)


def _time_block(T, reverse):
    nt = T // SCAN_ROWS
    tix = (lambda i: nt - 1 - i) if reverse else (lambda i: i)
    return nt, pl.BlockSpec((SCAN_ROWS, 2 * N_STATE), lambda i: (tix(i), 0))


def ssm_scan(bu, abar, *, reverse):
    T = bu.shape[0]
    nt, tspec = _time_block(T, reverse)

    def body(b_ref, a_ref, s_ref, c_ref):
        @pl.when(pl.program_id(0) == 0)
        def _():
            c_ref[...] = jnp.zeros(c_ref.shape, F32)

        for c0 in range(0, N_STATE, SCAN_COLS):
            re, im = pl.ds(c0, SCAN_COLS), pl.ds(N_STATE + c0, SCAN_COLS)
            ar, ai = a_ref[:, re], a_ref[:, im]

            def step(n, carry, re=re, im=im, ar=ar, ai=ai):
                sr, si = carry
                t = (SCAN_ROWS - 1 - n) if reverse else n
                nr = ar * sr - ai * si + b_ref[pl.ds(t, 1), re]
                ni = ar * si + ai * sr + b_ref[pl.ds(t, 1), im]
                s_ref[pl.ds(t, 1), re] = nr
                s_ref[pl.ds(t, 1), im] = ni
                return nr, ni

            sr, si = lax.fori_loop(0, SCAN_ROWS, step, (c_ref[:, re], c_ref[:, im]))
            c_ref[:, re] = sr
            c_ref[:, im] = si

    return pl.pallas_call(
        body, out_shape=jax.ShapeDtypeStruct((T, 2 * N_STATE), F32), grid=(nt,),
        in_specs=[tspec, pl.BlockSpec((1, 2 * N_STATE), lambda i: (0, 0))], out_specs=tspec,
        scratch_shapes=[pltpu.VMEM((1, 2 * N_STATE), F32)],
        compiler_params=_params("arbitrary"), name="ssm_scan_rev" if reverse else "ssm_scan_fwd")(bu, abar)


def ssm_scan_bwd(g, s, abar, *, reverse):
    T = g.shape[0]
    nt, tspec = _time_block(T, not reverse)
    back = not reverse

    def body(g_ref, s_ref, a_ref, l_ref, da_ref, c_ref):
        @pl.when(pl.program_id(0) == 0)
        def _():
            c_ref[...] = jnp.zeros(c_ref.shape, F32)
            da_ref[...] = jnp.zeros(da_ref.shape, F32)

        for c0 in range(0, N_STATE, SCAN_COLS):
            re, im = pl.ds(c0, SCAN_COLS), pl.ds(N_STATE + c0, SCAN_COLS)
            ar, ai = a_ref[:, re], a_ref[:, im]

            def step(n, carry, re=re, im=im, ar=ar, ai=ai):
                lr, li, dr, di = carry
                t = (SCAN_ROWS - 1 - n) if back else n
                sr, si = s_ref[pl.ds(t, 1), re], s_ref[pl.ds(t, 1), im]
                dr = dr + sr * lr + si * li
                di = di + sr * li - si * lr
                nr = g_ref[pl.ds(t, 1), re] + ar * lr + ai * li
                ni = g_ref[pl.ds(t, 1), im] + ar * li - ai * lr
                l_ref[pl.ds(t, 1), re] = nr
                l_ref[pl.ds(t, 1), im] = ni
                return nr, ni, dr, di

            zero = jnp.zeros((1, SCAN_COLS), F32)
            lr, li, dr, di = lax.fori_loop(0, SCAN_ROWS, step, (c_ref[:, re], c_ref[:, im], zero, zero))
            c_ref[:, re] = lr
            c_ref[:, im] = li
            da_ref[:, re] += jnp.broadcast_to(dr, (8, SCAN_COLS))
            da_ref[:, im] += jnp.broadcast_to(di, (8, SCAN_COLS))

    return pl.pallas_call(
        body, out_shape=[jax.ShapeDtypeStruct((T, 2 * N_STATE), F32), jax.ShapeDtypeStruct((8, 2 * N_STATE), F32)],
        grid=(nt,), in_specs=[tspec, tspec, pl.BlockSpec((1, 2 * N_STATE), lambda i: (0, 0))],
        out_specs=[tspec, pl.BlockSpec((8, 2 * N_STATE), lambda i: (0, 0))],
        scratch_shapes=[pltpu.VMEM((1, 2 * N_STATE), F32)],
        compiler_params=_params("arbitrary"),
        name="ssm_scan_bwd_rev" if reverse else "ssm_scan_bwd_fwd")(g, s, abar)


def _gelu(x):
    return 0.5 * x * (1.0 + jnp.tanh(0.7978845608028654 * (x + 0.044715 * x * x * x)))


def _gelu_grad(x):
    t = jnp.tanh(0.7978845608028654 * (x + 0.044715 * x * x * x))
    return 0.5 * (1.0 + t) + 0.5 * x * (1.0 - t * t) * 0.7978845608028654 * (1.0 + 3 * 0.044715 * x * x)


def _silu(z):
    return z * _sigmoid(z)


def _silu_grad(z):
    s = _sigmoid(z)
    return s * (1.0 + z * (1.0 - s))


def ssm_fwd(proj, ops, dsk, glu_w, *, L):
    T = proj.shape[0]
    abar, bcat, ccat = ops
    nb, nlat = T // ROW_TILE, L // ROW_TILE
    to_f = lambda i: (i + nlat) % nb
    states, ys = [], []
    for d in (0, 1):
        perm = to_f if d == 0 else None
        bu = ssm_in(proj, bcat[d], ucol0=COL["ssm_u"], perm=perm, name=f"ssm_in{d}")
        s = ssm_scan(bu, abar[d], reverse=(d == 1))
        states.append(s)
        ys.append(ssm_out(s, ccat[d], perm=perm, name=f"ssm_out{d}"))

    def pre(u, yf, yr, dsk):
        y = dsk * u + yf + yr
        return y, _gelu(y)

    ypre, gy = _ew(pre, [(proj, "row", COL["ssm_u"] // BRANCH, BRANCH), (ys[0], "row", 0, BRANCH),
                         (ys[1], "row", 0, BRANCH), (dsk, "bcast", 0, BRANCH)],
                   [(BRANCH, F32), (BRANCH, MM_DTYPE)], [], T=T, L=L, name="ssm_pre")
    gg = mm_nn(gy, glu_w, out_dtype=F32, name="ssm_glu")

    def post(ga, gb):
        return (ga * _sigmoid(gb),)

    o, = _ew(post, [(gg, "row", 0, BRANCH), (gg, "row", 1, BRANCH)], [(BRANCH, F32)], [], T=T, L=L, name="ssm_post")
    return o, dict(states=states, ypre=ypre, gy=gy, gg=gg)


def ssm_bwd(do, proj, ops, dsk, glu_w, saved, *, L):
    T = proj.shape[0]
    abar, bcat, ccat = ops
    nb, nlat = T // ROW_TILE, L // ROW_TILE
    to_f = lambda i: (i + nlat) % nb
    gg, gy, ypre = saved["gg"], saved["gy"], saved["ypre"]

    def post_bwd(do, ga, gb):
        sg = _sigmoid(gb)
        return (jnp.concatenate([do * sg, do * ga * sg * (1.0 - sg)], axis=1),)

    dgg, = _ew(post_bwd, [(do, "row", 0, BRANCH), (gg, "row", 0, BRANCH), (gg, "row", 1, BRANCH)],
               [(2 * BRANCH, MM_DTYPE)], [], T=T, L=L, name="ssm_post_bwd")
    dgy = mm_nt(dgg, glu_w, out_dtype=F32, name="ssm_glu_dx")
    dglu = mm_tn(gy, dgg, out_dtype=F32, name="ssm_glu_dw")

    def pre_bwd(dgy, y, u, dsk):
        dy = dgy * _gelu_grad(y)
        return dy, dy * dsk, dy * u

    dy, du_skip, dd = _ew(pre_bwd, [(dgy, "row", 0, BRANCH), (ypre, "row", 0, BRANCH),
                                    (proj, "row", COL["ssm_u"] // BRANCH, BRANCH), (dsk, "bcast", 0, BRANCH)],
                          [(BRANCH, MM_DTYPE), (BRANCH, F32)], [BRANCH], T=T, L=L, name="ssm_pre_bwd")
    du = du_skip
    dabar, dbcat, dccat = [], [], []
    for d in (0, 1):
        perm = to_f if d == 0 else None
        s = saved["states"][d]
        g = ssm_out_dx(dy, ccat[d], perm=perm, name=f"ssm_out{d}_dx")
        lam, da = ssm_scan_bwd(g, s, abar[d], reverse=(d == 1))
        dabar.append(da[0:1])
        dccat.append(ssm_out_dw(s, dy, perm=perm, name=f"ssm_out{d}_dw"))
        du = du + ssm_in_dx(lam, bcat[d], perm=perm, name=f"ssm_in{d}_dx")
        dbcat.append(ssm_in_dw(proj, lam, ucol0=COL["ssm_u"], perm=perm, name=f"ssm_in{d}_dw"))
    d_ops = (jnp.stack(dabar), jnp.stack(dbcat), jnp.stack(dccat))
    return du, d_ops, dd[0, 0] + dd[1, 0], dglu


Z_COLS = tuple(COL[n] // BRANCH for n in ("na_z", "pool_z", "conv_z", "ssm_z"))


def gate_act(o, proj, *, L):
    T = o.shape[0]

    def fn(o, z0, z1, z2, z3):
        return (o * _silu(jnp.concatenate([z0, z1, z2, z3], axis=1).astype(F32)),)

    a, = _ew(fn, [(o, "row", 0, D_MODEL)] + [(proj, "row", c, BRANCH) for c in Z_COLS],
             [(D_MODEL, MM_DTYPE)], [], T=T, L=L, name="gate_act")
    return a


def gate_act_bwd(da, o, proj, *, L):
    T = o.shape[0]

    def fn(da, o, z0, z1, z2, z3):
        z = jnp.concatenate([z0, z1, z2, z3], axis=1).astype(F32)
        return da * _silu(z), da * o * _silu_grad(z)

    return _ew(fn, [(da, "row", 0, D_MODEL), (o, "row", 0, D_MODEL)] + [(proj, "row", c, BRANCH) for c in Z_COLS],
               [(D_MODEL, F32), (D_MODEL, MM_DTYPE)], [], T=T, L=L, name="gate_act_bwd")


MERGE_TN = 1024


def merge_fwd(a, w_br, proj, b_gate):
    T = a.shape[0]
    tm, tn = _row_tile(T), MERGE_TN
    nn = D_MODEL // tn
    lb0 = COL["merge"] // tn

    def body(a_ref, w_ref, l_ref, b_ref, m_ref, br_ref, acc_ref):
        i = pl.program_id(2)
        br = _dot(a_ref[...].astype(MM_DTYPE), w_ref[...].astype(MM_DTYPE))
        br_ref[...] = br.astype(br_ref.dtype)
        term = _sigmoid(l_ref[...].astype(F32) + b_ref[...]) * br

        @pl.when(i == 0)
        def _():
            acc_ref[...] = term

        @pl.when(i > 0)
        def _():
            acc_ref[...] += term

        @pl.when(i == 3)
        def _():
            m_ref[...] = acc_ref[...].astype(m_ref.dtype)

    return pl.pallas_call(
        body, out_shape=[jax.ShapeDtypeStruct((T, D_MODEL), MM_DTYPE), jax.ShapeDtypeStruct((T, 4 * D_MODEL), MM_DTYPE)],
        grid=(T // tm, nn, 4),
        in_specs=[pl.BlockSpec((tm, BRANCH), lambda m, n, i: (m, i)),
                  pl.BlockSpec((BRANCH, tn), lambda m, n, i: (i, n)),
                  pl.BlockSpec((tm, tn), lambda m, n, i: (m, lb0 + i * nn + n)),
                  pl.BlockSpec((1, tn), lambda m, n, i: (0, i * nn + n))],
        out_specs=[pl.BlockSpec((tm, tn), lambda m, n, i: (m, n)), pl.BlockSpec((tm, tn), lambda m, n, i: (m, i * nn + n))],
        scratch_shapes=[pltpu.VMEM((tm, tn), F32)],
        compiler_params=_params("parallel", "parallel", "arbitrary"), name="merge_fwd")(a, w_br, proj, b_gate)


def merge_bwd(dmerged, br, proj, b_gate):
    T = dmerged.shape[0]
    tb = ROW_TILE
    lb0 = COL["merge"] // D_MODEL

    def body(dm_ref, br_ref, l_ref, b_ref, dbr_ref, dl_ref, db_ref):
        dm = dm_ref[...]
        gates = _sigmoid(l_ref[...] + b_ref[...])
        dbr_ref[...] = (dm * gates).astype(dbr_ref.dtype)
        dl = dm * br_ref[...] * gates * (1.0 - gates)
        dl_ref[...] = dl.astype(dl_ref.dtype)
        s = jnp.broadcast_to(jnp.sum(dl, axis=0, keepdims=True), db_ref.shape)

        @pl.when(pl.program_id(1) == 0)
        def _():
            db_ref[...] = s

        @pl.when(pl.program_id(1) > 0)
        def _():
            db_ref[...] += s

    wide = pl.BlockSpec((tb, D_MODEL), lambda b, i: (i, b))
    return pl.pallas_call(
        body, out_shape=[jax.ShapeDtypeStruct((T, 4 * D_MODEL), MM_DTYPE)] * 2 + [jax.ShapeDtypeStruct((8, 4 * D_MODEL), F32)],
        grid=(4, T // tb),
        in_specs=[pl.BlockSpec((tb, D_MODEL), lambda b, i: (i, 0)), wide,
                  pl.BlockSpec((tb, D_MODEL), lambda b, i: (i, lb0 + b)), pl.BlockSpec((1, D_MODEL), lambda b, i: (0, b))],
        out_specs=[wide, wide, pl.BlockSpec((8, D_MODEL), lambda b, i: (0, b))],
        compiler_params=_params("parallel", "arbitrary"), name="merge_bwd")(dmerged, br, proj, b_gate)


def branch_dx(dbr, w_br):
    T = dbr.shape[0]
    tm, tk = _row_tile(T), D_MODEL
    nk = D_MODEL // tk
    return _matmul(dbr, w_br, pl.BlockSpec((tm, tk), lambda m, i, k: (m, i * nk + k)),
                   pl.BlockSpec((BRANCH, tk), lambda m, i, k: (i, k)),
                   pl.BlockSpec((tm, BRANCH), lambda m, i, k: (m, i)),
                   jax.ShapeDtypeStruct((T, D_MODEL), F32), (T // tm, 4, nk), tb=True, name="branch_dx")


def branch_dw(a, dbr):
    T = a.shape[0]
    tk, tn = T, 1024
    nn = D_MODEL // tn
    return _matmul(a, dbr, pl.BlockSpec((tk, BRANCH), lambda i, n, k: (k, i)),
                   pl.BlockSpec((tk, tn), lambda i, n, k: (k, i * nn + n)),
                   pl.BlockSpec((BRANCH, tn), lambda i, n, k: (i, n)),
                   jax.ShapeDtypeStruct((D_MODEL, D_MODEL), F32), (4, nn, T // tk), ta=True, name="branch_dw")


def proj_fwd(h, w_in, carried=None):
    T = h.shape[0]
    tm, tn = _row_tile(T), 1792
    per = W_IN_SHARD // tn
    return _matmul(h, w_in, pl.BlockSpec((tm, D_MODEL), lambda i, j, k: (i, 0)),
                   pl.BlockSpec((None, D_MODEL, tn), lambda i, j, k: (j // per, 0, j % per)),
                   pl.BlockSpec((tm, tn), lambda i, j, k: (i, j)),
                   jax.ShapeDtypeStruct((T, IN_TOTAL), MM_DTYPE), (T // tm, IN_TOTAL // tn, 1),
                   name="proj_fwd" if carried is None else "proj_fwd_gather", carried=carried)


def proj_dx(dproj, w_in, carried=None):
    T = dproj.shape[0]
    tm, tk = _row_tile(T), 1792
    per = W_IN_SHARD // tk
    return _matmul(dproj, w_in, pl.BlockSpec((tm, tk), lambda i, j, k: (i, k)),
                   pl.BlockSpec((None, D_MODEL, tk), lambda i, j, k: (k // per, 0, k % per)),
                   pl.BlockSpec((tm, D_MODEL), lambda i, j, k: (i, 0)),
                   jax.ShapeDtypeStruct((T, D_MODEL), F32), (T // tm, 1, IN_TOTAL // tk), tb=True,
                   name="proj_dx" if carried is None else "proj_dx_exchange", carried=carried)


def proj_dw(h, dproj):
    T = h.shape[0]
    tk, tm, tn = T, 512, 512
    per = W_IN_SHARD // tn
    return _matmul(h, dproj, pl.BlockSpec((tk, tm), lambda i, j, k: (k, i)),
                   pl.BlockSpec((tk, tn), lambda i, j, k: (k, j)),
                   pl.BlockSpec((None, tm, tn), lambda i, j, k: (j // per, i, j % per)),
                   jax.ShapeDtypeStruct((N_SHARDS, D_MODEL, W_IN_SHARD), F32),
                   (D_MODEL // tm, IN_TOTAL // tn, T // tk), ta=True, name="proj_dw")


def layer_fwd(X, P, *, L, carried=None):
    h = prenorm_fwd(X, P["g_pre"], P["scale"], P["shift"], L=L)
    proj, extras = proj_fwd(h, P["w_in"], carried) if carried is not None else (proj_fwd(h, P["w_in"]), None)
    o_att, lse = attn_fwd(proj, P["ptiles"], L=L)
    o_att, lse_c = cattn_fwd(proj, o_att, L=L)
    o_pool, pooled = pool_fwd(proj, P["pool_w"], P["pool_scale"], L=L)
    o_conv = conv_fwd(proj, P["conv_w"], L=L)
    o_ssm, ssm_saved = ssm_fwd(proj, P["ssm_ops"], P["ssm_d"], P["glu_w"], L=L)
    o = jnp.concatenate([o_att, o_pool, o_conv, o_ssm], axis=1)
    a = gate_act(o, proj, L=L)
    merged, br = merge_fwd(a, P["w_br"], proj, P["b_gate"])
    y = mm_nn(merged, P["w_o"], out_dtype=F32, name="out_proj")
    Xn = postnorm_fwd(X, y, P["g_post"], P["gate"], L=L)
    saved = dict(X=X, h=h, proj=proj, lse=lse, lse_c=lse_c, pooled=pooled, ssm=ssm_saved, o=o, a=a, merged=merged,
                 br=br, y=y)
    return Xn, saved, extras


def layer_bwd(dXn, P, S, *, L, carried=None):
    proj = S["proj"]
    dy, cs_post = postnorm_bwd(dXn, S["y"], P["g_post"], P["gate"], L=L)
    dmerged = mm_nt(dy, P["w_o"], out_dtype=F32, name="out_proj_dx")
    d_w_o = mm_tn(S["merged"], dy, out_dtype=F32, name="out_proj_dw", tm=512, tn=1024)
    dbr, dlogit, d_bgate = merge_bwd(dmerged, S["br"], proj, P["b_gate"])
    da = branch_dx(dbr, P["w_br"])
    d_w_br = branch_dw(S["a"], dbr)
    do, dz = gate_act_bwd(da, S["o"], proj, L=L)
    dq, dk, dv, dptiles = attn_bwd(proj, P["ptiles"], S["o"], do, S["lse"], L=L)
    dq, dk, dv = cattn_bwd(proj, S["o"], do, S["lse_c"], dq, dk, dv, L=L)
    dpool_u, d_pool_w, d_pool_scale = pool_bwd(do[:, BRANCH:2 * BRANCH], S["pooled"], P["pool_w"], P["pool_scale"], L=L)
    dcx, dcb, dcc, d_conv_w = conv_bwd(do[:, 2 * BRANCH:3 * BRANCH], proj, P["conv_w"], L=L)
    dssm_u, d_ops, d_ssm_d, d_glu = ssm_bwd(do[:, 3 * BRANCH:], proj, P["ssm_ops"], P["ssm_d"], P["glu_w"], S["ssm"], L=L)
    z = lambda i: dz[:, i * BRANCH:(i + 1) * BRANCH]
    cast = lambda t: t.astype(MM_DTYPE)
    dproj = jnp.concatenate([cast(dq), cast(dk), cast(dv), z(0), dpool_u, z(1),
                             dcx, dcb, dcc, z(2), cast(dssm_u), z(3), dlogit], axis=1)
    dh, extras = proj_dx(dproj, P["w_in"], carried) if carried is not None else (proj_dx(dproj, P["w_in"]), None)
    d_w_in = proj_dw(S["h"], dproj)
    dX, cs_h, cs_hx = prenorm_bwd(dh, S["X"], P["g_pre"], P["scale"], dXn, L=L)
    g_pre, g_post = P["g_pre"], P["g_post"]
    d_shift = cs_h
    d_scale = cs_hx * g_pre
    d_gate = cs_post * g_post
    d_g_pre = jnp.sum(cs_hx * (1.0 + P["scale"]), axis=0)[0]
    d_g_post = jnp.sum(cs_post * P["gate"], axis=0)[0]
    grads = dict(w_in=d_w_in, w_br=d_w_br, w_o=d_w_o, glu_w=d_glu, conv_w=d_conv_w[0:3], pool_w=d_pool_w,
                 pool_scale=d_pool_scale[0], b_gate=d_bgate[0], d_ptiles=dptiles, d_ops=d_ops, ssm_d=d_ssm_d,
                 g_pre=d_g_pre, g_post=d_g_post,
                 mod=jnp.concatenate([d_shift, d_scale, d_gate], axis=-1)[:, 0])
    return dX, grads, extras


BIG_NAMES = ("w_in", "glu_w", "w_br", "w_o")


def gathered_weights(g):
    w_in, glu, w_br, w_o = g
    return dict(w_in=w_in, glu_w=glu.transpose(1, 0, 2).reshape(BRANCH, 2 * BRANCH),
                w_br=w_br.reshape(D_MODEL, D_MODEL), w_o=w_o.reshape(D_MODEL, D_MODEL))


def shard_major_grads(g):
    return [g["w_in"], g["glu_w"].reshape(BRANCH, N_SHARDS, -1).transpose(1, 0, 2),
            g["w_br"].reshape(N_SHARDS, BRANCH, D_MODEL), g["w_o"].reshape(N_SHARDS, BRANCH, D_MODEL)]


def local_step(x, ctx, target, mods, layers, *, shard_halves=None, reduce=False):
    L = x.shape[0]
    nl = len(layers)
    X = jnp.concatenate([x, ctx], axis=0)
    ssm_all = [jnp.stack([lay["ssm_params"][j] for lay in layers]) for j in range(len(layers[0]["ssm_params"]))]
    ops_all, ops_vjp = jax.vjp(ssm_operators, *[t.reshape((2 * nl,) + t.shape[2:]) for t in ssm_all])
    ptiles_all, bias_vjp = jax.vjp(bias_pair_tiles, jnp.stack([lay["na_rpb"] for lay in layers]))
    saved, Ps = [], []
    arrived = None
    for i, lay in enumerate(layers):
        P = dict(lay)
        if arrived is not None:
            P.update(gathered_weights(arrived))
        m = mods[i][:, None, :]
        P["shift"], P["scale"], P["gate"] = m[..., :D_MODEL], m[..., D_MODEL:2 * D_MODEL], m[..., 2 * D_MODEL:]
        P["ptiles"] = ptiles_all[i]
        P["ssm_ops"] = tuple(t[2 * i:2 * i + 2] for t in ops_all)
        ahead = shard_halves is not None and i + 1 < nl
        X, S, arrived = layer_fwd(X, P, L=L, carried=gather_plan(shard_halves[i + 1], shard_major=True) if ahead else None)
        saved.append(S)
        Ps.append(P)
    loss, dX = loss_and_grad(X, target, L=L)
    grads, reduced = [None] * nl, [None] * nl
    finish = lambda parts: [pair_finish(a, name=f"grad_finish_{n}") for n, a in zip(BIG_NAMES, parts)]
    pending = None
    for i in reversed(range(nl)):
        dX, grads[i], arrived = layer_bwd(dX, Ps[i], saved[i], L=L,
                                          carried=exchange_plan(pending) if pending is not None else None)
        if pending is not None:
            reduced[i + 1] = finish(arrived)
        if reduce:
            pending = [pair_reduce(a, out_dtype=MM_DTYPE, name=f"grad_pair_{n}")
                       for n, a in zip(BIG_NAMES, shard_major_grads(grads[i]))]
    if reduce:
        reduced[0] = finish(chip_exchange(pending, name="grad_exchange"))
    d_rpb, = bias_vjp(jnp.stack([g.pop("d_ptiles") for g in grads]))
    d_ops = [g.pop("d_ops") for g in grads]
    d_ssm = ops_vjp(tuple(jnp.concatenate([d[j] for d in d_ops]) for j in range(3)))
    for i, g in enumerate(grads):
        g["na_rpb"] = d_rpb[i]
        g["ssm"] = tuple(t.reshape((nl, 2) + t.shape[1:])[i] for t in d_ssm)
    return loss, dX[:L], grads, reduced


MESH_ID = pl.DeviceIdType.MESH
HBM_SPEC = pl.BlockSpec(memory_space=pltpu.HBM)


def _place():
    return lax.axis_index("x"), lax.axis_index("y"), lax.axis_index("c")


def _other_chips(x, y):
    return [(1 - x, y), (x, 1 - y), (1 - x, 1 - y)]


def _remote(src, dst, send_sem, recv_sem, to):
    return pltpu.make_async_remote_copy(src_ref=src, dst_ref=dst, send_sem=send_sem, recv_sem=recv_sem,
                                        device_id=to, device_id_type=MESH_ID)


DMA_CHUNK_BYTES = 1 << 20


def _row_pieces(rows, row_bytes):
    n = max(1, min(64, rows * row_bytes // DMA_CHUNK_BYTES))
    while n > 1 and (rows % n or (rows // n) % 16):
        n -= 1
    return [(r * (rows // n), rows // n) for r in range(n)]


def _row_bytes(ref):
    return math.prod(ref.shape[1:]) * jnp.dtype(ref.dtype).itemsize


def _start_in_pieces(make, src, dst):
    for r0, nr in _row_pieces(src.shape[0], _row_bytes(src)):
        make(src.at[pl.ds(r0, nr)], dst.at[pl.ds(r0, nr)]).start()


class Carried:
    def __init__(self, ins, out_shapes, sem_shapes, start, mid, finish):
        self.ins, self.out_shapes, self.sem_shapes = list(ins), list(out_shapes), list(sem_shapes)
        self.start, self.mid, self.finish = start, mid, finish


def run_carried(plan, name):
    n_in, n_out = len(plan.ins), len(plan.out_shapes)

    def body(*refs):
        parts = refs[:n_in], refs[n_in:n_in + n_out], refs[n_in + n_out:]
        plan.start(*parts)
        plan.mid(*parts)
        plan.finish(*parts)

    return pl.pallas_call(body, out_shape=plan.out_shapes, in_specs=[HBM_SPEC] * n_in, out_specs=[HBM_SPEC] * n_out,
                          scratch_shapes=plan.sem_shapes, name=name)(*plan.ins)


def gather_plan(blocks, *, shard_major=False):
    K = len(blocks)

    def tools(ins, outs, sems):
        send_sems, recv_sems, local_sems = sems
        x, y, c = _place()

        def slot(k, block):
            px, py, pc = block
            if shard_major:
                h = ins[k].shape[0]
                return outs[k].at[2 * px + py, pl.ds(pl.multiple_of(pc * h, 16), h)]
            return outs[k].at[4 * px + 2 * py + pc]

        def copy(k, j, to):
            return lambda s, d: _remote(s, d, send_sems.at[k, j], recv_sems.at[k, j], to)

        return (x, y, c), (x, y, 1 - c), _other_chips(x, y), c, slot, copy, local_sems

    def start(ins, outs, sems):
        me, sibling, chips, c, slot, copy, local_sems = tools(ins, outs, sems)
        for k in range(K):
            _start_in_pieces(lambda s, d, k=k: pltpu.make_async_copy(s, d, local_sems.at[k]), ins[k], slot(k, me))
            _start_in_pieces(copy(k, 0, sibling), ins[k], slot(k, me))
            for j, chip in enumerate(chips):
                _start_in_pieces(copy(k, 1 + j, (*chip, c)), ins[k], slot(k, me))

    def mid(ins, outs, sems):
        me, sibling, chips, c, slot, copy, _ = tools(ins, outs, sems)
        for j, chip in enumerate(chips):
            for k in range(K):
                got = slot(k, (*chip, c))
                copy(k, 1 + j, me)(got, got).wait_recv()
                _start_in_pieces(copy(k, 4 + j, sibling), got, got)

    def finish(ins, outs, sems):
        me, sibling, chips, c, slot, copy, local_sems = tools(ins, outs, sems)
        for k in range(K):
            sib = slot(k, sibling)
            copy(k, 0, me)(sib, sib).wait_recv()
            for j, chip in enumerate(chips):
                got = slot(k, (*chip, 1 - c))
                copy(k, 4 + j, me)(got, got).wait_recv()
        for k in range(K):
            own = slot(k, me)
            for j in range(4):
                copy(k, j, me)(ins[k], own).wait_send()
            for j, chip in enumerate(chips):
                got = slot(k, (*chip, c))
                copy(k, 4 + j, me)(got, got).wait_send()
            pltpu.make_async_copy(ins[k], own, local_sems.at[k]).wait()

    def out_shape(b):
        if shard_major:
            return jax.ShapeDtypeStruct((N_SHARDS, 2 * b.shape[0]) + b.shape[1:], b.dtype)
        return jax.ShapeDtypeStruct((8,) + b.shape, b.dtype)

    sems = [pltpu.SemaphoreType.DMA((K, 7)), pltpu.SemaphoreType.DMA((K, 7)), pltpu.SemaphoreType.DMA((K,))]
    return Carried(blocks, [out_shape(b) for b in blocks], sems, start, mid, finish)


def all_gather8(blocks, name, *, shard_major=False):
    return run_carried(gather_plan(blocks, shard_major=shard_major), name)


def exchange_plan(parts):
    K = len(parts)

    def copies(ins, outs, sems):
        send_sems, recv_sems, local_sems = sems
        x, y, c = _place()
        p = 2 * x + y
        makes = []
        for k in range(K):
            for j, (cx, cy) in enumerate(_other_chips(x, y)):
                make = lambda s, d, k=k, j=j, to=(cx, cy, c): _remote(s, d, send_sems.at[k, j], recv_sems.at[k, j], to)
                makes.append((k, 2 * cx + cy, make))
        return p, makes, local_sems

    def start(ins, outs, sems):
        p, makes, local_sems = copies(ins, outs, sems)
        for k in range(K):
            _start_in_pieces(lambda s, d, k=k: pltpu.make_async_copy(s, d, local_sems.at[k]), ins[k].at[p], outs[k].at[p])
        for k, q, make in makes:
            _start_in_pieces(make, ins[k].at[q], outs[k].at[p])

    def mid(ins, outs, sems):
        pass

    def finish(ins, outs, sems):
        p, makes, local_sems = copies(ins, outs, sems)
        waits = [make(ins[k].at[q], outs[k].at[q]) for k, q, make in makes]
        for cp in waits:
            cp.wait_recv()
        for cp in waits:
            cp.wait_send()
        for k in range(K):
            pltpu.make_async_copy(ins[k].at[p], outs[k].at[p], local_sems.at[k]).wait()

    sems = [pltpu.SemaphoreType.DMA((K, 3)), pltpu.SemaphoreType.DMA((K, 3)), pltpu.SemaphoreType.DMA((K,))]
    return Carried(parts, [jax.ShapeDtypeStruct(a.shape, a.dtype) for a in parts], sems, start, mid, finish)


def chip_exchange(parts, name):
    return run_carried(exchange_plan(parts), name)


PAIR_TILE_BYTES = 2 << 20


def _pair_rows(h, n):
    return _pick(h, tuple(t for t in (512, 256, 128, 64, 32, 16) if t * n * 4 <= PAIR_TILE_BYTES))


def _core_index():
    return jnp.reshape(lax.axis_index("c"), (1,)).astype(jnp.int32)


def pair_reduce(g, *, out_dtype, name):
    S, R, n = g.shape
    h = R // 2
    tr = _pair_rows(h, n)
    nt = h // tr

    def body(c_ref, keep_ref, give_ref, o_ref, recv_ref, send_sems, recv_sems):
        x, y, c = _place()
        slot = (pl.program_id(0) * nt + pl.program_id(1)) % 2
        cp = _remote(give_ref, recv_ref.at[slot], send_sems.at[slot], recv_sems.at[slot], (x, y, 1 - c))
        cp.start()
        cp.wait_recv()
        o_ref[...] = (keep_ref[...] + recv_ref[slot]).astype(o_ref.dtype)
        cp.wait_send()

    g2 = g.reshape(S * R, n)
    grid_spec = pltpu.PrefetchScalarGridSpec(
        num_scalar_prefetch=1, grid=(S, nt),
        in_specs=[pl.BlockSpec((tr, n), lambda q, i, c: ((2 * q + c[0]) * nt + i, 0)),
                  pl.BlockSpec((tr, n), lambda q, i, c: ((2 * q + 1 - c[0]) * nt + i, 0))],
        out_specs=pl.BlockSpec((tr, n), lambda q, i, c: (q * nt + i, 0)),
        scratch_shapes=[pltpu.VMEM((2, tr, n), F32), pltpu.SemaphoreType.DMA((2,)), pltpu.SemaphoreType.DMA((2,))])
    out = pl.pallas_call(body, out_shape=jax.ShapeDtypeStruct((S * h, n), out_dtype), grid_spec=grid_spec,
                         compiler_params=_params("arbitrary", "arbitrary"), name=name)(_core_index(), g2, g2)
    return out.reshape(S, h, n)


def pair_finish(parts, *, name):
    S, h, n = parts.shape
    tr = _pair_rows(h, n)
    nt = h // tr

    def body(c_ref, p_ref, o_ref, recv_ref, send_sem, recv_sem):
        x, y, c = _place()
        phase, i = pl.program_id(0), pl.program_id(1)
        rows = pl.ds(pl.multiple_of(i * tr, 16), tr)

        @pl.when(phase == 0)
        def _():
            acc = p_ref[0].astype(F32)
            for s in range(1, S):
                acc = acc + p_ref[s].astype(F32)
            o_ref[...] = acc
            cp = _remote(o_ref, recv_ref.at[rows], send_sem, recv_sem, (x, y, 1 - c))
            cp.start()
            cp.wait_send()

        @pl.when((phase == 1) & (i == 0))
        def _():
            _remote(recv_ref, recv_ref, send_sem, recv_sem, (x, y, 1 - c)).wait_recv()

        @pl.when(phase == 1)
        def _():
            o_ref[...] = recv_ref[rows]

    grid_spec = pltpu.PrefetchScalarGridSpec(
        num_scalar_prefetch=1, grid=(2, nt),
        in_specs=[pl.BlockSpec((S, tr, n), lambda ph, i, c: (0, jnp.where(ph == 0, i, nt - 1), 0))],
        out_specs=pl.BlockSpec((tr, n), lambda ph, i, c: (jnp.where(ph == 0, c[0], 1 - c[0]) * nt + i, 0)),
        scratch_shapes=[pltpu.VMEM((h, n), F32), pltpu.SemaphoreType.DMA(()), pltpu.SemaphoreType.DMA(())])
    return pl.pallas_call(body, out_shape=jax.ShapeDtypeStruct((2 * h, n), F32), grid_spec=grid_spec,
                          compiler_params=_params("arbitrary", "arbitrary"), name=name)(_core_index(), parts)


def _slab_rows(n):
    return max(8, min(256, (1 << 18) // n // 8 * 8))


def sum_slabs(a, *, name):
    S, h, n = a.shape
    tr = _pick(h, tuple(t for t in (256, 128, 64, 32, 16, 8) if t <= _slab_rows(n)))

    def body(a_ref, o_ref):
        acc = a_ref[0].astype(F32)
        for s in range(1, S):
            acc = acc + a_ref[s].astype(F32)
        o_ref[...] = acc

    return pl.pallas_call(body, out_shape=jax.ShapeDtypeStruct((h, n), F32), grid=(h // tr,),
                          in_specs=[pl.BlockSpec((S, tr, n), lambda r: (0, r, 0))],
                          out_specs=pl.BlockSpec((tr, n), lambda r: (r, 0)), compiler_params=_params("parallel"),
                          name=name)(a)


def _adam_math(w, g, m, v):
    m = ADAM_B1 * m + (1.0 - ADAM_B1) * g
    v = ADAM_B2 * v + (1.0 - ADAM_B2) * (g * g)
    m_hat = m / (1.0 - ADAM_B1 ** ADAM_STEP)
    v_hat = v / (1.0 - ADAM_B2 ** ADAM_STEP)
    delta = -ADAM_LR * (m_hat / (jnp.sqrt(v_hat) + ADAM_EPS) + ADAM_WD * w)
    return delta, m, v


def adamw(w, g, m, v, *, name):
    R, n = w.shape
    tr = _pick(R, tuple(t for t in (256, 128, 64, 32, 16, 8) if t <= _slab_rows(n)))
    spec = pl.BlockSpec((tr, n), lambda r: (r, 0))

    def body(w_ref, g_ref, m_ref, v_ref, d_ref, nm_ref, nv_ref):
        d, nm, nv = _adam_math(w_ref[...], g_ref[...], m_ref[...], v_ref[...])
        d_ref[...] = d
        nm_ref[...] = nm
        nv_ref[...] = nv

    return pl.pallas_call(body, out_shape=[jax.ShapeDtypeStruct(w.shape, F32)] * 3, grid=(R // tr,),
                          in_specs=[spec] * 4, out_specs=[spec] * 3, compiler_params=_params("parallel"),
                          name=name)(w, g, m, v)


MOD_ROWS = 16


def mod_fwd(cact_in, w_mod, b_mod):
    nl, _, cols = w_mod.shape
    tn = 512

    def body(c_ref, w_ref, b_ref, o_ref):
        o_ref[...] = _dot(_silu(c_ref[...]).astype(MM_DTYPE), w_ref[...].astype(MM_DTYPE)) + b_ref[...]

    return pl.pallas_call(
        body, out_shape=jax.ShapeDtypeStruct((nl, MOD_ROWS, cols), F32), grid=(nl, cols // tn),
        in_specs=[pl.BlockSpec((MOD_ROWS, D_MODEL), lambda i, j: (0, 0)),
                  pl.BlockSpec((None, D_MODEL, tn), lambda i, j: (i, 0, j)),
                  pl.BlockSpec((None, 1, tn), lambda i, j: (i, 0, j))],
        out_specs=pl.BlockSpec((None, MOD_ROWS, tn), lambda i, j: (i, 0, j)),
        compiler_params=_params("parallel", "parallel"), name="mod_fwd")(cact_in, w_mod, b_mod)


def mod_update(c_rows, dmod, w, m, v):
    nl, _, cols = w.shape
    tr, tn = 256, 512
    wspec = pl.BlockSpec((None, tr, tn), lambda i, r, j: (i, r, j))

    def body(c_ref, d_ref, w_ref, m_ref, v_ref, g_ref, dl_ref, nm_ref, nv_ref):
        g = _dot_tn(_silu(c_ref[...]).astype(MM_DTYPE), d_ref[...].astype(MM_DTYPE))
        g_ref[...] = g
        dl, nm, nv = _adam_math(w_ref[...], g, m_ref[...], v_ref[...])
        dl_ref[...] = dl
        nm_ref[...] = nm
        nv_ref[...] = nv

    return pl.pallas_call(
        body, out_shape=[jax.ShapeDtypeStruct(w.shape, F32)] * 4, grid=(nl, D_MODEL // tr, cols // tn),
        in_specs=[pl.BlockSpec((MOD_ROWS, tr), lambda i, r, j: (0, r)),
                  pl.BlockSpec((None, MOD_ROWS, tn), lambda i, r, j: (i, 0, j)), wspec, wspec, wspec],
        out_specs=[wspec] * 4, compiler_params=_params("parallel", "parallel", "parallel"),
        name="mod_update")(c_rows, dmod, w, m, v)


def cctx_partial(dmod_ctx, w_mod, c_ctx):
    nl, _, cols = w_mod.shape
    tk = 512
    per = cols // tk
    part = _matmul(dmod_ctx, w_mod, pl.BlockSpec((8, tk), lambda i, j, k: (0, k)),
                   pl.BlockSpec((None, D_MODEL, tk), lambda i, j, k: (k // per, 0, k % per)),
                   pl.BlockSpec((8, D_MODEL), lambda i, j, k: (0, 0)),
                   jax.ShapeDtypeStruct((8, D_MODEL), F32), (1, 1, nl * per), tb=True, name="cctx_partial")

    def body(p_ref, c_ref, o_ref):
        o_ref[...] = 0.5 * jnp.sum(p_ref[...], axis=0, keepdims=True) * _silu_grad(c_ref[...])

    return pl.pallas_call(body, out_shape=jax.ShapeDtypeStruct((1, D_MODEL), F32), name="cctx_scale")(part, c_ctx)


WEIGHT_NAMES = ("c_ctx", "w_mod", "b_mod", "g_pre", "g_post", "w_in", "b_gate", "na_rpb", "pool_w", "pool_scale",
                "conv_w", "ssm_a_re", "ssm_a_im", "ssm_log_dt", "ssm_b_re", "ssm_b_im", "ssm_c_re", "ssm_c_im",
                "ssm_d", "glu_w", "w_br", "w_o")
SSM_NAMES = ("ssm_a_re", "ssm_a_im", "ssm_log_dt", "ssm_b_re", "ssm_b_im", "ssm_c_re", "ssm_c_im")
SMALL_NAMES = ("c_ctx", "b_mod", "g_pre", "g_post", "b_gate", "na_rpb", "pool_w", "pool_scale") + SSM_NAMES + ("ssm_d",)
FLAT_COLS = 1024


def _flat(parts):
    v = jnp.concatenate([p.reshape(-1) for p in parts])
    pad = -v.shape[0] % (64 * FLAT_COLS)
    return jnp.pad(v, (0, pad)).reshape(-1, FLAT_COLS)


def _unflat(flat, shapes):
    v = flat.reshape(-1)
    out, off = [], 0
    for s in shapes:
        n = math.prod(s)
        out.append(v[off:off + n].reshape(s))
        off += n
    return out


def kernel(x, c, ctx, c_ctx, w_mod, b_mod, g_pre, g_post, w_in, b_gate, na_rpb, pool_w, pool_scale, conv_w, ssm_a_re, ssm_a_im, ssm_log_dt, ssm_b_re, ssm_b_im, ssm_c_re, ssm_c_im, ssm_d, glu_w, w_br, w_o, loss_target, m_c_ctx, m_w_mod, m_b_mod, m_g_pre, m_g_post, m_w_in, m_b_gate, m_na_rpb, m_pool_w, m_pool_scale, m_conv_w, m_ssm_a_re, m_ssm_a_im, m_ssm_log_dt, m_ssm_b_re, m_ssm_b_im, m_ssm_c_re, m_ssm_c_im, m_ssm_d, m_glu_w, m_w_br, m_w_o, v_c_ctx, v_w_mod, v_b_mod, v_g_pre, v_g_post, v_w_in, v_b_gate, v_na_rpb, v_pool_w, v_pool_scale, v_conv_w, v_ssm_a_re, v_ssm_a_im, v_ssm_log_dt, v_ssm_b_re, v_ssm_b_im, v_ssm_c_re, v_ssm_c_im, v_ssm_d, v_glu_w, v_w_br, v_w_o):
    W = dict(c_ctx=c_ctx, w_mod=w_mod, b_mod=b_mod, g_pre=g_pre, g_post=g_post, w_in=w_in, b_gate=b_gate,
             na_rpb=na_rpb, pool_w=pool_w, pool_scale=pool_scale, conv_w=conv_w, ssm_a_re=ssm_a_re, ssm_a_im=ssm_a_im,
             ssm_log_dt=ssm_log_dt, ssm_b_re=ssm_b_re, ssm_b_im=ssm_b_im, ssm_c_re=ssm_c_re, ssm_c_im=ssm_c_im,
             ssm_d=ssm_d, glu_w=glu_w, w_br=w_br, w_o=w_o)
    M = dict(c_ctx=m_c_ctx, w_mod=m_w_mod, b_mod=m_b_mod, g_pre=m_g_pre, g_post=m_g_post, w_in=m_w_in, b_gate=m_b_gate,
             na_rpb=m_na_rpb, pool_w=m_pool_w, pool_scale=m_pool_scale, conv_w=m_conv_w, ssm_a_re=m_ssm_a_re,
             ssm_a_im=m_ssm_a_im, ssm_log_dt=m_ssm_log_dt, ssm_b_re=m_ssm_b_re, ssm_b_im=m_ssm_b_im,
             ssm_c_re=m_ssm_c_re, ssm_c_im=m_ssm_c_im, ssm_d=m_ssm_d, glu_w=m_glu_w, w_br=m_w_br, w_o=m_w_o)
    V = dict(c_ctx=v_c_ctx, w_mod=v_w_mod, b_mod=v_b_mod, g_pre=v_g_pre, g_post=v_g_post, w_in=v_w_in, b_gate=v_b_gate,
             na_rpb=v_na_rpb, pool_w=v_pool_w, pool_scale=v_pool_scale, conv_w=v_conv_w, ssm_a_re=v_ssm_a_re,
             ssm_a_im=v_ssm_a_im, ssm_log_dt=v_ssm_log_dt, ssm_b_re=v_ssm_b_re, ssm_b_im=v_ssm_b_im,
             ssm_c_re=v_ssm_c_re, ssm_c_im=v_ssm_c_im, ssm_d=v_ssm_d, glu_w=v_glu_w, w_br=v_w_br, w_o=v_w_o)
    nl = w_in.shape[0]
    xi, yi, ci = _place()
    chip = 2 * xi + yi
    example = 4 * xi + 2 * yi + ci
    mod_cols = w_mod.shape[2]

    def my_half(a):
        half = a.shape[0] // 2
        return lax.dynamic_slice_in_dim(a, ci * half, half, axis=0).astype(MM_DTYPE)

    halves = [[my_half(W[n][i]) for n in BIG_NAMES] for i in range(nl)]
    first = gathered_weights(all_gather8(halves[0], name="gather_weights", shard_major=True))

    c8 = jnp.pad(c, ((0, 7), (0, 0)))
    c_all, = all_gather8([c8], name="gather_c")
    c_rows = jnp.concatenate([c_all[:, 0], jnp.broadcast_to(c_ctx[None], (8, D_MODEL))], axis=0)
    b_cols = lax.dynamic_slice_in_dim(b_mod, chip * mod_cols, mod_cols, axis=1)[:, None]
    mod_part = mod_fwd(c_rows, w_mod, b_cols)
    conv_part = jnp.pad(conv_w.reshape(nl * 3, -1), ((0, 16 - nl * 3), (0, 0)))
    parts, conv_all = all_gather8([mod_part.reshape(nl * MOD_ROWS, mod_cols), conv_part], name="gather_mod")
    mod_full = jnp.concatenate([parts[2 * p].reshape(nl, MOD_ROWS, mod_cols) for p in range(N_SHARDS)], axis=-1)
    conv_full = jnp.concatenate([conv_all[2 * p][:nl * 3].reshape(nl, 3, -1) for p in range(N_SHARDS)], axis=-1)
    own = lax.dynamic_index_in_dim(mod_full, example, axis=1, keepdims=False)
    mods = [jnp.stack([own[i], mod_full[i, 8]]) for i in range(nl)]

    layers = []
    for i in range(nl):
        layers.append(dict(
            conv_w=jnp.pad(conv_full[i], ((0, 5), (0, 0))), pool_w=pool_w[i], pool_scale=pool_scale[i][None],
            b_gate=b_gate[i][None], g_pre=g_pre[i][None], g_post=g_post[i][None], na_rpb=na_rpb[i],
            ssm_d=ssm_d[i][None], ssm_params=tuple(W[n][i] for n in SSM_NAMES)))
    layers[0].update(first)

    loss_local, grad_x, grads, reduced = local_step(x[0], ctx[0], loss_target[0], mods, layers,
                                                    shard_halves=halves, reduce=True)
    loss = lax.psum(loss_local, ("x", "y", "c"))

    dmod_local = jnp.stack([g["mod"] for g in grads])
    dmod_all, = all_gather8([jnp.pad(dmod_local.reshape(nl * 2, -1), ((0, 8 - nl * 2), (0, 0)))], name="gather_dmod")
    dmod_all = dmod_all[:, :nl * 2].reshape(8, nl, 2, 3 * D_MODEL)
    dmod_rows = jnp.concatenate([dmod_all[:, :, 0], dmod_all[:, :, 1]], axis=0).transpose(1, 0, 2)
    dmod_cols = lax.dynamic_slice_in_dim(dmod_rows, chip * mod_cols, mod_cols, axis=2)
    g_w_mod, d_w_mod, nm_w_mod, nv_w_mod = mod_update(c_rows, dmod_cols, w_mod, m_w_mod, v_w_mod)
    dctx_cols = dmod_cols[:, 8:].transpose(1, 0, 2).reshape(8, nl * mod_cols)
    g_cctx_part = cctx_partial(dctx_cols, w_mod, c_ctx[None])[0]

    def small_grad(n):
        if n == "c_ctx":
            return g_cctx_part
        if n == "b_mod":
            return jnp.stack([g["mod"][0] + g["mod"][1] for g in grads])
        if n in SSM_NAMES:
            return jnp.stack([g["ssm"][SSM_NAMES.index(n)] for g in grads])
        return jnp.stack([g[n] for g in grads])

    conv_grad_full = jnp.stack([g["conv_w"] for g in grads])
    flat_g = _flat([small_grad(n) for n in SMALL_NAMES] + [conv_grad_full])

    G, DL, NM, NV = {}, {}, {}, {}
    for k, n in enumerate(BIG_NAMES):
        g = jnp.stack([reduced[i][k] for i in range(nl)])
        rows = g.shape[0] * g.shape[1]
        d, nm, nv = adamw(W[n].reshape(rows, -1), g.reshape(rows, -1), M[n].reshape(rows, -1), V[n].reshape(rows, -1),
                          name=f"adamw_{n}")
        G[n], DL[n], NM[n], NV[n] = g, d.reshape(g.shape), nm.reshape(g.shape), nv.reshape(g.shape)

    flat_all, = all_gather8([flat_g], name="gather_small_grads")
    flat_sum = sum_slabs(flat_all, name="sum_small_grads")
    small_shapes = [W[n].shape for n in SMALL_NAMES]
    small_g = _unflat(flat_sum, small_shapes + [conv_grad_full.shape])
    conv_g = lax.dynamic_slice_in_dim(small_g[-1], chip * conv_w.shape[2], conv_w.shape[2], axis=2)
    adam_names = SMALL_NAMES + ("conv_w",)
    adam_shapes = small_shapes + [conv_w.shape]
    g_list = small_g[:-1] + [conv_g]
    upd = adamw(_flat([W[n] for n in adam_names]), _flat(g_list), _flat([M[n] for n in adam_names]),
                _flat([V[n] for n in adam_names]), name="adamw_small")
    G.update(zip(adam_names, g_list))
    for group, u in zip((DL, NM, NV), upd):
        group.update(zip(adam_names, _unflat(u, adam_shapes)))
    G["w_mod"], DL["w_mod"], NM["w_mod"], NV["w_mod"] = g_w_mod, d_w_mod, nm_w_mod, nv_w_mod

    out = [loss, grad_x[None]]
    for group in (G, DL, NM, NV):
        out += [group[n].reshape(W[n].shape) for n in WEIGHT_NAMES]
    return tuple(out)
```

```python
import functools
import math

import numpy as np
import jax
import jax.numpy as jnp
from jax import lax
from jax.experimental import pallas as pl
from jax.experimental.pallas import tpu as pltpu

F32 = jnp.float32
BF16 = jnp.bfloat16
MM_DTYPE = jnp.bfloat16

D_MODEL = 2048
BRANCH = 512
N_HEADS = 8
HEAD_DIM = 64
GRID_W = 64
WIN_ROWS = 8
WIN_COLS = 16
POOL_GROUPS = 4
POOL_DIM = 128
SSM_GROUPS = 32
SSM_GDIM = 16
SSM_STATE = 64
N_STATE = SSM_GROUPS * SSM_STATE
IN_TOTAL = 14336
RMS_EPS = 1e-6
NEG_INF = -1e30
COL = dict(q=0, k=512, v=1024, na_z=1536, pool_u=2048, pool_z=2560, conv_x=3072, conv_b=3584,
           conv_c=4096, conv_z=4608, ssm_u=5120, ssm_z=5632, merge=6144)
N_SHARDS = 4
W_IN_SHARD = IN_TOTAL // N_SHARDS
VMEM_LIMIT_BYTES = 48 * 1024 * 1024
ROW_TILE = 256

ADAM_LR = 0.001
ADAM_B1 = 0.9
ADAM_B2 = 0.999
ADAM_EPS = 1e-08
ADAM_WD = 0.01
ADAM_STEP = 10


def _params(*sem):
    return pltpu.CompilerParams(dimension_semantics=sem, vmem_limit_bytes=VMEM_LIMIT_BYTES)


def _sigmoid(x):
    return 1.0 / (1.0 + jnp.exp(-x))


def _matmul(a, b, a_spec, b_spec, o_spec, out_shape, grid, *, ta=False, tb=False, name, carried=None):
    nk = grid[-1]
    kaxis = len(grid) - 1
    dims = (((0,) if ta else (1,), (1,) if tb else (0,)), ((), ()))
    n_acc = 0 if nk == 1 else 1

    def compute(a_ref, b_ref, o_ref, acc):
        p = lax.dot_general(a_ref[...].astype(MM_DTYPE), b_ref[...].astype(MM_DTYPE), dims,
                            preferred_element_type=F32)
        if nk == 1:
            o_ref[...] = p.astype(o_ref.dtype)
            return
        acc_ref, = acc
        k = pl.program_id(kaxis)

        @pl.when(k == 0)
        def _():
            acc_ref[...] = p

        @pl.when(k > 0)
        def _():
            acc_ref[...] += p

        @pl.when(k == nk - 1)
        def _():
            o_ref[...] = acc_ref[...].astype(o_ref.dtype)

    oblock = tuple(s for s in o_spec.block_shape if s is not None)
    scratch = [] if nk == 1 else [pltpu.VMEM(oblock, F32)]
    if carried is None:
        def body(a_ref, b_ref, o_ref, *acc):
            compute(a_ref, b_ref, o_ref, acc)

        sem = ("parallel",) * (len(grid) - 1) + ("arbitrary",)
        return pl.pallas_call(body, out_shape=out_shape, grid=grid, in_specs=[a_spec, b_spec],
                              out_specs=o_spec, scratch_shapes=scratch, compiler_params=_params(*sem),
                              name=name)(a, b)

    n_in, n_out = len(carried.ins), len(carried.out_shapes)
    steps = math.prod(grid)

    def body(a_ref, b_ref, *rest):
        c_ins, o_ref, c_outs = rest[:n_in], rest[n_in], rest[n_in + 1:n_in + 1 + n_out]
        acc, sems = rest[n_in + 1 + n_out:][:n_acc], rest[n_in + 1 + n_out + n_acc:]
        step = pl.program_id(0)
        for ax in range(1, len(grid)):
            step = step * grid[ax] + pl.program_id(ax)

        @pl.when(step == 0)
        def _():
            carried.start(c_ins, c_outs, sems)

        compute(a_ref, b_ref, o_ref, acc)

        @pl.when(step == steps - max(2, steps // 16))
        def _():
            carried.mid(c_ins, c_outs, sems)

        @pl.when(step == steps - 1)
        def _():
            carried.finish(c_ins, c_outs, sems)

    res = pl.pallas_call(body, out_shape=[out_shape] + list(carried.out_shapes), grid=grid,
                         in_specs=[a_spec, b_spec] + [HBM_SPEC] * n_in, out_specs=[o_spec] + [HBM_SPEC] * n_out,
                         scratch_shapes=scratch + list(carried.sem_shapes),
                         compiler_params=_params(*(("arbitrary",) * len(grid))), name=name)(a, b, *carried.ins)
    return res[0], res[1:]


def _pick(n, cands):
    for c in cands:
        if n % c == 0:
            return c
    raise ValueError(f"no tile for {n}")


def _row_tile(T):
    return _pick(T, (544, 512, 256, 128))


def mm_nn(a, b, *, out_dtype, name, tn=512, a_rows=None, o_rows=None, a_cols=None):
    M = a.shape[0]
    c0, K = a_cols or (0, a.shape[1])
    N = b.shape[1]
    tm = ROW_TILE if (a_rows or o_rows) else _row_tile(M)
    tn = min(tn, N)
    tk = K if K <= 2048 else _pick(K, (2048, 1024, 512))
    kb0 = c0 // tk
    ar = a_rows or (lambda i: i)
    orr = o_rows or (lambda i: i)
    return _matmul(a, b, pl.BlockSpec((tm, tk), lambda i, j, k: (ar(i), kb0 + k)),
                   pl.BlockSpec((tk, tn), lambda i, j, k: (k, j)),
                   pl.BlockSpec((tm, tn), lambda i, j, k: (orr(i), j)),
                   jax.ShapeDtypeStruct((M, N), out_dtype), (M // tm, N // tn, K // tk), name=name)


def mm_nt(a, b, *, out_dtype, name, a_rows=None, o_rows=None):
    M, K = a.shape
    N = b.shape[0]
    tm = ROW_TILE if (a_rows or o_rows) else _row_tile(M)
    tn = min(N, 2048)
    tk = K if K <= 1024 else _pick(K, (1024, 512))
    ar = a_rows or (lambda i: i)
    orr = o_rows or (lambda i: i)
    return _matmul(a, b, pl.BlockSpec((tm, tk), lambda i, j, k: (ar(i), k)),
                   pl.BlockSpec((tn, tk), lambda i, j, k: (j, k)),
                   pl.BlockSpec((tm, tn), lambda i, j, k: (orr(i), j)),
                   jax.ShapeDtypeStruct((M, N), out_dtype), (M // tm, N // tn, K // tk), tb=True, name=name)


def mm_tn(a, b, *, out_dtype, name, a_rows=None, b_rows=None, tm=512, tn=1024, a_cols=None):
    K = a.shape[0]
    c0, M = a_cols or (0, a.shape[1])
    N = b.shape[1]
    tk = ROW_TILE if (a_rows or b_rows) else K
    tm = min(tm, M)
    tn = min(tn, N)
    mb0 = c0 // tm
    ar = a_rows or (lambda k: k)
    br = b_rows or (lambda k: k)
    return _matmul(a, b, pl.BlockSpec((tk, tm), lambda i, j, k: (ar(k), mb0 + i)),
                   pl.BlockSpec((tk, tn), lambda i, j, k: (br(k), j)),
                   pl.BlockSpec((tm, tn), lambda i, j, k: (i, j)),
                   jax.ShapeDtypeStruct((M, N), out_dtype), (M // tm, N // tn, K // tk), ta=True, name=name)


def _ew(fn, ins, outs, colsums, *, T, L, name):
    tb = ROW_TILE
    nlat = L // tb
    seg = lambda i: jnp.where(i >= nlat, 1, 0)
    in_specs, arrays = [], []
    for arr, kind, cb, width in ins:
        arrays.append(arr)
        if kind == "row":
            in_specs.append(pl.BlockSpec((tb, width), lambda i, cb=cb: (i, cb)))
        elif kind == "bcast":
            in_specs.append(pl.BlockSpec((1, width), lambda i, cb=cb: (0, cb)))
        else:
            in_specs.append(pl.BlockSpec((None, 1, width), lambda i, cb=cb: (seg(i), 0, cb)))
    out_specs = [pl.BlockSpec((tb, w), lambda i: (i, 0)) for w, _ in outs]
    out_shapes = [jax.ShapeDtypeStruct((T, w), dt) for w, dt in outs]
    out_specs += [pl.BlockSpec((None, 1, w), lambda i: (seg(i), 0, 0)) for w in colsums]
    out_shapes += [jax.ShapeDtypeStruct((2, 1, w), F32) for w in colsums]
    n_in, n_out = len(ins), len(outs)

    def body(*refs):
        i = pl.program_id(0)
        res = fn(*[r[...] for r in refs[:n_in]])
        for r, v in zip(refs[n_in:n_in + n_out], res[:n_out]):
            r[...] = v.astype(r.dtype)
        first = (i == 0) | (i == nlat)
        for r, v in zip(refs[n_in + n_out:], res[n_out:]):
            s = jnp.sum(v, axis=0, keepdims=True)

            @pl.when(first)
            def _(r=r, s=s):
                r[...] = s

            @pl.when(jnp.logical_not(first))
            def _(r=r, s=s):
                r[...] += s

    res = pl.pallas_call(body, out_shape=out_shapes, grid=(T // tb,), in_specs=in_specs,
                         out_specs=out_specs, compiler_params=_params("arbitrary"), name=name)(*arrays)
    return res


def _rms(x):
    return lax.rsqrt(jnp.mean(x * x, axis=-1, keepdims=True) + RMS_EPS)


def prenorm_fwd(X, g, scale, shift, *, L):
    T = X.shape[0]

    def fn(x, g, sc, sh):
        return ((x * _rms(x)) * (g * (1.0 + sc)) + sh,)

    h, = _ew(fn, [(X, "row", 0, D_MODEL), (g, "bcast", 0, D_MODEL), (scale, "seg", 0, D_MODEL),
                  (shift, "seg", 0, D_MODEL)], [(D_MODEL, MM_DTYPE)], [], T=T, L=L, name="prenorm_fwd")
    return h


def prenorm_bwd(dh, X, g, scale, dres, *, L):
    T = X.shape[0]

    def fn(dh, x, g, sc, dres):
        r = _rms(x)
        xn = x * r
        dxn = dh * (g * (1.0 + sc))
        dx = r * (dxn - xn * jnp.mean(dxn * xn, axis=-1, keepdims=True))
        return dres + dx, dh, dh * xn

    return _ew(fn, [(dh, "row", 0, D_MODEL), (X, "row", 0, D_MODEL), (g, "bcast", 0, D_MODEL),
                    (scale, "seg", 0, D_MODEL), (dres, "row", 0, D_MODEL)],
               [(D_MODEL, F32)], [D_MODEL, D_MODEL], T=T, L=L, name="prenorm_bwd")


def postnorm_fwd(X, y, g, gate, *, L):
    T = X.shape[0]

    def fn(x, y, g, gate):
        return (x + gate * ((y * _rms(y)) * g),)

    out, = _ew(fn, [(X, "row", 0, D_MODEL), (y, "row", 0, D_MODEL), (g, "bcast", 0, D_MODEL),
                    (gate, "seg", 0, D_MODEL)], [(D_MODEL, F32)], [], T=T, L=L, name="postnorm_fwd")
    return out


def postnorm_bwd(dX, y, g, gate, *, L):
    T = dX.shape[0]

    def fn(dx, y, g, gate):
        r = _rms(y)
        yn = y * r
        dyn = dx * (gate * g)
        dy = r * (dyn - yn * jnp.mean(dyn * yn, axis=-1, keepdims=True))
        return dy, dx * yn

    return _ew(fn, [(dX, "row", 0, D_MODEL), (y, "row", 0, D_MODEL), (g, "bcast", 0, D_MODEL),
                    (gate, "seg", 0, D_MODEL)], [(D_MODEL, MM_DTYPE)], [D_MODEL], T=T, L=L, name="postnorm_bwd")


def loss_and_grad(X, target, *, L):
    T = X.shape[0]
    tb = ROW_TILE
    nlat = L // tb

    def body(x_ref, t_ref, dx_ref, part_ref):
        i = pl.program_id(0)

        @pl.when(i < nlat)
        def _():
            err = x_ref[...] - t_ref[...]
            dx_ref[...] = err * (1.0 / D_MODEL)
            part_ref[...] = jnp.full(part_ref.shape, 0.5 / D_MODEL * jnp.sum(err * err), F32)

        @pl.when(i >= nlat)
        def _():
            dx_ref[...] = jnp.zeros(dx_ref.shape, F32)
            part_ref[...] = jnp.zeros(part_ref.shape, F32)

    dx, part = pl.pallas_call(
        body, out_shape=[jax.ShapeDtypeStruct((T, D_MODEL), F32), jax.ShapeDtypeStruct((T // tb, 8, 128), F32)],
        grid=(T // tb,),
        in_specs=[pl.BlockSpec((tb, D_MODEL), lambda i: (i, 0)),
                  pl.BlockSpec((tb, D_MODEL), lambda i: (jnp.minimum(i, nlat - 1), 0))],
        out_specs=[pl.BlockSpec((tb, D_MODEL), lambda i: (i, 0)), pl.BlockSpec((None, 8, 128), lambda i: (i, 0, 0))],
        compiler_params=_params("parallel"), name="loss_and_grad")(X, target)
    return jnp.sum(part[:, 0, 0]), dx


Q_BLOCK = WIN_ROWS * GRID_W
BAND = 2 * WIN_ROWS * GRID_W


PAIR_TILES = 2 * WIN_ROWS
HEAD_PAIRS = N_HEADS // 2
LANES = 2 * HEAD_DIM
ROW_SHIFT = GRID_W.bit_length() - 1


def bias_pair_tiles(rpb):
    col = np.arange(GRID_W)
    col_start = np.clip(col - WIN_COLS // 2, 0, GRID_W - WIN_COLS)
    in_win = (col[None, :] >= col_start[:, None]) & (col[None, :] < col_start[:, None] + WIN_COLS)
    dcol = np.clip(col[None, :] - col[:, None] + (WIN_COLS - 1), 0, 2 * WIN_COLS - 2)
    E = np.stack([(dcol == dc) & in_win for dc in range(2 * WIN_COLS - 1)]).astype(np.float32)
    tiles = jnp.einsum("...rd,dqk->...rqk", rpb, E, precision=lax.Precision.HIGHEST)
    z = jnp.zeros(tiles.shape[:-3] + (1, GRID_W, GRID_W), F32)
    return jnp.concatenate([jnp.concatenate([z, tiles], axis=-3), jnp.concatenate([tiles, z], axis=-3)], axis=-1)


def _band_row(i, rows):
    return jnp.clip(WIN_ROWS * i - WIN_ROWS // 2, 0, rows - 2 * WIN_ROWS)


def _band_start(i, rows):
    return pl.multiple_of(_band_row(i, rows) * GRID_W, 256)


def _window_mask(i, rows):
    r = lax.broadcasted_iota(jnp.int32, (Q_BLOCK, BAND), 0)
    k = lax.broadcasted_iota(jnp.int32, (Q_BLOCK, BAND), 1)
    qr, qc = WIN_ROWS * i + (r >> ROW_SHIFT), r & (GRID_W - 1)
    kr, kc = _band_row(i, rows) + (k >> ROW_SHIFT), k & (GRID_W - 1)
    ws = jnp.clip(qr - WIN_ROWS // 2, 0, rows - WIN_ROWS)
    cs = jnp.clip(qc - WIN_COLS // 2, 0, GRID_W - WIN_COLS)
    return (kr >= ws) & (kr < ws + WIN_ROWS) & (kc >= cs) & (kc < cs + WIN_COLS)


def _pair_index(i, rows, a, j):
    off = _band_row(i, rows) - WIN_ROWS * i
    return jnp.clip(2 * j - a + WIN_ROWS + off, 0, PAIR_TILES - 1)


def _band_bias(p_ref, hh, i, rows):
    bands = [jnp.concatenate([p_ref[hh, _pair_index(i, rows, a, j)] for j in range(WIN_ROWS)], axis=1)
             for a in range(WIN_ROWS)]
    return jnp.concatenate(bands, axis=0)


def _dot_nt(a, b):
    return lax.dot_general(a, b, (((1,), (1,)), ((), ())), preferred_element_type=F32)


def _dot_tn(a, b):
    return lax.dot_general(a, b, (((0,), (0,)), ((), ())), preferred_element_type=F32)


def _dot(a, b):
    return jnp.dot(a, b, preferred_element_type=F32)


QKV_BLOCKS = tuple(COL[n] // LANES for n in ("q", "k", "v"))
SCALE = HEAD_DIM ** -0.5


def _head(x, hh):
    return x[:, hh * HEAD_DIM:(hh + 1) * HEAD_DIM]


def _both(fn):
    res = [fn(0), fn(1)]
    return [jnp.concatenate([a, b], axis=1) for a, b in zip(*res)]


def attn_fwd(proj, ptiles, *, L):
    T = proj.shape[0]
    N = T - L
    rows, nq = L // GRID_W, L // Q_BLOCK
    qb, kb, vb = QKV_BLOCKS

    def body(q_ref, k_ref, v_ref, p_ref, o_ref, lse_ref):
        i = pl.program_id(1)
        ks = _band_start(i, rows)
        mask = _window_mask(i, rows)
        qv = q_ref[...].astype(MM_DTYPE)
        kband, vband = k_ref[pl.ds(ks, BAND), :].astype(MM_DTYPE), v_ref[pl.ds(ks, BAND), :].astype(MM_DTYPE)
        kctx, vctx = k_ref[pl.ds(L, N), :].astype(MM_DTYPE), v_ref[pl.ds(L, N), :].astype(MM_DTYPE)

        def head(hh):
            q = _head(qv, hh)
            sb = _dot_nt(q, _head(kband, hh)) * SCALE + jnp.where(mask, _band_bias(p_ref, hh, i, rows), NEG_INF)
            sc = _dot_nt(q, _head(kctx, hh)) * SCALE
            m = jnp.maximum(jnp.max(sb, axis=-1, keepdims=True), jnp.max(sc, axis=-1, keepdims=True))
            pb, pc = jnp.exp(sb - m), jnp.exp(sc - m)
            l = jnp.sum(pb, axis=-1, keepdims=True) + jnp.sum(pc, axis=-1, keepdims=True)
            o = _dot(pb.astype(MM_DTYPE), _head(vband, hh)) + _dot(pc.astype(MM_DTYPE), _head(vctx, hh))
            return o / l, jnp.broadcast_to(m + jnp.log(l), (Q_BLOCK, HEAD_DIM))

        o_ref[...], lse_ref[...] = _both(head)

    qspec = lambda b0: pl.BlockSpec((Q_BLOCK, LANES), lambda hp, i: (i, b0 + hp))
    kspec = lambda b0: pl.BlockSpec((T, LANES), lambda hp, i: (0, b0 + hp))
    return pl.pallas_call(
        body, out_shape=[jax.ShapeDtypeStruct((T, BRANCH), F32), jax.ShapeDtypeStruct((L, BRANCH), F32)],
        grid=(HEAD_PAIRS, nq),
        in_specs=[qspec(qb), kspec(kb), kspec(vb),
                  pl.BlockSpec((2, PAIR_TILES, GRID_W, LANES), lambda hp, i: (hp, 0, 0, 0))],
        out_specs=[qspec(0), qspec(0)],
        compiler_params=_params("parallel", "arbitrary"), name="attn_fwd")(proj, proj, proj, ptiles)


def attn_bwd(proj, ptiles, o, do, lse, *, L):
    T = proj.shape[0]
    N = T - L
    rows, nq = L // GRID_W, L // Q_BLOCK
    qb, kb, vb = QKV_BLOCKS

    def body(q_ref, k_ref, v_ref, p_ref, o_ref, do_ref, lse_ref, dq_ref, dk_ref, dv_ref, dp_ref):
        i = pl.program_id(1)
        ks = _band_start(i, rows)

        @pl.when(i == 0)
        def _():
            dk_ref[...] = jnp.zeros(dk_ref.shape, F32)
            dv_ref[...] = jnp.zeros(dv_ref.shape, F32)
            dp_ref[...] = jnp.zeros(dp_ref.shape, F32)

        mask = _window_mask(i, rows)
        qv = q_ref[...].astype(MM_DTYPE)
        kband, vband = k_ref[pl.ds(ks, BAND), :].astype(MM_DTYPE), v_ref[pl.ds(ks, BAND), :].astype(MM_DTYPE)
        kctx, vctx = k_ref[pl.ds(L, N), :].astype(MM_DTYPE), v_ref[pl.ds(L, N), :].astype(MM_DTYPE)
        ov, dof, lsev = o_ref[...], do_ref[...], lse_ref[...]

        def head(hh):
            q, kb_h, kc_h, vb_h, vc_h = (_head(t, hh) for t in (qv, kband, kctx, vband, vctx))
            lse = _head(lsev, hh)[:, 0:1]
            pb = jnp.exp(_dot_nt(q, kb_h) * SCALE + jnp.where(mask, _band_bias(p_ref, hh, i, rows), NEG_INF) - lse)
            pc = jnp.exp(_dot_nt(q, kc_h) * SCALE - lse)
            do_h = _head(dof, hh)
            delta = jnp.sum(do_h * _head(ov, hh), axis=-1, keepdims=True)
            dov = do_h.astype(MM_DTYPE)
            dsb = pb * (_dot_nt(dov, vb_h) - delta)
            dsc = pc * (_dot_nt(dov, vc_h) - delta)
            for a in range(WIN_ROWS):
                for j in range(WIN_ROWS):
                    dp_ref[hh, _pair_index(i, rows, a, j)] += dsb[a * GRID_W:(a + 1) * GRID_W, j * LANES:(j + 1) * LANES]
            dsb_s, dsc_s = (dsb * SCALE).astype(MM_DTYPE), (dsc * SCALE).astype(MM_DTYPE)
            dq = _dot(dsb_s, kb_h) + _dot(dsc_s, kc_h)
            return (dq, _dot_tn(dsb_s, q), _dot_tn(dsc_s, q), _dot_tn(pb.astype(MM_DTYPE), dov),
                    _dot_tn(pc.astype(MM_DTYPE), dov))

        dq, dkb, dkc, dvb, dvc = _both(head)
        dq_ref[...] = dq
        dk_ref[pl.ds(ks, BAND), :] += dkb
        dk_ref[pl.ds(L, N), :] += dkc
        dv_ref[pl.ds(ks, BAND), :] += dvb
        dv_ref[pl.ds(L, N), :] += dvc

    qspec = lambda b0: pl.BlockSpec((Q_BLOCK, LANES), lambda hp, i: (i, b0 + hp))
    kspec = lambda b0: pl.BlockSpec((T, LANES), lambda hp, i: (0, b0 + hp))
    pspec = pl.BlockSpec((2, PAIR_TILES, GRID_W, LANES), lambda hp, i: (hp, 0, 0, 0))
    return pl.pallas_call(
        body,
        out_shape=[jax.ShapeDtypeStruct((T, BRANCH), F32)] * 3 + [jax.ShapeDtypeStruct(ptiles.shape, F32)],
        grid=(HEAD_PAIRS, nq),
        in_specs=[qspec(qb), kspec(kb), kspec(vb), pspec, qspec(0), qspec(0), qspec(0)],
        out_specs=[qspec(0), kspec(0), kspec(0), pspec],
        compiler_params=_params("parallel", "arbitrary"), name="attn_bwd")(proj, proj, proj, ptiles, o, do, lse)


ANY_SPEC = pl.BlockSpec(memory_space=pl.ANY)


def cattn_fwd(proj, o, *, L):
    T = proj.shape[0]
    N = T - L
    qb, kb, vb = QKV_BLOCKS
    cspec = lambda b0: pl.BlockSpec((N, LANES), lambda hp: (L // N, b0 + hp))

    def body(q_ref, k_ref, v_ref, o_in, o_ref, lse_ref):
        qv, kv, vv = (r[...].astype(MM_DTYPE) for r in (q_ref, k_ref, v_ref))

        def head(hh):
            s = _dot_nt(_head(qv, hh), _head(kv, hh)) * SCALE
            m = jnp.max(s, axis=-1, keepdims=True)
            p = jnp.exp(s - m)
            l = jnp.sum(p, axis=-1, keepdims=True)
            return _dot(p.astype(MM_DTYPE), _head(vv, hh)) / l, jnp.broadcast_to(m + jnp.log(l), (N, HEAD_DIM))

        o_ref[...], lse_ref[...] = _both(head)

    return pl.pallas_call(
        body, out_shape=[jax.ShapeDtypeStruct(o.shape, F32), jax.ShapeDtypeStruct((N, BRANCH), F32)],
        grid=(HEAD_PAIRS,), in_specs=[cspec(qb), cspec(kb), cspec(vb), ANY_SPEC],
        out_specs=[cspec(0), pl.BlockSpec((N, LANES), lambda hp: (0, hp))], input_output_aliases={3: 0},
        compiler_params=_params("parallel"), name="cattn_fwd")(proj, proj, proj, o)


def cattn_bwd(proj, o, do, lse, dq, dk, dv, *, L):
    T = proj.shape[0]
    N = T - L
    qb, kb, vb = QKV_BLOCKS
    cspec = lambda b0: pl.BlockSpec((N, LANES), lambda hp: (L // N, b0 + hp))

    def body(q_ref, k_ref, v_ref, o_ref, do_ref, lse_ref, dq_in, dk_in, dv_in, dq_ref, dk_ref, dv_ref):
        qv, kv, vv = (r[...].astype(MM_DTYPE) for r in (q_ref, k_ref, v_ref))
        ov, dof, lsev = o_ref[...], do_ref[...], lse_ref[...]

        def head(hh):
            q, k, v = _head(qv, hh), _head(kv, hh), _head(vv, hh)
            p = jnp.exp(_dot_nt(q, k) * SCALE - _head(lsev, hh)[:, 0:1])
            do_h = _head(dof, hh)
            delta = jnp.sum(do_h * _head(ov, hh), axis=-1, keepdims=True)
            dov = do_h.astype(MM_DTYPE)
            ds = (p * (_dot_nt(dov, v) - delta) * SCALE).astype(MM_DTYPE)
            return _dot(ds, k), _dot_tn(ds, q), _dot_tn(p.astype(MM_DTYPE), dov)

        dq_c, dk_c, dv_c = _both(head)
        dq_ref[...] = dq_c
        dk_ref[...] = dk_in[...] + dk_c
        dv_ref[...] = dv_in[...] + dv_c

    return pl.pallas_call(
        body, out_shape=[jax.ShapeDtypeStruct(dq.shape, F32)] * 3, grid=(HEAD_PAIRS,),
        in_specs=[cspec(qb), cspec(kb), cspec(vb), cspec(0), cspec(0), pl.BlockSpec((N, LANES), lambda hp: (0, hp)),
                  ANY_SPEC, cspec(0), cspec(0)],
        out_specs=[cspec(0)] * 3, input_output_aliases={6: 0, 7: 1, 8: 2},
        compiler_params=_params("parallel"), name="cattn_bwd")(proj, proj, proj, o, do, lse, dq, dk, dv)


PAD = 16


def _row_ids(T):
    return lax.broadcasted_iota(jnp.int32, (T, POOL_DIM), 0)


def _same_segment(t, s, L, T):
    return (s >= 0) & (s < T) & ((t < L) == (s < L))


def _window_sum(buf_ref, x, half, *, L, T, transpose):
    buf_ref[pl.ds(PAD, T), :] = x
    t = _row_ids(T)
    acc = jnp.zeros((T, POOL_DIM), F32)
    for j in range(-8, 9):
        inside = ((j > -half) & (j <= half)) if transpose else ((j >= -half) & (j < half))
        ok = _same_segment(t, t + j, L, T) & inside
        acc = acc + jnp.where(ok, buf_ref[pl.ds(PAD + j, T), :], 0.0)
    return acc


def _window_count(half, *, L, T):
    t = _row_ids(T)
    pos = jnp.where(t < L, t, t - L)
    seg_len = jnp.where(t < L, L, T - L)
    return (jnp.minimum(pos + half, seg_len) - jnp.maximum(pos - half, 0)).astype(F32)


def _zero_pads(buf_ref, T):
    buf_ref[pl.ds(0, PAD), :] = jnp.zeros((PAD, POOL_DIM), F32)
    buf_ref[pl.ds(PAD + T, PAD), :] = jnp.zeros((PAD, POOL_DIM), F32)


def pool_fwd(proj, pool_w, pool_scale, *, L):
    T = proj.shape[0]
    cb0 = COL["pool_u"] // POOL_DIM

    def body(u_ref, w_ref, s_ref, o_ref, p_ref, buf_ref):
        half = jnp.left_shift(1, pl.program_id(0))
        _zero_pads(buf_ref, T)
        u = u_ref[...].astype(F32)
        pooled = _window_sum(buf_ref, u, half, L=L, T=T, transpose=False) / _window_count(half, L=L, T=T) - u
        pm = pooled.astype(MM_DTYPE)
        p_ref[...] = pm
        o_ref[...] = _dot(pm, w_ref[...].astype(MM_DTYPE)) * s_ref[...]

    cspec = pl.BlockSpec((T, POOL_DIM), lambda g: (0, g))
    return pl.pallas_call(
        body, out_shape=[jax.ShapeDtypeStruct((T, BRANCH), F32), jax.ShapeDtypeStruct((T, BRANCH), MM_DTYPE)],
        grid=(POOL_GROUPS,),
        in_specs=[pl.BlockSpec((T, POOL_DIM), lambda g: (0, cb0 + g)),
                  pl.BlockSpec((None, POOL_DIM, POOL_DIM), lambda g: (g, 0, 0)),
                  pl.BlockSpec((1, POOL_DIM), lambda g: (0, g))],
        out_specs=[cspec, cspec], scratch_shapes=[pltpu.VMEM((T + 2 * PAD, POOL_DIM), F32)],
        compiler_params=_params("parallel"), name="pool_fwd")(proj, pool_w, pool_scale)


def pool_bwd(do, pooled, pool_w, pool_scale, *, L):
    T = do.shape[0]

    def body(do_ref, p_ref, w_ref, s_ref, du_ref, dw_ref, ds_ref, buf_ref):
        half = jnp.left_shift(1, pl.program_id(0))
        _zero_pads(buf_ref, T)
        pm = p_ref[...]
        w = w_ref[...].astype(MM_DTYPE)
        mixed = _dot(pm, w)
        dov = do_ref[...]
        ds_ref[...] = jnp.broadcast_to(jnp.sum(dov * mixed, axis=0, keepdims=True), ds_ref.shape)
        dmixed = (dov * s_ref[...]).astype(MM_DTYPE)
        dw_ref[...] = _dot_tn(pm, dmixed)
        dpooled = _dot_nt(dmixed, w)
        scaled = dpooled / _window_count(half, L=L, T=T)
        du = _window_sum(buf_ref, scaled, half, L=L, T=T, transpose=True) - dpooled
        du_ref[...] = du.astype(du_ref.dtype)

    cspec = pl.BlockSpec((T, POOL_DIM), lambda g: (0, g))
    return pl.pallas_call(
        body, out_shape=[jax.ShapeDtypeStruct((T, BRANCH), MM_DTYPE),
                         jax.ShapeDtypeStruct((POOL_GROUPS, POOL_DIM, POOL_DIM), F32),
                         jax.ShapeDtypeStruct((8, BRANCH), F32)],
        grid=(POOL_GROUPS,),
        in_specs=[cspec, cspec, pl.BlockSpec((None, POOL_DIM, POOL_DIM), lambda g: (g, 0, 0)),
                  pl.BlockSpec((1, POOL_DIM), lambda g: (0, g))],
        out_specs=[cspec, pl.BlockSpec((None, POOL_DIM, POOL_DIM), lambda g: (g, 0, 0)),
                   pl.BlockSpec((8, POOL_DIM), lambda g: (0, g))],
        scratch_shapes=[pltpu.VMEM((T + 2 * PAD, POOL_DIM), F32)],
        compiler_params=_params("parallel"), name="pool_bwd")(do, pooled, pool_w, pool_scale)


def _shifted(buf_ref, x, j, *, L, T):
    buf_ref[pl.ds(PAD, T), :] = x
    t = _row_ids(T)
    return jnp.where(_same_segment(t, t + j, L, T), buf_ref[pl.ds(PAD + j, T), :], 0.0)


def conv_fwd(proj, conv_w, *, L):
    T = proj.shape[0]
    nb = BRANCH // POOL_DIM
    cx, cbb, cc = (COL[n] // POOL_DIM for n in ("conv_x", "conv_b", "conv_c"))

    def body(x_ref, b_ref, c_ref, w_ref, o_ref, buf_ref):
        _zero_pads(buf_ref, T)
        xc = c_ref[...].astype(F32) * x_ref[...].astype(F32)
        w = w_ref[...]
        conv = (w[0:1] * _shifted(buf_ref, xc, -1, L=L, T=T) + w[1:2] * xc
                + w[2:3] * _shifted(buf_ref, xc, 1, L=L, T=T))
        o_ref[...] = b_ref[...].astype(F32) * conv

    return pl.pallas_call(
        body, out_shape=jax.ShapeDtypeStruct((T, BRANCH), F32), grid=(nb,),
        in_specs=[pl.BlockSpec((T, POOL_DIM), lambda g: (0, cx + g)), pl.BlockSpec((T, POOL_DIM), lambda g: (0, cbb + g)),
                  pl.BlockSpec((T, POOL_DIM), lambda g: (0, cc + g)), pl.BlockSpec((8, POOL_DIM), lambda g: (0, g))],
        out_specs=pl.BlockSpec((T, POOL_DIM), lambda g: (0, g)),
        scratch_shapes=[pltpu.VMEM((T + 2 * PAD, POOL_DIM), F32)],
        compiler_params=_params("parallel"), name="conv_fwd")(proj, proj, proj, conv_w)


def conv_bwd(do, proj, conv_w, *, L):
    T = proj.shape[0]
    nb = BRANCH // POOL_DIM
    cx, cbb, cc = (COL[n] // POOL_DIM for n in ("conv_x", "conv_b", "conv_c"))

    def body(do_ref, x_ref, b_ref, c_ref, w_ref, dx_ref, db_ref, dc_ref, dw_ref, buf_ref):
        _zero_pads(buf_ref, T)
        xv, gb, gc = (r[...].astype(F32) for r in (x_ref, b_ref, c_ref))
        xc = gc * xv
        w = w_ref[...]
        xm = _shifted(buf_ref, xc, -1, L=L, T=T)
        xp = _shifted(buf_ref, xc, 1, L=L, T=T)
        conv = w[0:1] * xm + w[1:2] * xc + w[2:3] * xp
        dov = do_ref[...]
        db_ref[...] = (dov * conv).astype(db_ref.dtype)
        dconv = dov * gb
        sums = [jnp.sum(dconv * a, axis=0, keepdims=True) for a in (xm, xc, xp)]
        dw_ref[...] = jnp.concatenate(sums + [jnp.zeros((5, POOL_DIM), F32)], axis=0)
        dxc = (w[0:1] * _shifted(buf_ref, dconv, 1, L=L, T=T) + w[1:2] * dconv
               + w[2:3] * _shifted(buf_ref, dconv, -1, L=L, T=T))
        dc_ref[...] = (dxc * xv).astype(dc_ref.dtype)
        dx_ref[...] = (dxc * gc).astype(dx_ref.dtype)

    ospec = lambda off: pl.BlockSpec((T, POOL_DIM), lambda g: (0, off + g))
    return pl.pallas_call(
        body, out_shape=[jax.ShapeDtypeStruct((T, BRANCH), MM_DTYPE)] * 3 + [jax.ShapeDtypeStruct((8, BRANCH), F32)],
        grid=(nb,),
        in_specs=[ospec(0), ospec(cx), ospec(cbb), ospec(cc), pl.BlockSpec((8, POOL_DIM), lambda g: (0, g))],
        out_specs=[ospec(0), ospec(0), ospec(0), pl.BlockSpec((8, POOL_DIM), lambda g: (0, g))],
        scratch_shapes=[pltpu.VMEM((T + 2 * PAD, POOL_DIM), F32)],
        compiler_params=_params("parallel"), name="conv_bwd")(do, proj, proj, proj, conv_w)


SCAN_COLS = N_STATE
SCAN_ROWS = 256


def ssm_operators(a_re, a_im, log_dt, b_re, b_im, c_re, c_im):
    n = a_re.shape[0]
    dt = jnp.exp(log_dt)[..., None]
    mag = jnp.exp(a_re * dt)
    abar_re, abar_im = mag * jnp.cos(a_im * dt), mag * jnp.sin(a_im * dt)
    den = a_re * a_re + a_im * a_im
    num_re, num_im = abar_re - 1.0, abar_im
    f_re = (num_re * a_re + num_im * a_im) / den
    f_im = (num_im * a_re - num_re * a_im) / den
    bbar_re = f_re[..., None] * b_re - f_im[..., None] * b_im
    bbar_im = f_re[..., None] * b_im + f_im[..., None] * b_re
    gpb = SSM_GROUPS // SSM_BLOCKS
    eye = jnp.eye(gpb, dtype=bool)[None, None, :, None, :, None]

    def blocks(t):
        _, _, a, b = t.shape
        t = t.reshape(n, SSM_BLOCKS, gpb, a, 1, b)
        return jnp.where(eye, t, 0.0).reshape(n, SSM_BLOCKS, gpb * a, gpb * b)

    in_map = lambda bbar: blocks(bbar.transpose(0, 1, 3, 2))
    out_map = lambda c: blocks(c.transpose(0, 1, 3, 2))
    abar = jnp.concatenate([abar_re.reshape(n, 1, N_STATE), abar_im.reshape(n, 1, N_STATE)], axis=-1)
    bcat = jnp.concatenate([in_map(bbar_re), in_map(bbar_im)], axis=1)
    ccat = jnp.concatenate([out_map(c_re), -out_map(c_im)], axis=1)
    return abar, bcat, ccat


SSM_BLOCKS = 4
SSM_BCH = BRANCH // SSM_BLOCKS
SSM_BST = N_STATE // SSM_BLOCKS


def _ssm_rows(T, perm):
    tm = ROW_TILE if perm else _row_tile(T)
    return tm, (perm or (lambda i: i))


def _lanes(x, n, width):
    return x[:, n * width:(n + 1) * width]


def _ssm_specs(T, perm, ucol0=None):
    tm, rows = _ssm_rows(T, perm)
    chan = pl.BlockSpec((tm, BRANCH), lambda i: (rows(i), 0 if ucol0 is None else ucol0 // BRANCH))
    state = pl.BlockSpec((tm, 2 * N_STATE), lambda i: (i, 0))
    bspec = pl.BlockSpec((2 * SSM_BLOCKS, SSM_BCH, SSM_BST), lambda i: (0, 0, 0))
    cspec = pl.BlockSpec((2 * SSM_BLOCKS, SSM_BST, SSM_BCH), lambda i: (0, 0, 0))
    return T // tm, chan, state, bspec, cspec


def ssm_in(u, bcat, *, ucol0, perm, name):
    T = u.shape[0]
    steps, chan, state, bspec, _ = _ssm_specs(T, perm, ucol0)

    def body(u_ref, b_ref, o_ref):
        uv = u_ref[...].astype(MM_DTYPE)
        for n in range(2 * SSM_BLOCKS):
            o_ref[:, n * SSM_BST:(n + 1) * SSM_BST] = _dot(_lanes(uv, n % SSM_BLOCKS, SSM_BCH), b_ref[n].astype(MM_DTYPE))

    return pl.pallas_call(body, out_shape=jax.ShapeDtypeStruct((T, 2 * N_STATE), F32), grid=(steps,),
                          in_specs=[chan, bspec], out_specs=state, compiler_params=_params("parallel"), name=name)(u, bcat)


def ssm_out(s, ccat, *, perm, name):
    T = s.shape[0]
    steps, chan, state, _, cspec = _ssm_specs(T, perm)

    def body(s_ref, c_ref, o_ref):
        sv = s_ref[...].astype(MM_DTYPE)
        o_ref[...] = jnp.concatenate(
            [_dot(_lanes(sv, j, SSM_BST), c_ref[j].astype(MM_DTYPE))
             + _dot(_lanes(sv, SSM_BLOCKS + j, SSM_BST), c_ref[SSM_BLOCKS + j].astype(MM_DTYPE))
             for j in range(SSM_BLOCKS)], axis=1)

    return pl.pallas_call(body, out_shape=jax.ShapeDtypeStruct((T, BRANCH), F32), grid=(steps,),
                          in_specs=[state, cspec], out_specs=chan, compiler_params=_params("parallel"), name=name)(s, ccat)


def ssm_out_dx(dy, ccat, *, perm, name):
    T = dy.shape[0]
    steps, chan, state, _, cspec = _ssm_specs(T, perm)

    def body(d_ref, c_ref, o_ref):
        dv = d_ref[...].astype(MM_DTYPE)
        for n in range(2 * SSM_BLOCKS):
            o_ref[:, n * SSM_BST:(n + 1) * SSM_BST] = _dot_nt(_lanes(dv, n % SSM_BLOCKS, SSM_BCH), c_ref[n].astype(MM_DTYPE))

    return pl.pallas_call(body, out_shape=jax.ShapeDtypeStruct((T, 2 * N_STATE), F32), grid=(steps,),
                          in_specs=[chan, cspec], out_specs=state, compiler_params=_params("parallel"), name=name)(dy, ccat)


def ssm_in_dx(lam, bcat, *, perm, name):
    T = lam.shape[0]
    steps, chan, state, bspec, _ = _ssm_specs(T, perm)

    def body(l_ref, b_ref, o_ref):
        lv = l_ref[...].astype(MM_DTYPE)
        o_ref[...] = jnp.concatenate(
            [_dot_nt(_lanes(lv, j, SSM_BST), b_ref[j].astype(MM_DTYPE))
             + _dot_nt(_lanes(lv, SSM_BLOCKS + j, SSM_BST), b_ref[SSM_BLOCKS + j].astype(MM_DTYPE))
             for j in range(SSM_BLOCKS)], axis=1)

    return pl.pallas_call(body, out_shape=jax.ShapeDtypeStruct((T, BRANCH), F32), grid=(steps,),
                          in_specs=[state, bspec], out_specs=chan, compiler_params=_params("parallel"), name=name)(lam, bcat)


def _ssm_dw(chan_arr, state_arr, chan_spec, state_spec, out_block, steps, chan_first, name):
    def body(c_ref, s_ref, o_ref):
        @pl.when(pl.program_id(0) == 0)
        def _():
            o_ref[...] = jnp.zeros(o_ref.shape, F32)

        cv, sv = c_ref[...].astype(MM_DTYPE), s_ref[...].astype(MM_DTYPE)
        for n in range(2 * SSM_BLOCKS):
            c, s = _lanes(cv, n % SSM_BLOCKS, SSM_BCH), _lanes(sv, n, SSM_BST)
            o_ref[n] += _dot_tn(c, s) if chan_first else _dot_tn(s, c)

    shape = (2 * SSM_BLOCKS,) + out_block
    return pl.pallas_call(body, out_shape=jax.ShapeDtypeStruct(shape, F32), grid=(steps,),
                          in_specs=[chan_spec, state_spec], out_specs=pl.BlockSpec(shape, lambda k: (0, 0, 0)),
                          compiler_params=_params("arbitrary"), name=name)(chan_arr, state_arr)


def ssm_in_dw(u, lam, *, ucol0, perm, name):
    steps, chan, state, _, _ = _ssm_specs(u.shape[0], perm, ucol0)
    return _ssm_dw(u, lam, chan, state, (SSM_BCH, SSM_BST), steps, True, name)


def ssm_out_dw(s, dy, *, perm, name):
    steps, chan, state, _, _ = _ssm_specs(s.shape[0], perm)
    return _ssm_dw(dy, s, chan, state, (SSM_BST, SSM_BCH), steps, False, name)


def _time_block(T, reverse):
    nt = T // SCAN_ROWS
    tix = (lambda i: nt - 1 - i) if reverse else (lambda i: i)
    return nt, pl.BlockSpec((SCAN_ROWS, 2 * N_STATE), lambda i: (tix(i), 0))


def ssm_scan(bu, abar, *, reverse):
    T = bu.shape[0]
    nt, tspec = _time_block(T, reverse)

    def body(b_ref, a_ref, s_ref, c_ref):
        @pl.when(pl.program_id(0) == 0)
        def _():
            c_ref[...] = jnp.zeros(c_ref.shape, F32)

        for c0 in range(0, N_STATE, SCAN_COLS):
            re, im = pl.ds(c0, SCAN_COLS), pl.ds(N_STATE + c0, SCAN_COLS)
            ar, ai = a_ref[:, re], a_ref[:, im]

            def step(n, carry, re=re, im=im, ar=ar, ai=ai):
                sr, si = carry
                t = (SCAN_ROWS - 1 - n) if reverse else n
                nr = ar * sr - ai * si + b_ref[pl.ds(t, 1), re]
                ni = ar * si + ai * sr + b_ref[pl.ds(t, 1), im]
                s_ref[pl.ds(t, 1), re] = nr
                s_ref[pl.ds(t, 1), im] = ni
                return nr, ni

            sr, si = lax.fori_loop(0, SCAN_ROWS, step, (c_ref[:, re], c_ref[:, im]))
            c_ref[:, re] = sr
            c_ref[:, im] = si

    return pl.pallas_call(
        body, out_shape=jax.ShapeDtypeStruct((T, 2 * N_STATE), F32), grid=(nt,),
        in_specs=[tspec, pl.BlockSpec((1, 2 * N_STATE), lambda i: (0, 0))], out_specs=tspec,
        scratch_shapes=[pltpu.VMEM((1, 2 * N_STATE), F32)],
        compiler_params=_params("arbitrary"), name="ssm_scan_rev" if reverse else "ssm_scan_fwd")(bu, abar)


def ssm_scan_bwd(g, s, abar, *, reverse):
    T = g.shape[0]
    nt, tspec = _time_block(T, not reverse)
    back = not reverse

    def body(g_ref, s_ref, a_ref, l_ref, da_ref, c_ref):
        @pl.when(pl.program_id(0) == 0)
        def _():
            c_ref[...] = jnp.zeros(c_ref.shape, F32)
            da_ref[...] = jnp.zeros(da_ref.shape, F32)

        for c0 in range(0, N_STATE, SCAN_COLS):
            re, im = pl.ds(c0, SCAN_COLS), pl.ds(N_STATE + c0, SCAN_COLS)
            ar, ai = a_ref[:, re], a_ref[:, im]

            def step(n, carry, re=re, im=im, ar=ar, ai=ai):
                lr, li, dr, di = carry
                t = (SCAN_ROWS - 1 - n) if back else n
                sr, si = s_ref[pl.ds(t, 1), re], s_ref[pl.ds(t, 1), im]
                dr = dr + sr * lr + si * li
                di = di + sr * li - si * lr
                nr = g_ref[pl.ds(t, 1), re] + ar * lr + ai * li
                ni = g_ref[pl.ds(t, 1), im] + ar * li - ai * lr
                l_ref[pl.ds(t, 1), re] = nr
                l_ref[pl.ds(t, 1), im] = ni
                return nr, ni, dr, di

            zero = jnp.zeros((1, SCAN_COLS), F32)
            lr, li, dr, di = lax.fori_loop(0, SCAN_ROWS, step, (c_ref[:, re], c_ref[:, im], zero, zero))
            c_ref[:, re] = lr
            c_ref[:, im] = li
            da_ref[:, re] += jnp.broadcast_to(dr, (8, SCAN_COLS))
            da_ref[:, im] += jnp.broadcast_to(di, (8, SCAN_COLS))

    return pl.pallas_call(
        body, out_shape=[jax.ShapeDtypeStruct((T, 2 * N_STATE), F32), jax.ShapeDtypeStruct((8, 2 * N_STATE), F32)],
        grid=(nt,), in_specs=[tspec, tspec, pl.BlockSpec((1, 2 * N_STATE), lambda i: (0, 0))],
        out_specs=[tspec, pl.BlockSpec((8, 2 * N_STATE), lambda i: (0, 0))],
        scratch_shapes=[pltpu.VMEM((1, 2 * N_STATE), F32)],
        compiler_params=_params("arbitrary"),
        name="ssm_scan_bwd_rev" if reverse else "ssm_scan_bwd_fwd")(g, s, abar)


def _gelu(x):
    return 0.5 * x * (1.0 + jnp.tanh(0.7978845608028654 * (x + 0.044715 * x * x * x)))


def _gelu_grad(x):
    t = jnp.tanh(0.7978845608028654 * (x + 0.044715 * x * x * x))
    return 0.5 * (1.0 + t) + 0.5 * x * (1.0 - t * t) * 0.7978845608028654 * (1.0 + 3 * 0.044715 * x * x)


def _silu(z):
    return z * _sigmoid(z)


def _silu_grad(z):
    s = _sigmoid(z)
    return s * (1.0 + z * (1.0 - s))


def ssm_fwd(proj, ops, dsk, glu_w, *, L):
    T = proj.shape[0]
    abar, bcat, ccat = ops
    nb, nlat = T // ROW_TILE, L // ROW_TILE
    to_f = lambda i: (i + nlat) % nb
    states, ys = [], []
    for d in (0, 1):
        perm = to_f if d == 0 else None
        bu = ssm_in(proj, bcat[d], ucol0=COL["ssm_u"], perm=perm, name=f"ssm_in{d}")
        s = ssm_scan(bu, abar[d], reverse=(d == 1))
        states.append(s)
        ys.append(ssm_out(s, ccat[d], perm=perm, name=f"ssm_out{d}"))

    def pre(u, yf, yr, dsk):
        y = dsk * u + yf + yr
        return y, _gelu(y)

    ypre, gy = _ew(pre, [(proj, "row", COL["ssm_u"] // BRANCH, BRANCH), (ys[0], "row", 0, BRANCH),
                         (ys[1], "row", 0, BRANCH), (dsk, "bcast", 0, BRANCH)],
                   [(BRANCH, F32), (BRANCH, MM_DTYPE)], [], T=T, L=L, name="ssm_pre")
    gg = mm_nn(gy, glu_w, out_dtype=F32, name="ssm_glu")

    def post(ga, gb):
        return (ga * _sigmoid(gb),)

    o, = _ew(post, [(gg, "row", 0, BRANCH), (gg, "row", 1, BRANCH)], [(BRANCH, F32)], [], T=T, L=L, name="ssm_post")
    return o, dict(states=states, ypre=ypre, gy=gy, gg=gg)


def ssm_bwd(do, proj, ops, dsk, glu_w, saved, *, L):
    T = proj.shape[0]
    abar, bcat, ccat = ops
    nb, nlat = T // ROW_TILE, L // ROW_TILE
    to_f = lambda i: (i + nlat) % nb
    gg, gy, ypre = saved["gg"], saved["gy"], saved["ypre"]

    def post_bwd(do, ga, gb):
        sg = _sigmoid(gb)
        return (jnp.concatenate([do * sg, do * ga * sg * (1.0 - sg)], axis=1),)

    dgg, = _ew(post_bwd, [(do, "row", 0, BRANCH), (gg, "row", 0, BRANCH), (gg, "row", 1, BRANCH)],
               [(2 * BRANCH, MM_DTYPE)], [], T=T, L=L, name="ssm_post_bwd")
    dgy = mm_nt(dgg, glu_w, out_dtype=F32, name="ssm_glu_dx")
    dglu = mm_tn(gy, dgg, out_dtype=F32, name="ssm_glu_dw")

    def pre_bwd(dgy, y, u, dsk):
        dy = dgy * _gelu_grad(y)
        return dy, dy * dsk, dy * u

    dy, du_skip, dd = _ew(pre_bwd, [(dgy, "row", 0, BRANCH), (ypre, "row", 0, BRANCH),
                                    (proj, "row", COL["ssm_u"] // BRANCH, BRANCH), (dsk, "bcast", 0, BRANCH)],
                          [(BRANCH, MM_DTYPE), (BRANCH, F32)], [BRANCH], T=T, L=L, name="ssm_pre_bwd")
    du = du_skip
    dabar, dbcat, dccat = [], [], []
    for d in (0, 1):
        perm = to_f if d == 0 else None
        s = saved["states"][d]
        g = ssm_out_dx(dy, ccat[d], perm=perm, name=f"ssm_out{d}_dx")
        lam, da = ssm_scan_bwd(g, s, abar[d], reverse=(d == 1))
        dabar.append(da[0:1])
        dccat.append(ssm_out_dw(s, dy, perm=perm, name=f"ssm_out{d}_dw"))
        du = du + ssm_in_dx(lam, bcat[d], perm=perm, name=f"ssm_in{d}_dx")
        dbcat.append(ssm_in_dw(proj, lam, ucol0=COL["ssm_u"], perm=perm, name=f"ssm_in{d}_dw"))
    d_ops = (jnp.stack(dabar), jnp.stack(dbcat), jnp.stack(dccat))
    return du, d_ops, dd[0, 0] + dd[1, 0], dglu


Z_COLS = tuple(COL[n] // BRANCH for n in ("na_z", "pool_z", "conv_z", "ssm_z"))


def gate_act(o, proj, *, L):
    T = o.shape[0]

    def fn(o, z0, z1, z2, z3):
        return (o * _silu(jnp.concatenate([z0, z1, z2, z3], axis=1).astype(F32)),)

    a, = _ew(fn, [(o, "row", 0, D_MODEL)] + [(proj, "row", c, BRANCH) for c in Z_COLS],
             [(D_MODEL, MM_DTYPE)], [], T=T, L=L, name="gate_act")
    return a


def gate_act_bwd(da, o, proj, *, L):
    T = o.shape[0]

    def fn(da, o, z0, z1, z2, z3):
        z = jnp.concatenate([z0, z1, z2, z3], axis=1).astype(F32)
        return da * _silu(z), da * o * _silu_grad(z)

    return _ew(fn, [(da, "row", 0, D_MODEL), (o, "row", 0, D_MODEL)] + [(proj, "row", c, BRANCH) for c in Z_COLS],
               [(D_MODEL, F32), (D_MODEL, MM_DTYPE)], [], T=T, L=L, name="gate_act_bwd")


MERGE_TN = 1024


def merge_fwd(a, w_br, proj, b_gate):
    T = a.shape[0]
    tm, tn = _row_tile(T), MERGE_TN
    nn = D_MODEL // tn
    lb0 = COL["merge"] // tn

    def body(a_ref, w_ref, l_ref, b_ref, m_ref, br_ref, acc_ref):
        i = pl.program_id(2)
        br = _dot(a_ref[...].astype(MM_DTYPE), w_ref[...].astype(MM_DTYPE))
        br_ref[...] = br.astype(br_ref.dtype)
        term = _sigmoid(l_ref[...].astype(F32) + b_ref[...]) * br

        @pl.when(i == 0)
        def _():
            acc_ref[...] = term

        @pl.when(i > 0)
        def _():
            acc_ref[...] += term

        @pl.when(i == 3)
        def _():
            m_ref[...] = acc_ref[...].astype(m_ref.dtype)

    return pl.pallas_call(
        body, out_shape=[jax.ShapeDtypeStruct((T, D_MODEL), MM_DTYPE), jax.ShapeDtypeStruct((T, 4 * D_MODEL), MM_DTYPE)],
        grid=(T // tm, nn, 4),
        in_specs=[pl.BlockSpec((tm, BRANCH), lambda m, n, i: (m, i)),
                  pl.BlockSpec((BRANCH, tn), lambda m, n, i: (i, n)),
                  pl.BlockSpec((tm, tn), lambda m, n, i: (m, lb0 + i * nn + n)),
                  pl.BlockSpec((1, tn), lambda m, n, i: (0, i * nn + n))],
        out_specs=[pl.BlockSpec((tm, tn), lambda m, n, i: (m, n)), pl.BlockSpec((tm, tn), lambda m, n, i: (m, i * nn + n))],
        scratch_shapes=[pltpu.VMEM((tm, tn), F32)],
        compiler_params=_params("parallel", "parallel", "arbitrary"), name="merge_fwd")(a, w_br, proj, b_gate)


def merge_bwd(dmerged, br, proj, b_gate):
    T = dmerged.shape[0]
    tb = ROW_TILE
    lb0 = COL["merge"] // D_MODEL

    def body(dm_ref, br_ref, l_ref, b_ref, dbr_ref, dl_ref, db_ref):
        dm = dm_ref[...]
        gates = _sigmoid(l_ref[...] + b_ref[...])
        dbr_ref[...] = (dm * gates).astype(dbr_ref.dtype)
        dl = dm * br_ref[...] * gates * (1.0 - gates)
        dl_ref[...] = dl.astype(dl_ref.dtype)
        s = jnp.broadcast_to(jnp.sum(dl, axis=0, keepdims=True), db_ref.shape)

        @pl.when(pl.program_id(1) == 0)
        def _():
            db_ref[...] = s

        @pl.when(pl.program_id(1) > 0)
        def _():
            db_ref[...] += s

    wide = pl.BlockSpec((tb, D_MODEL), lambda b, i: (i, b))
    return pl.pallas_call(
        body, out_shape=[jax.ShapeDtypeStruct((T, 4 * D_MODEL), MM_DTYPE)] * 2 + [jax.ShapeDtypeStruct((8, 4 * D_MODEL), F32)],
        grid=(4, T // tb),
        in_specs=[pl.BlockSpec((tb, D_MODEL), lambda b, i: (i, 0)), wide,
                  pl.BlockSpec((tb, D_MODEL), lambda b, i: (i, lb0 + b)), pl.BlockSpec((1, D_MODEL), lambda b, i: (0, b))],
        out_specs=[wide, wide, pl.BlockSpec((8, D_MODEL), lambda b, i: (0, b))],
        compiler_params=_params("parallel", "arbitrary"), name="merge_bwd")(dmerged, br, proj, b_gate)


def branch_dx(dbr, w_br):
    T = dbr.shape[0]
    tm, tk = _row_tile(T), D_MODEL
    nk = D_MODEL // tk
    return _matmul(dbr, w_br, pl.BlockSpec((tm, tk), lambda m, i, k: (m, i * nk + k)),
                   pl.BlockSpec((BRANCH, tk), lambda m, i, k: (i, k)),
                   pl.BlockSpec((tm, BRANCH), lambda m, i, k: (m, i)),
                   jax.ShapeDtypeStruct((T, D_MODEL), F32), (T // tm, 4, nk), tb=True, name="branch_dx")


def branch_dw(a, dbr):
    T = a.shape[0]
    tk, tn = T, 1024
    nn = D_MODEL // tn
    return _matmul(a, dbr, pl.BlockSpec((tk, BRANCH), lambda i, n, k: (k, i)),
                   pl.BlockSpec((tk, tn), lambda i, n, k: (k, i * nn + n)),
                   pl.BlockSpec((BRANCH, tn), lambda i, n, k: (i, n)),
                   jax.ShapeDtypeStruct((D_MODEL, D_MODEL), F32), (4, nn, T // tk), ta=True, name="branch_dw")


def proj_fwd(h, w_in, carried=None):
    T = h.shape[0]
    tm, tn = _row_tile(T), 1792
    per = W_IN_SHARD // tn
    return _matmul(h, w_in, pl.BlockSpec((tm, D_MODEL), lambda i, j, k: (i, 0)),
                   pl.BlockSpec((None, D_MODEL, tn), lambda i, j, k: (j // per, 0, j % per)),
                   pl.BlockSpec((tm, tn), lambda i, j, k: (i, j)),
                   jax.ShapeDtypeStruct((T, IN_TOTAL), MM_DTYPE), (T // tm, IN_TOTAL // tn, 1),
                   name="proj_fwd" if carried is None else "proj_fwd_gather", carried=carried)


def proj_dx(dproj, w_in, carried=None):
    T = dproj.shape[0]
    tm, tk = _row_tile(T), 1792
    per = W_IN_SHARD // tk
    return _matmul(dproj, w_in, pl.BlockSpec((tm, tk), lambda i, j, k: (i, k)),
                   pl.BlockSpec((None, D_MODEL, tk), lambda i, j, k: (k // per, 0, k % per)),
                   pl.BlockSpec((tm, D_MODEL), lambda i, j, k: (i, 0)),
                   jax.ShapeDtypeStruct((T, D_MODEL), F32), (T // tm, 1, IN_TOTAL // tk), tb=True,
                   name="proj_dx" if carried is None else "proj_dx_exchange", carried=carried)


def proj_dw(h, dproj):
    T = h.shape[0]
    tk, tm, tn = T, 512, 512
    per = W_IN_SHARD // tn
    return _matmul(h, dproj, pl.BlockSpec((tk, tm), lambda i, j, k: (k, i)),
                   pl.BlockSpec((tk, tn), lambda i, j, k: (k, j)),
                   pl.BlockSpec((None, tm, tn), lambda i, j, k: (j // per, i, j % per)),
                   jax.ShapeDtypeStruct((N_SHARDS, D_MODEL, W_IN_SHARD), F32),
                   (D_MODEL // tm, IN_TOTAL // tn, T // tk), ta=True, name="proj_dw")


def layer_fwd(X, P, *, L, carried=None):
    h = prenorm_fwd(X, P["g_pre"], P["scale"], P["shift"], L=L)
    proj, extras = proj_fwd(h, P["w_in"], carried) if carried is not None else (proj_fwd(h, P["w_in"]), None)
    o_att, lse = attn_fwd(proj, P["ptiles"], L=L)
    o_att, lse_c = cattn_fwd(proj, o_att, L=L)
    o_pool, pooled = pool_fwd(proj, P["pool_w"], P["pool_scale"], L=L)
    o_conv = conv_fwd(proj, P["conv_w"], L=L)
    o_ssm, ssm_saved = ssm_fwd(proj, P["ssm_ops"], P["ssm_d"], P["glu_w"], L=L)
    o = jnp.concatenate([o_att, o_pool, o_conv, o_ssm], axis=1)
    a = gate_act(o, proj, L=L)
    merged, br = merge_fwd(a, P["w_br"], proj, P["b_gate"])
    y = mm_nn(merged, P["w_o"], out_dtype=F32, name="out_proj")
    Xn = postnorm_fwd(X, y, P["g_post"], P["gate"], L=L)
    saved = dict(X=X, h=h, proj=proj, lse=lse, lse_c=lse_c, pooled=pooled, ssm=ssm_saved, o=o, a=a, merged=merged,
                 br=br, y=y)
    return Xn, saved, extras


def layer_bwd(dXn, P, S, *, L, carried=None):
    proj = S["proj"]
    dy, cs_post = postnorm_bwd(dXn, S["y"], P["g_post"], P["gate"], L=L)
    dmerged = mm_nt(dy, P["w_o"], out_dtype=F32, name="out_proj_dx")
    d_w_o = mm_tn(S["merged"], dy, out_dtype=F32, name="out_proj_dw", tm=512, tn=1024)
    dbr, dlogit, d_bgate = merge_bwd(dmerged, S["br"], proj, P["b_gate"])
    da = branch_dx(dbr, P["w_br"])
    d_w_br = branch_dw(S["a"], dbr)
    do, dz = gate_act_bwd(da, S["o"], proj, L=L)
    dq, dk, dv, dptiles = attn_bwd(proj, P["ptiles"], S["o"], do, S["lse"], L=L)
    dq, dk, dv = cattn_bwd(proj, S["o"], do, S["lse_c"], dq, dk, dv, L=L)
    dpool_u, d_pool_w, d_pool_scale = pool_bwd(do[:, BRANCH:2 * BRANCH], S["pooled"], P["pool_w"], P["pool_scale"], L=L)
    dcx, dcb, dcc, d_conv_w = conv_bwd(do[:, 2 * BRANCH:3 * BRANCH], proj, P["conv_w"], L=L)
    dssm_u, d_ops, d_ssm_d, d_glu = ssm_bwd(do[:, 3 * BRANCH:], proj, P["ssm_ops"], P["ssm_d"], P["glu_w"], S["ssm"], L=L)
    z = lambda i: dz[:, i * BRANCH:(i + 1) * BRANCH]
    cast = lambda t: t.astype(MM_DTYPE)
    dproj = jnp.concatenate([cast(dq), cast(dk), cast(dv), z(0), dpool_u, z(1),
                             dcx, dcb, dcc, z(2), cast(dssm_u), z(3), dlogit], axis=1)
    dh, extras = proj_dx(dproj, P["w_in"], carried) if carried is not None else (proj_dx(dproj, P["w_in"]), None)
    d_w_in = proj_dw(S["h"], dproj)
    dX, cs_h, cs_hx = prenorm_bwd(dh, S["X"], P["g_pre"], P["scale"], dXn, L=L)
    g_pre, g_post = P["g_pre"], P["g_post"]
    d_shift = cs_h
    d_scale = cs_hx * g_pre
    d_gate = cs_post * g_post
    d_g_pre = jnp.sum(cs_hx * (1.0 + P["scale"]), axis=0)[0]
    d_g_post = jnp.sum(cs_post * P["gate"], axis=0)[0]
    grads = dict(w_in=d_w_in, w_br=d_w_br, w_o=d_w_o, glu_w=d_glu, conv_w=d_conv_w[0:3], pool_w=d_pool_w,
                 pool_scale=d_pool_scale[0], b_gate=d_bgate[0], d_ptiles=dptiles, d_ops=d_ops, ssm_d=d_ssm_d,
                 g_pre=d_g_pre, g_post=d_g_post,
                 mod=jnp.concatenate([d_shift, d_scale, d_gate], axis=-1)[:, 0])
    return dX, grads, extras


BIG_NAMES = ("w_in", "glu_w", "w_br", "w_o")


def gathered_weights(g):
    w_in, glu, w_br, w_o = g
    return dict(w_in=w_in, glu_w=glu.transpose(1, 0, 2).reshape(BRANCH, 2 * BRANCH),
                w_br=w_br.reshape(D_MODEL, D_MODEL), w_o=w_o.reshape(D_MODEL, D_MODEL))


def shard_major_grads(g):
    return [g["w_in"], g["glu_w"].reshape(BRANCH, N_SHARDS, -1).transpose(1, 0, 2),
            g["w_br"].reshape(N_SHARDS, BRANCH, D_MODEL), g["w_o"].reshape(N_SHARDS, BRANCH, D_MODEL)]


def local_step(x, ctx, target, mods, layers, *, shard_halves=None, reduce=False):
    L = x.shape[0]
    nl = len(layers)
    X = jnp.concatenate([x, ctx], axis=0)
    ssm_all = [jnp.stack([lay["ssm_params"][j] for lay in layers]) for j in range(len(layers[0]["ssm_params"]))]
    ops_all, ops_vjp = jax.vjp(ssm_operators, *[t.reshape((2 * nl,) + t.shape[2:]) for t in ssm_all])
    ptiles_all, bias_vjp = jax.vjp(bias_pair_tiles, jnp.stack([lay["na_rpb"] for lay in layers]))
    saved, Ps = [], []
    arrived = None
    for i, lay in enumerate(layers):
        P = dict(lay)
        if arrived is not None:
            P.update(gathered_weights(arrived))
        m = mods[i][:, None, :]
        P["shift"], P["scale"], P["gate"] = m[..., :D_MODEL], m[..., D_MODEL:2 * D_MODEL], m[..., 2 * D_MODEL:]
        P["ptiles"] = ptiles_all[i]
        P["ssm_ops"] = tuple(t[2 * i:2 * i + 2] for t in ops_all)
        ahead = shard_halves is not None and i + 1 < nl
        X, S, arrived = layer_fwd(X, P, L=L, carried=gather_plan(shard_halves[i + 1], shard_major=True) if ahead else None)
        saved.append(S)
        Ps.append(P)
    loss, dX = loss_and_grad(X, target, L=L)
    grads, reduced = [None] * nl, [None] * nl
    finish = lambda parts: [pair_finish(a, name=f"grad_finish_{n}") for n, a in zip(BIG_NAMES, parts)]
    pending = None
    for i in reversed(range(nl)):
        dX, grads[i], arrived = layer_bwd(dX, Ps[i], saved[i], L=L,
                                          carried=exchange_plan(pending) if pending is not None else None)
        if pending is not None:
            reduced[i + 1] = finish(arrived)
        if reduce:
            pending = [pair_reduce(a, out_dtype=MM_DTYPE, name=f"grad_pair_{n}")
                       for n, a in zip(BIG_NAMES, shard_major_grads(grads[i]))]
    if reduce:
        reduced[0] = finish(chip_exchange(pending, name="grad_exchange"))
    d_rpb, = bias_vjp(jnp.stack([g.pop("d_ptiles") for g in grads]))
    d_ops = [g.pop("d_ops") for g in grads]
    d_ssm = ops_vjp(tuple(jnp.concatenate([d[j] for d in d_ops]) for j in range(3)))
    for i, g in enumerate(grads):
        g["na_rpb"] = d_rpb[i]
        g["ssm"] = tuple(t.reshape((nl, 2) + t.shape[1:])[i] for t in d_ssm)
    return loss, dX[:L], grads, reduced


MESH_ID = pl.DeviceIdType.MESH
HBM_SPEC = pl.BlockSpec(memory_space=pltpu.HBM)


def _place():
    return lax.axis_index("x"), lax.axis_index("y"), lax.axis_index("c")


def _other_chips(x, y):
    return [(1 - x, y), (x, 1 - y), (1 - x, 1 - y)]


def _remote(src, dst, send_sem, recv_sem, to):
    return pltpu.make_async_remote_copy(src_ref=src, dst_ref=dst, send_sem=send_sem, recv_sem=recv_sem,
                                        device_id=to, device_id_type=MESH_ID)


DMA_CHUNK_BYTES = 1 << 20


def _row_pieces(rows, row_bytes):
    n = max(1, min(64, rows * row_bytes // DMA_CHUNK_BYTES))
    while n > 1 and (rows % n or (rows // n) % 16):
        n -= 1
    return [(r * (rows // n), rows // n) for r in range(n)]


def _row_bytes(ref):
    return math.prod(ref.shape[1:]) * jnp.dtype(ref.dtype).itemsize


def _start_in_pieces(make, src, dst):
    for r0, nr in _row_pieces(src.shape[0], _row_bytes(src)):
        make(src.at[pl.ds(r0, nr)], dst.at[pl.ds(r0, nr)]).start()


class Carried:
    def __init__(self, ins, out_shapes, sem_shapes, start, mid, finish):
        self.ins, self.out_shapes, self.sem_shapes = list(ins), list(out_shapes), list(sem_shapes)
        self.start, self.mid, self.finish = start, mid, finish


def run_carried(plan, name):
    n_in, n_out = len(plan.ins), len(plan.out_shapes)

    def body(*refs):
        parts = refs[:n_in], refs[n_in:n_in + n_out], refs[n_in + n_out:]
        plan.start(*parts)
        plan.mid(*parts)
        plan.finish(*parts)

    return pl.pallas_call(body, out_shape=plan.out_shapes, in_specs=[HBM_SPEC] * n_in, out_specs=[HBM_SPEC] * n_out,
                          scratch_shapes=plan.sem_shapes, name=name)(*plan.ins)


def gather_plan(blocks, *, shard_major=False):
    K = len(blocks)

    def tools(ins, outs, sems):
        send_sems, recv_sems, local_sems = sems
        x, y, c = _place()

        def slot(k, block):
            px, py, pc = block
            if shard_major:
                h = ins[k].shape[0]
                return outs[k].at[2 * px + py, pl.ds(pl.multiple_of(pc * h, 16), h)]
            return outs[k].at[4 * px + 2 * py + pc]

        def copy(k, j, to):
            return lambda s, d: _remote(s, d, send_sems.at[k, j], recv_sems.at[k, j], to)

        return (x, y, c), (x, y, 1 - c), _other_chips(x, y), c, slot, copy, local_sems

    def start(ins, outs, sems):
        me, sibling, chips, c, slot, copy, local_sems = tools(ins, outs, sems)
        for k in range(K):
            _start_in_pieces(lambda s, d, k=k: pltpu.make_async_copy(s, d, local_sems.at[k]), ins[k], slot(k, me))
            _start_in_pieces(copy(k, 0, sibling), ins[k], slot(k, me))
            for j, chip in enumerate(chips):
                _start_in_pieces(copy(k, 1 + j, (*chip, c)), ins[k], slot(k, me))

    def mid(ins, outs, sems):
        me, sibling, chips, c, slot, copy, _ = tools(ins, outs, sems)
        for j, chip in enumerate(chips):
            for k in range(K):
                got = slot(k, (*chip, c))
                copy(k, 1 + j, me)(got, got).wait_recv()
                _start_in_pieces(copy(k, 4 + j, sibling), got, got)

    def finish(ins, outs, sems):
        me, sibling, chips, c, slot, copy, local_sems = tools(ins, outs, sems)
        for k in range(K):
            sib = slot(k, sibling)
            copy(k, 0, me)(sib, sib).wait_recv()
            for j, chip in enumerate(chips):
                got = slot(k, (*chip, 1 - c))
                copy(k, 4 + j, me)(got, got).wait_recv()
        for k in range(K):
            own = slot(k, me)
            for j in range(4):
                copy(k, j, me)(ins[k], own).wait_send()
            for j, chip in enumerate(chips):
                got = slot(k, (*chip, c))
                copy(k, 4 + j, me)(got, got).wait_send()
            pltpu.make_async_copy(ins[k], own, local_sems.at[k]).wait()

    def out_shape(b):
        if shard_major:
            return jax.ShapeDtypeStruct((N_SHARDS, 2 * b.shape[0]) + b.shape[1:], b.dtype)
        return jax.ShapeDtypeStruct((8,) + b.shape, b.dtype)

    sems = [pltpu.SemaphoreType.DMA((K, 7)), pltpu.SemaphoreType.DMA((K, 7)), pltpu.SemaphoreType.DMA((K,))]
    return Carried(blocks, [out_shape(b) for b in blocks], sems, start, mid, finish)


def all_gather8(blocks, name, *, shard_major=False):
    return run_carried(gather_plan(blocks, shard_major=shard_major), name)


def exchange_plan(parts):
    K = len(parts)

    def copies(ins, outs, sems):
        send_sems, recv_sems, local_sems = sems
        x, y, c = _place()
        p = 2 * x + y
        makes = []
        for k in range(K):
            for j, (cx, cy) in enumerate(_other_chips(x, y)):
                make = lambda s, d, k=k, j=j, to=(cx, cy, c): _remote(s, d, send_sems.at[k, j], recv_sems.at[k, j], to)
                makes.append((k, 2 * cx + cy, make))
        return p, makes, local_sems

    def start(ins, outs, sems):
        p, makes, local_sems = copies(ins, outs, sems)
        for k in range(K):
            _start_in_pieces(lambda s, d, k=k: pltpu.make_async_copy(s, d, local_sems.at[k]), ins[k].at[p], outs[k].at[p])
        for k, q, make in makes:
            _start_in_pieces(make, ins[k].at[q], outs[k].at[p])

    def mid(ins, outs, sems):
        pass

    def finish(ins, outs, sems):
        p, makes, local_sems = copies(ins, outs, sems)
        waits = [make(ins[k].at[q], outs[k].at[q]) for k, q, make in makes]
        for cp in waits:
            cp.wait_recv()
        for cp in waits:
            cp.wait_send()
        for k in range(K):
            pltpu.make_async_copy(ins[k].at[p], outs[k].at[p], local_sems.at[k]).wait()

    sems = [pltpu.SemaphoreType.DMA((K, 3)), pltpu.SemaphoreType.DMA((K, 3)), pltpu.SemaphoreType.DMA((K,))]
    return Carried(parts, [jax.ShapeDtypeStruct(a.shape, a.dtype) for a in parts], sems, start, mid, finish)


def chip_exchange(parts, name):
    return run_carried(exchange_plan(parts), name)


PAIR_TILE_BYTES = 2 << 20


def _pair_rows(h, n):
    return _pick(h, tuple(t for t in (512, 256, 128, 64, 32, 16) if t * n * 4 <= PAIR_TILE_BYTES))


def _core_index():
    return jnp.reshape(lax.axis_index("c"), (1,)).astype(jnp.int32)


def pair_reduce(g, *, out_dtype, name):
    S, R, n = g.shape
    h = R // 2
    tr = _pair_rows(h, n)
    nt = h // tr

    def body(c_ref, keep_ref, give_ref, o_ref, recv_ref, send_sems, recv_sems):
        x, y, c = _place()
        slot = (pl.program_id(0) * nt + pl.program_id(1)) % 2
        cp = _remote(give_ref, recv_ref.at[slot], send_sems.at[slot], recv_sems.at[slot], (x, y, 1 - c))
        cp.start()
        cp.wait_recv()
        o_ref[...] = (keep_ref[...] + recv_ref[slot]).astype(o_ref.dtype)
        cp.wait_send()

    g2 = g.reshape(S * R, n)
    grid_spec = pltpu.PrefetchScalarGridSpec(
        num_scalar_prefetch=1, grid=(S, nt),
        in_specs=[pl.BlockSpec((tr, n), lambda q, i, c: ((2 * q + c[0]) * nt + i, 0)),
                  pl.BlockSpec((tr, n), lambda q, i, c: ((2 * q + 1 - c[0]) * nt + i, 0))],
        out_specs=pl.BlockSpec((tr, n), lambda q, i, c: (q * nt + i, 0)),
        scratch_shapes=[pltpu.VMEM((2, tr, n), F32), pltpu.SemaphoreType.DMA((2,)), pltpu.SemaphoreType.DMA((2,))])
    out = pl.pallas_call(body, out_shape=jax.ShapeDtypeStruct((S * h, n), out_dtype), grid_spec=grid_spec,
                         compiler_params=_params("arbitrary", "arbitrary"), name=name)(_core_index(), g2, g2)
    return out.reshape(S, h, n)


def pair_finish(parts, *, name):
    S, h, n = parts.shape
    tr = _pair_rows(h, n)
    nt = h // tr

    def body(c_ref, p_ref, o_ref, recv_ref, send_sem, recv_sem):
        x, y, c = _place()
        phase, i = pl.program_id(0), pl.program_id(1)
        rows = pl.ds(pl.multiple_of(i * tr, 16), tr)

        @pl.when(phase == 0)
        def _():
            acc = p_ref[0].astype(F32)
            for s in range(1, S):
                acc = acc + p_ref[s].astype(F32)
            o_ref[...] = acc
            cp = _remote(o_ref, recv_ref.at[rows], send_sem, recv_sem, (x, y, 1 - c))
            cp.start()
            cp.wait_send()

        @pl.when((phase == 1) & (i == 0))
        def _():
            _remote(recv_ref, recv_ref, send_sem, recv_sem, (x, y, 1 - c)).wait_recv()

        @pl.when(phase == 1)
        def _():
            o_ref[...] = recv_ref[rows]

    grid_spec = pltpu.PrefetchScalarGridSpec(
        num_scalar_prefetch=1, grid=(2, nt),
        in_specs=[pl.BlockSpec((S, tr, n), lambda ph, i, c: (0, jnp.where(ph == 0, i, nt - 1), 0))],
        out_specs=pl.BlockSpec((tr, n), lambda ph, i, c: (jnp.where(ph == 0, c[0], 1 - c[0]) * nt + i, 0)),
        scratch_shapes=[pltpu.VMEM((h, n), F32), pltpu.SemaphoreType.DMA(()), pltpu.SemaphoreType.DMA(())])
    return pl.pallas_call(body, out_shape=jax.ShapeDtypeStruct((2 * h, n), F32), grid_spec=grid_spec,
                          compiler_params=_params("arbitrary", "arbitrary"), name=name)(_core_index(), parts)


def _slab_rows(n):
    return max(8, min(256, (1 << 18) // n // 8 * 8))


def sum_slabs(a, *, name):
    S, h, n = a.shape
    tr = _pick(h, tuple(t for t in (256, 128, 64, 32, 16, 8) if t <= _slab_rows(n)))

    def body(a_ref, o_ref):
        acc = a_ref[0].astype(F32)
        for s in range(1, S):
            acc = acc + a_ref[s].astype(F32)
        o_ref[...] = acc

    return pl.pallas_call(body, out_shape=jax.ShapeDtypeStruct((h, n), F32), grid=(h // tr,),
                          in_specs=[pl.BlockSpec((S, tr, n), lambda r: (0, r, 0))],
                          out_specs=pl.BlockSpec((tr, n), lambda r: (r, 0)), compiler_params=_params("parallel"),
                          name=name)(a)


def _adam_math(w, g, m, v):
    m = ADAM_B1 * m + (1.0 - ADAM_B1) * g
    v = ADAM_B2 * v + (1.0 - ADAM_B2) * (g * g)
    m_hat = m / (1.0 - ADAM_B1 ** ADAM_STEP)
    v_hat = v / (1.0 - ADAM_B2 ** ADAM_STEP)
    delta = -ADAM_LR * (m_hat / (jnp.sqrt(v_hat) + ADAM_EPS) + ADAM_WD * w)
    return delta, m, v


def adamw(w, g, m, v, *, name):
    R, n = w.shape
    tr = _pick(R, tuple(t for t in (256, 128, 64, 32, 16, 8) if t <= _slab_rows(n)))
    spec = pl.BlockSpec((tr, n), lambda r: (r, 0))

    def body(w_ref, g_ref, m_ref, v_ref, d_ref, nm_ref, nv_ref):
        d, nm, nv = _adam_math(w_ref[...], g_ref[...], m_ref[...], v_ref[...])
        d_ref[...] = d
        nm_ref[...] = nm
        nv_ref[...] = nv

    return pl.pallas_call(body, out_shape=[jax.ShapeDtypeStruct(w.shape, F32)] * 3, grid=(R // tr,),
                          in_specs=[spec] * 4, out_specs=[spec] * 3, compiler_params=_params("parallel"),
                          name=name)(w, g, m, v)


MOD_ROWS = 16


def mod_fwd(cact_in, w_mod, b_mod):
    nl, _, cols = w_mod.shape
    tn = 512

    def body(c_ref, w_ref, b_ref, o_ref):
        o_ref[...] = _dot(_silu(c_ref[...]).astype(MM_DTYPE), w_ref[...].astype(MM_DTYPE)) + b_ref[...]

    return pl.pallas_call(
        body, out_shape=jax.ShapeDtypeStruct((nl, MOD_ROWS, cols), F32), grid=(nl, cols // tn),
        in_specs=[pl.BlockSpec((MOD_ROWS, D_MODEL), lambda i, j: (0, 0)),
                  pl.BlockSpec((None, D_MODEL, tn), lambda i, j: (i, 0, j)),
                  pl.BlockSpec((None, 1, tn), lambda i, j: (i, 0, j))],
        out_specs=pl.BlockSpec((None, MOD_ROWS, tn), lambda i, j: (i, 0, j)),
        compiler_params=_params("parallel", "parallel"), name="mod_fwd")(cact_in, w_mod, b_mod)


def mod_update(c_rows, dmod, w, m, v):
    nl, _, cols = w.shape
    tr, tn = 256, 512
    wspec = pl.BlockSpec((None, tr, tn), lambda i, r, j: (i, r, j))

    def body(c_ref, d_ref, w_ref, m_ref, v_ref, g_ref, dl_ref, nm_ref, nv_ref):
        g = _dot_tn(_silu(c_ref[...]).astype(MM_DTYPE), d_ref[...].astype(MM_DTYPE))
        g_ref[...] = g
        dl, nm, nv = _adam_math(w_ref[...], g, m_ref[...], v_ref[...])
        dl_ref[...] = dl
        nm_ref[...] = nm
        nv_ref[...] = nv

    return pl.pallas_call(
        body, out_shape=[jax.ShapeDtypeStruct(w.shape, F32)] * 4, grid=(nl, D_MODEL // tr, cols // tn),
        in_specs=[pl.BlockSpec((MOD_ROWS, tr), lambda i, r, j: (0, r)),
                  pl.BlockSpec((None, MOD_ROWS, tn), lambda i, r, j: (i, 0, j)), wspec, wspec, wspec],
        out_specs=[wspec] * 4, compiler_params=_params("parallel", "parallel", "parallel"),
        name="mod_update")(c_rows, dmod, w, m, v)


def cctx_partial(dmod_ctx, w_mod, c_ctx):
    nl, _, cols = w_mod.shape
    tk = 512
    per = cols // tk
    part = _matmul(dmod_ctx, w_mod, pl.BlockSpec((8, tk), lambda i, j, k: (0, k)),
                   pl.BlockSpec((None, D_MODEL, tk), lambda i, j, k: (k // per, 0, k % per)),
                   pl.BlockSpec((8, D_MODEL), lambda i, j, k: (0, 0)),
                   jax.ShapeDtypeStruct((8, D_MODEL), F32), (1, 1, nl * per), tb=True, name="cctx_partial")

    def body(p_ref, c_ref, o_ref):
        o_ref[...] = 0.5 * jnp.sum(p_ref[...], axis=0, keepdims=True) * _silu_grad(c_ref[...])

    return pl.pallas_call(body, out_shape=jax.ShapeDtypeStruct((1, D_MODEL), F32), name="cctx_scale")(part, c_ctx)


WEIGHT_NAMES = ("c_ctx", "w_mod", "b_mod", "g_pre", "g_post", "w_in", "b_gate", "na_rpb", "pool_w", "pool_scale",
                "conv_w", "ssm_a_re", "ssm_a_im", "ssm_log_dt", "ssm_b_re", "ssm_b_im", "ssm_c_re", "ssm_c_im",
                "ssm_d", "glu_w", "w_br", "w_o")
SSM_NAMES = ("ssm_a_re", "ssm_a_im", "ssm_log_dt", "ssm_b_re", "ssm_b_im", "ssm_c_re", "ssm_c_im")
SMALL_NAMES = ("c_ctx", "b_mod", "g_pre", "g_post", "b_gate", "na_rpb", "pool_w", "pool_scale") + SSM_NAMES + ("ssm_d",)
FLAT_COLS = 1024


def _flat(parts):
    v = jnp.concatenate([p.reshape(-1) for p in parts])
    pad = -v.shape[0] % (64 * FLAT_COLS)
    return jnp.pad(v, (0, pad)).reshape(-1, FLAT_COLS)


def _unflat(flat, shapes):
    v = flat.reshape(-1)
    out, off = [], 0
    for s in shapes:
        n = math.prod(s)
        out.append(v[off:off + n].reshape(s))
        off += n
    return out


def kernel(x, c, ctx, c_ctx, w_mod, b_mod, g_pre, g_post, w_in, b_gate, na_rpb, pool_w, pool_scale, conv_w, ssm_a_re, ssm_a_im, ssm_log_dt, ssm_b_re, ssm_b_im, ssm_c_re, ssm_c_im, ssm_d, glu_w, w_br, w_o, loss_target, m_c_ctx, m_w_mod, m_b_mod, m_g_pre, m_g_post, m_w_in, m_b_gate, m_na_rpb, m_pool_w, m_pool_scale, m_conv_w, m_ssm_a_re, m_ssm_a_im, m_ssm_log_dt, m_ssm_b_re, m_ssm_b_im, m_ssm_c_re, m_ssm_c_im, m_ssm_d, m_glu_w, m_w_br, m_w_o, v_c_ctx, v_w_mod, v_b_mod, v_g_pre, v_g_post, v_w_in, v_b_gate, v_na_rpb, v_pool_w, v_pool_scale, v_conv_w, v_ssm_a_re, v_ssm_a_im, v_ssm_log_dt, v_ssm_b_re, v_ssm_b_im, v_ssm_c_re, v_ssm_c_im, v_ssm_d, v_glu_w, v_w_br, v_w_o):
    W = dict(c_ctx=c_ctx, w_mod=w_mod, b_mod=b_mod, g_pre=g_pre, g_post=g_post, w_in=w_in, b_gate=b_gate,
             na_rpb=na_rpb, pool_w=pool_w, pool_scale=pool_scale, conv_w=conv_w, ssm_a_re=ssm_a_re, ssm_a_im=ssm_a_im,
             ssm_log_dt=ssm_log_dt, ssm_b_re=ssm_b_re, ssm_b_im=ssm_b_im, ssm_c_re=ssm_c_re, ssm_c_im=ssm_c_im,
             ssm_d=ssm_d, glu_w=glu_w, w_br=w_br, w_o=w_o)
    M = dict(c_ctx=m_c_ctx, w_mod=m_w_mod, b_mod=m_b_mod, g_pre=m_g_pre, g_post=m_g_post, w_in=m_w_in, b_gate=m_b_gate,
             na_rpb=m_na_rpb, pool_w=m_pool_w, pool_scale=m_pool_scale, conv_w=m_conv_w, ssm_a_re=m_ssm_a_re,
             ssm_a_im=m_ssm_a_im, ssm_log_dt=m_ssm_log_dt, ssm_b_re=m_ssm_b_re, ssm_b_im=m_ssm_b_im,
             ssm_c_re=m_ssm_c_re, ssm_c_im=m_ssm_c_im, ssm_d=m_ssm_d, glu_w=m_glu_w, w_br=m_w_br, w_o=m_w_o)
    V = dict(c_ctx=v_c_ctx, w_mod=v_w_mod, b_mod=v_b_mod, g_pre=v_g_pre, g_post=v_g_post, w_in=v_w_in, b_gate=v_b_gate,
             na_rpb=v_na_rpb, pool_w=v_pool_w, pool_scale=v_pool_scale, conv_w=v_conv_w, ssm_a_re=v_ssm_a_re,
             ssm_a_im=v_ssm_a_im, ssm_log_dt=v_ssm_log_dt, ssm_b_re=v_ssm_b_re, ssm_b_im=v_ssm_b_im,
             ssm_c_re=v_ssm_c_re, ssm_c_im=v_ssm_c_im, ssm_d=v_ssm_d, glu_w=v_glu_w, w_br=v_w_br, w_o=v_w_o)
    nl = w_in.shape[0]
    xi, yi, ci = _place()
    chip = 2 * xi + yi
    example = 4 * xi + 2 * yi + ci
    mod_cols = w_mod.shape[2]

    def my_half(a):
        half = a.shape[0] // 2
        return lax.dynamic_slice_in_dim(a, ci * half, half, axis=0).astype(MM_DTYPE)

    halves = [[my_half(W[n][i]) for n in BIG_NAMES] for i in range(nl)]
    first = gathered_weights(all_gather8(halves[0], name="gather_weights", shard_major=True))

    c8 = jnp.pad(c, ((0, 7), (0, 0)))
    c_all, = all_gather8([c8], name="gather_c")
    c_rows = jnp.concatenate([c_all[:, 0], jnp.broadcast_to(c_ctx[None], (8, D_MODEL))], axis=0)
    b_cols = lax.dynamic_slice_in_dim(b_mod, chip * mod_cols, mod_cols, axis=1)[:, None]
    mod_part = mod_fwd(c_rows, w_mod, b_cols)
    conv_part = jnp.pad(conv_w.reshape(nl * 3, -1), ((0, 16 - nl * 3), (0, 0)))
    parts, conv_all = all_gather8([mod_part.reshape(nl * MOD_ROWS, mod_cols), conv_part], name="gather_mod")
    mod_full = jnp.concatenate([parts[2 * p].reshape(nl, MOD_ROWS, mod_cols) for p in range(N_SHARDS)], axis=-1)
    conv_full = jnp.concatenate([conv_all[2 * p][:nl * 3].reshape(nl, 3, -1) for p in range(N_SHARDS)], axis=-1)
    own = lax.dynamic_index_in_dim(mod_full, example, axis=1, keepdims=False)
    mods = [jnp.stack([own[i], mod_full[i, 8]]) for i in range(nl)]

    layers = []
    for i in range(nl):
        layers.append(dict(
            conv_w=jnp.pad(conv_full[i], ((0, 5), (0, 0))), pool_w=pool_w[i], pool_scale=pool_scale[i][None],
            b_gate=b_gate[i][None], g_pre=g_pre[i][None], g_post=g_post[i][None], na_rpb=na_rpb[i],
            ssm_d=ssm_d[i][None], ssm_params=tuple(W[n][i] for n in SSM_NAMES)))
    layers[0].update(first)

    loss_local, grad_x, grads, reduced = local_step(x[0], ctx[0], loss_target[0], mods, layers,
                                                    shard_halves=halves, reduce=True)
    loss = lax.psum(loss_local, ("x", "y", "c"))

    dmod_local = jnp.stack([g["mod"] for g in grads])
    dmod_all, = all_gather8([jnp.pad(dmod_local.reshape(nl * 2, -1), ((0, 8 - nl * 2), (0, 0)))], name="gather_dmod")
    dmod_all = dmod_all[:, :nl * 2].reshape(8, nl, 2, 3 * D_MODEL)
    dmod_rows = jnp.concatenate([dmod_all[:, :, 0], dmod_all[:, :, 1]], axis=0).transpose(1, 0, 2)
    dmod_cols = lax.dynamic_slice_in_dim(dmod_rows, chip * mod_cols, mod_cols, axis=2)
    g_w_mod, d_w_mod, nm_w_mod, nv_w_mod = mod_update(c_rows, dmod_cols, w_mod, m_w_mod, v_w_mod)
    dctx_cols = dmod_cols[:, 8:].transpose(1, 0, 2).reshape(8, nl * mod_cols)
    g_cctx_part = cctx_partial(dctx_cols, w_mod, c_ctx[None])[0]

    def small_grad(n):
        if n == "c_ctx":
            return g_cctx_part
        if n == "b_mod":
            return jnp.stack([g["mod"][0] + g["mod"][1] for g in grads])
        if n in SSM_NAMES:
            return jnp.stack([g["ssm"][SSM_NAMES.index(n)] for g in grads])
        return jnp.stack([g[n] for g in grads])

    conv_grad_full = jnp.stack([g["conv_w"] for g in grads])
    flat_g = _flat([small_grad(n) for n in SMALL_NAMES] + [conv_grad_full])

    G, DL, NM, NV = {}, {}, {}, {}
    for k, n in enumerate(BIG_NAMES):
        g = jnp.stack([reduced[i][k] for i in range(nl)])
        rows = g.shape[0] * g.shape[1]
        d, nm, nv = adamw(W[n].reshape(rows, -1), g.reshape(rows, -1), M[n].reshape(rows, -1), V[n].reshape(rows, -1),
                          name=f"adamw_{n}")
        G[n], DL[n], NM[n], NV[n] = g, d.reshape(g.shape), nm.reshape(g.shape), nv.reshape(g.shape)

    flat_all, = all_gather8([flat_g], name="gather_small_grads")
    flat_sum = sum_slabs(flat_all, name="sum_small_grads")
    small_shapes = [W[n].shape for n in SMALL_NAMES]
    small_g = _unflat(flat_sum, small_shapes + [conv_grad_full.shape])
    conv_g = lax.dynamic_slice_in_dim(small_g[-1], chip * conv_w.shape[2], conv_w.shape[2], axis=2)
    adam_names = SMALL_NAMES + ("conv_w",)
    adam_shapes = small_shapes + [conv_w.shape]
    g_list = small_g[:-1] + [conv_g]
    upd = adamw(_flat([W[n] for n in adam_names]), _flat(g_list), _flat([M[n] for n in adam_names]),
                _flat([V[n] for n in adam_names]), name="adamw_small")
    G.update(zip(adam_names, g_list))
    for group, u in zip((DL, NM, NV), upd):
        group.update(zip(adam_names, _unflat(u, adam_shapes)))
    G["w_mod"], DL["w_mod"], NM["w_mod"], NV["w_mod"] = g_w_mod, d_w_mod, nm_w_mod, nv_w_mod

    out = [loss, grad_x[None]]
    for group in (G, DL, NM, NV):
        out += [group[n].reshape(W[n].shape) for n in WEIGHT_NAMES]
    return tuple(out)
```
